```python
import math
import jax
import jax.numpy as jnp
from jax import lax
import numpy as np

D_MODEL = 1024
BATCH = 8
SEQ = 2048
DEPTH = 1
DEC_BATCH = 128
DEC_SEQ = 8
PAST_LEN = 8192
PAGE_SIZE = 128

N_HEADS_A = 8
HEAD_DIM = 64
D_ATTN = N_HEADS_A * HEAD_DIM
DILATED_PATTERNS = ((128, 1), (512, 4), (2048, 16))
MAX_WINDOW = 2048
BAND_BLOCK = 128
N_BUCKETS = 32
MAX_DISTANCE = MAX_WINDOW
N_GROUPS_B = 8
D_SGU = 512
GROUP_DIM_B = D_SGU // N_GROUPS_B
CHUNK = 128
D_MIX = D_ATTN + D_SGU
D_IN = 3 * D_ATTN + 2 * D_SGU
N_EXPERTS = 256
TOP_K = 8
N_EXPERT_GROUPS = 8
TOPK_GROUPS = 4
D_EXPERT = 256
ROUTED_SCALE = 2.5
MOE_BLOCK = 128
D_PLE = 256
ALPHA = (2 * DEPTH) ** 0.25
BETA = (8 * DEPTH) ** -0.25
LN_EPS = 1e-5

kernel_name = 'hybrid_dilated_sgu_moe_step'


def layer_norm(x, g, b):
    xf = x.astype(jnp.float32)
    mu = jnp.mean(xf, axis=-1, keepdims=True)
    var = jnp.mean(jnp.square(xf - mu), axis=-1, keepdims=True)
    y = (xf - mu) * lax.rsqrt(var + LN_EPS) * g.astype(jnp.float32) + b.astype(jnp.float32)
    return y.astype(x.dtype)


def t5_bucket(dist):
    max_exact = N_BUCKETS // 2
    df = jnp.maximum(dist, max_exact).astype(jnp.float32)
    large = max_exact + (jnp.log(df / max_exact) / math.log(MAX_DISTANCE / max_exact)
                         * (N_BUCKETS - max_exact)).astype(jnp.int32)
    return jnp.where(dist < max_exact, dist, jnp.minimum(large, N_BUCKETS - 1))


def softmax_stats(s):
    m = jnp.max(s, axis=-1, keepdims=True)
    p = jnp.exp(s - m)
    l = jnp.sum(p, axis=-1, keepdims=True)
    return p, l, m + jnp.log(l)


def dilated_attn_prompt(q, k, v, rel_bias, window, dilation):
    B, S, H, E = q.shape
    w_sub = window // dilation
    L = S // dilation
    nb = -(-L // BAND_BLOCK)
    pad = nb * BAND_BLOCK - L

    def to_blocks(t):
        t = t.reshape(B, L, dilation, H, E).transpose(0, 2, 1, 3, 4)
        t = jnp.pad(t, ((0, 0), (0, 0), (0, pad), (0, 0), (0, 0)))
        return t.reshape(B, dilation, nb, BAND_BLOCK, H, E)

    def with_prev(t):
        prev = jnp.pad(t, ((0, 0), (0, 0), (1, 0), (0, 0), (0, 0), (0, 0)))[:, :, :nb]
        return jnp.concatenate([prev, t], axis=3)

    def from_blocks(t):
        rest = t.shape[4:]
        t = t.reshape((B, dilation, nb * BAND_BLOCK) + rest)[:, :, :L]
        return jnp.swapaxes(t, 1, 2).reshape((B, S) + rest)

    qb = to_blocks(q)
    kb = with_prev(to_blocks(k))
    vb = with_prev(to_blocks(v))
    qi = jnp.arange(BAND_BLOCK)[:, None]
    ki = jnp.arange(2 * BAND_BLOCK)[None, :]
    dsub = qi + BAND_BLOCK - ki
    key_sub = jnp.arange(nb)[:, None, None] * BAND_BLOCK + ki[None] - BAND_BLOCK
    valid = (dsub >= 0) & (dsub <= w_sub) & (key_sub >= 0)
    bias = rel_bias[t5_bucket(jnp.clip(dsub, 0, w_sub) * dilation)].transpose(2, 0, 1).astype(jnp.float32)
    s = jnp.einsum('brnqhe,brnkhe->brnhqk', qb, kb) * (HEAD_DIM ** -0.5) + bias
    s = jnp.where(valid[:, None], s, -jnp.inf)
    p, l, lse = softmax_stats(s)
    o = jnp.einsum('brnhqk,brnkhe->brnqhe', p, vb) / jnp.swapaxes(l, 3, 4)
    lse = jnp.swapaxes(lse[..., 0], 3, 4)
    return from_blocks(o), from_blocks(lse)


def dilated_attn_sample(q, k_all, v_all, rel_bias, window, dilation):
    T = q.shape[1]
    W = k_all.shape[1] - T
    j = jnp.arange(window // dilation + 1)
    idx = W + jnp.arange(T)[:, None] - dilation * j[None, :]
    valid = idx >= 0
    idx = jnp.maximum(idx, 0)
    kg = k_all[:, idx]
    vg = v_all[:, idx]
    bias = rel_bias[t5_bucket(dilation * j)].T.astype(jnp.float32)
    s = jnp.einsum('bthe,btjhe->bthj', q, kg) * (HEAD_DIM ** -0.5) + bias
    s = jnp.where(valid[:, None, :], s, -jnp.inf)
    p, l, lse = softmax_stats(s)
    o = jnp.einsum('bthj,btjhe->bthe', p, vg) / l
    return o, lse[..., 0]


def merge_dilations(outs, lses):
    w = jax.nn.softmax(jnp.stack(lses, axis=0), axis=0)
    return jnp.einsum('pbth,pbthe->bthe', w, jnp.stack(outs, axis=0))


def mixer_inputs(x, w_in):
    h = x @ w_in
    return jnp.split(h, [D_ATTN, 2 * D_ATTN, 3 * D_ATTN, 3 * D_ATTN + D_SGU], axis=-1)


def to_heads(t):
    return t.reshape(t.shape[:-1] + (N_HEADS_A, HEAD_DIM))


def sgu_halves(u, g, ln_g, ln_b):
    return jax.nn.gelu(u, approximate=False), layer_norm(jax.nn.gelu(g, approximate=False), ln_g, ln_b)


def spatial_gate(z2, w_s, b_s):
    b, n, t, _ = z2.shape
    zc = z2.reshape(b, n, t, N_GROUPS_B, GROUP_DIM_B)
    wm = jnp.tril(w_s[:, :t, :t])
    mixed = jnp.einsum('gts,bnsgc->bntgc', wm, zc) + b_s[:, :t].T[:, :, None]
    return mixed.reshape(b, n, t, D_SGU)


def swiglu(x, wg, wu, wd):
    return (jax.nn.silu(x @ wg) * (x @ wu)) @ wd


def route(xf, w_router, router_bias):
    scores = jax.nn.sigmoid(xf.astype(jnp.float32) @ w_router.astype(jnp.float32))
    biased = scores + router_bias.astype(jnp.float32)
    grouped = biased.reshape(-1, N_EXPERT_GROUPS, N_EXPERTS // N_EXPERT_GROUPS)
    group_score = jnp.sum(lax.top_k(grouped, 2)[0], axis=-1)
    top_groups = lax.top_k(group_score, TOPK_GROUPS)[1]
    group_mask = jnp.sum(jax.nn.one_hot(top_groups, N_EXPERT_GROUPS, dtype=jnp.float32), axis=1) > 0
    expert_mask = jnp.repeat(group_mask, N_EXPERTS // N_EXPERT_GROUPS, axis=1)
    idx = lax.top_k(jnp.where(expert_mask, biased, -jnp.inf), TOP_K)[1]
    gates = jnp.take_along_axis(scores, idx, axis=1)
    gates = gates / jnp.sum(gates, axis=-1, keepdims=True) * ROUTED_SCALE
    return idx, gates


def routed_experts(xf, idx, gates, w_gate_e, w_up_e, w_down_e):
    T, D = xf.shape
    n_assign = T * TOP_K
    flat_e = idx.reshape(-1)
    order = jnp.argsort(flat_e)
    sorted_e = flat_e[order]
    counts = jnp.bincount(flat_e, length=N_EXPERTS)
    padded = (counts + MOE_BLOCK - 1) // MOE_BLOCK * MOE_BLOCK
    padded_end = jnp.cumsum(padded)
    padded_start = padded_end - padded
    start = jnp.cumsum(counts) - counts
    dest = padded_start[sorted_e] + jnp.arange(n_assign) - start[sorted_e]
    n_blocks = -(-(n_assign + N_EXPERTS * (MOE_BLOCK - 1)) // MOE_BLOCK)
    n_rows = n_blocks * MOE_BLOCK
    token_sorted = (order // TOP_K).astype(jnp.int32)
    row_token = jnp.full((n_rows,), T, jnp.int32).at[dest].set(token_sorted)
    block_expert = jnp.minimum(
        jnp.searchsorted(padded_end, jnp.arange(n_blocks) * MOE_BLOCK, side='right'), N_EXPERTS - 1)
    x_pad = jnp.concatenate([xf, jnp.zeros((1, D), xf.dtype)], axis=0)
    x_rows = x_pad[row_token].reshape(n_blocks, MOE_BLOCK, D)

    def expert_block(args):
        xb, e = args
        return swiglu(xb, w_gate_e[e], w_up_e[e], w_down_e[e])

    y_rows = lax.map(expert_block, (x_rows, block_expert)).reshape(n_rows, D)
    y_assign = y_rows[dest] * gates.reshape(-1)[order][:, None].astype(y_rows.dtype)
    return jax.ops.segment_sum(y_assign, token_sorted, num_segments=T)


def channel_tail(x, mix, p, lw):
    x1 = layer_norm(ALPHA * x + mix @ lw['w_mix_out'], lw['ln1_g'], lw['ln1_b'])
    xf = x1.reshape(-1, x1.shape[-1])
    idx, gates = route(xf, lw['w_router'], lw['router_bias'])
    ffn = (routed_experts(xf, idx, gates, lw['w_gate_e'], lw['w_up_e'], lw['w_down_e'])
           + swiglu(xf, lw['w_gate_s'], lw['w_up_s'], lw['w_down_s']))
    ple = jax.nn.sigmoid(x1 @ lw['w_ple_gate']) * (p @ lw['w_ple_proj'])
    return layer_norm(ALPHA * x1 + ffn.reshape(x1.shape) + ple, lw['ln2_g'], lw['ln2_b'])


def prompt_layer(x, p, lw, rel_bias):
    b, s, _ = x.shape
    q, k, v, u, g = mixer_inputs(x, lw['w_in'])
    k_rows, v_rows = to_heads(k), to_heads(v)
    qf = to_heads(q).astype(jnp.float32)
    kf, vf = k_rows.astype(jnp.float32), v_rows.astype(jnp.float32)
    outs, lses = [], []
    for window, dilation in DILATED_PATTERNS:
        o, lse = dilated_attn_prompt(qf, kf, vf, rel_bias, window, dilation)
        outs.append(o)
        lses.append(lse)
    attn = merge_dilations(outs, lses).reshape(b, s, D_ATTN).astype(x.dtype)
    z1, z2 = sgu_halves(u, g, lw['sgu_ln_g'], lw['sgu_ln_b'])
    sgu = z1 * spatial_gate(z2.reshape(b, s // CHUNK, CHUNK, D_SGU), lw['sgu_w'], lw['sgu_b']).reshape(b, s, D_SGU)
    y = channel_tail(x, jnp.concatenate([attn, sgu.astype(x.dtype)], axis=-1), p, lw)
    keep = min(MAX_WINDOW, s)
    return y, k_rows[:, s - keep:], v_rows[:, s - keep:]


def sample_layer(x, p, ck, cv, lw, rel_bias):
    b, t, _ = x.shape
    q, k, v, u, g = mixer_inputs(x, lw['w_in'])
    k_rows, v_rows = to_heads(k), to_heads(v)
    qf = to_heads(q).astype(jnp.float32)
    k_all = jnp.concatenate([ck.astype(jnp.float32), k_rows.astype(jnp.float32)], axis=1)
    v_all = jnp.concatenate([cv.astype(jnp.float32), v_rows.astype(jnp.float32)], axis=1)
    outs, lses = [], []
    for window, dilation in DILATED_PATTERNS:
        o, lse = dilated_attn_sample(qf, k_all, v_all, rel_bias, window, dilation)
        outs.append(o)
        lses.append(lse)
    attn = merge_dilations(outs, lses).reshape(b, t, D_ATTN).astype(x.dtype)
    z1, z2 = sgu_halves(u, g, lw['sgu_ln_g'], lw['sgu_ln_b'])
    sgu = z1 * spatial_gate(z2[:, None], lw['sgu_w'], lw['sgu_b'])[:, 0]
    y = channel_tail(x, jnp.concatenate([attn, sgu.astype(x.dtype)], axis=-1), p, lw)
    return y, k_rows, v_rows, z2


def setup_inputs(seed: int = 0) -> dict:
    key = jax.random.key(seed)
    ks = jax.random.split(key, 40)
    f32 = jnp.float32
    w_buf = min(MAX_WINDOW, PAST_LEN)
    inv = D_MODEL ** -0.5

    def nrm(k, shape, scale):
        return jax.random.normal(k, shape, f32) * scale

    w_in = jnp.concatenate([
        nrm(ks[6], (DEPTH, D_MODEL, D_ATTN), inv),
        nrm(ks[7], (DEPTH, D_MODEL, D_ATTN), inv),
        nrm(ks[8], (DEPTH, D_MODEL, D_ATTN), inv * BETA),
        nrm(ks[9], (DEPTH, D_MODEL, D_SGU), inv),
        nrm(ks[10], (DEPTH, D_MODEL, D_SGU), inv)], axis=-1)
    return {
        'x_prompt': nrm(ks[0], (BATCH, SEQ, D_MODEL), 1.0),
        'x_sample': nrm(ks[1], (DEC_BATCH, DEC_SEQ, D_MODEL), 1.0),
        'cache_k': nrm(ks[2], (DEPTH, DEC_BATCH, w_buf, N_HEADS_A, HEAD_DIM), 1.0),
        'cache_v': nrm(ks[3], (DEPTH, DEC_BATCH, w_buf, N_HEADS_A, HEAD_DIM), 1.0),
        'p_prompt': nrm(ks[4], (DEPTH, BATCH, SEQ, D_PLE), 1.0),
        'p_sample': nrm(ks[5], (DEPTH, DEC_BATCH, DEC_SEQ, D_PLE), 1.0),
        'w_in': w_in,
        'rel_bias': nrm(ks[11], (N_BUCKETS, N_HEADS_A), 0.5),
        'sgu_w': nrm(ks[12], (DEPTH, N_GROUPS_B, CHUNK, CHUNK), CHUNK ** -0.5),
        'sgu_b': 1.0 + nrm(ks[13], (DEPTH, N_GROUPS_B, CHUNK), 0.1),
        'sgu_ln_g': 1.0 + nrm(ks[14], (DEPTH, D_SGU), 0.1),
        'sgu_ln_b': nrm(ks[15], (DEPTH, D_SGU), 0.1),
        'w_mix_out': nrm(ks[16], (DEPTH, D_MIX, D_MODEL), D_MIX ** -0.5 * BETA),
        'ln1_g': 1.0 + nrm(ks[17], (DEPTH, D_MODEL), 0.1),
        'ln1_b': nrm(ks[18], (DEPTH, D_MODEL), 0.1),
        'w_router': nrm(ks[19], (DEPTH, D_MODEL, N_EXPERTS), inv),
        'router_bias': nrm(ks[20], (DEPTH, N_EXPERTS), 0.01),
        'w_gate_e': nrm(ks[21], (DEPTH, N_EXPERTS, D_MODEL, D_EXPERT), inv),
        'w_up_e': nrm(ks[22], (DEPTH, N_EXPERTS, D_MODEL, D_EXPERT), inv),
        'w_down_e': nrm(ks[23], (DEPTH, N_EXPERTS, D_EXPERT, D_MODEL), D_EXPERT ** -0.5 * BETA),
        'w_gate_s': nrm(ks[24], (DEPTH, D_MODEL, D_EXPERT), inv),
        'w_up_s': nrm(ks[25], (DEPTH, D_MODEL, D_EXPERT), inv),
        'w_down_s': nrm(ks[26], (DEPTH, D_EXPERT, D_MODEL), D_EXPERT ** -0.5 * BETA),
        'w_ple_gate': nrm(ks[27], (DEPTH, D_MODEL, D_MODEL), inv),
        'w_ple_proj': nrm(ks[28], (DEPTH, D_PLE, D_MODEL), D_PLE ** -0.5),
        'ln2_g': 1.0 + nrm(ks[29], (DEPTH, D_MODEL), 0.1),
        'ln2_b': nrm(ks[30], (DEPTH, D_MODEL), 0.1),
    }


def reference(x_prompt, x_sample, cache_k, cache_v, p_prompt, p_sample, w_in, rel_bias, sgu_w, sgu_b,
              sgu_ln_g, sgu_ln_b, w_mix_out, ln1_g, ln1_b, w_router, router_bias, w_gate_e, w_up_e,
              w_down_e, w_gate_s, w_up_s, w_down_s, w_ple_gate, w_ple_proj, ln2_g, ln2_b):
    xp, xs = x_prompt, x_sample
    kp_list, vp_list, ks_list, vs_list, zs_list = [], [], [], [], []
    for i in range(DEPTH):
        lw = {
            'w_in': w_in[i], 'sgu_w': sgu_w[i], 'sgu_b': sgu_b[i], 'sgu_ln_g': sgu_ln_g[i],
            'sgu_ln_b': sgu_ln_b[i], 'w_mix_out': w_mix_out[i], 'ln1_g': ln1_g[i], 'ln1_b': ln1_b[i],
            'w_router': w_router[i], 'router_bias': router_bias[i], 'w_gate_e': w_gate_e[i],
            'w_up_e': w_up_e[i], 'w_down_e': w_down_e[i], 'w_gate_s': w_gate_s[i], 'w_up_s': w_up_s[i],
            'w_down_s': w_down_s[i], 'w_ple_gate': w_ple_gate[i], 'w_ple_proj': w_ple_proj[i],
            'ln2_g': ln2_g[i], 'ln2_b': ln2_b[i],
        }
        xp, kp, vp = prompt_layer(xp, p_prompt[i], lw, rel_bias)
        xs, k_s, v_s, z_s = sample_layer(xs, p_sample[i], cache_k[i], cache_v[i], lw, rel_bias)
        kp_list.append(kp)
        vp_list.append(vp)
        ks_list.append(k_s)
        vs_list.append(v_s)
        zs_list.append(z_s)
    return (xp, xs, jnp.stack(kp_list), jnp.stack(vp_list), jnp.stack(ks_list), jnp.stack(vs_list), jnp.stack(zs_list))
```

```python
import functools
import math

import numpy as np
import jax
import jax.numpy as jnp
from jax import lax
from jax.experimental import pallas as pl
from jax.experimental.pallas import tpu as pltpu

F32 = jnp.float32
BF16 = jnp.bfloat16
NEG_INF = float("-inf")

N_HEADS = 8
HEAD_DIM = 64
D_ATTN = N_HEADS * HEAD_DIM
PATTERNS = ((128, 1), (512, 4), (2048, 16))
BAND = 128
N_BUCKETS = 32
MAX_DISTANCE = 2048
N_GROUPS_SGU = 8
D_SGU = 512
CHUNK = 128
N_EXPERTS = 256
TOP_K = 8
N_EXPERT_GROUPS = 8
TOPK_GROUPS = 4
ROUTED_SCALE = 2.5
LN_EPS = 1e-5
EXPERT_ROWS = 256

LANES = 128
SUBLANES = 8
VMEM_LIMIT = 56 * 1024 * 1024


def _cparams(*sem):
    return pltpu.CompilerParams(dimension_semantics=sem, vmem_limit_bytes=VMEM_LIMIT)


def _layer_norm(x, g, b):
    mu = jnp.mean(x, axis=-1, keepdims=True)
    xc = x - mu
    var = jnp.mean(xc * xc, axis=-1, keepdims=True)
    return xc * lax.rsqrt(var + LN_EPS) * g + b


def _sigmoid(x):
    return 1.0 / (1.0 + jnp.exp(-x))


def _gelu(x):
    return 0.5 * x * (1.0 + lax.erf(x * math.sqrt(0.5)))


def _dot(a, b):
    return jnp.dot(a, b, preferred_element_type=F32)


def _dot_nt(a, b):
    return lax.dot_general(a, b, (((1,), (1,)), ((), ())), preferred_element_type=F32)


def _t5_bucket_np(dist):
    max_exact = N_BUCKETS // 2
    df = np.maximum(dist, max_exact).astype(np.float32)
    large = max_exact + (np.log(df / np.float32(max_exact)) / np.float32(math.log(MAX_DISTANCE / max_exact))
                         * np.float32(N_BUCKETS - max_exact)).astype(np.int32)
    return np.where(dist < max_exact, dist, np.minimum(large, N_BUCKETS - 1)).astype(np.int32)


def _band_bucket_table(dilation):
    qi = np.arange(BAND)[:, None]
    ki = np.arange(2 * BAND)[None, :]
    dsub = qi + BAND - ki
    valid = (dsub >= 0) & (dsub <= BAND)
    return np.where(valid, _t5_bucket_np(np.clip(dsub, 0, BAND) * dilation), -1).astype(np.int32)


def _sample_bucket_tables(w_buf, t_len):
    t = np.arange(t_len)[:, None]
    c = np.arange(w_buf)[None, :]
    cn = np.arange(LANES)[None, :]
    cache, new = [], []
    for window, dilation in PATTERNS:
        d = w_buf + t - c
        ok = (d >= 0) & (d % dilation == 0) & (d <= window)
        cache.append(np.where(ok, _t5_bucket_np(np.maximum(d, 0)), -1))
        dn = t - cn
        okn = (dn >= 0) & (dn % dilation == 0) & (dn <= window) & (cn < t_len)
        new.append(np.where(okn, _t5_bucket_np(np.maximum(dn, 0)), -1))
    return np.stack(cache).astype(np.int32), np.stack(new).astype(np.int32)


def _bias_kernel(rb_ref, bucket_ref, out_ref):
    bucket = bucket_ref[...]
    for h in range(N_HEADS):
        acc = jnp.full(bucket.shape, NEG_INF, F32)
        for b in range(N_BUCKETS):
            acc = jnp.where(bucket == b, rb_ref[b, h], acc)
        out_ref[h] = acc


def _bias_table(rel_bias, bucket_np):
    r, c = bucket_np.shape
    return pl.pallas_call(
        _bias_kernel,
        out_shape=jax.ShapeDtypeStruct((N_HEADS, r, c), F32),
        in_specs=[pl.BlockSpec(memory_space=pltpu.SMEM), pl.BlockSpec(memory_space=pltpu.VMEM)],
        out_specs=pl.BlockSpec(memory_space=pltpu.VMEM),
        name="bias_table",
    )(rel_bias, jnp.asarray(bucket_np))


def _proj_kernel(x_ref, w_ref, q_ref, k_ref, v_ref, u_ref, g_ref):
    x = x_ref[...].astype(BF16)
    col = 0
    for o in (q_ref, k_ref, v_ref, u_ref, g_ref):
        n = o.shape[1]
        o[...] = _dot(x, w_ref[:, col:col + n])
        col += n


def _proj(x, w, tm):
    m, d = x.shape
    n_out = (D_ATTN, D_ATTN, D_ATTN, D_SGU, D_SGU)
    return pl.pallas_call(
        _proj_kernel,
        out_shape=[jax.ShapeDtypeStruct((m, n), F32) for n in n_out],
        grid=(m // tm,),
        in_specs=[pl.BlockSpec((tm, d), lambda i: (i, 0)), pl.BlockSpec(w.shape, lambda i: (0, 0))],
        out_specs=[pl.BlockSpec((tm, n), lambda i: (i, 0)) for n in n_out],
        compiler_params=_cparams("parallel"),
        name="in_proj",
    )(x, w)


def _band_attn(q, k, v, bias_e, bias_o, even):
    kb = k.astype(BF16)
    vb = v.astype(BF16)
    parts = []
    for sel, bias in ((even, bias_e), (jnp.logical_not(even), bias_o)):
        qm = jnp.where(sel, q, 0.0).astype(BF16)
        s = _dot_nt(qm, kb) + bias
        m = jnp.max(s, axis=-1, keepdims=True)
        p = jnp.exp(s - m)
        l = jnp.sum(p, axis=-1, keepdims=True)
        parts.append((m, l, _dot(p.astype(BF16), vb)))
    (m0, l0, a0), (m1, l1, a1) = parts
    return jnp.where(even, m0, m1), jnp.where(even, l0, l1), jnp.where(even, a0, a1)


def _attn_prompt_kernel(q_ref, k_ref, v_ref, b1_ref, b4_ref, b16_ref, o_ref,
                        m1, l1, a1, m4, l4, a4, m16, l16, a16):
    seq = q_ref.shape[0]
    even = lax.broadcasted_iota(jnp.int32, (BAND, LANES), 1) < HEAD_DIM

    m, l, a = _band_attn(q_ref[0:BAND], k_ref[0:BAND], v_ref[0:BAND],
                         b1_ref[0, :, BAND:], b1_ref[1, :, BAND:], even)
    m1[0:BAND], l1[0:BAND], a1[0:BAND] = m, l, a

    def d1_body(i, carry):
        r0 = pl.multiple_of(i * BAND, BAND)
        m, l, a = _band_attn(q_ref[pl.ds(r0, BAND)], k_ref[pl.ds(r0 - BAND, 2 * BAND)],
                             v_ref[pl.ds(r0 - BAND, 2 * BAND)], b1_ref[0], b1_ref[1], even)
        m1[pl.ds(r0, BAND)], l1[pl.ds(r0, BAND)], a1[pl.ds(r0, BAND)] = m, l, a
        return carry

    lax.fori_loop(1, seq // BAND, d1_body, 0)

    n4 = seq // 4

    def d4_body(r, carry):
        q4 = q_ref[pl.ds(r, n4, stride=4), :]
        k4 = k_ref[pl.ds(r, n4, stride=4), :]
        v4 = v_ref[pl.ds(r, n4, stride=4), :]
        for i in range(n4 // BAND):
            lo = max(i - 1, 0) * BAND
            hi = (i + 1) * BAND
            col = 0 if i > 0 else BAND
            m, l, a = _band_attn(q4[i * BAND:hi], k4[lo:hi], v4[lo:hi],
                                 b4_ref[0, :, col:], b4_ref[1, :, col:], even)
            rows = pl.ds(r + 4 * BAND * i, BAND, stride=4)
            m4[rows, :], l4[rows, :], a4[rows, :] = m, l, a
        return carry

    lax.fori_loop(0, 4, d4_body, 0)

    def d16_body(r, carry):
        rows = pl.ds(r, seq // 16, stride=16)
        m, l, a = _band_attn(q_ref[rows, :], k_ref[rows, :], v_ref[rows, :],
                             b16_ref[0, :, BAND:], b16_ref[1, :, BAND:], even)
        m16[rows, :], l16[rows, :], a16[rows, :] = m, l, a
        return carry

    lax.fori_loop(0, 16, d16_body, 0)

    def merge_body(i, carry):
        rows = pl.ds(pl.multiple_of(i * BAND, BAND), BAND)
        ma, mb, mc = m1[rows], m4[rows], m16[rows]
        mx = jnp.maximum(jnp.maximum(ma, mb), mc)
        wa, wb, wc = jnp.exp(ma - mx), jnp.exp(mb - mx), jnp.exp(mc - mx)
        num = wa * a1[rows] + wb * a4[rows] + wc * a16[rows]
        den = wa * l1[rows] + wb * l4[rows] + wc * l16[rows]
        o_ref[rows] = (num / den).astype(o_ref.dtype)
        return carry

    lax.fori_loop(0, seq // BAND, merge_body, 0)


def _attn_prompt(q, k, v, b1, b4, b16, batch, seq):
    blk = pl.BlockSpec((seq, LANES), lambda b, j: (b, j))
    bias_spec = pl.BlockSpec((2, BAND, 2 * BAND), lambda b, j: (j, 0, 0))
    return pl.pallas_call(
        _attn_prompt_kernel,
        out_shape=jax.ShapeDtypeStruct(q.shape, BF16),
        grid=(batch, D_ATTN // LANES),
        in_specs=[blk, blk, blk, bias_spec, bias_spec, bias_spec],
        out_specs=blk,
        scratch_shapes=[pltpu.VMEM((seq, LANES), F32) for _ in range(9)],
        compiler_params=_cparams("parallel", "parallel"),
        name="attn_prompt",
    )(q, k, v, b1, b4, b16)


def _attn_sample_kernel(q_ref, kn_ref, vn_ref, ck_ref, cv_ref, bc_ref, bn_ref, o_ref):
    t_len = q_ref.shape[1]
    rows = N_HEADS * t_len
    q = q_ref[0]
    head_of_row = lax.broadcasted_iota(jnp.int32, (rows, D_ATTN), 0) // t_len
    head_of_lane = lax.broadcasted_iota(jnp.int32, (rows, D_ATTN), 1) // HEAD_DIM
    own = head_of_row == head_of_lane
    qrows = jnp.where(own, jnp.concatenate([q] * N_HEADS, axis=0), 0.0).astype(BF16)
    pad = jnp.zeros((LANES - t_len, D_ATTN), F32)
    kn = jnp.concatenate([kn_ref[0], pad], axis=0).astype(BF16)
    vn = jnp.concatenate([vn_ref[0], pad], axis=0).astype(BF16)
    kc = ck_ref[0].astype(BF16)
    vc = cv_ref[0].astype(BF16)
    s_c = _dot_nt(qrows, kc)
    s_n = _dot_nt(qrows, kn)
    ms, ls, accs = [], [], []
    for p in range(len(PATTERNS)):
        sc = s_c + bc_ref[p]
        sn = s_n + bn_ref[p]
        m = jnp.maximum(jnp.max(sc, axis=-1, keepdims=True), jnp.max(sn, axis=-1, keepdims=True))
        pc = jnp.exp(sc - m)
        pn = jnp.exp(sn - m)
        ls.append(jnp.sum(pc, axis=-1, keepdims=True) + jnp.sum(pn, axis=-1, keepdims=True))
        accs.append(_dot(pc.astype(BF16), vc) + _dot(pn.astype(BF16), vn))
        ms.append(m)
    mx = jnp.maximum(jnp.maximum(ms[0], ms[1]), ms[2])
    ws = [jnp.exp(m - mx) for m in ms]
    num = ws[0] * accs[0] + ws[1] * accs[1] + ws[2] * accs[2]
    den = ws[0] * ls[0] + ws[1] * ls[1] + ws[2] * ls[2]
    full = jnp.where(own, num / den, 0.0)
    out = full[0:t_len]
    for h in range(1, N_HEADS):
        out = out + full[h * t_len:(h + 1) * t_len]
    o_ref[0] = out.astype(o_ref.dtype)


def _attn_sample(q, kn, vn, ck, cv, bc, bn):
    b, t_len, _ = q.shape
    w_buf = ck.shape[1]
    rows = N_HEADS * t_len
    new_spec = pl.BlockSpec((1, t_len, D_ATTN), lambda i: (i, 0, 0))
    cache_spec = pl.BlockSpec((1, w_buf, D_ATTN), lambda i: (i, 0, 0))
    return pl.pallas_call(
        _attn_sample_kernel,
        out_shape=jax.ShapeDtypeStruct(q.shape, BF16),
        grid=(b,),
        in_specs=[new_spec, new_spec, new_spec, cache_spec, cache_spec,
                  pl.BlockSpec((3, rows, w_buf), lambda i: (0, 0, 0)),
                  pl.BlockSpec((3, rows, LANES), lambda i: (0, 0, 0))],
        out_specs=new_spec,
        compiler_params=_cparams("parallel"),
        name="attn_sample",
    )(q, kn, vn, ck, cv, bc, bn)


def _sgu_kernel(u_ref, g_ref, mix_ref, bias_ref, lng_ref, lnb_ref, sgu_ref, *z2_out):
    n_chunks = u_ref.shape[0] // CHUNK
    group_of_lane = lax.broadcasted_iota(jnp.int32, (CHUNK, D_SGU), 1) // (D_SGU // N_GROUPS_SGU)
    for c in range(n_chunks):
        rows = slice(c * CHUNK, (c + 1) * CHUNK)
        z1 = _gelu(u_ref[rows])
        z2 = _layer_norm(_gelu(g_ref[rows]), lng_ref[...], lnb_ref[...])
        if z2_out:
            z2_out[0][rows] = z2
        mixed = bias_ref[...]
        for g in range(N_GROUPS_SGU):
            mixed = mixed + _dot(mix_ref[g], jnp.where(group_of_lane == g, z2, 0.0).astype(BF16))
        sgu_ref[rows] = (z1 * mixed).astype(sgu_ref.dtype)


def _sgu(u, g, mix, bias, ln_g, ln_b, want_z2, chunks_per_step):
    m = u.shape[0]
    tm = CHUNK * chunks_per_step
    row_spec = pl.BlockSpec((tm, D_SGU), lambda i: (i, 0))
    const2 = lambda i: (0, 0)
    out_shape = [jax.ShapeDtypeStruct((m, D_SGU), BF16)]
    out_specs = [row_spec]
    if want_z2:
        out_shape.append(jax.ShapeDtypeStruct((m, D_SGU), F32))
        out_specs.append(row_spec)
    return pl.pallas_call(
        _sgu_kernel,
        out_shape=out_shape,
        grid=(m // tm,),
        in_specs=[row_spec, row_spec,
                  pl.BlockSpec(mix.shape, lambda i: (0, 0, 0)),
                  pl.BlockSpec(bias.shape, const2),
                  pl.BlockSpec(ln_g.shape, const2), pl.BlockSpec(ln_b.shape, const2)],
        out_specs=out_specs,
        compiler_params=_cparams("parallel"),
        name="sgu",
    )(u, g, mix, bias, ln_g, ln_b)


def _top1_rows(x, row):
    m = jnp.max(x, axis=0, keepdims=True)
    i = jnp.min(jnp.where(x == m, row, x.shape[0]), axis=0, keepdims=True)
    return m, i


def _mid_kernel(alpha, x_ref, attn_ref, sgu_ref, wmix_ref, g1_ref, b1_ref, wrt_ref, rb_ref,
                x1_ref, idx_ref, gate_ref):
    y = (_dot(attn_ref[...], wmix_ref[0:D_ATTN]) + _dot(sgu_ref[...], wmix_ref[D_ATTN:])
         + alpha * x_ref[...])
    x1 = _layer_norm(y, g1_ref[...], b1_ref[...])
    x1_ref[...] = x1
    tm = x1.shape[0]
    scores = _sigmoid(_dot_nt(wrt_ref[...], x1.astype(BF16)))
    biased = scores + rb_ref[...]
    per_group = N_EXPERTS // N_EXPERT_GROUPS
    row_g = lax.broadcasted_iota(jnp.int32, (per_group, tm), 0)
    row_8 = lax.broadcasted_iota(jnp.int32, (N_EXPERT_GROUPS, tm), 0)
    row_e = lax.broadcasted_iota(jnp.int32, (N_EXPERTS, tm), 0)
    cur = jnp.full((N_EXPERT_GROUPS, tm), NEG_INF, F32)
    for g in range(N_EXPERT_GROUPS):
        blk = biased[g * per_group:(g + 1) * per_group]
        m_a, i_a = _top1_rows(blk, row_g)
        m_b = jnp.max(jnp.where(row_g == i_a, NEG_INF, blk), axis=0, keepdims=True)
        cur = jnp.where(row_8 == g, m_a + m_b, cur)
    chosen = jnp.zeros((N_EXPERT_GROUPS, tm), F32)
    for _ in range(TOPK_GROUPS):
        _, i_g = _top1_rows(cur, row_8)
        sel = row_8 == i_g
        chosen = jnp.where(sel, 1.0, chosen)
        cur = jnp.where(sel, NEG_INF, cur)
    group_of_row = row_e // per_group
    emask = jnp.zeros((N_EXPERTS, tm), F32)
    for g in range(N_EXPERT_GROUPS):
        emask = jnp.where(group_of_row == g, chosen[g:g + 1], emask)
    cur = jnp.where(emask > 0.0, biased, NEG_INF)
    idx = jnp.zeros((TOP_K, tm), jnp.int32)
    gates = jnp.zeros((TOP_K, tm), F32)
    for k in range(TOP_K):
        _, i_e = _top1_rows(cur, row_e)
        sel = row_e == i_e
        gate_k = jnp.sum(jnp.where(sel, scores, 0.0), axis=0, keepdims=True)
        cur = jnp.where(sel, NEG_INF, cur)
        idx = jnp.where(row_8 == k, i_e, idx)
        gates = jnp.where(row_8 == k, gate_k, gates)
    idx_ref[...] = idx
    gate_ref[...] = gates / jnp.sum(gates, axis=0, keepdims=True) * ROUTED_SCALE


def _mid(x, attn, sgu, wmix, g1, b1, wrt, rb, alpha, tm):
    t, d = x.shape
    const2 = lambda i: (0, 0)
    return pl.pallas_call(
        functools.partial(_mid_kernel, alpha),
        out_shape=[jax.ShapeDtypeStruct((t, d), F32),
                   jax.ShapeDtypeStruct((TOP_K, t), jnp.int32),
                   jax.ShapeDtypeStruct((TOP_K, t), F32)],
        grid=(t // tm,),
        in_specs=[pl.BlockSpec((tm, d), lambda i: (i, 0)),
                  pl.BlockSpec((tm, D_ATTN), lambda i: (i, 0)),
                  pl.BlockSpec((tm, D_SGU), lambda i: (i, 0)),
                  pl.BlockSpec(wmix.shape, const2), pl.BlockSpec(g1.shape, const2),
                  pl.BlockSpec(b1.shape, const2), pl.BlockSpec(wrt.shape, const2),
                  pl.BlockSpec(rb.shape, const2)],
        out_specs=[pl.BlockSpec((tm, d), lambda i: (i, 0)),
                   pl.BlockSpec((TOP_K, tm), lambda i: (0, i)),
                   pl.BlockSpec((TOP_K, tm), lambda i: (0, i))],
        compiler_params=_cparams("parallel"),
        name="mix_ln_router",
    )(x, attn, sgu, wmix, g1, b1, wrt, rb)


def _rank_kernel(idx_ref, rank_ref, counts_ref, run_ref):
    @pl.when(pl.program_id(0) == 0)
    def _():
        run_ref[...] = jnp.zeros_like(run_ref)

    tm = idx_ref.shape[1]
    row_e = lax.broadcasted_iota(jnp.int32, (N_EXPERTS, tm), 0)
    idx = idx_ref[...]
    onehot = jnp.zeros((N_EXPERTS, tm), F32)
    for k in range(TOP_K):
        onehot = onehot + jnp.where(row_e == idx[k:k + 1], 1.0, 0.0)
    earlier = (lax.broadcasted_iota(jnp.int32, (tm, tm), 0)
               < lax.broadcasted_iota(jnp.int32, (tm, tm), 1))
    before = run_ref[...] + _dot(onehot.astype(BF16), jnp.where(earlier, 1.0, 0.0).astype(BF16))
    row_k = lax.broadcasted_iota(jnp.int32, (TOP_K, tm), 0)
    ranks = jnp.zeros((TOP_K, tm), F32)
    for k in range(TOP_K):
        rank_k = jnp.sum(jnp.where(row_e == idx[k:k + 1], before, 0.0), axis=0, keepdims=True)
        ranks = jnp.where(row_k == k, rank_k, ranks)
    rank_ref[...] = ranks.astype(jnp.int32)
    run_ref[...] = run_ref[...] + jnp.sum(onehot, axis=1, keepdims=True)
    counts_ref[...] = run_ref[...]


def _rank(idx, tm):
    t = idx.shape[1]
    return pl.pallas_call(
        _rank_kernel,
        out_shape=[jax.ShapeDtypeStruct((TOP_K, t), jnp.int32),
                   jax.ShapeDtypeStruct((N_EXPERTS, 1), F32)],
        grid=(t // tm,),
        in_specs=[pl.BlockSpec((TOP_K, tm), lambda i: (0, i))],
        out_specs=[pl.BlockSpec((TOP_K, tm), lambda i: (0, i)),
                   pl.BlockSpec((N_EXPERTS, 1), lambda i: (0, 0))],
        scratch_shapes=[pltpu.VMEM((N_EXPERTS, 1), F32)],
        compiler_params=_cparams("arbitrary"),
        name="expert_rank",
    )(idx)


def _dispatch_kernel(pstart_ref, count_ref, idx_ref, rank_ref, x_ref, xs_ref, zero_ref, sem, zsem):
    tm = x_ref.shape[0]

    @pl.when(pl.program_id(0) == 0)
    def _():
        zero_ref[...] = jnp.zeros_like(zero_ref)

        def zero_rows(base, size):
            cp = pltpu.make_async_copy(zero_ref.at[pl.ds(0, size)], xs_ref.at[pl.ds(base, size)], zsem)
            cp.start()
            cp.wait()

        def zero_body(e, carry):
            n = count_ref[e]
            start = pstart_ref[e]
            n8 = (n + SUBLANES - 1) // SUBLANES * SUBLANES
            for r in range(SUBLANES - 1):
                @pl.when(n + r < n8)
                def _():
                    zero_rows(start + n + r, 1)
            pad = (EXPERT_ROWS - n8 % EXPERT_ROWS) % EXPERT_ROWS
            base = start + n8
            size = EXPERT_ROWS // 2
            while size >= SUBLANES:
                take = (pad & size) != 0

                @pl.when(take)
                def _():
                    zero_rows(pl.multiple_of(base, SUBLANES), size)

                base = base + jnp.where(take, size, 0)
                size //= 2
            return carry

        lax.fori_loop(0, N_EXPERTS, zero_body, 0)

    def row_copy(t, k):
        dest = pstart_ref[idx_ref[k, t]] + rank_ref[k, t]
        return pltpu.make_async_copy(x_ref.at[pl.ds(t, 1)], xs_ref.at[pl.ds(dest, 1)], sem)

    def start_body(t, carry):
        for k in range(TOP_K):
            row_copy(t, k).start()
        return carry

    lax.fori_loop(0, tm, start_body, 0)

    def wait_body(t, carry):
        for k in range(TOP_K):
            row_copy(t, k).wait()
        return carry

    lax.fori_loop(0, tm, wait_body, 0)


def _dispatch(pstart, counts, idx, rank, x1, n_rows, tm):
    t, d = x1.shape
    smem_tok = pl.BlockSpec((TOP_K, tm), lambda i, *_: (0, i), memory_space=pltpu.SMEM)
    return pl.pallas_call(
        _dispatch_kernel,
        out_shape=jax.ShapeDtypeStruct((n_rows, d), F32),
        grid_spec=pltpu.PrefetchScalarGridSpec(
            num_scalar_prefetch=2,
            grid=(t // tm,),
            in_specs=[smem_tok, smem_tok, pl.BlockSpec((tm, d), lambda i, *_: (i, 0))],
            out_specs=pl.BlockSpec(memory_space=pl.ANY),
            scratch_shapes=[pltpu.VMEM((EXPERT_ROWS // 2, d), F32),
                            pltpu.SemaphoreType.DMA(()), pltpu.SemaphoreType.DMA(())]),
        compiler_params=_cparams("arbitrary"),
        name="moe_dispatch",
    )(pstart, counts, idx, rank, x1)


def _expert_kernel(be_ref, nused_ref, xs_ref, wg_ref, wu_ref, wd_ref, y_ref):
    i = pl.program_id(0)

    @pl.when(i < nused_ref[0])
    def _():
        x = xs_ref[...].astype(BF16)
        g = _dot(x, wg_ref[0].astype(BF16))
        u = _dot(x, wu_ref[0].astype(BF16))
        h = (g * _sigmoid(g) * u).astype(BF16)
        y_ref[...] = _dot(h, wd_ref[0].astype(BF16))

    @pl.when(i >= nused_ref[0])
    def _():
        y_ref[...] = jnp.zeros_like(y_ref)


def _experts(block_expert, n_used, xs, wg, wu, wd):
    n_rows, d = xs.shape
    n_blocks = n_rows // EXPERT_ROWS
    de = wg.shape[2]

    def row_map(i, be, nu):
        return (jnp.minimum(i, nu[0] - 1), 0)

    def w_map(i, be, nu):
        return (be[jnp.minimum(i, nu[0] - 1)], 0, 0)

    return pl.pallas_call(
        _expert_kernel,
        out_shape=jax.ShapeDtypeStruct((n_rows, d), F32),
        grid_spec=pltpu.PrefetchScalarGridSpec(
            num_scalar_prefetch=2,
            grid=(n_blocks,),
            in_specs=[pl.BlockSpec((EXPERT_ROWS, d), row_map),
                      pl.BlockSpec((1, d, de), w_map), pl.BlockSpec((1, d, de), w_map),
                      pl.BlockSpec((1, de, d), w_map)],
            out_specs=pl.BlockSpec((EXPERT_ROWS, d), lambda i, be, nu: (i, 0))),
        compiler_params=_cparams("arbitrary"),
        name="routed_experts",
    )(block_expert, n_used, xs, wg, wu, wd)


def _dense_kernel(alpha, x1_ref, p_ref, wgs_ref, wus_ref, wds_ref, wpg_ref, wpp_ref, base_ref):
    x1 = x1_ref[...]
    xb = x1.astype(BF16)
    g = _dot(xb, wgs_ref[...])
    h = (g * _sigmoid(g) * _dot(xb, wus_ref[...])).astype(BF16)
    shared = _dot(h, wds_ref[...])
    ple = _sigmoid(_dot(xb, wpg_ref[...])) * _dot(p_ref[...].astype(BF16), wpp_ref[...])
    base_ref[...] = alpha * x1 + shared + ple


def _dense(x1, p, wgs, wus, wds, wpg, wpp, alpha, tm):
    t, d = x1.shape
    const2 = lambda i: (0, 0)
    return pl.pallas_call(
        functools.partial(_dense_kernel, alpha),
        out_shape=jax.ShapeDtypeStruct((t, d), F32),
        grid=(t // tm,),
        in_specs=[pl.BlockSpec((tm, d), lambda i: (i, 0)), pl.BlockSpec((tm, p.shape[1]), lambda i: (i, 0)),
                  pl.BlockSpec(wgs.shape, const2), pl.BlockSpec(wus.shape, const2),
                  pl.BlockSpec(wds.shape, const2), pl.BlockSpec(wpg.shape, const2),
                  pl.BlockSpec(wpp.shape, const2)],
        out_specs=pl.BlockSpec((tm, d), lambda i: (i, 0)),
        compiler_params=_cparams("parallel"),
        name="shared_ple",
    )(x1, p, wgs, wus, wds, wpg, wpp)


def _combine_kernel(pstart_ref, idx_ref, rank_ref, gate_ref, base_ref, g2_ref, b2_ref, ys_ref,
                    out_ref, buf_ref, sem):
    tm = base_ref.shape[0]

    def row_copy(t, k):
        src = pstart_ref[idx_ref[k, t]] + rank_ref[k, t]
        return pltpu.make_async_copy(ys_ref.at[pl.ds(src, 1)], buf_ref.at[k, pl.ds(t, 1)], sem)

    def start_body(t, carry):
        for k in range(TOP_K):
            row_copy(t, k).start()
        return carry

    lax.fori_loop(0, tm, start_body, 0)

    def wait_body(t, carry):
        for k in range(TOP_K):
            row_copy(t, k).wait()
        return carry

    lax.fori_loop(0, tm, wait_body, 0)

    gates = gate_ref[...]
    acc = base_ref[...]
    for k in range(TOP_K):
        acc = acc + gates[:, k:k + 1] * buf_ref[k]
    out_ref[...] = _layer_norm(acc, g2_ref[...], b2_ref[...])


def _combine(pstart, idx, rank, gates_tok, base, g2, b2, ys, tm):
    t, d = base.shape
    smem_tok = pl.BlockSpec((TOP_K, tm), lambda i, *_: (0, i), memory_space=pltpu.SMEM)
    const2 = lambda i, *_: (0, 0)
    return pl.pallas_call(
        _combine_kernel,
        out_shape=jax.ShapeDtypeStruct((t, d), F32),
        grid_spec=pltpu.PrefetchScalarGridSpec(
            num_scalar_prefetch=1,
            grid=(t // tm,),
            in_specs=[smem_tok, smem_tok,
                      pl.BlockSpec((tm, TOP_K), lambda i, *_: (i, 0)),
                      pl.BlockSpec((tm, d), lambda i, *_: (i, 0)),
                      pl.BlockSpec(g2.shape, const2), pl.BlockSpec(b2.shape, const2),
                      pl.BlockSpec(memory_space=pl.ANY)],
            out_specs=pl.BlockSpec((tm, d), lambda i, *_: (i, 0)),
            scratch_shapes=[pltpu.VMEM((TOP_K, tm, d), F32), pltpu.SemaphoreType.DMA(())]),
        compiler_params=_cparams("arbitrary"),
        name="moe_combine",
    )(pstart, idx, rank, gates_tok, base, g2, b2, ys)


def _sgu_tables(sgu_w, sgu_b, rows_per_seq):
    reps = CHUNK // rows_per_seq
    tril = jnp.tril(sgu_w[:, :rows_per_seq, :rows_per_seq])
    eye = jnp.eye(reps, dtype=F32)
    mix = jnp.einsum("ab,gts->gatbs", eye, tril).reshape(N_GROUPS_SGU, CHUNK, CHUNK)
    bias = jnp.tile(jnp.repeat(sgu_b[:, :rows_per_seq].T, D_SGU // N_GROUPS_SGU, axis=1), (reps, 1))
    return mix.astype(BF16), bias


def _layer(xp, xs, ck, cv, pp, ps, w, rel_bias, alpha):
    batch, seq, d = xp.shape
    dec_b, dec_t, _ = xs.shape
    w_buf = ck.shape[1]
    n_p, n_s = batch * seq, dec_b * dec_t
    n_tok = n_p + n_s

    scale = jnp.concatenate([jnp.full((D_ATTN,), HEAD_DIM ** -0.5, F32),
                             jnp.ones((w["w_in"].shape[1] - D_ATTN,), F32)])
    w_in = (w["w_in"] * scale).astype(BF16)

    qp, kp, vp, up, gp = _proj(xp.reshape(n_p, d), w_in, 512)
    qs, ks, vs, us, gs = _proj(xs.reshape(n_s, d), w_in, 512)

    b1 = _bias_table(rel_bias, _band_bucket_table(1))
    b4 = _bias_table(rel_bias, _band_bucket_table(4))
    b16 = _bias_table(rel_bias, _band_bucket_table(16))
    attn_p = _attn_prompt(qp, kp, vp, b1, b4, b16, batch, seq)

    cache_t, new_t = _sample_bucket_tables(w_buf, dec_t)
    bc = _bias_table(rel_bias, cache_t.reshape(3 * dec_t, w_buf))
    bn = _bias_table(rel_bias, new_t.reshape(3 * dec_t, LANES))
    bc = bc.reshape(N_HEADS, 3, dec_t, w_buf).transpose(1, 0, 2, 3).reshape(3, N_HEADS * dec_t, w_buf)
    bn = bn.reshape(N_HEADS, 3, dec_t, LANES).transpose(1, 0, 2, 3).reshape(3, N_HEADS * dec_t, LANES)
    attn_s = _attn_sample(qs.reshape(dec_b, dec_t, D_ATTN), ks.reshape(dec_b, dec_t, D_ATTN),
                          vs.reshape(dec_b, dec_t, D_ATTN), ck.reshape(dec_b, w_buf, D_ATTN),
                          cv.reshape(dec_b, w_buf, D_ATTN), bc, bn).reshape(n_s, D_ATTN)

    ln_g, ln_b = w["sgu_ln_g"][None], w["sgu_ln_b"][None]
    mix_p, bias_p = _sgu_tables(w["sgu_w"], w["sgu_b"], CHUNK)
    mix_s, bias_s = _sgu_tables(w["sgu_w"], w["sgu_b"], dec_t)
    (sgu_p,) = _sgu(up, gp, mix_p, bias_p, ln_g, ln_b, False, 4)
    sgu_s, z2_s = _sgu(us, gs, mix_s, bias_s, ln_g, ln_b, True, 4)

    x_all = jnp.concatenate([xp.reshape(n_p, d), xs.reshape(n_s, d)], axis=0)
    attn_all = jnp.concatenate([attn_p, attn_s], axis=0)
    sgu_all = jnp.concatenate([sgu_p, sgu_s], axis=0)
    p_all = jnp.concatenate([pp.reshape(n_p, -1), ps.reshape(n_s, -1)], axis=0)

    x1, idx, gates = _mid(x_all, attn_all, sgu_all, w["w_mix_out"].astype(BF16),
                          w["ln1_g"][None], w["ln1_b"][None],
                          w["w_router"].T.astype(BF16), w["router_bias"][:, None], alpha, 512)

    rank, counts = _rank(idx, 512)
    counts = counts[:, 0].astype(jnp.int32)
    padded = (counts + EXPERT_ROWS - 1) // EXPERT_ROWS * EXPERT_ROWS
    pend = jnp.cumsum(padded)
    pstart = (pend - padded).astype(jnp.int32)
    n_blocks = (n_tok * TOP_K + N_EXPERTS * (EXPERT_ROWS - 1)) // EXPERT_ROWS
    n_used = (pend[-1:] // EXPERT_ROWS).astype(jnp.int32)
    block_expert = jnp.minimum(
        jnp.searchsorted(pend, jnp.arange(n_blocks, dtype=jnp.int32) * EXPERT_ROWS, side="right"),
        N_EXPERTS - 1).astype(jnp.int32)

    x_sorted = _dispatch(pstart, counts, idx, rank, x1, n_blocks * EXPERT_ROWS, 256)
    y_sorted = _experts(block_expert, n_used, x_sorted, w["w_gate_e"], w["w_up_e"], w["w_down_e"])

    base = _dense(x1, p_all, w["w_gate_s"].astype(BF16), w["w_up_s"].astype(BF16),
                  w["w_down_s"].astype(BF16), w["w_ple_gate"].astype(BF16),
                  w["w_ple_proj"].astype(BF16), alpha, 512)

    y = _combine(pstart, idx, rank, gates.T, base, w["ln2_g"][None], w["ln2_b"][None], y_sorted, 128)

    y_p = y[:n_p].reshape(batch, seq, d)
    y_s = y[n_p:].reshape(dec_b, dec_t, d)
    keep = min(MAX_DISTANCE, seq)
    k_rows = kp.reshape(batch, seq, N_HEADS, HEAD_DIM)[:, seq - keep:]
    v_rows = vp.reshape(batch, seq, N_HEADS, HEAD_DIM)[:, seq - keep:]
    return (y_p, y_s, k_rows, v_rows,
            ks.reshape(dec_b, dec_t, N_HEADS, HEAD_DIM), vs.reshape(dec_b, dec_t, N_HEADS, HEAD_DIM),
            z2_s.reshape(dec_b, dec_t, D_SGU))


def kernel(x_prompt, x_sample, cache_k, cache_v, p_prompt, p_sample, w_in, rel_bias, sgu_w, sgu_b, sgu_ln_g, sgu_ln_b, w_mix_out, ln1_g, ln1_b, w_router, router_bias, w_gate_e, w_up_e, w_down_e, w_gate_s, w_up_s, w_down_s, w_ple_gate, w_ple_proj, ln2_g, ln2_b):
    depth = w_in.shape[0]
    alpha = (2 * depth) ** 0.25
    xp, xs = x_prompt, x_sample
    outs = [[] for _ in range(5)]
    for i in range(depth):
        w = {"w_in": w_in[i], "sgu_w": sgu_w[i], "sgu_b": sgu_b[i], "sgu_ln_g": sgu_ln_g[i],
             "sgu_ln_b": sgu_ln_b[i], "w_mix_out": w_mix_out[i], "ln1_g": ln1_g[i], "ln1_b": ln1_b[i],
             "w_router": w_router[i], "router_bias": router_bias[i], "w_gate_e": w_gate_e[i],
             "w_up_e": w_up_e[i], "w_down_e": w_down_e[i], "w_gate_s": w_gate_s[i], "w_up_s": w_up_s[i],
             "w_down_s": w_down_s[i], "w_ple_gate": w_ple_gate[i], "w_ple_proj": w_ple_proj[i],
             "ln2_g": ln2_g[i], "ln2_b": ln2_b[i]}
        xp, xs, kp, vp, ks, vs, zs = _layer(xp, xs, cache_k[i], cache_v[i], p_prompt[i], p_sample[i],
                                            w, rel_bias, alpha)
        for lst, val in zip(outs, (kp, vp, ks, vs, zs)):
            lst.append(val)
    return (xp, xs) + tuple(jnp.stack(lst) for lst in outs)
```

```python
import functools
import math

import numpy as np
import jax
import jax.numpy as jnp
from jax import lax
from jax.experimental import pallas as pl
from jax.experimental.pallas import tpu as pltpu
from jax.experimental.pallas import tpu_sc as plsc

F32 = jnp.float32
BF16 = jnp.bfloat16
NEG_INF = float("-inf")

N_HEADS = 8
HEAD_DIM = 64
D_ATTN = N_HEADS * HEAD_DIM
PATTERNS = ((128, 1), (512, 4), (2048, 16))
BAND = 128
N_BUCKETS = 32
MAX_DISTANCE = 2048
N_GROUPS_SGU = 8
D_SGU = 512
CHUNK = 128
N_EXPERTS = 256
TOP_K = 8
N_EXPERT_GROUPS = 8
TOPK_GROUPS = 4
ROUTED_SCALE = 2.5
LN_EPS = 1e-5
EXPERT_ROWS = 256

LANES = 128
SC_CORES = 2
SC_SUBCORES = 16
SC_CHUNK = 128
SC_GATHER_ROWS = 32
VMEM_LIMIT = 56 * 1024 * 1024


def _cparams(*sem):
    return pltpu.CompilerParams(dimension_semantics=sem, vmem_limit_bytes=VMEM_LIMIT)


def _layer_norm(x, g, b):
    mu = jnp.mean(x, axis=-1, keepdims=True)
    xc = x - mu
    var = jnp.mean(xc * xc, axis=-1, keepdims=True)
    return xc * lax.rsqrt(var + LN_EPS) * g + b


def _sigmoid(x):
    return 1.0 / (1.0 + jnp.exp(-x))


def _gelu(x):
    return 0.5 * x * (1.0 + lax.erf(x * math.sqrt(0.5)))


def _pack_bf16_pairs(xb):
    n = xb.shape[1] // 2
    bits = lax.bitcast_convert_type(xb.astype(F32), jnp.uint32)
    return (bits[:, :n] >> 16) | (bits[:, n:] & jnp.uint32(0xFFFF0000))


def _unpack_bf16_pairs(p):
    lo = lax.bitcast_convert_type(p << 16, F32).astype(BF16)
    hi = lax.bitcast_convert_type(p & jnp.uint32(0xFFFF0000), F32).astype(BF16)
    return lo, hi


def _dot(a, b):
    return jnp.dot(a, b, preferred_element_type=F32)


def _dot_nt(a, b):
    return lax.dot_general(a, b, (((1,), (1,)), ((), ())), preferred_element_type=F32)


def _t5_bucket_np(dist):
    max_exact = N_BUCKETS // 2
    df = np.maximum(dist, max_exact).astype(np.float32)
    large = max_exact + (np.log(df / np.float32(max_exact)) / np.float32(math.log(MAX_DISTANCE / max_exact))
                         * np.float32(N_BUCKETS - max_exact)).astype(np.int32)
    return np.where(dist < max_exact, dist, np.minimum(large, N_BUCKETS - 1)).astype(np.int32)


def _band_bucket_table(dilation):
    qi = np.arange(BAND)[:, None]
    ki = np.arange(2 * BAND)[None, :]
    dsub = qi + BAND - ki
    valid = (dsub >= 0) & (dsub <= BAND)
    return np.where(valid, _t5_bucket_np(np.clip(dsub, 0, BAND) * dilation), -1).astype(np.int32)


def _sample_bucket_tables(w_buf, t_len):
    t = np.arange(t_len)[:, None]
    c = np.arange(w_buf)[None, :]
    cn = np.arange(LANES)[None, :]
    cache, new = [], []
    for window, dilation in PATTERNS:
        d = w_buf + t - c
        ok = (d >= 0) & (d % dilation == 0) & (d <= window)
        cache.append(np.where(ok, _t5_bucket_np(np.maximum(d, 0)), -1))
        dn = t - cn
        okn = (dn >= 0) & (dn % dilation == 0) & (dn <= window) & (cn < t_len)
        new.append(np.where(okn, _t5_bucket_np(np.maximum(dn, 0)), -1))
    return np.stack(cache).astype(np.int32), np.stack(new).astype(np.int32)


def _bias_kernel(rb_ref, bucket_ref, out_ref):
    bucket = bucket_ref[...]
    for h in range(N_HEADS):
        acc = jnp.full(bucket.shape, NEG_INF, F32)
        for b in range(N_BUCKETS):
            acc = jnp.where(bucket == b, rb_ref[b, h], acc)
        out_ref[h] = acc


def _bias_table(rel_bias, bucket_np):
    r, c = bucket_np.shape
    return pl.pallas_call(
        _bias_kernel,
        out_shape=jax.ShapeDtypeStruct((N_HEADS, r, c), F32),
        in_specs=[pl.BlockSpec(memory_space=pltpu.SMEM), pl.BlockSpec(memory_space=pltpu.VMEM)],
        out_specs=pl.BlockSpec(memory_space=pltpu.VMEM),
        name="bias_table",
    )(rel_bias, jnp.asarray(bucket_np))


def _proj_kernel(x_ref, w_ref, q_ref, k_ref, v_ref, u_ref, g_ref):
    x = x_ref[...].astype(BF16)
    col = 0
    for o in (q_ref, k_ref, v_ref, u_ref, g_ref):
        n = o.shape[1]
        o[...] = _dot(x, w_ref[:, col:col + n])
        col += n


def _proj(x, w, tm):
    m, d = x.shape
    n_out = (D_ATTN, D_ATTN, D_ATTN, D_SGU, D_SGU)
    return pl.pallas_call(
        _proj_kernel,
        out_shape=[jax.ShapeDtypeStruct((m, n), F32) for n in n_out],
        grid=(m // tm,),
        in_specs=[pl.BlockSpec((tm, d), lambda i: (i, 0)), pl.BlockSpec(w.shape, lambda i: (0, 0))],
        out_specs=[pl.BlockSpec((tm, n), lambda i: (i, 0)) for n in n_out],
        compiler_params=_cparams("parallel"),
        name="in_proj",
    )(x, w)


def _band_attn(q, k, v, bias_e, bias_o, even):
    kb = k.astype(BF16)
    vb = v.astype(BF16)
    parts = []
    for sel, bias in ((even, bias_e), (jnp.logical_not(even), bias_o)):
        qm = jnp.where(sel, q, 0.0).astype(BF16)
        s = _dot_nt(qm, kb) + bias
        m = jnp.max(s, axis=-1, keepdims=True)
        p = jnp.exp(s - m)
        l = jnp.sum(p, axis=-1, keepdims=True)
        parts.append((m, l, _dot(p.astype(BF16), vb)))
    (m0, l0, a0), (m1, l1, a1) = parts
    return jnp.where(even, m0, m1), jnp.where(even, l0, l1), jnp.where(even, a0, a1)


def _attn_prompt_kernel(q_ref, k_ref, v_ref, b1_ref, b4_ref, b16_ref, o_ref,
                        m1, l1, a1, m4, l4, a4, m16, l16, a16):
    seq = q_ref.shape[0]
    even = lax.broadcasted_iota(jnp.int32, (BAND, LANES), 1) < HEAD_DIM

    m, l, a = _band_attn(q_ref[0:BAND], k_ref[0:BAND], v_ref[0:BAND],
                         b1_ref[0, :, BAND:], b1_ref[1, :, BAND:], even)
    m1[0:BAND], l1[0:BAND], a1[0:BAND] = m, l, a

    def d1_body(i, carry):
        r0 = pl.multiple_of(i * BAND, BAND)
        m, l, a = _band_attn(q_ref[pl.ds(r0, BAND)], k_ref[pl.ds(r0 - BAND, 2 * BAND)],
                             v_ref[pl.ds(r0 - BAND, 2 * BAND)], b1_ref[0], b1_ref[1], even)
        m1[pl.ds(r0, BAND)], l1[pl.ds(r0, BAND)], a1[pl.ds(r0, BAND)] = m, l, a
        return carry

    lax.fori_loop(1, seq // BAND, d1_body, 0)

    n4 = seq // 4

    def d4_body(r, carry):
        q4 = q_ref[pl.ds(r, n4, stride=4), :]
        k4 = k_ref[pl.ds(r, n4, stride=4), :]
        v4 = v_ref[pl.ds(r, n4, stride=4), :]
        for i in range(n4 // BAND):
            lo = max(i - 1, 0) * BAND
            hi = (i + 1) * BAND
            col = 0 if i > 0 else BAND
            m, l, a = _band_attn(q4[i * BAND:hi], k4[lo:hi], v4[lo:hi],
                                 b4_ref[0, :, col:], b4_ref[1, :, col:], even)
            rows = pl.ds(r + 4 * BAND * i, BAND, stride=4)
            m4[rows, :], l4[rows, :], a4[rows, :] = m, l, a
        return carry

    lax.fori_loop(0, 4, d4_body, 0)

    def d16_body(r, carry):
        rows = pl.ds(r, seq // 16, stride=16)
        m, l, a = _band_attn(q_ref[rows, :], k_ref[rows, :], v_ref[rows, :],
                             b16_ref[0, :, BAND:], b16_ref[1, :, BAND:], even)
        m16[rows, :], l16[rows, :], a16[rows, :] = m, l, a
        return carry

    lax.fori_loop(0, 16, d16_body, 0)

    def merge_body(i, carry):
        rows = pl.ds(pl.multiple_of(i * BAND, BAND), BAND)
        ma, mb, mc = m1[rows], m4[rows], m16[rows]
        mx = jnp.maximum(jnp.maximum(ma, mb), mc)
        wa, wb, wc = jnp.exp(ma - mx), jnp.exp(mb - mx), jnp.exp(mc - mx)
        num = wa * a1[rows] + wb * a4[rows] + wc * a16[rows]
        den = wa * l1[rows] + wb * l4[rows] + wc * l16[rows]
        o_ref[rows] = (num / den).astype(o_ref.dtype)
        return carry

    lax.fori_loop(0, seq // BAND, merge_body, 0)


def _attn_prompt(q, k, v, b1, b4, b16, batch, seq):
    blk = pl.BlockSpec((seq, LANES), lambda b, j: (b, j))
    bias_spec = pl.BlockSpec((2, BAND, 2 * BAND), lambda b, j: (j, 0, 0))
    return pl.pallas_call(
        _attn_prompt_kernel,
        out_shape=jax.ShapeDtypeStruct(q.shape, BF16),
        grid=(batch, D_ATTN // LANES),
        in_specs=[blk, blk, blk, bias_spec, bias_spec, bias_spec],
        out_specs=blk,
        scratch_shapes=[pltpu.VMEM((seq, LANES), F32) for _ in range(9)],
        compiler_params=_cparams("parallel", "parallel"),
        name="attn_prompt",
    )(q, k, v, b1, b4, b16)


def _attn_sample_kernel(q_ref, kn_ref, vn_ref, ck_ref, cv_ref, bc_ref, bn_ref, o_ref):
    t_len = q_ref.shape[1]
    rows = N_HEADS * t_len
    q = q_ref[0]
    head_of_row = lax.broadcasted_iota(jnp.int32, (rows, D_ATTN), 0) // t_len
    head_of_lane = lax.broadcasted_iota(jnp.int32, (rows, D_ATTN), 1) // HEAD_DIM
    own = head_of_row == head_of_lane
    qrows = jnp.where(own, jnp.concatenate([q] * N_HEADS, axis=0), 0.0).astype(BF16)
    pad = jnp.zeros((LANES - t_len, D_ATTN), F32)
    kn = jnp.concatenate([kn_ref[0], pad], axis=0).astype(BF16)
    vn = jnp.concatenate([vn_ref[0], pad], axis=0).astype(BF16)
    kc = ck_ref[0].astype(BF16)
    vc = cv_ref[0].astype(BF16)
    s_c = _dot_nt(qrows, kc)
    s_n = _dot_nt(qrows, kn)
    ms, ls, accs = [], [], []
    for p in range(len(PATTERNS)):
        sc = s_c + bc_ref[p]
        sn = s_n + bn_ref[p]
        m = jnp.maximum(jnp.max(sc, axis=-1, keepdims=True), jnp.max(sn, axis=-1, keepdims=True))
        pc = jnp.exp(sc - m)
        pn = jnp.exp(sn - m)
        ls.append(jnp.sum(pc, axis=-1, keepdims=True) + jnp.sum(pn, axis=-1, keepdims=True))
        accs.append(_dot(pc.astype(BF16), vc) + _dot(pn.astype(BF16), vn))
        ms.append(m)
    mx = jnp.maximum(jnp.maximum(ms[0], ms[1]), ms[2])
    ws = [jnp.exp(m - mx) for m in ms]
    num = ws[0] * accs[0] + ws[1] * accs[1] + ws[2] * accs[2]
    den = ws[0] * ls[0] + ws[1] * ls[1] + ws[2] * ls[2]
    full = jnp.where(own, num / den, 0.0)
    out = full[0:t_len]
    for h in range(1, N_HEADS):
        out = out + full[h * t_len:(h + 1) * t_len]
    o_ref[0] = out.astype(o_ref.dtype)


def _attn_sample(q, kn, vn, ck, cv, bc, bn):
    b, t_len, _ = q.shape
    w_buf = ck.shape[1]
    rows = N_HEADS * t_len
    new_spec = pl.BlockSpec((1, t_len, D_ATTN), lambda i: (i, 0, 0))
    cache_spec = pl.BlockSpec((1, w_buf, D_ATTN), lambda i: (i, 0, 0))
    return pl.pallas_call(
        _attn_sample_kernel,
        out_shape=jax.ShapeDtypeStruct(q.shape, BF16),
        grid=(b,),
        in_specs=[new_spec, new_spec, new_spec, cache_spec, cache_spec,
                  pl.BlockSpec((3, rows, w_buf), lambda i: (0, 0, 0)),
                  pl.BlockSpec((3, rows, LANES), lambda i: (0, 0, 0))],
        out_specs=new_spec,
        compiler_params=_cparams("parallel"),
        name="attn_sample",
    )(q, kn, vn, ck, cv, bc, bn)


def _sgu_kernel(u_ref, g_ref, mix_ref, bias_ref, lng_ref, lnb_ref, sgu_ref, *z2_out):
    n_chunks = u_ref.shape[0] // CHUNK
    group_of_lane = lax.broadcasted_iota(jnp.int32, (CHUNK, D_SGU), 1) // (D_SGU // N_GROUPS_SGU)
    for c in range(n_chunks):
        rows = slice(c * CHUNK, (c + 1) * CHUNK)
        z1 = _gelu(u_ref[rows])
        z2 = _layer_norm(_gelu(g_ref[rows]), lng_ref[...], lnb_ref[...])
        if z2_out:
            z2_out[0][rows] = z2
        mixed = bias_ref[...]
        for g in range(N_GROUPS_SGU):
            mixed = mixed + _dot(mix_ref[g], jnp.where(group_of_lane == g, z2, 0.0).astype(BF16))
        sgu_ref[rows] = (z1 * mixed).astype(sgu_ref.dtype)


def _sgu(u, g, mix, bias, ln_g, ln_b, want_z2, chunks_per_step):
    m = u.shape[0]
    tm = CHUNK * chunks_per_step
    row_spec = pl.BlockSpec((tm, D_SGU), lambda i: (i, 0))
    const2 = lambda i: (0, 0)
    out_shape = [jax.ShapeDtypeStruct((m, D_SGU), BF16)]
    out_specs = [row_spec]
    if want_z2:
        out_shape.append(jax.ShapeDtypeStruct((m, D_SGU), F32))
        out_specs.append(row_spec)
    return pl.pallas_call(
        _sgu_kernel,
        out_shape=out_shape,
        grid=(m // tm,),
        in_specs=[row_spec, row_spec,
                  pl.BlockSpec(mix.shape, lambda i: (0, 0, 0)),
                  pl.BlockSpec(bias.shape, const2),
                  pl.BlockSpec(ln_g.shape, const2), pl.BlockSpec(ln_b.shape, const2)],
        out_specs=out_specs,
        compiler_params=_cparams("parallel"),
        name="sgu",
    )(u, g, mix, bias, ln_g, ln_b)


def _top1_rows(x, row):
    m = jnp.max(x, axis=0, keepdims=True)
    i = jnp.min(jnp.where(x == m, row, x.shape[0]), axis=0, keepdims=True)
    return m, i


def _mid_kernel(alpha, x_ref, attn_ref, sgu_ref, wmix_ref, g1_ref, b1_ref, wrt_ref, rb_ref,
                x1_ref, x1p_ref, idx_ref, gate_ref):
    y = (_dot(attn_ref[...], wmix_ref[0:D_ATTN]) + _dot(sgu_ref[...], wmix_ref[D_ATTN:])
         + alpha * x_ref[...])
    x1 = _layer_norm(y, g1_ref[...], b1_ref[...])
    x1_ref[...] = x1
    tm = x1.shape[0]
    x1b = x1.astype(BF16)
    x1p_ref[...] = _pack_bf16_pairs(x1b)
    scores = _sigmoid(_dot_nt(wrt_ref[...], x1b))
    biased = scores + rb_ref[...]
    per_group = N_EXPERTS // N_EXPERT_GROUPS
    row_g = lax.broadcasted_iota(jnp.int32, (per_group, tm), 0)
    row_8 = lax.broadcasted_iota(jnp.int32, (N_EXPERT_GROUPS, tm), 0)
    row_e = lax.broadcasted_iota(jnp.int32, (N_EXPERTS, tm), 0)
    cur = jnp.full((N_EXPERT_GROUPS, tm), NEG_INF, F32)
    for g in range(N_EXPERT_GROUPS):
        blk = biased[g * per_group:(g + 1) * per_group]
        m_a, i_a = _top1_rows(blk, row_g)
        m_b = jnp.max(jnp.where(row_g == i_a, NEG_INF, blk), axis=0, keepdims=True)
        cur = jnp.where(row_8 == g, m_a + m_b, cur)
    chosen = jnp.zeros((N_EXPERT_GROUPS, tm), F32)
    for _ in range(TOPK_GROUPS):
        _, i_g = _top1_rows(cur, row_8)
        sel = row_8 == i_g
        chosen = jnp.where(sel, 1.0, chosen)
        cur = jnp.where(sel, NEG_INF, cur)
    group_of_row = row_e // per_group
    emask = jnp.zeros((N_EXPERTS, tm), F32)
    for g in range(N_EXPERT_GROUPS):
        emask = jnp.where(group_of_row == g, chosen[g:g + 1], emask)
    cur = jnp.where(emask > 0.0, biased, NEG_INF)
    idx = jnp.zeros((TOP_K, tm), jnp.int32)
    gates = jnp.zeros((TOP_K, tm), F32)
    for k in range(TOP_K):
        _, i_e = _top1_rows(cur, row_e)
        sel = row_e == i_e
        gate_k = jnp.sum(jnp.where(sel, scores, 0.0), axis=0, keepdims=True)
        cur = jnp.where(sel, NEG_INF, cur)
        idx = jnp.where(row_8 == k, i_e, idx)
        gates = jnp.where(row_8 == k, gate_k, gates)
    idx_ref[...] = idx
    gate_ref[...] = gates / jnp.sum(gates, axis=0, keepdims=True) * ROUTED_SCALE


def _mid(x, attn, sgu, wmix, g1, b1, wrt, rb, alpha, tm):
    t, d = x.shape
    const2 = lambda i: (0, 0)
    return pl.pallas_call(
        functools.partial(_mid_kernel, alpha),
        out_shape=[jax.ShapeDtypeStruct((t, d), F32),
                   jax.ShapeDtypeStruct((t, d // 2), jnp.uint32),
                   jax.ShapeDtypeStruct((TOP_K, t), jnp.int32),
                   jax.ShapeDtypeStruct((TOP_K, t), F32)],
        grid=(t // tm,),
        in_specs=[pl.BlockSpec((tm, d), lambda i: (i, 0)),
                  pl.BlockSpec((tm, D_ATTN), lambda i: (i, 0)),
                  pl.BlockSpec((tm, D_SGU), lambda i: (i, 0)),
                  pl.BlockSpec(wmix.shape, const2), pl.BlockSpec(g1.shape, const2),
                  pl.BlockSpec(b1.shape, const2), pl.BlockSpec(wrt.shape, const2),
                  pl.BlockSpec(rb.shape, const2)],
        out_specs=[pl.BlockSpec((tm, d), lambda i: (i, 0)),
                   pl.BlockSpec((tm, d // 2), lambda i: (i, 0)),
                   pl.BlockSpec((TOP_K, tm), lambda i: (0, i)),
                   pl.BlockSpec((TOP_K, tm), lambda i: (0, i))],
        compiler_params=_cparams("parallel"),
        name="mix_ln_router",
    )(x, attn, sgu, wmix, g1, b1, wrt, rb)


def _rank_kernel(idx_ref, rank_ref, counts_ref, run_ref):
    @pl.when(pl.program_id(0) == 0)
    def _():
        run_ref[...] = jnp.zeros_like(run_ref)

    tm = idx_ref.shape[1]
    row_e = lax.broadcasted_iota(jnp.int32, (N_EXPERTS, tm), 0)
    idx = idx_ref[...]
    onehot = jnp.zeros((N_EXPERTS, tm), F32)
    for k in range(TOP_K):
        onehot = onehot + jnp.where(row_e == idx[k:k + 1], 1.0, 0.0)
    earlier = (lax.broadcasted_iota(jnp.int32, (tm, tm), 0)
               < lax.broadcasted_iota(jnp.int32, (tm, tm), 1))
    before = run_ref[...] + _dot(onehot.astype(BF16), jnp.where(earlier, 1.0, 0.0).astype(BF16))
    row_k = lax.broadcasted_iota(jnp.int32, (TOP_K, tm), 0)
    ranks = jnp.zeros((TOP_K, tm), F32)
    for k in range(TOP_K):
        rank_k = jnp.sum(jnp.where(row_e == idx[k:k + 1], before, 0.0), axis=0, keepdims=True)
        ranks = jnp.where(row_k == k, rank_k, ranks)
    rank_ref[...] = ranks.astype(jnp.int32)
    run_ref[...] = run_ref[...] + jnp.sum(onehot, axis=1, keepdims=True)
    counts_ref[...] = run_ref[...]


def _rank(idx, tm):
    t = idx.shape[1]
    return pl.pallas_call(
        _rank_kernel,
        out_shape=[jax.ShapeDtypeStruct((TOP_K, t), jnp.int32),
                   jax.ShapeDtypeStruct((N_EXPERTS, 1), F32)],
        grid=(t // tm,),
        in_specs=[pl.BlockSpec((TOP_K, tm), lambda i: (0, i))],
        out_specs=[pl.BlockSpec((TOP_K, tm), lambda i: (0, i)),
                   pl.BlockSpec((N_EXPERTS, 1), lambda i: (0, 0))],
        scratch_shapes=[pltpu.VMEM((N_EXPERTS, 1), F32)],
        compiler_params=_cparams("arbitrary"),
        name="expert_rank",
    )(idx)


def _dest_kernel(idx_ref, rank_ref, pstart_ref, dest_ref):
    tm = idx_ref.shape[1]
    row_e = lax.broadcasted_iota(jnp.int32, (N_EXPERTS, tm), 0)
    row_k = lax.broadcasted_iota(jnp.int32, (TOP_K, tm), 0)
    idx = idx_ref[...]
    start = jnp.zeros((TOP_K, tm), F32)
    for k in range(TOP_K):
        start_k = jnp.sum(jnp.where(row_e == idx[k:k + 1], pstart_ref[...], 0.0), axis=0, keepdims=True)
        start = jnp.where(row_k == k, start_k, start)
    dest = start.astype(jnp.int32) + rank_ref[...]
    for c in range(tm // SC_CHUNK):
        dest_ref[c] = dest[:, c * SC_CHUNK:(c + 1) * SC_CHUNK]


def _dest(idx, rank, pstart_col, tm):
    t = idx.shape[1]
    tok_spec = pl.BlockSpec((TOP_K, tm), lambda i: (0, i))
    return pl.pallas_call(
        _dest_kernel,
        out_shape=jax.ShapeDtypeStruct((t // SC_CHUNK, TOP_K, SC_CHUNK), jnp.int32),
        grid=(t // tm,),
        in_specs=[tok_spec, tok_spec, pl.BlockSpec(pstart_col.shape, lambda i: (0, 0))],
        out_specs=pl.BlockSpec((tm // SC_CHUNK, TOP_K, SC_CHUNK), lambda i: (i, 0, 0)),
        compiler_params=_cparams("parallel"),
        name="moe_dest",
    )(idx, rank, pstart_col)


def _sc_mesh():
    return plsc.VectorSubcoreMesh(core_axis_name="c", subcore_axis_name="s",
                                  num_cores=SC_CORES, num_subcores=SC_SUBCORES)


def _sc_chunks(n_chunks):
    workers = SC_CORES * SC_SUBCORES
    wid = lax.axis_index("s") * SC_CORES + lax.axis_index("c")
    return wid, workers, (n_chunks - wid + workers - 1) // workers


def _sc_dispatch(x_rows, dest, n_rows):
    n_chunks = dest.shape[0]
    width = x_rows.shape[1]

    @functools.partial(
        pl.kernel, mesh=_sc_mesh(),
        out_type=jax.ShapeDtypeStruct((n_rows, width), x_rows.dtype),
        scratch_types=[pltpu.VMEM((TOP_K, SC_CHUNK), jnp.int32), pltpu.VMEM((SC_CHUNK, width), x_rows.dtype),
                       pltpu.SemaphoreType.DMA],
    )
    def scatter(x_hbm, dest_hbm, out_hbm, dest_v, rows_v, sem):
        wid, workers, n_own = _sc_chunks(n_chunks)

        @pl.loop(0, n_own)
        def _(j):
            ch = wid + j * workers
            pltpu.sync_copy(dest_hbm.at[ch], dest_v)
            pltpu.sync_copy(x_hbm.at[pl.ds(ch * SC_CHUNK, SC_CHUNK)], rows_v)
            copies = [pltpu.async_copy(rows_v, out_hbm.at[dest_v.at[k]], sem) for k in range(TOP_K)]
            for cp in copies:
                cp.wait()

    return scatter(x_rows, dest)


def _expert_kernel(be_ref, nused_ref, valid_ref, xs_ref, wg_ref, wu_ref, wd_ref, y_ref):
    i = pl.program_id(0)

    @pl.when(i < nused_ref[0])
    def _():
        half = xs_ref.shape[1]
        row = lax.broadcasted_iota(jnp.int32, xs_ref.shape, 0)
        lo, hi = _unpack_bf16_pairs(jnp.where(row < valid_ref[i], xs_ref[...], jnp.uint32(0)))
        g = _dot(lo, wg_ref[0, :half].astype(BF16)) + _dot(hi, wg_ref[0, half:].astype(BF16))
        u = _dot(lo, wu_ref[0, :half].astype(BF16)) + _dot(hi, wu_ref[0, half:].astype(BF16))
        h = (g * _sigmoid(g) * u).astype(BF16)
        y_ref[...] = _dot(h, wd_ref[0].astype(BF16))


def _experts(block_expert, n_used, valid_rows, xs, wg, wu, wd):
    n_rows = xs.shape[0]
    n_blocks = n_rows // EXPERT_ROWS
    d, de = wg.shape[1], wg.shape[2]

    def row_map(i, be, nu, va):
        return (jnp.minimum(i, nu[0] - 1), 0)

    def w_map(i, be, nu, va):
        return (be[jnp.minimum(i, nu[0] - 1)], 0, 0)

    return pl.pallas_call(
        _expert_kernel,
        out_shape=jax.ShapeDtypeStruct((n_rows, d), F32),
        grid_spec=pltpu.PrefetchScalarGridSpec(
            num_scalar_prefetch=3,
            grid=(n_blocks,),
            in_specs=[pl.BlockSpec((EXPERT_ROWS, xs.shape[1]), row_map),
                      pl.BlockSpec((1, d, de), w_map), pl.BlockSpec((1, d, de), w_map),
                      pl.BlockSpec((1, de, d), w_map)],
            out_specs=pl.BlockSpec((EXPERT_ROWS, d), row_map)),
        compiler_params=_cparams("arbitrary"),
        name="routed_experts",
    )(block_expert, n_used, valid_rows, xs, wg, wu, wd)


def _dense_kernel(alpha, x1_ref, p_ref, wgs_ref, wus_ref, wds_ref, wpg_ref, wpp_ref, base_ref):
    x1 = x1_ref[...]
    xb = x1.astype(BF16)
    g = _dot(xb, wgs_ref[...])
    h = (g * _sigmoid(g) * _dot(xb, wus_ref[...])).astype(BF16)
    shared = _dot(h, wds_ref[...])
    ple = _sigmoid(_dot(xb, wpg_ref[...])) * _dot(p_ref[...].astype(BF16), wpp_ref[...])
    base_ref[...] = alpha * x1 + shared + ple


def _dense(x1, p, wgs, wus, wds, wpg, wpp, alpha, tm):
    t, d = x1.shape
    const2 = lambda i: (0, 0)
    return pl.pallas_call(
        functools.partial(_dense_kernel, alpha),
        out_shape=jax.ShapeDtypeStruct((t, d), F32),
        grid=(t // tm,),
        in_specs=[pl.BlockSpec((tm, d), lambda i: (i, 0)), pl.BlockSpec((tm, p.shape[1]), lambda i: (i, 0)),
                  pl.BlockSpec(wgs.shape, const2), pl.BlockSpec(wus.shape, const2),
                  pl.BlockSpec(wds.shape, const2), pl.BlockSpec(wpg.shape, const2),
                  pl.BlockSpec(wpp.shape, const2)],
        out_specs=pl.BlockSpec((tm, d), lambda i: (i, 0)),
        compiler_params=_cparams("parallel"),
        name="shared_ple",
    )(x1, p, wgs, wus, wds, wpg, wpp)


def _sc_gather(y_rows, dest, n_tok):
    n_chunks = dest.shape[0]
    width = y_rows.shape[1]
    items = [(k, h) for k in range(TOP_K) for h in range(SC_CHUNK // SC_GATHER_ROWS)]

    @functools.partial(
        pl.kernel, mesh=_sc_mesh(),
        out_type=jax.ShapeDtypeStruct((TOP_K, n_tok, width), y_rows.dtype),
        scratch_types=[pltpu.VMEM((TOP_K, SC_CHUNK), jnp.int32),
                       pltpu.VMEM((2, SC_GATHER_ROWS, width), y_rows.dtype),
                       pltpu.SemaphoreType.DMA, pltpu.SemaphoreType.DMA],
    )
    def gather(y_hbm, dest_hbm, out_hbm, dest_v, rows_v, gsem, wsem):
        wid, workers, n_own = _sc_chunks(n_chunks)

        @pl.loop(0, n_own)
        def _(j):
            ch = wid + j * workers
            pltpu.sync_copy(dest_hbm.at[ch], dest_v)

            def fetch(n):
                k, h = items[n]
                return pltpu.async_copy(y_hbm.at[dest_v.at[k, pl.ds(h * SC_GATHER_ROWS, SC_GATHER_ROWS)]],
                                        rows_v.at[n % 2], gsem)

            def write(n):
                k, h = items[n]
                rows = pl.ds(ch * SC_CHUNK + h * SC_GATHER_ROWS, SC_GATHER_ROWS)
                return pltpu.async_copy(rows_v.at[n % 2], out_hbm.at[k, rows], wsem)

            pending_fetch = fetch(0)
            pending_write = None
            for n in range(len(items)):
                pending_fetch.wait()
                if pending_write is not None:
                    pending_write.wait()
                if n + 1 < len(items):
                    pending_fetch = fetch(n + 1)
                pending_write = write(n)
            pending_write.wait()

    return gather(y_rows, dest)


def _combine_kernel(gate_ref, base_ref, yk_ref, g2_ref, b2_ref, out_ref):
    gates = gate_ref[...]
    acc = base_ref[...]
    for k in range(TOP_K):
        acc = acc + gates[:, k:k + 1] * yk_ref[k]
    out_ref[...] = _layer_norm(acc, g2_ref[...], b2_ref[...])


def _combine(gates_tok, base, yk, g2, b2, tm):
    t, d = base.shape
    const2 = lambda i: (0, 0)
    return pl.pallas_call(
        _combine_kernel,
        out_shape=jax.ShapeDtypeStruct((t, d), F32),
        grid=(t // tm,),
        in_specs=[pl.BlockSpec((tm, TOP_K), lambda i: (i, 0)),
                  pl.BlockSpec((tm, d), lambda i: (i, 0)),
                  pl.BlockSpec((TOP_K, tm, d), lambda i: (0, i, 0)),
                  pl.BlockSpec(g2.shape, const2), pl.BlockSpec(b2.shape, const2)],
        out_specs=pl.BlockSpec((tm, d), lambda i: (i, 0)),
        compiler_params=_cparams("parallel"),
        name="moe_combine",
    )(gates_tok, base, yk, g2, b2)


def _sgu_tables(sgu_w, sgu_b, rows_per_seq):
    reps = CHUNK // rows_per_seq
    tril = jnp.tril(sgu_w[:, :rows_per_seq, :rows_per_seq])
    eye = jnp.eye(reps, dtype=F32)
    mix = jnp.einsum("ab,gts->gatbs", eye, tril).reshape(N_GROUPS_SGU, CHUNK, CHUNK)
    bias = jnp.tile(jnp.repeat(sgu_b[:, :rows_per_seq].T, D_SGU // N_GROUPS_SGU, axis=1), (reps, 1))
    return mix.astype(BF16), bias


def _layer(xp, xs, ck, cv, pp, ps, w, rel_bias, alpha):
    batch, seq, d = xp.shape
    dec_b, dec_t, _ = xs.shape
    w_buf = ck.shape[1]
    n_p, n_s = batch * seq, dec_b * dec_t
    n_tok = n_p + n_s

    scale = jnp.concatenate([jnp.full((D_ATTN,), HEAD_DIM ** -0.5, F32),
                             jnp.ones((w["w_in"].shape[1] - D_ATTN,), F32)])
    w_in = (w["w_in"] * scale).astype(BF16)

    qp, kp, vp, up, gp = _proj(xp.reshape(n_p, d), w_in, 512)
    qs, ks, vs, us, gs = _proj(xs.reshape(n_s, d), w_in, 512)

    b1 = _bias_table(rel_bias, _band_bucket_table(1))
    b4 = _bias_table(rel_bias, _band_bucket_table(4))
    b16 = _bias_table(rel_bias, _band_bucket_table(16))
    attn_p = _attn_prompt(qp, kp, vp, b1, b4, b16, batch, seq)

    cache_t, new_t = _sample_bucket_tables(w_buf, dec_t)
    bc = _bias_table(rel_bias, cache_t.reshape(3 * dec_t, w_buf))
    bn = _bias_table(rel_bias, new_t.reshape(3 * dec_t, LANES))
    bc = bc.reshape(N_HEADS, 3, dec_t, w_buf).transpose(1, 0, 2, 3).reshape(3, N_HEADS * dec_t, w_buf)
    bn = bn.reshape(N_HEADS, 3, dec_t, LANES).transpose(1, 0, 2, 3).reshape(3, N_HEADS * dec_t, LANES)
    attn_s = _attn_sample(qs.reshape(dec_b, dec_t, D_ATTN), ks.reshape(dec_b, dec_t, D_ATTN),
                          vs.reshape(dec_b, dec_t, D_ATTN), ck.reshape(dec_b, w_buf, D_ATTN),
                          cv.reshape(dec_b, w_buf, D_ATTN), bc, bn).reshape(n_s, D_ATTN)

    ln_g, ln_b = w["sgu_ln_g"][None], w["sgu_ln_b"][None]
    mix_p, bias_p = _sgu_tables(w["sgu_w"], w["sgu_b"], CHUNK)
    mix_s, bias_s = _sgu_tables(w["sgu_w"], w["sgu_b"], dec_t)
    (sgu_p,) = _sgu(up, gp, mix_p, bias_p, ln_g, ln_b, False, 4)
    sgu_s, z2_s = _sgu(us, gs, mix_s, bias_s, ln_g, ln_b, True, 4)

    x_all = jnp.concatenate([xp.reshape(n_p, d), xs.reshape(n_s, d)], axis=0)
    attn_all = jnp.concatenate([attn_p, attn_s], axis=0)
    sgu_all = jnp.concatenate([sgu_p, sgu_s], axis=0)
    p_all = jnp.concatenate([pp.reshape(n_p, -1), ps.reshape(n_s, -1)], axis=0)

    x1, x1_packed, idx, gates = _mid(x_all, attn_all, sgu_all, w["w_mix_out"].astype(BF16),
                                     w["ln1_g"][None], w["ln1_b"][None],
                                     w["w_router"].T.astype(BF16), w["router_bias"][:, None], alpha, 512)

    rank, counts = _rank(idx, 512)
    counts = counts[:, 0].astype(jnp.int32)
    padded = (counts + EXPERT_ROWS - 1) // EXPERT_ROWS * EXPERT_ROWS
    pend = jnp.cumsum(padded)
    pstart = (pend - padded).astype(jnp.int32)
    n_blocks = (n_tok * TOP_K + N_EXPERTS * (EXPERT_ROWS - 1)) // EXPERT_ROWS
    n_used = (pend[-1:] // EXPERT_ROWS).astype(jnp.int32)
    block_row = jnp.arange(n_blocks, dtype=jnp.int32) * EXPERT_ROWS
    block_expert = jnp.minimum(jnp.searchsorted(pend, block_row, side="right"), N_EXPERTS - 1).astype(jnp.int32)
    valid_rows = jnp.clip(pstart[block_expert] + counts[block_expert] - block_row, 0, EXPERT_ROWS).astype(jnp.int32)

    dest = _dest(idx, rank, pstart.astype(F32)[:, None], 512)
    x_sorted = _sc_dispatch(x1_packed, dest, n_blocks * EXPERT_ROWS)
    y_sorted = _experts(block_expert, n_used, valid_rows, x_sorted, w["w_gate_e"], w["w_up_e"], w["w_down_e"])

    base = _dense(x1, p_all, w["w_gate_s"].astype(BF16), w["w_up_s"].astype(BF16),
                  w["w_down_s"].astype(BF16), w["w_ple_gate"].astype(BF16),
                  w["w_ple_proj"].astype(BF16), alpha, 512)

    y_slots = _sc_gather(y_sorted, dest, n_tok)
    y = _combine(gates.T, base, y_slots, w["ln2_g"][None], w["ln2_b"][None], 256)

    y_p = y[:n_p].reshape(batch, seq, d)
    y_s = y[n_p:].reshape(dec_b, dec_t, d)
    keep = min(MAX_DISTANCE, seq)
    k_rows = kp.reshape(batch, seq, N_HEADS, HEAD_DIM)[:, seq - keep:]
    v_rows = vp.reshape(batch, seq, N_HEADS, HEAD_DIM)[:, seq - keep:]
    return (y_p, y_s, k_rows, v_rows,
            ks.reshape(dec_b, dec_t, N_HEADS, HEAD_DIM), vs.reshape(dec_b, dec_t, N_HEADS, HEAD_DIM),
            z2_s.reshape(dec_b, dec_t, D_SGU))


def kernel(x_prompt, x_sample, cache_k, cache_v, p_prompt, p_sample, w_in, rel_bias, sgu_w, sgu_b, sgu_ln_g, sgu_ln_b, w_mix_out, ln1_g, ln1_b, w_router, router_bias, w_gate_e, w_up_e, w_down_e, w_gate_s, w_up_s, w_down_s, w_ple_gate, w_ple_proj, ln2_g, ln2_b):
    depth = w_in.shape[0]
    alpha = (2 * depth) ** 0.25
    xp, xs = x_prompt, x_sample
    outs = [[] for _ in range(5)]
    for i in range(depth):
        w = {"w_in": w_in[i], "sgu_w": sgu_w[i], "sgu_b": sgu_b[i], "sgu_ln_g": sgu_ln_g[i],
             "sgu_ln_b": sgu_ln_b[i], "w_mix_out": w_mix_out[i], "ln1_g": ln1_g[i], "ln1_b": ln1_b[i],
             "w_router": w_router[i], "router_bias": router_bias[i], "w_gate_e": w_gate_e[i],
             "w_up_e": w_up_e[i], "w_down_e": w_down_e[i], "w_gate_s": w_gate_s[i], "w_up_s": w_up_s[i],
             "w_down_s": w_down_s[i], "w_ple_gate": w_ple_gate[i], "w_ple_proj": w_ple_proj[i],
             "ln2_g": ln2_g[i], "ln2_b": ln2_b[i]}
        xp, xs, kp, vp, ks, vs, zs = _layer(xp, xs, cache_k[i], cache_v[i], p_prompt[i], p_sample[i],
                                            w, rel_bias, alpha)
        for lst, val in zip(outs, (kp, vp, ks, vs, zs)):
            lst.append(val)
    return (xp, xs) + tuple(jnp.stack(lst) for lst in outs)
```

```python
import functools
import math

import numpy as np
import jax
import jax.numpy as jnp
from jax import lax
from jax.experimental import pallas as pl
from jax.experimental.pallas import tpu as pltpu
from jax.experimental.pallas import tpu_sc as plsc

F32 = jnp.float32
BF16 = jnp.bfloat16
NEG_INF = float("-inf")

N_HEADS = 8
HEAD_DIM = 64
D_ATTN = N_HEADS * HEAD_DIM
PATTERNS = ((128, 1), (512, 4), (2048, 16))
BAND = 128
N_BUCKETS = 32
MAX_DISTANCE = 2048
N_GROUPS_SGU = 8
D_SGU = 512
CHUNK = 128
N_EXPERTS = 256
TOP_K = 8
N_EXPERT_GROUPS = 8
TOPK_GROUPS = 4
ROUTED_SCALE = 2.5
LN_EPS = 1e-5
EXPERT_ROWS = 256

LANES = 128
SC_CORES = 2
SC_SUBCORES = 16
SC_CHUNK = 128
SC_GATHER_ROWS = 64
VMEM_LIMIT = 56 * 1024 * 1024


def _cparams(*sem):
    return pltpu.CompilerParams(dimension_semantics=sem, vmem_limit_bytes=VMEM_LIMIT)


def _layer_norm(x, g, b):
    mu = jnp.mean(x, axis=-1, keepdims=True)
    xc = x - mu
    var = jnp.mean(xc * xc, axis=-1, keepdims=True)
    return xc * lax.rsqrt(var + LN_EPS) * g + b


def _sigmoid(x):
    return 1.0 / (1.0 + jnp.exp(-x))


def _gelu(x):
    return 0.5 * x * (1.0 + lax.erf(x * math.sqrt(0.5)))


def _pack_bf16_pairs(xb):
    n = xb.shape[1] // 2
    bits = lax.bitcast_convert_type(xb.astype(F32), jnp.uint32)
    return (bits[:, :n] >> 16) | (bits[:, n:] & jnp.uint32(0xFFFF0000))


def _unpack_bf16_pairs(p):
    lo = lax.bitcast_convert_type(p << 16, F32).astype(BF16)
    hi = lax.bitcast_convert_type(p & jnp.uint32(0xFFFF0000), F32).astype(BF16)
    return lo, hi


def _dot(a, b):
    return jnp.dot(a, b, preferred_element_type=F32)


def _dot_nt(a, b):
    return lax.dot_general(a, b, (((1,), (1,)), ((), ())), preferred_element_type=F32)


def _t5_bucket_np(dist):
    max_exact = N_BUCKETS // 2
    df = np.maximum(dist, max_exact).astype(np.float32)
    large = max_exact + (np.log(df / np.float32(max_exact)) / np.float32(math.log(MAX_DISTANCE / max_exact))
                         * np.float32(N_BUCKETS - max_exact)).astype(np.int32)
    return np.where(dist < max_exact, dist, np.minimum(large, N_BUCKETS - 1)).astype(np.int32)


def _band_bucket_table(dilation):
    qi = np.arange(BAND)[:, None]
    ki = np.arange(2 * BAND)[None, :]
    dsub = qi + BAND - ki
    valid = (dsub >= 0) & (dsub <= BAND)
    return np.where(valid, _t5_bucket_np(np.clip(dsub, 0, BAND) * dilation), -1).astype(np.int32)


def _sample_bucket_tables(w_buf, t_len):
    t = np.arange(t_len)[:, None]
    c = np.arange(w_buf)[None, :]
    cn = np.arange(LANES)[None, :]
    cache, new = [], []
    for window, dilation in PATTERNS:
        d = w_buf + t - c
        ok = (d >= 0) & (d % dilation == 0) & (d <= window)
        cache.append(np.where(ok, _t5_bucket_np(np.maximum(d, 0)), -1))
        dn = t - cn
        okn = (dn >= 0) & (dn % dilation == 0) & (dn <= window) & (cn < t_len)
        new.append(np.where(okn, _t5_bucket_np(np.maximum(dn, 0)), -1))
    return np.stack(cache).astype(np.int32), np.stack(new).astype(np.int32)


def _bias_kernel(rb_ref, bucket_ref, out_ref):
    bucket = bucket_ref[...]
    for h in range(N_HEADS):
        acc = jnp.full(bucket.shape, NEG_INF, F32)
        for b in range(N_BUCKETS):
            acc = jnp.where(bucket == b, rb_ref[b, h], acc)
        out_ref[h] = acc


def _bias_table(rel_bias, bucket_np):
    r, c = bucket_np.shape
    return pl.pallas_call(
        _bias_kernel,
        out_shape=jax.ShapeDtypeStruct((N_HEADS, r, c), F32),
        in_specs=[pl.BlockSpec(memory_space=pltpu.SMEM), pl.BlockSpec(memory_space=pltpu.VMEM)],
        out_specs=pl.BlockSpec(memory_space=pltpu.VMEM),
        name="bias_table",
    )(rel_bias, jnp.asarray(bucket_np))


def _proj_kernel(x_ref, w_ref, q_ref, k_ref, v_ref, u_ref, g_ref):
    x = x_ref[...].astype(BF16)
    col = 0
    for o in (q_ref, k_ref, v_ref, u_ref, g_ref):
        n = o.shape[1]
        o[...] = _dot(x, w_ref[:, col:col + n])
        col += n


def _proj(x, w, tm):
    m, d = x.shape
    n_out = (D_ATTN, D_ATTN, D_ATTN, D_SGU, D_SGU)
    return pl.pallas_call(
        _proj_kernel,
        out_shape=[jax.ShapeDtypeStruct((m, n), F32) for n in n_out],
        grid=(m // tm,),
        in_specs=[pl.BlockSpec((tm, d), lambda i: (i, 0)), pl.BlockSpec(w.shape, lambda i: (0, 0))],
        out_specs=[pl.BlockSpec((tm, n), lambda i: (i, 0)) for n in n_out],
        compiler_params=_cparams("parallel"),
        name="in_proj",
    )(x, w)


def _band_attn(q, k, v, bias, even):
    q2 = jnp.concatenate([jnp.where(even, q, 0.0), jnp.where(even, 0.0, q)], axis=0).astype(BF16)
    s = _dot_nt(q2, k.astype(BF16)) + bias
    m = jnp.max(s, axis=-1, keepdims=True)
    p = jnp.exp(s - m)
    l = jnp.sum(p, axis=-1, keepdims=True)
    pv = _dot(p.astype(BF16), v.astype(BF16))
    return (jnp.where(even, m[:BAND], m[BAND:]), jnp.where(even, l[:BAND], l[BAND:]),
            jnp.where(even, pv[:BAND], pv[BAND:]))


def _attn_prompt_kernel(q_ref, k_ref, v_ref, b1_ref, b4_ref, b16_ref, o_ref,
                        m1, l1, a1, m4, l4, a4, m16, l16, a16, qc, kc, vc):
    seq = q_ref.shape[0]
    n4 = seq // 4
    per_trip = seq // BAND // 4
    even = lax.broadcasted_iota(jnp.int32, (BAND, LANES), 1) < HEAD_DIM

    def store_all(refs, rows_list, results):
        for rows, (m, l, a) in zip(rows_list, results):
            refs[0][rows, :], refs[1][rows, :], refs[2][rows, :] = m, l, a

    def body(r, carry):
        rows1, loaded1 = [], []
        for j in range(per_trip):
            i = r * per_trip + j
            r0 = pl.multiple_of(i * BAND, BAND)
            k0 = pl.multiple_of(jnp.maximum(i - 1, 0) * BAND, BAND)
            rows1.append(pl.ds(r0, BAND))
            loaded1.append((q_ref[pl.ds(r0, BAND)], k_ref[pl.ds(k0, 2 * BAND)], v_ref[pl.ds(k0, 2 * BAND)],
                            b1_ref[jnp.where(i == 0, 1, 0)]))
        base = pl.multiple_of(r * n4, BAND)
        q4 = q_ref[pl.ds(r, n4, stride=4), :]
        k4 = k_ref[pl.ds(r, n4, stride=4), :]
        v4 = v_ref[pl.ds(r, n4, stride=4), :]
        qc[pl.ds(base, n4)], kc[pl.ds(base, n4)], vc[pl.ds(base, n4)] = q4, k4, v4
        rows16 = [pl.ds(r * n4 + s, seq // 16, stride=4) for s in range(4)]
        loaded16 = [(qc[rows, :], kc[rows, :], vc[rows, :]) for rows in rows16]

        results1 = [_band_attn(q, k, v, bias, even) for q, k, v, bias in loaded1]
        results4, rows4 = [], []
        for i in range(n4 // BAND):
            lo = max(i - 1, 0) * BAND
            hi = (i + 1) * BAND
            col = 0 if i > 0 else BAND
            results4.append(_band_attn(q4[i * BAND:hi], k4[lo:hi], v4[lo:hi], b4_ref[:, col:], even))
            rows4.append(pl.ds(base + i * BAND, BAND))
        results16 = [_band_attn(q, k, v, b16_ref[:, BAND:], even) for q, k, v in loaded16]

        store_all((m1, l1, a1), rows1, results1)
        store_all((m4, l4, a4), rows4, results4)
        store_all((m16, l16, a16), rows16, results16)
        return carry

    lax.fori_loop(0, 4, body, 0)

    def merge_body(i, carry):
        r = i // (n4 // BAND)
        c = i % (n4 // BAND)
        rows = pl.ds(pl.multiple_of(i * BAND, BAND), BAND)
        nat = pl.ds(r + 4 * BAND * c, BAND, stride=4)
        ma, mb, mc = m1[nat, :], m4[rows], m16[rows]
        mx = jnp.maximum(jnp.maximum(ma, mb), mc)
        wa, wb, wc = jnp.exp(ma - mx), jnp.exp(mb - mx), jnp.exp(mc - mx)
        num = wa * a1[nat, :] + wb * a4[rows] + wc * a16[rows]
        den = wa * l1[nat, :] + wb * l4[rows] + wc * l16[rows]
        o_ref[nat, :] = num / den
        return carry

    lax.fori_loop(0, seq // BAND, merge_body, 0)


def _prompt_bias_tables(rel_bias):
    pairs = N_HEADS // 2
    first = _band_bucket_table(1)
    first = np.concatenate([first[:, BAND:], np.full((BAND, BAND), -1, np.int32)], axis=1)
    b1 = jnp.stack([_bias_table(rel_bias, _band_bucket_table(1)).reshape(pairs, 2 * BAND, 2 * BAND),
                    _bias_table(rel_bias, first).reshape(pairs, 2 * BAND, 2 * BAND)], axis=1)
    b4 = _bias_table(rel_bias, _band_bucket_table(4)).reshape(pairs, 2 * BAND, 2 * BAND)
    b16 = _bias_table(rel_bias, _band_bucket_table(16)).reshape(pairs, 2 * BAND, 2 * BAND)
    return b1, b4, b16


def _attn_prompt(q, k, v, b1, b4, b16, batch, seq):
    blk = pl.BlockSpec((seq, LANES), lambda b, j: (b, j))
    bias_spec = pl.BlockSpec((None, 2 * BAND, 2 * BAND), lambda b, j: (j, 0, 0))
    return pl.pallas_call(
        _attn_prompt_kernel,
        out_shape=jax.ShapeDtypeStruct(q.shape, F32),
        grid=(batch, D_ATTN // LANES),
        in_specs=[blk, blk, blk, pl.BlockSpec((None, 2, 2 * BAND, 2 * BAND), lambda b, j: (j, 0, 0, 0)),
                  bias_spec, bias_spec],
        out_specs=blk,
        scratch_shapes=[pltpu.VMEM((seq, LANES), F32) for _ in range(12)],
        compiler_params=_cparams("parallel", "parallel"),
        name="attn_prompt",
    )(q, k, v, b1, b4, b16)


def _attn_sample_kernel(q_ref, kn_ref, vn_ref, ck_ref, cv_ref, bc_ref, bn_ref, o_ref):
    t_len = q_ref.shape[1]
    rows = N_HEADS * t_len
    q = q_ref[0]
    head_of_row = lax.broadcasted_iota(jnp.int32, (rows, D_ATTN), 0) // t_len
    head_of_lane = lax.broadcasted_iota(jnp.int32, (rows, D_ATTN), 1) // HEAD_DIM
    own = head_of_row == head_of_lane
    qrows = jnp.where(own, jnp.concatenate([q] * N_HEADS, axis=0), 0.0).astype(BF16)
    pad = jnp.zeros((LANES - t_len, D_ATTN), F32)
    kn = jnp.concatenate([kn_ref[0], pad], axis=0).astype(BF16)
    vn = jnp.concatenate([vn_ref[0], pad], axis=0).astype(BF16)
    kc = ck_ref[0].astype(BF16)
    vc = cv_ref[0].astype(BF16)
    s_c = _dot_nt(qrows, kc)
    s_n = _dot_nt(qrows, kn)
    ms, ls, accs = [], [], []
    for p in range(len(PATTERNS)):
        sc = s_c + bc_ref[p]
        sn = s_n + bn_ref[p]
        m = jnp.maximum(jnp.max(sc, axis=-1, keepdims=True), jnp.max(sn, axis=-1, keepdims=True))
        pc = jnp.exp(sc - m)
        pn = jnp.exp(sn - m)
        ls.append(jnp.sum(pc, axis=-1, keepdims=True) + jnp.sum(pn, axis=-1, keepdims=True))
        accs.append(_dot(pc.astype(BF16), vc) + _dot(pn.astype(BF16), vn))
        ms.append(m)
    mx = jnp.maximum(jnp.maximum(ms[0], ms[1]), ms[2])
    ws = [jnp.exp(m - mx) for m in ms]
    num = ws[0] * accs[0] + ws[1] * accs[1] + ws[2] * accs[2]
    den = ws[0] * ls[0] + ws[1] * ls[1] + ws[2] * ls[2]
    full = jnp.where(own, num / den, 0.0)
    out = full[0:t_len]
    for h in range(1, N_HEADS):
        out = out + full[h * t_len:(h + 1) * t_len]
    o_ref[0] = out.astype(o_ref.dtype)


def _attn_sample(q, kn, vn, ck, cv, bc, bn):
    b, t_len, _ = q.shape
    w_buf = ck.shape[1]
    rows = N_HEADS * t_len
    new_spec = pl.BlockSpec((1, t_len, D_ATTN), lambda i: (i, 0, 0))
    cache_spec = pl.BlockSpec((1, w_buf, D_ATTN), lambda i: (i, 0, 0))
    return pl.pallas_call(
        _attn_sample_kernel,
        out_shape=jax.ShapeDtypeStruct(q.shape, BF16),
        grid=(b,),
        in_specs=[new_spec, new_spec, new_spec, cache_spec, cache_spec,
                  pl.BlockSpec((3, rows, w_buf), lambda i: (0, 0, 0)),
                  pl.BlockSpec((3, rows, LANES), lambda i: (0, 0, 0))],
        out_specs=new_spec,
        compiler_params=_cparams("parallel"),
        name="attn_sample",
    )(q, kn, vn, ck, cv, bc, bn)


def _sgu_kernel(u_ref, g_ref, mix_ref, bias_ref, lng_ref, lnb_ref, sgu_ref, *z2_out):
    n_chunks = u_ref.shape[0] // CHUNK
    group_of_lane = lax.broadcasted_iota(jnp.int32, (CHUNK, D_SGU), 1) // (D_SGU // N_GROUPS_SGU)
    for c in range(n_chunks):
        rows = slice(c * CHUNK, (c + 1) * CHUNK)
        z1 = _gelu(u_ref[rows])
        z2 = _layer_norm(_gelu(g_ref[rows]), lng_ref[...], lnb_ref[...])
        if z2_out:
            z2_out[0][rows] = z2
        mixed = bias_ref[...]
        for g in range(N_GROUPS_SGU):
            mixed = mixed + _dot(mix_ref[g], jnp.where(group_of_lane == g, z2, 0.0).astype(BF16))
        sgu_ref[rows] = (z1 * mixed).astype(sgu_ref.dtype)


def _sgu(u, g, mix, bias, ln_g, ln_b, want_z2, chunks_per_step):
    m = u.shape[0]
    tm = CHUNK * chunks_per_step
    row_spec = pl.BlockSpec((tm, D_SGU), lambda i: (i, 0))
    const2 = lambda i: (0, 0)
    out_shape = [jax.ShapeDtypeStruct((m, D_SGU), BF16)]
    out_specs = [row_spec]
    if want_z2:
        out_shape.append(jax.ShapeDtypeStruct((m, D_SGU), F32))
        out_specs.append(row_spec)
    return pl.pallas_call(
        _sgu_kernel,
        out_shape=out_shape,
        grid=(m // tm,),
        in_specs=[row_spec, row_spec,
                  pl.BlockSpec(mix.shape, lambda i: (0, 0, 0)),
                  pl.BlockSpec(bias.shape, const2),
                  pl.BlockSpec(ln_g.shape, const2), pl.BlockSpec(ln_b.shape, const2)],
        out_specs=out_specs,
        compiler_params=_cparams("parallel"),
        name="sgu",
    )(u, g, mix, bias, ln_g, ln_b)


def _top1_rows(x, row):
    m = jnp.max(x, axis=0, keepdims=True)
    i = jnp.min(jnp.where(x == m, row, x.shape[0]), axis=0, keepdims=True)
    return m, i


def _pool_specs(n_prompt, tm, width):
    tiles_p = n_prompt // tm
    return (pl.BlockSpec((tm, width), lambda i, *_: (jnp.minimum(i, tiles_p - 1), 0)),
            pl.BlockSpec((tm, width), lambda i, *_: (jnp.maximum(i - tiles_p, 0), 0)))


def _pool_rows(tiles_p, prompt_ref, sample_ref, dtype):
    return jnp.where(pl.program_id(0) < tiles_p, prompt_ref[...].astype(dtype), sample_ref[...].astype(dtype))


def _mid_kernel(alpha, tiles_p, xp_ref, xs_ref, ap_ref, as_ref, sp_ref, ss_ref, wmix_ref, g1_ref, b1_ref,
                wrt_ref, rb_ref, x1_ref, x1p_ref, idx_ref, gate_ref):
    y = (_dot(_pool_rows(tiles_p, ap_ref, as_ref, BF16), wmix_ref[0:D_ATTN])
         + _dot(_pool_rows(tiles_p, sp_ref, ss_ref, BF16), wmix_ref[D_ATTN:])
         + alpha * _pool_rows(tiles_p, xp_ref, xs_ref, F32))
    x1 = _layer_norm(y, g1_ref[...], b1_ref[...])
    x1_ref[...] = x1
    tm = x1.shape[0]
    x1b = x1.astype(BF16)
    x1p_ref[...] = _pack_bf16_pairs(x1b)
    scores = _sigmoid(_dot_nt(wrt_ref[...], x1b))
    biased = scores + rb_ref[...]
    per_group = N_EXPERTS // N_EXPERT_GROUPS
    row_g = lax.broadcasted_iota(jnp.int32, (per_group, tm), 0)
    row_8 = lax.broadcasted_iota(jnp.int32, (N_EXPERT_GROUPS, tm), 0)
    row_e = lax.broadcasted_iota(jnp.int32, (N_EXPERTS, tm), 0)
    cur = jnp.full((N_EXPERT_GROUPS, tm), NEG_INF, F32)
    for g in range(N_EXPERT_GROUPS):
        blk = biased[g * per_group:(g + 1) * per_group]
        m_a, i_a = _top1_rows(blk, row_g)
        m_b = jnp.max(jnp.where(row_g == i_a, NEG_INF, blk), axis=0, keepdims=True)
        cur = jnp.where(row_8 == g, m_a + m_b, cur)
    chosen = jnp.zeros((N_EXPERT_GROUPS, tm), F32)
    for _ in range(TOPK_GROUPS):
        _, i_g = _top1_rows(cur, row_8)
        sel = row_8 == i_g
        chosen = jnp.where(sel, 1.0, chosen)
        cur = jnp.where(sel, NEG_INF, cur)
    group_of_row = row_e // per_group
    emask = jnp.zeros((N_EXPERTS, tm), F32)
    for g in range(N_EXPERT_GROUPS):
        emask = jnp.where(group_of_row == g, chosen[g:g + 1], emask)
    cur = jnp.where(emask > 0.0, biased, NEG_INF)
    idx = jnp.zeros((TOP_K, tm), jnp.int32)
    gates = jnp.zeros((TOP_K, tm), F32)
    for k in range(TOP_K):
        _, i_e = _top1_rows(cur, row_e)
        sel = row_e == i_e
        gate_k = jnp.sum(jnp.where(sel, scores, 0.0), axis=0, keepdims=True)
        cur = jnp.where(sel, NEG_INF, cur)
        idx = jnp.where(row_8 == k, i_e, idx)
        gates = jnp.where(row_8 == k, gate_k, gates)
    idx_ref[...] = idx
    gate_ref[...] = gates / jnp.sum(gates, axis=0, keepdims=True) * ROUTED_SCALE


def _mid(xp, xs, attn_p, attn_s, sgu_p, sgu_s, wmix, g1, b1, wrt, rb, alpha, tm):
    n_p, d = xp.shape
    t = n_p + xs.shape[0]
    const2 = lambda i: (0, 0)
    return pl.pallas_call(
        functools.partial(_mid_kernel, alpha, n_p // tm),
        out_shape=[jax.ShapeDtypeStruct((t, d), F32),
                   jax.ShapeDtypeStruct((t, d // 2), jnp.uint32),
                   jax.ShapeDtypeStruct((TOP_K, t), jnp.int32),
                   jax.ShapeDtypeStruct((TOP_K, t), F32)],
        grid=(t // tm,),
        in_specs=[*_pool_specs(n_p, tm, d), *_pool_specs(n_p, tm, D_ATTN), *_pool_specs(n_p, tm, D_SGU),
                  pl.BlockSpec(wmix.shape, const2), pl.BlockSpec(g1.shape, const2),
                  pl.BlockSpec(b1.shape, const2), pl.BlockSpec(wrt.shape, const2),
                  pl.BlockSpec(rb.shape, const2)],
        out_specs=[pl.BlockSpec((tm, d), lambda i: (i, 0)),
                   pl.BlockSpec((tm, d // 2), lambda i: (i, 0)),
                   pl.BlockSpec((TOP_K, tm), lambda i: (0, i)),
                   pl.BlockSpec((TOP_K, tm), lambda i: (0, i))],
        compiler_params=_cparams("parallel"),
        name="mix_ln_router",
    )(xp, xs, attn_p, attn_s, sgu_p, sgu_s, wmix, g1, b1, wrt, rb)


def _rank_kernel(idx_ref, rank_ref, counts_ref, run_ref):
    @pl.when(pl.program_id(0) == 0)
    def _():
        run_ref[...] = jnp.zeros_like(run_ref)

    tm = idx_ref.shape[1]
    row_e = lax.broadcasted_iota(jnp.int32, (N_EXPERTS, tm), 0)
    idx = idx_ref[...]
    onehot = jnp.zeros((N_EXPERTS, tm), F32)
    for k in range(TOP_K):
        onehot = onehot + jnp.where(row_e == idx[k:k + 1], 1.0, 0.0)
    earlier = (lax.broadcasted_iota(jnp.int32, (tm, tm), 0)
               < lax.broadcasted_iota(jnp.int32, (tm, tm), 1))
    before = run_ref[...] + _dot(onehot.astype(BF16), jnp.where(earlier, 1.0, 0.0).astype(BF16))
    row_k = lax.broadcasted_iota(jnp.int32, (TOP_K, tm), 0)
    ranks = jnp.zeros((TOP_K, tm), F32)
    for k in range(TOP_K):
        rank_k = jnp.sum(jnp.where(row_e == idx[k:k + 1], before, 0.0), axis=0, keepdims=True)
        ranks = jnp.where(row_k == k, rank_k, ranks)
    rank_ref[...] = ranks.astype(jnp.int32)
    run_ref[...] = run_ref[...] + jnp.sum(onehot, axis=1, keepdims=True)
    counts_ref[...] = run_ref[...]


def _rank(idx, tm):
    t = idx.shape[1]
    return pl.pallas_call(
        _rank_kernel,
        out_shape=[jax.ShapeDtypeStruct((TOP_K, t), jnp.int32),
                   jax.ShapeDtypeStruct((N_EXPERTS, 1), F32)],
        grid=(t // tm,),
        in_specs=[pl.BlockSpec((TOP_K, tm), lambda i: (0, i))],
        out_specs=[pl.BlockSpec((TOP_K, tm), lambda i: (0, i)),
                   pl.BlockSpec((N_EXPERTS, 1), lambda i: (0, 0))],
        scratch_shapes=[pltpu.VMEM((N_EXPERTS, 1), F32)],
        compiler_params=_cparams("arbitrary"),
        name="expert_rank",
    )(idx)


def _dest_kernel(idx_ref, rank_ref, pstart_ref, dest_ref):
    tm = idx_ref.shape[1]
    row_e = lax.broadcasted_iota(jnp.int32, (N_EXPERTS, tm), 0)
    row_k = lax.broadcasted_iota(jnp.int32, (TOP_K, tm), 0)
    idx = idx_ref[...]
    start = jnp.zeros((TOP_K, tm), F32)
    for k in range(TOP_K):
        start_k = jnp.sum(jnp.where(row_e == idx[k:k + 1], pstart_ref[...], 0.0), axis=0, keepdims=True)
        start = jnp.where(row_k == k, start_k, start)
    dest = start.astype(jnp.int32) + rank_ref[...]
    for c in range(tm // SC_CHUNK):
        dest_ref[c] = dest[:, c * SC_CHUNK:(c + 1) * SC_CHUNK]


def _dest(idx, rank, pstart_col, tm):
    t = idx.shape[1]
    tok_spec = pl.BlockSpec((TOP_K, tm), lambda i: (0, i))
    return pl.pallas_call(
        _dest_kernel,
        out_shape=jax.ShapeDtypeStruct((t // SC_CHUNK, TOP_K, SC_CHUNK), jnp.int32),
        grid=(t // tm,),
        in_specs=[tok_spec, tok_spec, pl.BlockSpec(pstart_col.shape, lambda i: (0, 0))],
        out_specs=pl.BlockSpec((tm // SC_CHUNK, TOP_K, SC_CHUNK), lambda i: (i, 0, 0)),
        compiler_params=_cparams("parallel"),
        name="moe_dest",
    )(idx, rank, pstart_col)


def _sc_mesh():
    return plsc.VectorSubcoreMesh(core_axis_name="c", subcore_axis_name="s",
                                  num_cores=SC_CORES, num_subcores=SC_SUBCORES)


def _sc_chunks(n_chunks):
    workers = SC_CORES * SC_SUBCORES
    wid = lax.axis_index("s") * SC_CORES + lax.axis_index("c")
    return wid, workers, (n_chunks - wid + workers - 1) // workers


def _sc_dispatch(x_rows, dest, n_rows):
    n_chunks = dest.shape[0]
    width = x_rows.shape[1]

    @functools.partial(
        pl.kernel, mesh=_sc_mesh(),
        out_type=jax.ShapeDtypeStruct((n_rows, width), x_rows.dtype),
        scratch_types=[pltpu.VMEM((TOP_K, SC_CHUNK), jnp.int32), pltpu.VMEM((SC_CHUNK, width), x_rows.dtype),
                       pltpu.SemaphoreType.DMA],
    )
    def scatter(x_hbm, dest_hbm, out_hbm, dest_v, rows_v, sem):
        wid, workers, n_own = _sc_chunks(n_chunks)

        @pl.loop(0, n_own)
        def _(j):
            ch = wid + j * workers
            pltpu.sync_copy(dest_hbm.at[ch], dest_v)
            pltpu.sync_copy(x_hbm.at[pl.ds(ch * SC_CHUNK, SC_CHUNK)], rows_v)
            copies = [pltpu.async_copy(rows_v, out_hbm.at[dest_v.at[k]], sem) for k in range(TOP_K)]
            for cp in copies:
                cp.wait()

    return scatter(x_rows, dest)


def _expert_kernel(be_ref, nused_ref, valid_ref, xs_ref, wg_ref, wu_ref, wd_ref, y_ref, wg_b, wu_b, wd_b):
    i = pl.program_id(0)
    used = i < nused_ref[0]

    @pl.when(jnp.logical_and(used, jnp.logical_or(i == 0, be_ref[i] != be_ref[jnp.maximum(i - 1, 0)])))
    def _():
        wg_b[...] = wg_ref[0].astype(BF16)
        wu_b[...] = wu_ref[0].astype(BF16)
        wd_b[...] = wd_ref[0].astype(BF16)

    @pl.when(used)
    def _():
        half = xs_ref.shape[1]
        row = lax.broadcasted_iota(jnp.int32, xs_ref.shape, 0)
        lo, hi = _unpack_bf16_pairs(jnp.where(row < valid_ref[i], xs_ref[...], jnp.uint32(0)))
        g = _dot(lo, wg_b[:half]) + _dot(hi, wg_b[half:])
        u = _dot(lo, wu_b[:half]) + _dot(hi, wu_b[half:])
        h = (g * _sigmoid(g) * u).astype(BF16)
        y_ref[...] = _pack_bf16_pairs(_dot(h, wd_b[...]).astype(BF16))


def _experts(block_expert, n_used, valid_rows, xs, wg, wu, wd):
    n_rows = xs.shape[0]
    n_blocks = n_rows // EXPERT_ROWS
    d, de = wg.shape[1], wg.shape[2]

    def row_map(i, be, nu, va):
        return (jnp.minimum(i, nu[0] - 1), 0)

    def w_map(i, be, nu, va):
        return (be[jnp.minimum(i, nu[0] - 1)], 0, 0)

    return pl.pallas_call(
        _expert_kernel,
        out_shape=jax.ShapeDtypeStruct((n_rows, d // 2), jnp.uint32),
        grid_spec=pltpu.PrefetchScalarGridSpec(
            num_scalar_prefetch=3,
            grid=(n_blocks,),
            in_specs=[pl.BlockSpec((EXPERT_ROWS, xs.shape[1]), row_map),
                      pl.BlockSpec((1, d, de), w_map), pl.BlockSpec((1, d, de), w_map),
                      pl.BlockSpec((1, de, d), w_map)],
            out_specs=pl.BlockSpec((EXPERT_ROWS, d // 2), row_map),
            scratch_shapes=[pltpu.VMEM((d, de), BF16), pltpu.VMEM((d, de), BF16), pltpu.VMEM((de, d), BF16)]),
        compiler_params=_cparams("arbitrary"),
        name="routed_experts",
    )(block_expert, n_used, valid_rows, xs, wg, wu, wd)


def _dense_kernel(alpha, tiles_p, x1_ref, pp_ref, ps_ref, wgs_ref, wus_ref, wds_ref, wpg_ref, wpp_ref, base_ref):
    x1 = x1_ref[...]
    xb = x1.astype(BF16)
    g = _dot(xb, wgs_ref[...])
    h = (g * _sigmoid(g) * _dot(xb, wus_ref[...])).astype(BF16)
    shared = _dot(h, wds_ref[...])
    ple = _sigmoid(_dot(xb, wpg_ref[...])) * _dot(_pool_rows(tiles_p, pp_ref, ps_ref, BF16), wpp_ref[...])
    base_ref[...] = alpha * x1 + shared + ple


def _dense(x1, pp, ps, wgs, wus, wds, wpg, wpp, alpha, tm):
    t, d = x1.shape
    const2 = lambda i: (0, 0)
    return pl.pallas_call(
        functools.partial(_dense_kernel, alpha, pp.shape[0] // tm),
        out_shape=jax.ShapeDtypeStruct((t, d), F32),
        grid=(t // tm,),
        in_specs=[pl.BlockSpec((tm, d), lambda i: (i, 0)), *_pool_specs(pp.shape[0], tm, pp.shape[1]),
                  pl.BlockSpec(wgs.shape, const2), pl.BlockSpec(wus.shape, const2),
                  pl.BlockSpec(wds.shape, const2), pl.BlockSpec(wpg.shape, const2),
                  pl.BlockSpec(wpp.shape, const2)],
        out_specs=pl.BlockSpec((tm, d), lambda i: (i, 0)),
        compiler_params=_cparams("parallel"),
        name="shared_ple",
    )(x1, pp, ps, wgs, wus, wds, wpg, wpp)


def _sc_gather(y_rows, dest, n_tok):
    n_chunks = dest.shape[0]
    width = y_rows.shape[1]
    items = [(k, h) for k in range(TOP_K) for h in range(SC_CHUNK // SC_GATHER_ROWS)]

    @functools.partial(
        pl.kernel, mesh=_sc_mesh(),
        out_type=jax.ShapeDtypeStruct((TOP_K, n_tok, width), y_rows.dtype),
        scratch_types=[pltpu.VMEM((TOP_K, SC_CHUNK), jnp.int32),
                       pltpu.VMEM((2, SC_GATHER_ROWS, width), y_rows.dtype),
                       pltpu.SemaphoreType.DMA, pltpu.SemaphoreType.DMA],
    )
    def gather(y_hbm, dest_hbm, out_hbm, dest_v, rows_v, gsem, wsem):
        wid, workers, n_own = _sc_chunks(n_chunks)

        @pl.loop(0, n_own)
        def _(j):
            ch = wid + j * workers
            pltpu.sync_copy(dest_hbm.at[ch], dest_v)

            def fetch(n):
                k, h = items[n]
                return pltpu.async_copy(y_hbm.at[dest_v.at[k, pl.ds(h * SC_GATHER_ROWS, SC_GATHER_ROWS)]],
                                        rows_v.at[n % 2], gsem)

            def write(n):
                k, h = items[n]
                rows = pl.ds(ch * SC_CHUNK + h * SC_GATHER_ROWS, SC_GATHER_ROWS)
                return pltpu.async_copy(rows_v.at[n % 2], out_hbm.at[k, rows], wsem)

            pending_fetch = fetch(0)
            pending_write = None
            for n in range(len(items)):
                pending_fetch.wait()
                if pending_write is not None:
                    pending_write.wait()
                if n + 1 < len(items):
                    pending_fetch = fetch(n + 1)
                pending_write = write(n)
            pending_write.wait()

    return gather(y_rows, dest)


def _combine_kernel(tiles_p, gate_ref, base_ref, yk_ref, g2_ref, b2_ref, outp_ref, outs_ref):
    gates = gate_ref[...]
    half = yk_ref.shape[2]
    acc_lo = base_ref[:, :half]
    acc_hi = base_ref[:, half:]
    for k in range(TOP_K):
        packed = yk_ref[k]
        gate = gates[:, k:k + 1]
        acc_lo = acc_lo + gate * lax.bitcast_convert_type(packed << 16, F32)
        acc_hi = acc_hi + gate * lax.bitcast_convert_type(packed & jnp.uint32(0xFFFF0000), F32)
    out = _layer_norm(jnp.concatenate([acc_lo, acc_hi], axis=1), g2_ref[...], b2_ref[...])
    is_prompt = pl.program_id(0) < tiles_p

    @pl.when(is_prompt)
    def _():
        outp_ref[...] = out

    @pl.when(jnp.logical_not(is_prompt))
    def _():
        outs_ref[...] = out


def _combine(gates_tok, base, yk, g2, b2, n_prompt, tm):
    t, d = base.shape
    const2 = lambda i: (0, 0)
    return pl.pallas_call(
        functools.partial(_combine_kernel, n_prompt // tm),
        out_shape=[jax.ShapeDtypeStruct((n_prompt, d), F32), jax.ShapeDtypeStruct((t - n_prompt, d), F32)],
        grid=(t // tm,),
        in_specs=[pl.BlockSpec((tm, TOP_K), lambda i: (i, 0)),
                  pl.BlockSpec((tm, d), lambda i: (i, 0)),
                  pl.BlockSpec((TOP_K, tm, yk.shape[2]), lambda i: (0, i, 0)),
                  pl.BlockSpec(g2.shape, const2), pl.BlockSpec(b2.shape, const2)],
        out_specs=list(_pool_specs(n_prompt, tm, d)),
        compiler_params=_cparams("arbitrary"),
        name="moe_combine",
    )(gates_tok, base, yk, g2, b2)


def _sgu_tables(sgu_w, sgu_b, rows_per_seq):
    reps = CHUNK // rows_per_seq
    tril = jnp.tril(sgu_w[:, :rows_per_seq, :rows_per_seq])
    eye = jnp.eye(reps, dtype=F32)
    mix = jnp.einsum("ab,gts->gatbs", eye, tril).reshape(N_GROUPS_SGU, CHUNK, CHUNK)
    bias = jnp.tile(jnp.repeat(sgu_b[:, :rows_per_seq].T, D_SGU // N_GROUPS_SGU, axis=1), (reps, 1))
    return mix.astype(BF16), bias


def _layer(xp, xs, ck, cv, pp, ps, w, rel_bias, alpha):
    batch, seq, d = xp.shape
    dec_b, dec_t, _ = xs.shape
    w_buf = ck.shape[1]
    n_p, n_s = batch * seq, dec_b * dec_t
    n_tok = n_p + n_s

    scale = jnp.concatenate([jnp.full((D_ATTN,), HEAD_DIM ** -0.5, F32),
                             jnp.ones((w["w_in"].shape[1] - D_ATTN,), F32)])
    w_in = (w["w_in"] * scale).astype(BF16)

    qp, kp, vp, up, gp = _proj(xp.reshape(n_p, d), w_in, 512)
    qs, ks, vs, us, gs = _proj(xs.reshape(n_s, d), w_in, 512)

    b1, b4, b16 = _prompt_bias_tables(rel_bias)
    attn_p = _attn_prompt(qp, kp, vp, b1, b4, b16, batch, seq)

    cache_t, new_t = _sample_bucket_tables(w_buf, dec_t)
    bc = _bias_table(rel_bias, cache_t.reshape(3 * dec_t, w_buf))
    bn = _bias_table(rel_bias, new_t.reshape(3 * dec_t, LANES))
    bc = bc.reshape(N_HEADS, 3, dec_t, w_buf).transpose(1, 0, 2, 3).reshape(3, N_HEADS * dec_t, w_buf)
    bn = bn.reshape(N_HEADS, 3, dec_t, LANES).transpose(1, 0, 2, 3).reshape(3, N_HEADS * dec_t, LANES)
    attn_s = _attn_sample(qs.reshape(dec_b, dec_t, D_ATTN), ks.reshape(dec_b, dec_t, D_ATTN),
                          vs.reshape(dec_b, dec_t, D_ATTN), ck.reshape(dec_b, w_buf, D_ATTN),
                          cv.reshape(dec_b, w_buf, D_ATTN), bc, bn).reshape(n_s, D_ATTN)

    ln_g, ln_b = w["sgu_ln_g"][None], w["sgu_ln_b"][None]
    mix_p, bias_p = _sgu_tables(w["sgu_w"], w["sgu_b"], CHUNK)
    mix_s, bias_s = _sgu_tables(w["sgu_w"], w["sgu_b"], dec_t)
    (sgu_p,) = _sgu(up, gp, mix_p, bias_p, ln_g, ln_b, False, 4)
    sgu_s, z2_s = _sgu(us, gs, mix_s, bias_s, ln_g, ln_b, True, 4)

    x1, x1_packed, idx, gates = _mid(xp.reshape(n_p, d), xs.reshape(n_s, d), attn_p, attn_s, sgu_p, sgu_s,
                                     w["w_mix_out"].astype(BF16), w["ln1_g"][None], w["ln1_b"][None],
                                     w["w_router"].T.astype(BF16), w["router_bias"][:, None], alpha, 512)

    rank, counts = _rank(idx, 512)
    counts = counts[:, 0].astype(jnp.int32)
    padded = (counts + EXPERT_ROWS - 1) // EXPERT_ROWS * EXPERT_ROWS
    pend = jnp.cumsum(padded)
    pstart = (pend - padded).astype(jnp.int32)
    n_blocks = (n_tok * TOP_K + N_EXPERTS * (EXPERT_ROWS - 1)) // EXPERT_ROWS
    n_used = (pend[-1:] // EXPERT_ROWS).astype(jnp.int32)
    block_row = jnp.arange(n_blocks, dtype=jnp.int32) * EXPERT_ROWS
    block_expert = jnp.minimum(jnp.sum(pend[None, :] <= block_row[:, None], axis=1), N_EXPERTS - 1).astype(jnp.int32)
    own = block_expert[:, None] == jnp.arange(N_EXPERTS, dtype=jnp.int32)[None, :]
    run_end = jnp.sum(jnp.where(own, (pstart + counts)[None, :], 0), axis=1)
    valid_rows = jnp.clip(run_end - block_row, 0, EXPERT_ROWS).astype(jnp.int32)

    dest = _dest(idx, rank, pstart.astype(F32)[:, None], 512)
    x_sorted = _sc_dispatch(x1_packed, dest, n_blocks * EXPERT_ROWS)
    y_sorted = _experts(block_expert, n_used, valid_rows, x_sorted, w["w_gate_e"], w["w_up_e"], w["w_down_e"])

    base = _dense(x1, pp.reshape(n_p, -1), ps.reshape(n_s, -1), w["w_gate_s"].astype(BF16), w["w_up_s"].astype(BF16),
                  w["w_down_s"].astype(BF16), w["w_ple_gate"].astype(BF16),
                  w["w_ple_proj"].astype(BF16), alpha, 512)

    y_slots = _sc_gather(y_sorted, dest, n_tok)
    y_p, y_s = _combine(gates.T, base, y_slots, w["ln2_g"][None], w["ln2_b"][None], n_p, 256)
    y_p = y_p.reshape(batch, seq, d)
    y_s = y_s.reshape(dec_b, dec_t, d)
    keep = min(MAX_DISTANCE, seq)
    k_rows = kp.reshape(batch, seq, N_HEADS, HEAD_DIM)[:, seq - keep:]
    v_rows = vp.reshape(batch, seq, N_HEADS, HEAD_DIM)[:, seq - keep:]
    return (y_p, y_s, k_rows, v_rows,
            ks.reshape(dec_b, dec_t, N_HEADS, HEAD_DIM), vs.reshape(dec_b, dec_t, N_HEADS, HEAD_DIM),
            z2_s.reshape(dec_b, dec_t, D_SGU))


def kernel(x_prompt, x_sample, cache_k, cache_v, p_prompt, p_sample, w_in, rel_bias, sgu_w, sgu_b, sgu_ln_g, sgu_ln_b, w_mix_out, ln1_g, ln1_b, w_router, router_bias, w_gate_e, w_up_e, w_down_e, w_gate_s, w_up_s, w_down_s, w_ple_gate, w_ple_proj, ln2_g, ln2_b):
    depth = w_in.shape[0]
    alpha = (2 * depth) ** 0.25
    xp, xs = x_prompt, x_sample
    outs = [[] for _ in range(5)]
    for i in range(depth):
        w = {"w_in": w_in[i], "sgu_w": sgu_w[i], "sgu_b": sgu_b[i], "sgu_ln_g": sgu_ln_g[i],
             "sgu_ln_b": sgu_ln_b[i], "w_mix_out": w_mix_out[i], "ln1_g": ln1_g[i], "ln1_b": ln1_b[i],
             "w_router": w_router[i], "router_bias": router_bias[i], "w_gate_e": w_gate_e[i],
             "w_up_e": w_up_e[i], "w_down_e": w_down_e[i], "w_gate_s": w_gate_s[i], "w_up_s": w_up_s[i],
             "w_down_s": w_down_s[i], "w_ple_gate": w_ple_gate[i], "w_ple_proj": w_ple_proj[i],
             "ln2_g": ln2_g[i], "ln2_b": ln2_b[i]}
        xp, xs, kp, vp, ks, vs, zs = _layer(xp, xs, cache_k[i], cache_v[i], p_prompt[i], p_sample[i],
                                            w, rel_bias, alpha)
        for lst, val in zip(outs, (kp, vp, ks, vs, zs)):
            lst.append(val)
    return (xp, xs) + tuple(jnp.stack(lst) for lst in outs)
```

```python
import functools
import math

import numpy as np
import jax
import jax.numpy as jnp
from jax import lax
from jax.experimental import pallas as pl
from jax.experimental.pallas import tpu as pltpu
from jax.experimental.pallas import tpu_sc as plsc

F32 = jnp.float32
BF16 = jnp.bfloat16
NEG_INF = float("-inf")

N_HEADS = 8
HEAD_DIM = 64
D_ATTN = N_HEADS * HEAD_DIM
PATTERNS = ((128, 1), (512, 4), (2048, 16))
BAND = 128
N_BUCKETS = 32
MAX_DISTANCE = 2048
N_GROUPS_SGU = 8
D_SGU = 512
CHUNK = 128
N_EXPERTS = 256
TOP_K = 8
N_EXPERT_GROUPS = 8
TOPK_GROUPS = 4
ROUTED_SCALE = 2.5
LN_EPS = 1e-5
EXPERT_ROWS = 256

LANES = 128
SC_CORES = 2
SC_SUBCORES = 16
SC_CHUNK = 128
SC_GATHER_ROWS = 64
VMEM_LIMIT = 56 * 1024 * 1024


def _cparams(*sem):
    return pltpu.CompilerParams(dimension_semantics=sem, vmem_limit_bytes=VMEM_LIMIT)


def _layer_norm(x, g, b):
    mu = jnp.mean(x, axis=-1, keepdims=True)
    xc = x - mu
    var = jnp.mean(xc * xc, axis=-1, keepdims=True)
    return xc * lax.rsqrt(var + LN_EPS) * g + b


def _sigmoid(x):
    return 1.0 / (1.0 + jnp.exp(-x))


def _gelu(x):
    return 0.5 * x * (1.0 + lax.erf(x * math.sqrt(0.5)))


def _pack_bf16_pairs(xb):
    n = xb.shape[1] // 2
    bits = lax.bitcast_convert_type(xb.astype(F32), jnp.uint32)
    return (bits[:, :n] >> 16) | (bits[:, n:] & jnp.uint32(0xFFFF0000))


def _unpack_bf16_pairs(p):
    lo = lax.bitcast_convert_type(p << 16, F32).astype(BF16)
    hi = lax.bitcast_convert_type(p & jnp.uint32(0xFFFF0000), F32).astype(BF16)
    return lo, hi


def _dot(a, b):
    return jnp.dot(a, b, preferred_element_type=F32)


def _dot_nt(a, b):
    return lax.dot_general(a, b, (((1,), (1,)), ((), ())), preferred_element_type=F32)


def _t5_bucket_np(dist):
    max_exact = N_BUCKETS // 2
    df = np.maximum(dist, max_exact).astype(np.float32)
    large = max_exact + (np.log(df / np.float32(max_exact)) / np.float32(math.log(MAX_DISTANCE / max_exact))
                         * np.float32(N_BUCKETS - max_exact)).astype(np.int32)
    return np.where(dist < max_exact, dist, np.minimum(large, N_BUCKETS - 1)).astype(np.int32)


def _band_bucket_table(dilation):
    qi = np.arange(BAND)[:, None]
    ki = np.arange(2 * BAND)[None, :]
    dsub = qi + BAND - ki
    valid = (dsub >= 0) & (dsub <= BAND)
    return np.where(valid, _t5_bucket_np(np.clip(dsub, 0, BAND) * dilation), -1).astype(np.int32)


def _cache_rows(w_buf, t_len):
    d_far = PATTERNS[2][1]
    assert w_buf % d_far == 0 and t_len <= d_far and w_buf >= PATTERNS[2][0]
    far = (np.arange(w_buf // d_far)[:, None] * d_far + np.arange(t_len)[None, :]).reshape(-1)
    near = np.arange(w_buf - PATTERNS[1][0], w_buf)
    return far, near


def _sample_bucket_tables(w_buf, t_len):
    far, near = _cache_rows(w_buf, t_len)
    t = np.arange(t_len)[:, None]

    def table(rows, window, dilation):
        d = w_buf + t - rows[None, :]
        ok = (d >= 0) & (d % dilation == 0) & (d <= window)
        return np.where(ok, _t5_bucket_np(np.maximum(d, 0)), -1).astype(np.int32)

    pieces = (near[-PATTERNS[0][0]:], near, far)
    tables = []
    for (window, dilation), rows in zip(PATTERNS, pieces):
        full = table(np.arange(w_buf), window, dilation)
        outside = np.ones(w_buf, bool)
        outside[rows] = False
        assert not (full[:, outside] >= 0).any()
        tables.append(table(rows, window, dilation))
    new_rows = w_buf + np.arange(LANES)
    new = np.stack([table(new_rows, window, dilation) for window, dilation in PATTERNS])
    new[:, :, t_len:] = -1
    return tables, new


def _bias_kernel(rb_ref, bucket_ref, out_ref):
    bucket = bucket_ref[...]
    for h in range(N_HEADS):
        acc = jnp.full(bucket.shape, NEG_INF, F32)
        for b in range(N_BUCKETS):
            acc = jnp.where(bucket == b, rb_ref[b, h], acc)
        out_ref[h] = acc


def _bias_table(rel_bias, bucket_np):
    r, c = bucket_np.shape
    return pl.pallas_call(
        _bias_kernel,
        out_shape=jax.ShapeDtypeStruct((N_HEADS, r, c), F32),
        in_specs=[pl.BlockSpec(memory_space=pltpu.SMEM), pl.BlockSpec(memory_space=pltpu.VMEM)],
        out_specs=pl.BlockSpec(memory_space=pltpu.VMEM),
        name="bias_table",
    )(rel_bias, jnp.asarray(bucket_np))


def _proj_kernel(x_ref, w_ref, q_ref, k_ref, v_ref, u_ref, g_ref):
    x = x_ref[...].astype(BF16)
    col = 0
    for o in (q_ref, k_ref, v_ref, u_ref, g_ref):
        n = o.shape[1]
        o[...] = _dot(x, w_ref[:, col:col + n])
        col += n


def _proj(x, w, tm):
    m, d = x.shape
    n_out = (D_ATTN, D_ATTN, D_ATTN, D_SGU, D_SGU)
    return pl.pallas_call(
        _proj_kernel,
        out_shape=[jax.ShapeDtypeStruct((m, n), F32) for n in n_out],
        grid=(m // tm,),
        in_specs=[pl.BlockSpec((tm, d), lambda i: (i, 0)), pl.BlockSpec(w.shape, lambda i: (0, 0))],
        out_specs=[pl.BlockSpec((tm, n), lambda i: (i, 0)) for n in n_out],
        compiler_params=_cparams("parallel"),
        name="in_proj",
    )(x, w)


def _band_attn(q, k, v, bias, even):
    q2 = jnp.concatenate([jnp.where(even, q, 0.0), jnp.where(even, 0.0, q)], axis=0).astype(BF16)
    s = _dot_nt(q2, k.astype(BF16)) + bias
    m = jnp.max(s, axis=-1, keepdims=True)
    p = jnp.exp(s - m)
    l = jnp.sum(p, axis=-1, keepdims=True)
    pv = _dot(p.astype(BF16), v.astype(BF16))
    return (jnp.where(even, m[:BAND], m[BAND:]), jnp.where(even, l[:BAND], l[BAND:]),
            jnp.where(even, pv[:BAND], pv[BAND:]))


def _attn_prompt_kernel(q_ref, k_ref, v_ref, b1_ref, b4_ref, b16_ref, o_ref,
                        m1, l1, a1, m4, l4, a4, m16, l16, a16, qc, kc, vc):
    seq = q_ref.shape[0]
    n4 = seq // 4
    per_trip = seq // BAND // 4
    even = lax.broadcasted_iota(jnp.int32, (BAND, LANES), 1) < HEAD_DIM

    def store_all(refs, rows_list, results):
        for rows, (m, l, a) in zip(rows_list, results):
            refs[0][rows, :], refs[1][rows, :], refs[2][rows, :] = m, l, a

    def body(r, carry):
        rows1, loaded1 = [], []
        for j in range(per_trip):
            i = r * per_trip + j
            r0 = pl.multiple_of(i * BAND, BAND)
            k0 = pl.multiple_of(jnp.maximum(i - 1, 0) * BAND, BAND)
            rows1.append(pl.ds(r0, BAND))
            loaded1.append((q_ref[pl.ds(r0, BAND)], k_ref[pl.ds(k0, 2 * BAND)], v_ref[pl.ds(k0, 2 * BAND)],
                            b1_ref[jnp.where(i == 0, 1, 0)]))
        base = pl.multiple_of(r * n4, BAND)
        q4 = q_ref[pl.ds(r, n4, stride=4), :]
        k4 = k_ref[pl.ds(r, n4, stride=4), :]
        v4 = v_ref[pl.ds(r, n4, stride=4), :]
        qc[pl.ds(base, n4)], kc[pl.ds(base, n4)], vc[pl.ds(base, n4)] = q4, k4, v4
        rows16 = [pl.ds(r * n4 + s, seq // 16, stride=4) for s in range(4)]
        loaded16 = [(qc[rows, :], kc[rows, :], vc[rows, :]) for rows in rows16]

        results1 = [_band_attn(q, k, v, bias, even) for q, k, v, bias in loaded1]
        results4, rows4 = [], []
        for i in range(n4 // BAND):
            lo = max(i - 1, 0) * BAND
            hi = (i + 1) * BAND
            col = 0 if i > 0 else BAND
            results4.append(_band_attn(q4[i * BAND:hi], k4[lo:hi], v4[lo:hi], b4_ref[:, col:], even))
            rows4.append(pl.ds(base + i * BAND, BAND))
        results16 = [_band_attn(q, k, v, b16_ref[:, BAND:], even) for q, k, v in loaded16]

        store_all((m1, l1, a1), rows1, results1)
        store_all((m4, l4, a4), rows4, results4)
        store_all((m16, l16, a16), rows16, results16)
        return carry

    lax.fori_loop(0, 4, body, 0)

    def merge_body(i, carry):
        r = i // (n4 // BAND)
        c = i % (n4 // BAND)
        rows = pl.ds(pl.multiple_of(i * BAND, BAND), BAND)
        nat = pl.ds(r + 4 * BAND * c, BAND, stride=4)
        ma, mb, mc = m1[nat, :], m4[rows], m16[rows]
        mx = jnp.maximum(jnp.maximum(ma, mb), mc)
        wa, wb, wc = jnp.exp(ma - mx), jnp.exp(mb - mx), jnp.exp(mc - mx)
        num = wa * a1[nat, :] + wb * a4[rows] + wc * a16[rows]
        den = wa * l1[nat, :] + wb * l4[rows] + wc * l16[rows]
        o_ref[nat, :] = num / den
        return carry

    lax.fori_loop(0, seq // BAND, merge_body, 0)


def _prompt_bias_tables(rel_bias):
    pairs = N_HEADS // 2
    first = _band_bucket_table(1)
    first = np.concatenate([first[:, BAND:], np.full((BAND, BAND), -1, np.int32)], axis=1)
    b1 = jnp.stack([_bias_table(rel_bias, _band_bucket_table(1)).reshape(pairs, 2 * BAND, 2 * BAND),
                    _bias_table(rel_bias, first).reshape(pairs, 2 * BAND, 2 * BAND)], axis=1)
    b4 = _bias_table(rel_bias, _band_bucket_table(4)).reshape(pairs, 2 * BAND, 2 * BAND)
    b16 = _bias_table(rel_bias, _band_bucket_table(16)).reshape(pairs, 2 * BAND, 2 * BAND)
    return b1, b4, b16


def _attn_prompt(q, k, v, b1, b4, b16, batch, seq):
    blk = pl.BlockSpec((seq, LANES), lambda b, j: (b, j))
    bias_spec = pl.BlockSpec((None, 2 * BAND, 2 * BAND), lambda b, j: (j, 0, 0))
    return pl.pallas_call(
        _attn_prompt_kernel,
        out_shape=jax.ShapeDtypeStruct(q.shape, F32),
        grid=(batch, D_ATTN // LANES),
        in_specs=[blk, blk, blk, pl.BlockSpec((None, 2, 2 * BAND, 2 * BAND), lambda b, j: (j, 0, 0, 0)),
                  bias_spec, bias_spec],
        out_specs=blk,
        scratch_shapes=[pltpu.VMEM((seq, LANES), F32) for _ in range(12)],
        compiler_params=_cparams("parallel", "parallel"),
        name="attn_prompt",
    )(q, k, v, b1, b4, b16)


def _attn_sample_kernel(q_ref, kn_ref, vn_ref, kfar_ref, vfar_ref, knear_ref, vnear_ref,
                        b1_ref, b4_ref, b16_ref, bn_ref, o_ref):
    t_len = q_ref.shape[1]
    rows = N_HEADS * t_len
    q = q_ref[0]
    head_of_row = lax.broadcasted_iota(jnp.int32, (rows, D_ATTN), 0) // t_len
    head_of_lane = lax.broadcasted_iota(jnp.int32, (rows, D_ATTN), 1) // HEAD_DIM
    own = head_of_row == head_of_lane
    qrows = jnp.where(own, jnp.concatenate([q] * N_HEADS, axis=0), 0.0).astype(BF16)
    pad = jnp.zeros((LANES - t_len, D_ATTN), F32)
    kn = jnp.concatenate([kn_ref[0], pad], axis=0).astype(BF16)
    vn = jnp.concatenate([vn_ref[0], pad], axis=0).astype(BF16)
    vfar, vnear = vfar_ref[0], vnear_ref[0]
    s_far = _dot_nt(qrows, kfar_ref[0])
    s_near = _dot_nt(qrows, knear_ref[0])
    s_new = _dot_nt(qrows, kn)
    w1 = b1_ref.shape[1]
    branches = ((s_near[:, -w1:] + b1_ref[...], vnear[-w1:], s_new + bn_ref[0]),
                (s_near + b4_ref[...], vnear, s_new + bn_ref[1]),
                (s_far + b16_ref[...], vfar, s_new + bn_ref[2]))
    ms, ls, accs = [], [], []
    for sc, vals, sn in branches:
        m = jnp.maximum(jnp.max(sc, axis=-1, keepdims=True), jnp.max(sn, axis=-1, keepdims=True))
        pc = jnp.exp(sc - m)
        pn = jnp.exp(sn - m)
        ls.append(jnp.sum(pc, axis=-1, keepdims=True) + jnp.sum(pn, axis=-1, keepdims=True))
        accs.append(_dot(pc.astype(BF16), vals) + _dot(pn.astype(BF16), vn))
        ms.append(m)
    mx = jnp.maximum(jnp.maximum(ms[0], ms[1]), ms[2])
    ws = [jnp.exp(m - mx) for m in ms]
    num = ws[0] * accs[0] + ws[1] * accs[1] + ws[2] * accs[2]
    den = ws[0] * ls[0] + ws[1] * ls[1] + ws[2] * ls[2]
    full = jnp.where(own, num / den, 0.0)
    out = full[0:t_len]
    for h in range(1, N_HEADS):
        out = out + full[h * t_len:(h + 1) * t_len]
    o_ref[0] = out.astype(o_ref.dtype)


def _attn_sample(q, kn, vn, kfar, vfar, knear, vnear, b1, b4, b16, bn):
    b, t_len, _ = q.shape
    new_spec = pl.BlockSpec((1, t_len, D_ATTN), lambda i: (i, 0, 0))
    far_spec = pl.BlockSpec((1,) + kfar.shape[1:], lambda i: (i, 0, 0))
    near_spec = pl.BlockSpec((1,) + knear.shape[1:], lambda i: (i, 0, 0))
    const2 = lambda i: (0, 0)
    return pl.pallas_call(
        _attn_sample_kernel,
        out_shape=jax.ShapeDtypeStruct(q.shape, BF16),
        grid=(b,),
        in_specs=[new_spec, new_spec, new_spec, far_spec, far_spec, near_spec, near_spec,
                  pl.BlockSpec(b1.shape, const2), pl.BlockSpec(b4.shape, const2), pl.BlockSpec(b16.shape, const2),
                  pl.BlockSpec(bn.shape, lambda i: (0, 0, 0))],
        out_specs=new_spec,
        compiler_params=_cparams("parallel"),
        name="attn_sample",
    )(q, kn, vn, kfar, vfar, knear, vnear, b1, b4, b16, bn)


def _sample_attention(rel_bias, q, kn, vn, ck, cv):
    b, t_len, _ = q.shape
    w_buf = ck.shape[1]
    d_far = PATTERNS[2][1]
    rows = N_HEADS * t_len
    tables, new_t = _sample_bucket_tables(w_buf, t_len)
    b1, b4, b16 = (_bias_table(rel_bias, tb).reshape(rows, tb.shape[1]) for tb in tables)
    bn = _bias_table(rel_bias, new_t.reshape(3 * t_len, LANES))
    bn = bn.reshape(N_HEADS, 3, t_len, LANES).transpose(1, 0, 2, 3).reshape(3, rows, LANES)

    def pieces(c):
        cb = c.astype(BF16).reshape(b, w_buf, D_ATTN)
        far = cb.reshape(b, w_buf // d_far, d_far, D_ATTN)[:, :, :t_len].reshape(b, -1, D_ATTN)
        return far, cb[:, w_buf - PATTERNS[1][0]:]

    kfar, knear = pieces(ck)
    vfar, vnear = pieces(cv)
    return _attn_sample(q, kn, vn, kfar, vfar, knear, vnear, b1, b4, b16, bn)


def _sgu_kernel(u_ref, g_ref, mix_ref, bias_ref, lng_ref, lnb_ref, sgu_ref, *z2_out):
    n_chunks = u_ref.shape[0] // CHUNK
    group_of_lane = lax.broadcasted_iota(jnp.int32, (CHUNK, D_SGU), 1) // (D_SGU // N_GROUPS_SGU)
    for c in range(n_chunks):
        rows = slice(c * CHUNK, (c + 1) * CHUNK)
        z1 = _gelu(u_ref[rows])
        z2 = _layer_norm(_gelu(g_ref[rows]), lng_ref[...], lnb_ref[...])
        if z2_out:
            z2_out[0][rows] = z2
        mixed = bias_ref[...]
        for g in range(N_GROUPS_SGU):
            mixed = mixed + _dot(mix_ref[g], jnp.where(group_of_lane == g, z2, 0.0).astype(BF16))
        sgu_ref[rows] = (z1 * mixed).astype(sgu_ref.dtype)


def _sgu(u, g, mix, bias, ln_g, ln_b, want_z2, chunks_per_step):
    m = u.shape[0]
    tm = CHUNK * chunks_per_step
    row_spec = pl.BlockSpec((tm, D_SGU), lambda i: (i, 0))
    const2 = lambda i: (0, 0)
    out_shape = [jax.ShapeDtypeStruct((m, D_SGU), BF16)]
    out_specs = [row_spec]
    if want_z2:
        out_shape.append(jax.ShapeDtypeStruct((m, D_SGU), F32))
        out_specs.append(row_spec)
    return pl.pallas_call(
        _sgu_kernel,
        out_shape=out_shape,
        grid=(m // tm,),
        in_specs=[row_spec, row_spec,
                  pl.BlockSpec(mix.shape, lambda i: (0, 0, 0)),
                  pl.BlockSpec(bias.shape, const2),
                  pl.BlockSpec(ln_g.shape, const2), pl.BlockSpec(ln_b.shape, const2)],
        out_specs=out_specs,
        compiler_params=_cparams("parallel"),
        name="sgu",
    )(u, g, mix, bias, ln_g, ln_b)


def _top1_rows(x, row):
    m = jnp.max(x, axis=0, keepdims=True)
    i = jnp.min(jnp.where(x == m, row, x.shape[0]), axis=0, keepdims=True)
    return m, i


def _pool_specs(n_prompt, tm, width):
    tiles_p = n_prompt // tm
    return (pl.BlockSpec((tm, width), lambda i, *_: (jnp.minimum(i, tiles_p - 1), 0)),
            pl.BlockSpec((tm, width), lambda i, *_: (jnp.maximum(i - tiles_p, 0), 0)))


def _pool_rows(tiles_p, prompt_ref, sample_ref, dtype):
    return jnp.where(pl.program_id(0) < tiles_p, prompt_ref[...].astype(dtype), sample_ref[...].astype(dtype))


def _mid_kernel(alpha, tiles_p, xp_ref, xs_ref, ap_ref, as_ref, sp_ref, ss_ref, wmix_ref, g1_ref, b1_ref,
                wrt_ref, rb_ref, x1_ref, x1p_ref, idx_ref, gate_ref):
    y = (_dot(_pool_rows(tiles_p, ap_ref, as_ref, BF16), wmix_ref[0:D_ATTN])
         + _dot(_pool_rows(tiles_p, sp_ref, ss_ref, BF16), wmix_ref[D_ATTN:])
         + alpha * _pool_rows(tiles_p, xp_ref, xs_ref, F32))
    x1 = _layer_norm(y, g1_ref[...], b1_ref[...])
    x1_ref[...] = x1
    tm = x1.shape[0]
    x1b = x1.astype(BF16)
    x1p_ref[...] = _pack_bf16_pairs(x1b)
    scores = _sigmoid(_dot_nt(wrt_ref[...], x1b))
    biased = scores + rb_ref[...]
    per_group = N_EXPERTS // N_EXPERT_GROUPS
    row_g = lax.broadcasted_iota(jnp.int32, (per_group, tm), 0)
    row_8 = lax.broadcasted_iota(jnp.int32, (N_EXPERT_GROUPS, tm), 0)
    row_e = lax.broadcasted_iota(jnp.int32, (N_EXPERTS, tm), 0)
    cur = jnp.full((N_EXPERT_GROUPS, tm), NEG_INF, F32)
    for g in range(N_EXPERT_GROUPS):
        blk = biased[g * per_group:(g + 1) * per_group]
        m_a, i_a = _top1_rows(blk, row_g)
        m_b = jnp.max(jnp.where(row_g == i_a, NEG_INF, blk), axis=0, keepdims=True)
        cur = jnp.where(row_8 == g, m_a + m_b, cur)
    chosen = jnp.zeros((N_EXPERT_GROUPS, tm), F32)
    for _ in range(TOPK_GROUPS):
        _, i_g = _top1_rows(cur, row_8)
        sel = row_8 == i_g
        chosen = jnp.where(sel, 1.0, chosen)
        cur = jnp.where(sel, NEG_INF, cur)
    group_of_row = row_e // per_group
    emask = jnp.zeros((N_EXPERTS, tm), F32)
    for g in range(N_EXPERT_GROUPS):
        emask = jnp.where(group_of_row == g, chosen[g:g + 1], emask)
    cur = jnp.where(emask > 0.0, biased, NEG_INF)
    idx = jnp.zeros((TOP_K, tm), jnp.int32)
    gates = jnp.zeros((TOP_K, tm), F32)
    for k in range(TOP_K):
        _, i_e = _top1_rows(cur, row_e)
        sel = row_e == i_e
        gate_k = jnp.sum(jnp.where(sel, scores, 0.0), axis=0, keepdims=True)
        cur = jnp.where(sel, NEG_INF, cur)
        idx = jnp.where(row_8 == k, i_e, idx)
        gates = jnp.where(row_8 == k, gate_k, gates)
    idx_ref[...] = idx
    gate_ref[...] = gates / jnp.sum(gates, axis=0, keepdims=True) * ROUTED_SCALE


def _mid(xp, xs, attn_p, attn_s, sgu_p, sgu_s, wmix, g1, b1, wrt, rb, alpha, tm):
    n_p, d = xp.shape
    t = n_p + xs.shape[0]
    const2 = lambda i: (0, 0)
    return pl.pallas_call(
        functools.partial(_mid_kernel, alpha, n_p // tm),
        out_shape=[jax.ShapeDtypeStruct((t, d), F32),
                   jax.ShapeDtypeStruct((t, d // 2), jnp.uint32),
                   jax.ShapeDtypeStruct((TOP_K, t), jnp.int32),
                   jax.ShapeDtypeStruct((TOP_K, t), F32)],
        grid=(t // tm,),
        in_specs=[*_pool_specs(n_p, tm, d), *_pool_specs(n_p, tm, D_ATTN), *_pool_specs(n_p, tm, D_SGU),
                  pl.BlockSpec(wmix.shape, const2), pl.BlockSpec(g1.shape, const2),
                  pl.BlockSpec(b1.shape, const2), pl.BlockSpec(wrt.shape, const2),
                  pl.BlockSpec(rb.shape, const2)],
        out_specs=[pl.BlockSpec((tm, d), lambda i: (i, 0)),
                   pl.BlockSpec((tm, d // 2), lambda i: (i, 0)),
                   pl.BlockSpec((TOP_K, tm), lambda i: (0, i)),
                   pl.BlockSpec((TOP_K, tm), lambda i: (0, i))],
        compiler_params=_cparams("parallel"),
        name="mix_ln_router",
    )(xp, xs, attn_p, attn_s, sgu_p, sgu_s, wmix, g1, b1, wrt, rb)


def _rank_kernel(idx_ref, rank_ref, counts_ref, run_ref):
    @pl.when(pl.program_id(0) == 0)
    def _():
        run_ref[...] = jnp.zeros_like(run_ref)

    tm = idx_ref.shape[1]
    row_e = lax.broadcasted_iota(jnp.int32, (N_EXPERTS, tm), 0)
    idx = idx_ref[...]
    onehot = jnp.zeros((N_EXPERTS, tm), F32)
    for k in range(TOP_K):
        onehot = onehot + jnp.where(row_e == idx[k:k + 1], 1.0, 0.0)
    earlier = (lax.broadcasted_iota(jnp.int32, (tm, tm), 0)
               < lax.broadcasted_iota(jnp.int32, (tm, tm), 1))
    before = run_ref[...] + _dot(onehot.astype(BF16), jnp.where(earlier, 1.0, 0.0).astype(BF16))
    row_k = lax.broadcasted_iota(jnp.int32, (TOP_K, tm), 0)
    ranks = jnp.zeros((TOP_K, tm), F32)
    for k in range(TOP_K):
        rank_k = jnp.sum(jnp.where(row_e == idx[k:k + 1], before, 0.0), axis=0, keepdims=True)
        ranks = jnp.where(row_k == k, rank_k, ranks)
    rank_ref[...] = ranks.astype(jnp.int32)
    run_ref[...] = run_ref[...] + jnp.sum(onehot, axis=1, keepdims=True)
    counts_ref[...] = run_ref[...]


def _rank(idx, tm):
    t = idx.shape[1]
    return pl.pallas_call(
        _rank_kernel,
        out_shape=[jax.ShapeDtypeStruct((TOP_K, t), jnp.int32),
                   jax.ShapeDtypeStruct((N_EXPERTS, 1), F32)],
        grid=(t // tm,),
        in_specs=[pl.BlockSpec((TOP_K, tm), lambda i: (0, i))],
        out_specs=[pl.BlockSpec((TOP_K, tm), lambda i: (0, i)),
                   pl.BlockSpec((N_EXPERTS, 1), lambda i: (0, 0))],
        scratch_shapes=[pltpu.VMEM((N_EXPERTS, 1), F32)],
        compiler_params=_cparams("arbitrary"),
        name="expert_rank",
    )(idx)


def _dest_kernel(idx_ref, rank_ref, pstart_ref, dest_ref):
    tm = idx_ref.shape[1]
    row_e = lax.broadcasted_iota(jnp.int32, (N_EXPERTS, tm), 0)
    row_k = lax.broadcasted_iota(jnp.int32, (TOP_K, tm), 0)
    idx = idx_ref[...]
    start = jnp.zeros((TOP_K, tm), F32)
    for k in range(TOP_K):
        start_k = jnp.sum(jnp.where(row_e == idx[k:k + 1], pstart_ref[...], 0.0), axis=0, keepdims=True)
        start = jnp.where(row_k == k, start_k, start)
    dest = start.astype(jnp.int32) + rank_ref[...]
    for c in range(tm // SC_CHUNK):
        dest_ref[c] = dest[:, c * SC_CHUNK:(c + 1) * SC_CHUNK]


def _dest(idx, rank, pstart_col, tm):
    t = idx.shape[1]
    tok_spec = pl.BlockSpec((TOP_K, tm), lambda i: (0, i))
    return pl.pallas_call(
        _dest_kernel,
        out_shape=jax.ShapeDtypeStruct((t // SC_CHUNK, TOP_K, SC_CHUNK), jnp.int32),
        grid=(t // tm,),
        in_specs=[tok_spec, tok_spec, pl.BlockSpec(pstart_col.shape, lambda i: (0, 0))],
        out_specs=pl.BlockSpec((tm // SC_CHUNK, TOP_K, SC_CHUNK), lambda i: (i, 0, 0)),
        compiler_params=_cparams("parallel"),
        name="moe_dest",
    )(idx, rank, pstart_col)


def _sc_mesh():
    return plsc.VectorSubcoreMesh(core_axis_name="c", subcore_axis_name="s",
                                  num_cores=SC_CORES, num_subcores=SC_SUBCORES)


def _sc_chunks(n_chunks):
    workers = SC_CORES * SC_SUBCORES
    wid = lax.axis_index("s") * SC_CORES + lax.axis_index("c")
    return wid, workers, (n_chunks - wid + workers - 1) // workers


def _sc_dispatch(x_rows, dest, n_rows):
    n_chunks = dest.shape[0]
    width = x_rows.shape[1]

    @functools.partial(
        pl.kernel, mesh=_sc_mesh(),
        out_type=jax.ShapeDtypeStruct((n_rows, width), x_rows.dtype),
        scratch_types=[pltpu.VMEM((TOP_K, SC_CHUNK), jnp.int32), pltpu.VMEM((SC_CHUNK, width), x_rows.dtype),
                       pltpu.SemaphoreType.DMA],
    )
    def scatter(x_hbm, dest_hbm, out_hbm, dest_v, rows_v, sem):
        wid, workers, n_own = _sc_chunks(n_chunks)

        @pl.loop(0, n_own)
        def _(j):
            ch = wid + j * workers
            pltpu.sync_copy(dest_hbm.at[ch], dest_v)
            pltpu.sync_copy(x_hbm.at[pl.ds(ch * SC_CHUNK, SC_CHUNK)], rows_v)
            copies = [pltpu.async_copy(rows_v, out_hbm.at[dest_v.at[k]], sem) for k in range(TOP_K)]
            for cp in copies:
                cp.wait()

    return scatter(x_rows, dest)


def _expert_kernel(first_ref, nblk_ref, count_ref, nused_ref, xs_hbm, wg_ref, wu_ref, wd_ref, ys_hbm,
                   xbuf, ybuf, xsem, ysem, wg_b, wu_b, wd_b):
    e = pl.program_id(0)
    n_used = nused_ref[0]
    half = xbuf.shape[2]

    def block_rows(j):
        return pl.ds(pl.multiple_of(j * EXPERT_ROWS, EXPERT_ROWS), EXPERT_ROWS)

    def x_copy(j, slot):
        return pltpu.make_async_copy(xs_hbm.at[block_rows(j)], xbuf.at[slot], xsem.at[slot])

    def y_copy(j, slot):
        return pltpu.make_async_copy(ybuf.at[slot], ys_hbm.at[block_rows(j)], ysem.at[slot])

    @pl.when(e == 0)
    def _():
        x_copy(0, 0).start()

    @pl.when(nblk_ref[e] > 0)
    def _():
        wg_b[...] = wg_ref[0].astype(BF16)
        wu_b[...] = wu_ref[0].astype(BF16)
        wd_b[...] = wd_ref[0].astype(BF16)

    def block(b, carry):
        j = first_ref[e] + b
        slot = j % 2
        x_copy(j, slot).wait()

        @pl.when(j + 1 < n_used)
        def _():
            x_copy(j + 1, 1 - slot).start()

        @pl.when(j >= 2)
        def _():
            y_copy(j - 2, slot).wait()

        row = lax.broadcasted_iota(jnp.int32, (EXPERT_ROWS, half), 0)
        packed = jnp.where(row < count_ref[e] - b * EXPERT_ROWS, xbuf[slot], jnp.uint32(0))
        lo, hi = _unpack_bf16_pairs(packed)
        g = _dot(lo, wg_b[:half]) + _dot(hi, wg_b[half:])
        u = _dot(lo, wu_b[:half]) + _dot(hi, wu_b[half:])
        h = (g * _sigmoid(g) * u).astype(BF16)
        ybuf[slot] = _pack_bf16_pairs(_dot(h, wd_b[...]).astype(BF16))
        y_copy(j, slot).start()
        return carry

    lax.fori_loop(0, nblk_ref[e], block, 0)

    @pl.when(e == pl.num_programs(0) - 1)
    def _():
        @pl.when(n_used >= 2)
        def _():
            y_copy(n_used - 2, (n_used - 2) % 2).wait()

        y_copy(n_used - 1, (n_used - 1) % 2).wait()


def _experts(first_block, n_block, counts, n_used, xs, wg, wu, wd):
    n_rows, half = xs.shape
    n_exp, d, de = wg.shape
    w_map = lambda e, *_: (e, 0, 0)
    return pl.pallas_call(
        _expert_kernel,
        out_shape=jax.ShapeDtypeStruct((n_rows, half), jnp.uint32),
        grid_spec=pltpu.PrefetchScalarGridSpec(
            num_scalar_prefetch=4,
            grid=(n_exp,),
            in_specs=[pl.BlockSpec(memory_space=pl.ANY),
                      pl.BlockSpec((1, d, de), w_map), pl.BlockSpec((1, d, de), w_map),
                      pl.BlockSpec((1, de, d), w_map)],
            out_specs=pl.BlockSpec(memory_space=pl.ANY),
            scratch_shapes=[pltpu.VMEM((2, EXPERT_ROWS, half), jnp.uint32),
                            pltpu.VMEM((2, EXPERT_ROWS, half), jnp.uint32),
                            pltpu.SemaphoreType.DMA((2,)), pltpu.SemaphoreType.DMA((2,)),
                            pltpu.VMEM((d, de), BF16), pltpu.VMEM((d, de), BF16), pltpu.VMEM((de, d), BF16)]),
        compiler_params=_cparams("arbitrary"),
        name="routed_experts",
    )(first_block, n_block, counts, n_used, xs, wg, wu, wd)


def _dense_kernel(alpha, tiles_p, x1_ref, pp_ref, ps_ref, wgs_ref, wus_ref, wds_ref, wpg_ref, wpp_ref, base_ref):
    x1 = x1_ref[...]
    xb = x1.astype(BF16)
    g = _dot(xb, wgs_ref[...])
    h = (g * _sigmoid(g) * _dot(xb, wus_ref[...])).astype(BF16)
    shared = _dot(h, wds_ref[...])
    ple = _sigmoid(_dot(xb, wpg_ref[...])) * _dot(_pool_rows(tiles_p, pp_ref, ps_ref, BF16), wpp_ref[...])
    base_ref[...] = alpha * x1 + shared + ple


def _dense(x1, pp, ps, wgs, wus, wds, wpg, wpp, alpha, tm):
    t, d = x1.shape
    const2 = lambda i: (0, 0)
    return pl.pallas_call(
        functools.partial(_dense_kernel, alpha, pp.shape[0] // tm),
        out_shape=jax.ShapeDtypeStruct((t, d), F32),
        grid=(t // tm,),
        in_specs=[pl.BlockSpec((tm, d), lambda i: (i, 0)), *_pool_specs(pp.shape[0], tm, pp.shape[1]),
                  pl.BlockSpec(wgs.shape, const2), pl.BlockSpec(wus.shape, const2),
                  pl.BlockSpec(wds.shape, const2), pl.BlockSpec(wpg.shape, const2),
                  pl.BlockSpec(wpp.shape, const2)],
        out_specs=pl.BlockSpec((tm, d), lambda i: (i, 0)),
        compiler_params=_cparams("parallel"),
        name="shared_ple",
    )(x1, pp, ps, wgs, wus, wds, wpg, wpp)


def _sc_gather(y_rows, dest, n_tok):
    n_chunks = dest.shape[0]
    width = y_rows.shape[1]
    items = [(k, h) for k in range(TOP_K) for h in range(SC_CHUNK // SC_GATHER_ROWS)]

    @functools.partial(
        pl.kernel, mesh=_sc_mesh(),
        out_type=jax.ShapeDtypeStruct((TOP_K, n_tok, width), y_rows.dtype),
        scratch_types=[pltpu.VMEM((TOP_K, SC_CHUNK), jnp.int32),
                       pltpu.VMEM((2, SC_GATHER_ROWS, width), y_rows.dtype),
                       pltpu.SemaphoreType.DMA, pltpu.SemaphoreType.DMA],
    )
    def gather(y_hbm, dest_hbm, out_hbm, dest_v, rows_v, gsem, wsem):
        wid, workers, n_own = _sc_chunks(n_chunks)

        @pl.loop(0, n_own)
        def _(j):
            ch = wid + j * workers
            pltpu.sync_copy(dest_hbm.at[ch], dest_v)

            def fetch(n):
                k, h = items[n]
                return pltpu.async_copy(y_hbm.at[dest_v.at[k, pl.ds(h * SC_GATHER_ROWS, SC_GATHER_ROWS)]],
                                        rows_v.at[n % 2], gsem)

            def write(n):
                k, h = items[n]
                rows = pl.ds(ch * SC_CHUNK + h * SC_GATHER_ROWS, SC_GATHER_ROWS)
                return pltpu.async_copy(rows_v.at[n % 2], out_hbm.at[k, rows], wsem)

            pending_fetch = fetch(0)
            pending_write = None
            for n in range(len(items)):
                pending_fetch.wait()
                if pending_write is not None:
                    pending_write.wait()
                if n + 1 < len(items):
                    pending_fetch = fetch(n + 1)
                pending_write = write(n)
            pending_write.wait()

    return gather(y_rows, dest)


def _combine_kernel(tiles_p, gate_ref, base_ref, yk_ref, g2_ref, b2_ref, outp_ref, outs_ref):
    gates = gate_ref[...]
    half = yk_ref.shape[2]
    acc_lo = base_ref[:, :half]
    acc_hi = base_ref[:, half:]
    for k in range(TOP_K):
        packed = yk_ref[k]
        gate = gates[:, k:k + 1]
        acc_lo = acc_lo + gate * lax.bitcast_convert_type(packed << 16, F32)
        acc_hi = acc_hi + gate * lax.bitcast_convert_type(packed & jnp.uint32(0xFFFF0000), F32)
    out = _layer_norm(jnp.concatenate([acc_lo, acc_hi], axis=1), g2_ref[...], b2_ref[...])
    is_prompt = pl.program_id(0) < tiles_p

    @pl.when(is_prompt)
    def _():
        outp_ref[...] = out

    @pl.when(jnp.logical_not(is_prompt))
    def _():
        outs_ref[...] = out


def _combine(gates_tok, base, yk, g2, b2, n_prompt, tm):
    t, d = base.shape
    const2 = lambda i: (0, 0)
    return pl.pallas_call(
        functools.partial(_combine_kernel, n_prompt // tm),
        out_shape=[jax.ShapeDtypeStruct((n_prompt, d), F32), jax.ShapeDtypeStruct((t - n_prompt, d), F32)],
        grid=(t // tm,),
        in_specs=[pl.BlockSpec((tm, TOP_K), lambda i: (i, 0)),
                  pl.BlockSpec((tm, d), lambda i: (i, 0)),
                  pl.BlockSpec((TOP_K, tm, yk.shape[2]), lambda i: (0, i, 0)),
                  pl.BlockSpec(g2.shape, const2), pl.BlockSpec(b2.shape, const2)],
        out_specs=list(_pool_specs(n_prompt, tm, d)),
        compiler_params=_cparams("arbitrary"),
        name="moe_combine",
    )(gates_tok, base, yk, g2, b2)


def _sgu_tables(sgu_w, sgu_b, rows_per_seq):
    reps = CHUNK // rows_per_seq
    tril = jnp.tril(sgu_w[:, :rows_per_seq, :rows_per_seq])
    eye = jnp.eye(reps, dtype=F32)
    mix = jnp.einsum("ab,gts->gatbs", eye, tril).reshape(N_GROUPS_SGU, CHUNK, CHUNK)
    bias = jnp.tile(jnp.repeat(sgu_b[:, :rows_per_seq].T, D_SGU // N_GROUPS_SGU, axis=1), (reps, 1))
    return mix.astype(BF16), bias


def _layer(xp, xs, ck, cv, pp, ps, w, rel_bias, alpha):
    batch, seq, d = xp.shape
    dec_b, dec_t, _ = xs.shape
    w_buf = ck.shape[1]
    n_p, n_s = batch * seq, dec_b * dec_t
    n_tok = n_p + n_s

    scale = jnp.concatenate([jnp.full((D_ATTN,), HEAD_DIM ** -0.5, F32),
                             jnp.ones((w["w_in"].shape[1] - D_ATTN,), F32)])
    w_in = (w["w_in"] * scale).astype(BF16)

    qp, kp, vp, up, gp = _proj(xp.reshape(n_p, d), w_in, 512)
    qs, ks, vs, us, gs = _proj(xs.reshape(n_s, d), w_in, 512)

    b1, b4, b16 = _prompt_bias_tables(rel_bias)
    attn_p = _attn_prompt(qp, kp, vp, b1, b4, b16, batch, seq)

    attn_s = _sample_attention(rel_bias, qs.reshape(dec_b, dec_t, D_ATTN), ks.reshape(dec_b, dec_t, D_ATTN),
                               vs.reshape(dec_b, dec_t, D_ATTN), ck, cv).reshape(n_s, D_ATTN)

    ln_g, ln_b = w["sgu_ln_g"][None], w["sgu_ln_b"][None]
    mix_p, bias_p = _sgu_tables(w["sgu_w"], w["sgu_b"], CHUNK)
    mix_s, bias_s = _sgu_tables(w["sgu_w"], w["sgu_b"], dec_t)
    (sgu_p,) = _sgu(up, gp, mix_p, bias_p, ln_g, ln_b, False, 4)
    sgu_s, z2_s = _sgu(us, gs, mix_s, bias_s, ln_g, ln_b, True, 4)

    x1, x1_packed, idx, gates = _mid(xp.reshape(n_p, d), xs.reshape(n_s, d), attn_p, attn_s, sgu_p, sgu_s,
                                     w["w_mix_out"].astype(BF16), w["ln1_g"][None], w["ln1_b"][None],
                                     w["w_router"].T.astype(BF16), w["router_bias"][:, None], alpha, 512)

    rank, counts = _rank(idx, 512)
    counts = counts[:, 0].astype(jnp.int32)
    padded = (counts + EXPERT_ROWS - 1) // EXPERT_ROWS * EXPERT_ROWS
    pend = jnp.cumsum(padded)
    pstart = (pend - padded).astype(jnp.int32)
    n_blocks = (n_tok * TOP_K + N_EXPERTS * (EXPERT_ROWS - 1)) // EXPERT_ROWS
    n_used = (pend[-1:] // EXPERT_ROWS).astype(jnp.int32)

    dest = _dest(idx, rank, pstart.astype(F32)[:, None], 512)
    x_sorted = _sc_dispatch(x1_packed, dest, n_blocks * EXPERT_ROWS)
    y_sorted = _experts(pstart // EXPERT_ROWS, padded // EXPERT_ROWS, counts, n_used, x_sorted,
                        w["w_gate_e"], w["w_up_e"], w["w_down_e"])

    base = _dense(x1, pp.reshape(n_p, -1), ps.reshape(n_s, -1), w["w_gate_s"].astype(BF16), w["w_up_s"].astype(BF16),
                  w["w_down_s"].astype(BF16), w["w_ple_gate"].astype(BF16),
                  w["w_ple_proj"].astype(BF16), alpha, 512)

    y_slots = _sc_gather(y_sorted, dest, n_tok)
    y_p, y_s = _combine(gates.T, base, y_slots, w["ln2_g"][None], w["ln2_b"][None], n_p, 256)
    y_p = y_p.reshape(batch, seq, d)
    y_s = y_s.reshape(dec_b, dec_t, d)
    keep = min(MAX_DISTANCE, seq)
    k_rows = kp.reshape(batch, seq, N_HEADS, HEAD_DIM)[:, seq - keep:]
    v_rows = vp.reshape(batch, seq, N_HEADS, HEAD_DIM)[:, seq - keep:]
    return (y_p, y_s, k_rows, v_rows,
            ks.reshape(dec_b, dec_t, N_HEADS, HEAD_DIM), vs.reshape(dec_b, dec_t, N_HEADS, HEAD_DIM),
            z2_s.reshape(dec_b, dec_t, D_SGU))


def kernel(x_prompt, x_sample, cache_k, cache_v, p_prompt, p_sample, w_in, rel_bias, sgu_w, sgu_b, sgu_ln_g, sgu_ln_b, w_mix_out, ln1_g, ln1_b, w_router, router_bias, w_gate_e, w_up_e, w_down_e, w_gate_s, w_up_s, w_down_s, w_ple_gate, w_ple_proj, ln2_g, ln2_b):
    depth = w_in.shape[0]
    alpha = (2 * depth) ** 0.25
    xp, xs = x_prompt, x_sample
    outs = [[] for _ in range(5)]
    for i in range(depth):
        w = {"w_in": w_in[i], "sgu_w": sgu_w[i], "sgu_b": sgu_b[i], "sgu_ln_g": sgu_ln_g[i],
             "sgu_ln_b": sgu_ln_b[i], "w_mix_out": w_mix_out[i], "ln1_g": ln1_g[i], "ln1_b": ln1_b[i],
             "w_router": w_router[i], "router_bias": router_bias[i], "w_gate_e": w_gate_e[i],
             "w_up_e": w_up_e[i], "w_down_e": w_down_e[i], "w_gate_s": w_gate_s[i], "w_up_s": w_up_s[i],
             "w_down_s": w_down_s[i], "w_ple_gate": w_ple_gate[i], "w_ple_proj": w_ple_proj[i],
             "ln2_g": ln2_g[i], "ln2_b": ln2_b[i]}
        xp, xs, kp, vp, ks, vs, zs = _layer(xp, xs, cache_k[i], cache_v[i], p_prompt[i], p_sample[i],
                                            w, rel_bias, alpha)
        for lst, val in zip(outs, (kp, vp, ks, vs, zs)):
            lst.append(val)
    return (xp, xs) + tuple(jnp.stack(lst) for lst in outs)
```

```python
import functools
import math

import numpy as np
import jax
import jax.numpy as jnp
from jax import lax
from jax.experimental import pallas as pl
from jax.experimental.pallas import tpu as pltpu
from jax.experimental.pallas import tpu_sc as plsc

F32 = jnp.float32
BF16 = jnp.bfloat16
NEG_INF = float("-inf")

N_HEADS = 8
HEAD_DIM = 64
D_ATTN = N_HEADS * HEAD_DIM
PATTERNS = ((128, 1), (512, 4), (2048, 16))
BAND = 128
N_BUCKETS = 32
MAX_DISTANCE = 2048
N_GROUPS_SGU = 8
D_SGU = 512
CHUNK = 128
N_EXPERTS = 256
TOP_K = 8
N_EXPERT_GROUPS = 8
TOPK_GROUPS = 4
ROUTED_SCALE = 2.5
LN_EPS = 1e-5
EXPERT_ROWS = 256
EXPERT_SLOTS = 4

LANES = 128
SC_CORES = 2
SC_SUBCORES = 16
SC_CHUNK = 128
SC_GATHER_ROWS = 64
VMEM_LIMIT = 56 * 1024 * 1024


def _cparams(*sem):
    return pltpu.CompilerParams(dimension_semantics=sem, vmem_limit_bytes=VMEM_LIMIT)


def _layer_norm(x, g, b):
    mu = jnp.mean(x, axis=-1, keepdims=True)
    xc = x - mu
    var = jnp.mean(xc * xc, axis=-1, keepdims=True)
    return xc * lax.rsqrt(var + LN_EPS) * g + b


def _sigmoid(x):
    return 1.0 / (1.0 + jnp.exp(-x))


def _gelu(x):
    return 0.5 * x * (1.0 + lax.erf(x * math.sqrt(0.5)))


def _pack_bf16_pairs(xb):
    n = xb.shape[1] // 2
    bits = lax.bitcast_convert_type(xb.astype(F32), jnp.uint32)
    return (bits[:, :n] >> 16) | (bits[:, n:] & jnp.uint32(0xFFFF0000))


def _unpack_bf16_pairs(p):
    lo = lax.bitcast_convert_type(p << 16, F32).astype(BF16)
    hi = lax.bitcast_convert_type(p & jnp.uint32(0xFFFF0000), F32).astype(BF16)
    return lo, hi


def _dot(a, b):
    return jnp.dot(a, b, preferred_element_type=F32)


def _dot_nt(a, b):
    return lax.dot_general(a, b, (((1,), (1,)), ((), ())), preferred_element_type=F32)


def _t5_bucket_np(dist):
    max_exact = N_BUCKETS // 2
    df = np.maximum(dist, max_exact).astype(np.float32)
    large = max_exact + (np.log(df / np.float32(max_exact)) / np.float32(math.log(MAX_DISTANCE / max_exact))
                         * np.float32(N_BUCKETS - max_exact)).astype(np.int32)
    return np.where(dist < max_exact, dist, np.minimum(large, N_BUCKETS - 1)).astype(np.int32)


def _band_bucket_table(dilation):
    qi = np.arange(BAND)[:, None]
    ki = np.arange(2 * BAND)[None, :]
    dsub = qi + BAND - ki
    valid = (dsub >= 0) & (dsub <= BAND)
    return np.where(valid, _t5_bucket_np(np.clip(dsub, 0, BAND) * dilation), -1).astype(np.int32)


def _cache_rows(w_buf, t_len):
    d_far = PATTERNS[2][1]
    assert w_buf % d_far == 0 and t_len <= d_far and w_buf >= PATTERNS[2][0]
    far = (np.arange(w_buf // d_far)[:, None] * d_far + np.arange(t_len)[None, :]).reshape(-1)
    near = np.arange(w_buf - PATTERNS[1][0], w_buf)
    return far, near


def _sample_bucket_tables(w_buf, t_len):
    far, near = _cache_rows(w_buf, t_len)
    t = np.arange(t_len)[:, None]

    def table(rows, window, dilation):
        d = w_buf + t - rows[None, :]
        ok = (d >= 0) & (d % dilation == 0) & (d <= window)
        return np.where(ok, _t5_bucket_np(np.maximum(d, 0)), -1).astype(np.int32)

    pieces = (near[-PATTERNS[0][0]:], near, far)
    tables = []
    for (window, dilation), rows in zip(PATTERNS, pieces):
        full = table(np.arange(w_buf), window, dilation)
        outside = np.ones(w_buf, bool)
        outside[rows] = False
        assert not (full[:, outside] >= 0).any()
        tables.append(np.repeat(table(rows, window, dilation), N_HEADS, axis=1))
    new_rows = w_buf + np.arange(t_len)
    new = np.stack([np.repeat(table(new_rows, window, dilation), N_HEADS, axis=1) for window, dilation in PATTERNS])
    return tables, new


def _bias_kernel(head_cols, rb_ref, bucket_ref, out_ref):
    bucket = bucket_ref[...]
    col_head = lax.broadcasted_iota(jnp.int32, bucket.shape, 1) % N_HEADS
    for h in range(N_HEADS):
        acc = jnp.full(bucket.shape, NEG_INF, F32)
        for b in range(N_BUCKETS):
            acc = jnp.where(bucket == b, rb_ref[b, h], acc)
        out_ref[h] = jnp.where(col_head == h, acc, NEG_INF) if head_cols else acc


def _bias_table(rel_bias, bucket_np, head_cols=False):
    r, c = bucket_np.shape
    return pl.pallas_call(
        functools.partial(_bias_kernel, head_cols),
        out_shape=jax.ShapeDtypeStruct((N_HEADS, r, c), F32),
        in_specs=[pl.BlockSpec(memory_space=pltpu.SMEM), pl.BlockSpec(memory_space=pltpu.VMEM)],
        out_specs=pl.BlockSpec(memory_space=pltpu.VMEM),
        name="bias_table",
    )(rel_bias, jnp.asarray(bucket_np))


def _proj_kernel(x_ref, w_ref, q_ref, k_ref, v_ref, u_ref, g_ref):
    x = x_ref[...].astype(BF16)
    col = 0
    for o in (q_ref, k_ref, v_ref, u_ref, g_ref):
        n = o.shape[1]
        o[...] = _dot(x, w_ref[:, col:col + n])
        col += n


def _proj(x, w, tm):
    m, d = x.shape
    n_out = (D_ATTN, D_ATTN, D_ATTN, D_SGU, D_SGU)
    return pl.pallas_call(
        _proj_kernel,
        out_shape=[jax.ShapeDtypeStruct((m, n), F32) for n in n_out],
        grid=(m // tm,),
        in_specs=[pl.BlockSpec((tm, d), lambda i: (i, 0)), pl.BlockSpec(w.shape, lambda i: (0, 0))],
        out_specs=[pl.BlockSpec((tm, n), lambda i: (i, 0)) for n in n_out],
        compiler_params=_cparams("parallel"),
        name="in_proj",
    )(x, w)


def _band_attn(q, k, v, bias, even):
    q2 = jnp.concatenate([jnp.where(even, q, 0.0), jnp.where(even, 0.0, q)], axis=0).astype(BF16)
    s = _dot_nt(q2, k.astype(BF16)) + bias
    m = jnp.max(s, axis=-1, keepdims=True)
    p = jnp.exp(s - m)
    l = jnp.sum(p, axis=-1, keepdims=True)
    pv = _dot(p.astype(BF16), v.astype(BF16))
    return (jnp.where(even, m[:BAND], m[BAND:]), jnp.where(even, l[:BAND], l[BAND:]),
            jnp.where(even, pv[:BAND], pv[BAND:]))


def _attn_prompt_kernel(q_ref, k_ref, v_ref, b1_ref, b4_ref, b16_ref, o_ref,
                        m1, l1, a1, m4, l4, a4, m16, l16, a16, qc, kc, vc):
    seq = q_ref.shape[0]
    n4 = seq // 4
    per_trip = seq // BAND // 4
    even = lax.broadcasted_iota(jnp.int32, (BAND, LANES), 1) < HEAD_DIM

    def store_all(refs, rows_list, results):
        for rows, (m, l, a) in zip(rows_list, results):
            refs[0][rows, :], refs[1][rows, :], refs[2][rows, :] = m, l, a

    def body(r, carry):
        rows1, loaded1 = [], []
        for j in range(per_trip):
            i = r * per_trip + j
            r0 = pl.multiple_of(i * BAND, BAND)
            k0 = pl.multiple_of(jnp.maximum(i - 1, 0) * BAND, BAND)
            rows1.append(pl.ds(r0, BAND))
            loaded1.append((q_ref[pl.ds(r0, BAND)], k_ref[pl.ds(k0, 2 * BAND)], v_ref[pl.ds(k0, 2 * BAND)],
                            b1_ref[jnp.where(i == 0, 1, 0)]))
        base = pl.multiple_of(r * n4, BAND)
        q4 = q_ref[pl.ds(r, n4, stride=4), :]
        k4 = k_ref[pl.ds(r, n4, stride=4), :]
        v4 = v_ref[pl.ds(r, n4, stride=4), :]
        qc[pl.ds(base, n4)], kc[pl.ds(base, n4)], vc[pl.ds(base, n4)] = q4, k4, v4
        rows16 = [pl.ds(r * n4 + s, seq // 16, stride=4) for s in range(4)]
        loaded16 = [(qc[rows, :], kc[rows, :], vc[rows, :]) for rows in rows16]

        results1 = [_band_attn(q, k, v, bias, even) for q, k, v, bias in loaded1]
        results4, rows4 = [], []
        for i in range(n4 // BAND):
            lo = max(i - 1, 0) * BAND
            hi = (i + 1) * BAND
            col = 0 if i > 0 else BAND
            results4.append(_band_attn(q4[i * BAND:hi], k4[lo:hi], v4[lo:hi], b4_ref[:, col:], even))
            rows4.append(pl.ds(base + i * BAND, BAND))
        results16 = [_band_attn(q, k, v, b16_ref[:, BAND:], even) for q, k, v in loaded16]

        store_all((m1, l1, a1), rows1, results1)
        store_all((m4, l4, a4), rows4, results4)
        store_all((m16, l16, a16), rows16, results16)
        return carry

    lax.fori_loop(0, 4, body, 0)

    def merge_body(i, carry):
        r = i // (n4 // BAND)
        c = i % (n4 // BAND)
        rows = pl.ds(pl.multiple_of(i * BAND, BAND), BAND)
        nat = pl.ds(r + 4 * BAND * c, BAND, stride=4)
        ma, mb, mc = m1[nat, :], m4[rows], m16[rows]
        mx = jnp.maximum(jnp.maximum(ma, mb), mc)
        wa, wb, wc = jnp.exp(ma - mx), jnp.exp(mb - mx), jnp.exp(mc - mx)
        num = wa * a1[nat, :] + wb * a4[rows] + wc * a16[rows]
        den = wa * l1[nat, :] + wb * l4[rows] + wc * l16[rows]
        o_ref[nat, :] = num / den
        return carry

    lax.fori_loop(0, seq // BAND, merge_body, 0)


def _prompt_bias_tables(rel_bias):
    pairs = N_HEADS // 2
    first = _band_bucket_table(1)
    first = np.concatenate([first[:, BAND:], np.full((BAND, BAND), -1, np.int32)], axis=1)
    b1 = jnp.stack([_bias_table(rel_bias, _band_bucket_table(1)).reshape(pairs, 2 * BAND, 2 * BAND),
                    _bias_table(rel_bias, first).reshape(pairs, 2 * BAND, 2 * BAND)], axis=1)
    b4 = _bias_table(rel_bias, _band_bucket_table(4)).reshape(pairs, 2 * BAND, 2 * BAND)
    b16 = _bias_table(rel_bias, _band_bucket_table(16)).reshape(pairs, 2 * BAND, 2 * BAND)
    return b1, b4, b16


def _attn_prompt(q, k, v, b1, b4, b16, batch, seq):
    blk = pl.BlockSpec((seq, LANES), lambda b, j: (b, j))
    bias_spec = pl.BlockSpec((None, 2 * BAND, 2 * BAND), lambda b, j: (j, 0, 0))
    return pl.pallas_call(
        _attn_prompt_kernel,
        out_shape=jax.ShapeDtypeStruct(q.shape, F32),
        grid=(batch, D_ATTN // LANES),
        in_specs=[blk, blk, blk, pl.BlockSpec((None, 2, 2 * BAND, 2 * BAND), lambda b, j: (j, 0, 0, 0)),
                  bias_spec, bias_spec],
        out_specs=blk,
        scratch_shapes=[pltpu.VMEM((seq, LANES), F32) for _ in range(12)],
        compiler_params=_cparams("parallel", "parallel"),
        name="attn_prompt",
    )(q, k, v, b1, b4, b16)


def _attn_sample_kernel(q_ref, kn_ref, vn_ref, kfar_ref, vfar_ref, knear_ref, vnear_ref,
                        b1_ref, b4_ref, b16_ref, bn_ref, o_ref):
    def keys(ref):
        return ref[0].reshape(-1, HEAD_DIM).astype(BF16)

    q = q_ref[0].astype(BF16)
    kn, vn = keys(kn_ref), keys(vn_ref)
    vfar, vnear = keys(vfar_ref), keys(vnear_ref)
    s_far = _dot_nt(q, keys(kfar_ref))
    s_near = _dot_nt(q, keys(knear_ref))
    s_new = _dot_nt(q, kn)
    w1 = b1_ref.shape[1]
    branches = ((s_near[:, -w1:] + b1_ref[...], vnear[-w1:], s_new + bn_ref[0]),
                (s_near + b4_ref[...], vnear, s_new + bn_ref[1]),
                (s_far + b16_ref[...], vfar, s_new + bn_ref[2]))
    ms, ls, accs = [], [], []
    for sc, vals, sn in branches:
        m = jnp.maximum(jnp.max(sc, axis=-1, keepdims=True), jnp.max(sn, axis=-1, keepdims=True))
        pc = jnp.exp(sc - m)
        pn = jnp.exp(sn - m)
        ls.append(jnp.sum(pc, axis=-1, keepdims=True) + jnp.sum(pn, axis=-1, keepdims=True))
        accs.append(_dot(pc.astype(BF16), vals) + _dot(pn.astype(BF16), vn))
        ms.append(m)
    mx = jnp.maximum(jnp.maximum(ms[0], ms[1]), ms[2])
    ws = [jnp.exp(m - mx) for m in ms]
    num = ws[0] * accs[0] + ws[1] * accs[1] + ws[2] * accs[2]
    den = ws[0] * ls[0] + ws[1] * ls[1] + ws[2] * ls[2]
    o_ref[0] = num / den


def _sample_attention(rel_bias, q, kn, vn, ck, cv):
    b, t_len, _ = q.shape
    w_buf = ck.shape[1]
    d_far = PATTERNS[2][1]
    n_near = PATTERNS[1][0]
    assert (w_buf - n_near) % n_near == 0
    rows = N_HEADS * t_len
    tables, new_t = _sample_bucket_tables(w_buf, t_len)
    b1, b4, b16 = (_bias_table(rel_bias, tb, True).reshape(rows, tb.shape[1]) for tb in tables)
    bn = _bias_table(rel_bias, new_t.reshape(3 * t_len, rows), True)
    bn = bn.reshape(N_HEADS, 3, t_len, rows).transpose(1, 0, 2, 3).reshape(3, rows, rows)

    q_rows = q.reshape(b, t_len, N_HEADS, HEAD_DIM).transpose(0, 2, 1, 3).reshape(b, rows, HEAD_DIM)
    new4 = lambda x: x.reshape(b, t_len, N_HEADS, HEAD_DIM)
    far5 = lambda c: c.reshape(b, w_buf // d_far, d_far, N_HEADS, HEAD_DIM)
    q_spec = pl.BlockSpec((1, rows, HEAD_DIM), lambda i: (i, 0, 0))
    new_spec = pl.BlockSpec((1, t_len, N_HEADS, HEAD_DIM), lambda i: (i, 0, 0, 0))
    far_spec = pl.BlockSpec((1, w_buf // d_far, t_len, N_HEADS, HEAD_DIM), lambda i: (i, 0, 0, 0, 0))
    near_spec = pl.BlockSpec((1, n_near, N_HEADS, HEAD_DIM), lambda i: (i, (w_buf - n_near) // n_near, 0, 0))
    const2 = lambda i: (0, 0)
    out = pl.pallas_call(
        _attn_sample_kernel,
        out_shape=jax.ShapeDtypeStruct((b, rows, HEAD_DIM), F32),
        grid=(b,),
        in_specs=[q_spec, new_spec, new_spec, far_spec, far_spec, near_spec, near_spec,
                  pl.BlockSpec(b1.shape, const2), pl.BlockSpec(b4.shape, const2), pl.BlockSpec(b16.shape, const2),
                  pl.BlockSpec(bn.shape, lambda i: (0, 0, 0))],
        out_specs=q_spec,
        compiler_params=_cparams("parallel"),
        name="attn_sample",
    )(q_rows, new4(kn), new4(vn), far5(ck), far5(cv), ck, cv, b1, b4, b16, bn)
    return out.reshape(b, N_HEADS, t_len, HEAD_DIM).transpose(0, 2, 1, 3).reshape(b, t_len, D_ATTN)


def _sgu_kernel(u_ref, g_ref, mix_ref, bias_ref, lng_ref, lnb_ref, sgu_ref, *z2_out):
    n_chunks = u_ref.shape[0] // CHUNK
    group_of_lane = lax.broadcasted_iota(jnp.int32, (CHUNK, D_SGU), 1) // (D_SGU // N_GROUPS_SGU)
    for c in range(n_chunks):
        rows = slice(c * CHUNK, (c + 1) * CHUNK)
        z1 = _gelu(u_ref[rows])
        z2 = _layer_norm(_gelu(g_ref[rows]), lng_ref[...], lnb_ref[...])
        if z2_out:
            z2_out[0][rows] = z2
        mixed = bias_ref[...]
        for g in range(N_GROUPS_SGU):
            mixed = mixed + _dot(mix_ref[g], jnp.where(group_of_lane == g, z2, 0.0).astype(BF16))
        sgu_ref[rows] = (z1 * mixed).astype(sgu_ref.dtype)


def _sgu(u, g, mix, bias, ln_g, ln_b, want_z2, chunks_per_step):
    m = u.shape[0]
    tm = CHUNK * chunks_per_step
    row_spec = pl.BlockSpec((tm, D_SGU), lambda i: (i, 0))
    const2 = lambda i: (0, 0)
    out_shape = [jax.ShapeDtypeStruct((m, D_SGU), BF16)]
    out_specs = [row_spec]
    if want_z2:
        out_shape.append(jax.ShapeDtypeStruct((m, D_SGU), F32))
        out_specs.append(row_spec)
    return pl.pallas_call(
        _sgu_kernel,
        out_shape=out_shape,
        grid=(m // tm,),
        in_specs=[row_spec, row_spec,
                  pl.BlockSpec(mix.shape, lambda i: (0, 0, 0)),
                  pl.BlockSpec(bias.shape, const2),
                  pl.BlockSpec(ln_g.shape, const2), pl.BlockSpec(ln_b.shape, const2)],
        out_specs=out_specs,
        compiler_params=_cparams("parallel"),
        name="sgu",
    )(u, g, mix, bias, ln_g, ln_b)


def _top1_rows(x, row):
    m = jnp.max(x, axis=0, keepdims=True)
    i = jnp.min(jnp.where(x == m, row, x.shape[0]), axis=0, keepdims=True)
    return m, i


def _pool_specs(n_prompt, tm, width):
    tiles_p = n_prompt // tm
    return (pl.BlockSpec((tm, width), lambda i, *_: (jnp.minimum(i, tiles_p - 1), 0)),
            pl.BlockSpec((tm, width), lambda i, *_: (jnp.maximum(i - tiles_p, 0), 0)))


def _pool_rows(tiles_p, prompt_ref, sample_ref, dtype):
    return jnp.where(pl.program_id(0) < tiles_p, prompt_ref[...].astype(dtype), sample_ref[...].astype(dtype))


def _mid_kernel(alpha, tiles_p, xp_ref, xs_ref, ap_ref, as_ref, sp_ref, ss_ref, wmix_ref, g1_ref, b1_ref,
                wrt_ref, rb_ref, x1_ref, x1p_ref, idx_ref, gate_ref):
    y = (_dot(_pool_rows(tiles_p, ap_ref, as_ref, BF16), wmix_ref[0:D_ATTN])
         + _dot(_pool_rows(tiles_p, sp_ref, ss_ref, BF16), wmix_ref[D_ATTN:])
         + alpha * _pool_rows(tiles_p, xp_ref, xs_ref, F32))
    x1 = _layer_norm(y, g1_ref[...], b1_ref[...])
    x1_ref[...] = x1
    tm = x1.shape[0]
    x1b = x1.astype(BF16)
    x1p_ref[...] = _pack_bf16_pairs(x1b)
    scores = _sigmoid(_dot_nt(wrt_ref[...], x1b))
    biased = scores + rb_ref[...]
    per_group = N_EXPERTS // N_EXPERT_GROUPS
    row_g = lax.broadcasted_iota(jnp.int32, (per_group, tm), 0)
    row_8 = lax.broadcasted_iota(jnp.int32, (N_EXPERT_GROUPS, tm), 0)
    row_e = lax.broadcasted_iota(jnp.int32, (N_EXPERTS, tm), 0)
    cur = jnp.full((N_EXPERT_GROUPS, tm), NEG_INF, F32)
    for g in range(N_EXPERT_GROUPS):
        blk = biased[g * per_group:(g + 1) * per_group]
        m_a, i_a = _top1_rows(blk, row_g)
        m_b = jnp.max(jnp.where(row_g == i_a, NEG_INF, blk), axis=0, keepdims=True)
        cur = jnp.where(row_8 == g, m_a + m_b, cur)
    chosen = jnp.zeros((N_EXPERT_GROUPS, tm), F32)
    for _ in range(TOPK_GROUPS):
        _, i_g = _top1_rows(cur, row_8)
        sel = row_8 == i_g
        chosen = jnp.where(sel, 1.0, chosen)
        cur = jnp.where(sel, NEG_INF, cur)
    group_of_row = row_e // per_group
    emask = jnp.zeros((N_EXPERTS, tm), F32)
    for g in range(N_EXPERT_GROUPS):
        emask = jnp.where(group_of_row == g, chosen[g:g + 1], emask)
    cur = jnp.where(emask > 0.0, biased, NEG_INF)
    idx = jnp.zeros((TOP_K, tm), jnp.int32)
    gates = jnp.zeros((TOP_K, tm), F32)
    for k in range(TOP_K):
        _, i_e = _top1_rows(cur, row_e)
        sel = row_e == i_e
        gate_k = jnp.sum(jnp.where(sel, scores, 0.0), axis=0, keepdims=True)
        cur = jnp.where(sel, NEG_INF, cur)
        idx = jnp.where(row_8 == k, i_e, idx)
        gates = jnp.where(row_8 == k, gate_k, gates)
    idx_ref[...] = idx
    gate_ref[...] = gates / jnp.sum(gates, axis=0, keepdims=True) * ROUTED_SCALE


def _mid(xp, xs, attn_p, attn_s, sgu_p, sgu_s, wmix, g1, b1, wrt, rb, alpha, tm):
    n_p, d = xp.shape
    t = n_p + xs.shape[0]
    const2 = lambda i: (0, 0)
    return pl.pallas_call(
        functools.partial(_mid_kernel, alpha, n_p // tm),
        out_shape=[jax.ShapeDtypeStruct((t, d), F32),
                   jax.ShapeDtypeStruct((t, d // 2), jnp.uint32),
                   jax.ShapeDtypeStruct((TOP_K, t), jnp.int32),
                   jax.ShapeDtypeStruct((TOP_K, t), F32)],
        grid=(t // tm,),
        in_specs=[*_pool_specs(n_p, tm, d), *_pool_specs(n_p, tm, D_ATTN), *_pool_specs(n_p, tm, D_SGU),
                  pl.BlockSpec(wmix.shape, const2), pl.BlockSpec(g1.shape, const2),
                  pl.BlockSpec(b1.shape, const2), pl.BlockSpec(wrt.shape, const2),
                  pl.BlockSpec(rb.shape, const2)],
        out_specs=[pl.BlockSpec((tm, d), lambda i: (i, 0)),
                   pl.BlockSpec((tm, d // 2), lambda i: (i, 0)),
                   pl.BlockSpec((TOP_K, tm), lambda i: (0, i)),
                   pl.BlockSpec((TOP_K, tm), lambda i: (0, i))],
        compiler_params=_cparams("parallel"),
        name="mix_ln_router",
    )(xp, xs, attn_p, attn_s, sgu_p, sgu_s, wmix, g1, b1, wrt, rb)


def _rank_kernel(idx_ref, rank_ref, counts_ref, run_ref):
    @pl.when(pl.program_id(0) == 0)
    def _():
        run_ref[...] = jnp.zeros_like(run_ref)

    tm = idx_ref.shape[1]
    row_e = lax.broadcasted_iota(jnp.int32, (N_EXPERTS, tm), 0)
    idx = idx_ref[...]
    onehot = jnp.zeros((N_EXPERTS, tm), F32)
    for k in range(TOP_K):
        onehot = onehot + jnp.where(row_e == idx[k:k + 1], 1.0, 0.0)
    earlier = (lax.broadcasted_iota(jnp.int32, (tm, tm), 0)
               < lax.broadcasted_iota(jnp.int32, (tm, tm), 1))
    before = run_ref[...] + _dot(onehot.astype(BF16), jnp.where(earlier, 1.0, 0.0).astype(BF16))
    row_k = lax.broadcasted_iota(jnp.int32, (TOP_K, tm), 0)
    ranks = jnp.zeros((TOP_K, tm), F32)
    for k in range(TOP_K):
        rank_k = jnp.sum(jnp.where(row_e == idx[k:k + 1], before, 0.0), axis=0, keepdims=True)
        ranks = jnp.where(row_k == k, rank_k, ranks)
    rank_ref[...] = ranks.astype(jnp.int32)
    run_ref[...] = run_ref[...] + jnp.sum(onehot, axis=1, keepdims=True)
    counts_ref[...] = run_ref[...]


def _rank(idx, tm):
    t = idx.shape[1]
    return pl.pallas_call(
        _rank_kernel,
        out_shape=[jax.ShapeDtypeStruct((TOP_K, t), jnp.int32),
                   jax.ShapeDtypeStruct((N_EXPERTS, 1), F32)],
        grid=(t // tm,),
        in_specs=[pl.BlockSpec((TOP_K, tm), lambda i: (0, i))],
        out_specs=[pl.BlockSpec((TOP_K, tm), lambda i: (0, i)),
                   pl.BlockSpec((N_EXPERTS, 1), lambda i: (0, 0))],
        scratch_shapes=[pltpu.VMEM((N_EXPERTS, 1), F32)],
        compiler_params=_cparams("arbitrary"),
        name="expert_rank",
    )(idx)


def _dest_kernel(idx_ref, rank_ref, pstart_ref, dest_ref):
    tm = idx_ref.shape[1]
    row_e = lax.broadcasted_iota(jnp.int32, (N_EXPERTS, tm), 0)
    row_k = lax.broadcasted_iota(jnp.int32, (TOP_K, tm), 0)
    idx = idx_ref[...]
    start = jnp.zeros((TOP_K, tm), F32)
    for k in range(TOP_K):
        start_k = jnp.sum(jnp.where(row_e == idx[k:k + 1], pstart_ref[...], 0.0), axis=0, keepdims=True)
        start = jnp.where(row_k == k, start_k, start)
    dest = start.astype(jnp.int32) + rank_ref[...]
    for c in range(tm // SC_CHUNK):
        dest_ref[c] = dest[:, c * SC_CHUNK:(c + 1) * SC_CHUNK]


def _dest(idx, rank, pstart_col, tm):
    t = idx.shape[1]
    tok_spec = pl.BlockSpec((TOP_K, tm), lambda i: (0, i))
    return pl.pallas_call(
        _dest_kernel,
        out_shape=jax.ShapeDtypeStruct((t // SC_CHUNK, TOP_K, SC_CHUNK), jnp.int32),
        grid=(t // tm,),
        in_specs=[tok_spec, tok_spec, pl.BlockSpec(pstart_col.shape, lambda i: (0, 0))],
        out_specs=pl.BlockSpec((tm // SC_CHUNK, TOP_K, SC_CHUNK), lambda i: (i, 0, 0)),
        compiler_params=_cparams("parallel"),
        name="moe_dest",
    )(idx, rank, pstart_col)


def _sc_mesh():
    return plsc.VectorSubcoreMesh(core_axis_name="c", subcore_axis_name="s",
                                  num_cores=SC_CORES, num_subcores=SC_SUBCORES)


def _sc_chunks(n_chunks):
    workers = SC_CORES * SC_SUBCORES
    wid = lax.axis_index("s") * SC_CORES + lax.axis_index("c")
    return wid, workers, (n_chunks - wid + workers - 1) // workers


def _sc_dispatch(x_rows, dest, n_rows):
    n_chunks = dest.shape[0]
    width = x_rows.shape[1]

    @functools.partial(
        pl.kernel, mesh=_sc_mesh(),
        out_type=jax.ShapeDtypeStruct((n_rows, width), x_rows.dtype),
        scratch_types=[pltpu.VMEM((TOP_K, SC_CHUNK), jnp.int32), pltpu.VMEM((SC_CHUNK, width), x_rows.dtype),
                       pltpu.SemaphoreType.DMA],
    )
    def scatter(x_hbm, dest_hbm, out_hbm, dest_v, rows_v, sem):
        wid, workers, n_own = _sc_chunks(n_chunks)

        @pl.loop(0, n_own)
        def _(j):
            ch = wid + j * workers
            pltpu.sync_copy(dest_hbm.at[ch], dest_v)
            pltpu.sync_copy(x_hbm.at[pl.ds(ch * SC_CHUNK, SC_CHUNK)], rows_v)
            copies = [pltpu.async_copy(rows_v, out_hbm.at[dest_v.at[k]], sem) for k in range(TOP_K)]
            for cp in copies:
                cp.wait()

    return scatter(x_rows, dest)


def _expert_kernel(first_ref, nblk_ref, count_ref, nused_ref, xs_hbm, wg_ref, wu_ref, wd_ref, ys_hbm,
                   xbuf, ybuf, xsem, ysem, wg_b, wu_b, wd_b):
    e = pl.program_id(0)
    n_used = nused_ref[0]
    slots, _, half = xbuf.shape

    def block_rows(j):
        return pl.ds(pl.multiple_of(j * EXPERT_ROWS, EXPERT_ROWS), EXPERT_ROWS)

    def x_copy(j, slot):
        return pltpu.make_async_copy(xs_hbm.at[block_rows(j)], xbuf.at[slot], xsem.at[slot])

    def y_copy(j, slot):
        return pltpu.make_async_copy(ybuf.at[slot], ys_hbm.at[block_rows(j)], ysem.at[slot])

    @pl.when(e == 0)
    def _():
        for k in range(slots - 1):
            @pl.when(k < n_used)
            def _():
                x_copy(k, k).start()

    @pl.when(nblk_ref[e] > 0)
    def _():
        wg_b[...] = wg_ref[0].astype(BF16)
        wu_b[...] = wu_ref[0].astype(BF16)
        wd_b[...] = wd_ref[0].astype(BF16)

    def block(b, carry):
        j = first_ref[e] + b
        slot = j % slots
        x_copy(j, slot).wait()
        ahead = j + slots - 1

        @pl.when(ahead < n_used)
        def _():
            x_copy(ahead, ahead % slots).start()

        @pl.when(j >= slots)
        def _():
            y_copy(j - slots, slot).wait()

        row = lax.broadcasted_iota(jnp.int32, (EXPERT_ROWS, half), 0)
        packed = jnp.where(row < count_ref[e] - b * EXPERT_ROWS, xbuf[slot], jnp.uint32(0))
        lo, hi = _unpack_bf16_pairs(packed)
        g = _dot(lo, wg_b[:half]) + _dot(hi, wg_b[half:])
        u = _dot(lo, wu_b[:half]) + _dot(hi, wu_b[half:])
        h = (g * _sigmoid(g) * u).astype(BF16)
        ybuf[slot] = _pack_bf16_pairs(_dot(h, wd_b[...]).astype(BF16))
        y_copy(j, slot).start()
        return carry

    lax.fori_loop(0, nblk_ref[e], block, 0)

    @pl.when(e == pl.num_programs(0) - 1)
    def _():
        for k in range(1, slots + 1):
            @pl.when(n_used >= k)
            def _():
                y_copy(n_used - k, (n_used - k) % slots).wait()


def _experts(first_block, n_block, counts, n_used, xs, wg, wu, wd):
    n_rows, half = xs.shape
    n_exp, d, de = wg.shape
    w_map = lambda e, *_: (e, 0, 0)
    return pl.pallas_call(
        _expert_kernel,
        out_shape=jax.ShapeDtypeStruct((n_rows, half), jnp.uint32),
        grid_spec=pltpu.PrefetchScalarGridSpec(
            num_scalar_prefetch=4,
            grid=(n_exp,),
            in_specs=[pl.BlockSpec(memory_space=pl.ANY),
                      pl.BlockSpec((1, d, de), w_map), pl.BlockSpec((1, d, de), w_map),
                      pl.BlockSpec((1, de, d), w_map)],
            out_specs=pl.BlockSpec(memory_space=pl.ANY),
            scratch_shapes=[pltpu.VMEM((EXPERT_SLOTS, EXPERT_ROWS, half), jnp.uint32),
                            pltpu.VMEM((EXPERT_SLOTS, EXPERT_ROWS, half), jnp.uint32),
                            pltpu.SemaphoreType.DMA((EXPERT_SLOTS,)), pltpu.SemaphoreType.DMA((EXPERT_SLOTS,)),
                            pltpu.VMEM((d, de), BF16), pltpu.VMEM((d, de), BF16), pltpu.VMEM((de, d), BF16)]),
        compiler_params=_cparams("arbitrary"),
        name="routed_experts",
    )(first_block, n_block, counts, n_used, xs, wg, wu, wd)


def _dense_kernel(alpha, tiles_p, x1_ref, pp_ref, ps_ref, wgs_ref, wus_ref, wds_ref, wpg_ref, wpp_ref, base_ref):
    x1 = x1_ref[...]
    xb = x1.astype(BF16)
    g = _dot(xb, wgs_ref[...])
    h = (g * _sigmoid(g) * _dot(xb, wus_ref[...])).astype(BF16)
    shared = _dot(h, wds_ref[...])
    ple = _sigmoid(_dot(xb, wpg_ref[...])) * _dot(_pool_rows(tiles_p, pp_ref, ps_ref, BF16), wpp_ref[...])
    base_ref[...] = alpha * x1 + shared + ple


def _dense(x1, pp, ps, wgs, wus, wds, wpg, wpp, alpha, tm):
    t, d = x1.shape
    const2 = lambda i: (0, 0)
    return pl.pallas_call(
        functools.partial(_dense_kernel, alpha, pp.shape[0] // tm),
        out_shape=jax.ShapeDtypeStruct((t, d), F32),
        grid=(t // tm,),
        in_specs=[pl.BlockSpec((tm, d), lambda i: (i, 0)), *_pool_specs(pp.shape[0], tm, pp.shape[1]),
                  pl.BlockSpec(wgs.shape, const2), pl.BlockSpec(wus.shape, const2),
                  pl.BlockSpec(wds.shape, const2), pl.BlockSpec(wpg.shape, const2),
                  pl.BlockSpec(wpp.shape, const2)],
        out_specs=pl.BlockSpec((tm, d), lambda i: (i, 0)),
        compiler_params=_cparams("parallel"),
        name="shared_ple",
    )(x1, pp, ps, wgs, wus, wds, wpg, wpp)


def _sc_gather(y_rows, dest, n_tok):
    n_chunks = dest.shape[0]
    width = y_rows.shape[1]
    items = [(k, h) for k in range(TOP_K) for h in range(SC_CHUNK // SC_GATHER_ROWS)]

    @functools.partial(
        pl.kernel, mesh=_sc_mesh(),
        out_type=jax.ShapeDtypeStruct((TOP_K, n_tok, width), y_rows.dtype),
        scratch_types=[pltpu.VMEM((TOP_K, SC_CHUNK), jnp.int32),
                       pltpu.VMEM((2, SC_GATHER_ROWS, width), y_rows.dtype),
                       pltpu.SemaphoreType.DMA, pltpu.SemaphoreType.DMA],
    )
    def gather(y_hbm, dest_hbm, out_hbm, dest_v, rows_v, gsem, wsem):
        wid, workers, n_own = _sc_chunks(n_chunks)

        @pl.loop(0, n_own)
        def _(j):
            ch = wid + j * workers
            pltpu.sync_copy(dest_hbm.at[ch], dest_v)

            def fetch(n):
                k, h = items[n]
                return pltpu.async_copy(y_hbm.at[dest_v.at[k, pl.ds(h * SC_GATHER_ROWS, SC_GATHER_ROWS)]],
                                        rows_v.at[n % 2], gsem)

            def write(n):
                k, h = items[n]
                rows = pl.ds(ch * SC_CHUNK + h * SC_GATHER_ROWS, SC_GATHER_ROWS)
                return pltpu.async_copy(rows_v.at[n % 2], out_hbm.at[k, rows], wsem)

            pending_fetch = fetch(0)
            pending_write = None
            for n in range(len(items)):
                pending_fetch.wait()
                if pending_write is not None:
                    pending_write.wait()
                if n + 1 < len(items):
                    pending_fetch = fetch(n + 1)
                pending_write = write(n)
            pending_write.wait()

    return gather(y_rows, dest)


def _combine_kernel(tiles_p, gate_ref, base_ref, yk_ref, g2_ref, b2_ref, outp_ref, outs_ref):
    gates = gate_ref[...]
    half = yk_ref.shape[2]
    acc_lo = base_ref[:, :half]
    acc_hi = base_ref[:, half:]
    for k in range(TOP_K):
        packed = yk_ref[k]
        gate = gates[:, k:k + 1]
        acc_lo = acc_lo + gate * lax.bitcast_convert_type(packed << 16, F32)
        acc_hi = acc_hi + gate * lax.bitcast_convert_type(packed & jnp.uint32(0xFFFF0000), F32)
    out = _layer_norm(jnp.concatenate([acc_lo, acc_hi], axis=1), g2_ref[...], b2_ref[...])
    is_prompt = pl.program_id(0) < tiles_p

    @pl.when(is_prompt)
    def _():
        outp_ref[...] = out

    @pl.when(jnp.logical_not(is_prompt))
    def _():
        outs_ref[...] = out


def _combine(gates_tok, base, yk, g2, b2, n_prompt, tm):
    t, d = base.shape
    const2 = lambda i: (0, 0)
    return pl.pallas_call(
        functools.partial(_combine_kernel, n_prompt // tm),
        out_shape=[jax.ShapeDtypeStruct((n_prompt, d), F32), jax.ShapeDtypeStruct((t - n_prompt, d), F32)],
        grid=(t // tm,),
        in_specs=[pl.BlockSpec((tm, TOP_K), lambda i: (i, 0)),
                  pl.BlockSpec((tm, d), lambda i: (i, 0)),
                  pl.BlockSpec((TOP_K, tm, yk.shape[2]), lambda i: (0, i, 0)),
                  pl.BlockSpec(g2.shape, const2), pl.BlockSpec(b2.shape, const2)],
        out_specs=list(_pool_specs(n_prompt, tm, d)),
        compiler_params=_cparams("arbitrary"),
        name="moe_combine",
    )(gates_tok, base, yk, g2, b2)


def _sgu_tables(sgu_w, sgu_b, rows_per_seq):
    reps = CHUNK // rows_per_seq
    tril = jnp.tril(sgu_w[:, :rows_per_seq, :rows_per_seq])
    eye = jnp.eye(reps, dtype=F32)
    mix = jnp.einsum("ab,gts->gatbs", eye, tril).reshape(N_GROUPS_SGU, CHUNK, CHUNK)
    bias = jnp.tile(jnp.repeat(sgu_b[:, :rows_per_seq].T, D_SGU // N_GROUPS_SGU, axis=1), (reps, 1))
    return mix.astype(BF16), bias


def _layer(xp, xs, ck, cv, pp, ps, w, rel_bias, alpha):
    batch, seq, d = xp.shape
    dec_b, dec_t, _ = xs.shape
    w_buf = ck.shape[1]
    n_p, n_s = batch * seq, dec_b * dec_t
    n_tok = n_p + n_s

    scale = jnp.concatenate([jnp.full((D_ATTN,), HEAD_DIM ** -0.5, F32),
                             jnp.ones((w["w_in"].shape[1] - D_ATTN,), F32)])
    w_in = (w["w_in"] * scale).astype(BF16)

    qp, kp, vp, up, gp = _proj(xp.reshape(n_p, d), w_in, 512)
    qs, ks, vs, us, gs = _proj(xs.reshape(n_s, d), w_in, 512)

    b1, b4, b16 = _prompt_bias_tables(rel_bias)
    attn_p = _attn_prompt(qp, kp, vp, b1, b4, b16, batch, seq)

    attn_s = _sample_attention(rel_bias, qs.reshape(dec_b, dec_t, D_ATTN), ks.reshape(dec_b, dec_t, D_ATTN),
                               vs.reshape(dec_b, dec_t, D_ATTN), ck, cv).reshape(n_s, D_ATTN)

    ln_g, ln_b = w["sgu_ln_g"][None], w["sgu_ln_b"][None]
    mix_p, bias_p = _sgu_tables(w["sgu_w"], w["sgu_b"], CHUNK)
    mix_s, bias_s = _sgu_tables(w["sgu_w"], w["sgu_b"], dec_t)
    (sgu_p,) = _sgu(up, gp, mix_p, bias_p, ln_g, ln_b, False, 4)
    sgu_s, z2_s = _sgu(us, gs, mix_s, bias_s, ln_g, ln_b, True, 4)

    x1, x1_packed, idx, gates = _mid(xp.reshape(n_p, d), xs.reshape(n_s, d), attn_p, attn_s, sgu_p, sgu_s,
                                     w["w_mix_out"].astype(BF16), w["ln1_g"][None], w["ln1_b"][None],
                                     w["w_router"].T.astype(BF16), w["router_bias"][:, None], alpha, 512)

    rank, counts = _rank(idx, 512)
    counts = counts[:, 0].astype(jnp.int32)
    padded = (counts + EXPERT_ROWS - 1) // EXPERT_ROWS * EXPERT_ROWS
    pend = jnp.cumsum(padded)
    pstart = (pend - padded).astype(jnp.int32)
    n_blocks = (n_tok * TOP_K + N_EXPERTS * (EXPERT_ROWS - 1)) // EXPERT_ROWS
    n_used = (pend[-1:] // EXPERT_ROWS).astype(jnp.int32)

    dest = _dest(idx, rank, pstart.astype(F32)[:, None], 512)
    x_sorted = _sc_dispatch(x1_packed, dest, n_blocks * EXPERT_ROWS)
    y_sorted = _experts(pstart // EXPERT_ROWS, padded // EXPERT_ROWS, counts, n_used, x_sorted,
                        w["w_gate_e"], w["w_up_e"], w["w_down_e"])

    base = _dense(x1, pp.reshape(n_p, -1), ps.reshape(n_s, -1), w["w_gate_s"].astype(BF16), w["w_up_s"].astype(BF16),
                  w["w_down_s"].astype(BF16), w["w_ple_gate"].astype(BF16),
                  w["w_ple_proj"].astype(BF16), alpha, 512)

    y_slots = _sc_gather(y_sorted, dest, n_tok)
    y_p, y_s = _combine(gates.T, base, y_slots, w["ln2_g"][None], w["ln2_b"][None], n_p, 256)
    y_p = y_p.reshape(batch, seq, d)
    y_s = y_s.reshape(dec_b, dec_t, d)
    keep = min(MAX_DISTANCE, seq)
    k_rows = kp.reshape(batch, seq, N_HEADS, HEAD_DIM)[:, seq - keep:]
    v_rows = vp.reshape(batch, seq, N_HEADS, HEAD_DIM)[:, seq - keep:]
    return (y_p, y_s, k_rows, v_rows,
            ks.reshape(dec_b, dec_t, N_HEADS, HEAD_DIM), vs.reshape(dec_b, dec_t, N_HEADS, HEAD_DIM),
            z2_s.reshape(dec_b, dec_t, D_SGU))


def kernel(x_prompt, x_sample, cache_k, cache_v, p_prompt, p_sample, w_in, rel_bias, sgu_w, sgu_b, sgu_ln_g, sgu_ln_b, w_mix_out, ln1_g, ln1_b, w_router, router_bias, w_gate_e, w_up_e, w_down_e, w_gate_s, w_up_s, w_down_s, w_ple_gate, w_ple_proj, ln2_g, ln2_b):
    depth = w_in.shape[0]
    alpha = (2 * depth) ** 0.25
    xp, xs = x_prompt, x_sample
    outs = [[] for _ in range(5)]
    for i in range(depth):
        w = {"w_in": w_in[i], "sgu_w": sgu_w[i], "sgu_b": sgu_b[i], "sgu_ln_g": sgu_ln_g[i],
             "sgu_ln_b": sgu_ln_b[i], "w_mix_out": w_mix_out[i], "ln1_g": ln1_g[i], "ln1_b": ln1_b[i],
             "w_router": w_router[i], "router_bias": router_bias[i], "w_gate_e": w_gate_e[i],
             "w_up_e": w_up_e[i], "w_down_e": w_down_e[i], "w_gate_s": w_gate_s[i], "w_up_s": w_up_s[i],
             "w_down_s": w_down_s[i], "w_ple_gate": w_ple_gate[i], "w_ple_proj": w_ple_proj[i],
             "ln2_g": ln2_g[i], "ln2_b": ln2_b[i]}
        xp, xs, kp, vp, ks, vs, zs = _layer(xp, xs, cache_k[i], cache_v[i], p_prompt[i], p_sample[i],
                                            w, rel_bias, alpha)
        for lst, val in zip(outs, (kp, vp, ks, vs, zs)):
            lst.append(val)
    return (xp, xs) + tuple(jnp.stack(lst) for lst in outs)
```

```python
import functools
import math

import numpy as np
import jax
import jax.numpy as jnp
from jax import lax
from jax.experimental import pallas as pl
from jax.experimental.pallas import tpu as pltpu
from jax.experimental.pallas import tpu_sc as plsc

F32 = jnp.float32
BF16 = jnp.bfloat16
NEG_INF = float("-inf")

N_HEADS = 8
HEAD_DIM = 64
D_ATTN = N_HEADS * HEAD_DIM
PATTERNS = ((128, 1), (512, 4), (2048, 16))
BAND = 128
N_BUCKETS = 32
MAX_DISTANCE = 2048
N_GROUPS_SGU = 8
D_SGU = 512
CHUNK = 128
N_EXPERTS = 256
TOP_K = 8
N_EXPERT_GROUPS = 8
TOPK_GROUPS = 4
ROUTED_SCALE = 2.5
LN_EPS = 1e-5
EXPERT_ROWS = 256
EXPERT_SLOTS = 4

LANES = 128
SC_CORES = 2
SC_SUBCORES = 16
SC_CHUNK = 128
SC_GATHER_ROWS = 64
VMEM_LIMIT = 56 * 1024 * 1024


def _cparams(*sem):
    return pltpu.CompilerParams(dimension_semantics=sem, vmem_limit_bytes=VMEM_LIMIT)


def _layer_norm(x, g, b):
    mu = jnp.mean(x, axis=-1, keepdims=True)
    xc = x - mu
    var = jnp.mean(xc * xc, axis=-1, keepdims=True)
    return xc * lax.rsqrt(var + LN_EPS) * g + b


def _sigmoid(x):
    return 1.0 / (1.0 + jnp.exp(-x))


def _gelu(x):
    return 0.5 * x * (1.0 + lax.erf(x * math.sqrt(0.5)))


def _pack_bf16_pairs(xb):
    n = xb.shape[1] // 2
    bits = lax.bitcast_convert_type(xb.astype(F32), jnp.uint32)
    return (bits[:, :n] >> 16) | (bits[:, n:] & jnp.uint32(0xFFFF0000))


def _unpack_bf16_pairs(p):
    lo = lax.bitcast_convert_type(p << 16, F32).astype(BF16)
    hi = lax.bitcast_convert_type(p & jnp.uint32(0xFFFF0000), F32).astype(BF16)
    return lo, hi


def _dot(a, b):
    return jnp.dot(a, b, preferred_element_type=F32)


def _dot_nt(a, b):
    return lax.dot_general(a, b, (((1,), (1,)), ((), ())), preferred_element_type=F32)


def _t5_bucket_np(dist):
    max_exact = N_BUCKETS // 2
    df = np.maximum(dist, max_exact).astype(np.float32)
    large = max_exact + (np.log(df / np.float32(max_exact)) / np.float32(math.log(MAX_DISTANCE / max_exact))
                         * np.float32(N_BUCKETS - max_exact)).astype(np.int32)
    return np.where(dist < max_exact, dist, np.minimum(large, N_BUCKETS - 1)).astype(np.int32)


def _band_bucket_table(dilation):
    qi = np.arange(BAND)[:, None]
    ki = np.arange(2 * BAND)[None, :]
    dsub = qi + BAND - ki
    valid = (dsub >= 0) & (dsub <= BAND)
    return np.where(valid, _t5_bucket_np(np.clip(dsub, 0, BAND) * dilation), -1).astype(np.int32)


def _sample_bucket_tables(w_buf, t_len):
    t = np.arange(t_len)[:, None]

    def table(rows, window, dilation):
        d = w_buf + t - rows[None, :]
        ok = (d >= 0) & (d % dilation == 0) & (d <= window)
        return np.where(ok, _t5_bucket_np(np.maximum(d, 0)), -1).astype(np.int32)

    assert w_buf >= PATTERNS[-1][0]
    tables = [table(np.arange(w_buf - window, w_buf), window, dilation) for window, dilation in PATTERNS]
    new_rows = w_buf + np.arange(LANES)
    new = np.stack([table(new_rows, window, dilation) for window, dilation in PATTERNS])
    new[:, :, t_len:] = -1
    return tables, new


def _bias_kernel(rb_ref, bucket_ref, out_ref):
    bucket = bucket_ref[...]
    for h in range(N_HEADS):
        acc = jnp.full(bucket.shape, NEG_INF, F32)
        for b in range(N_BUCKETS):
            acc = jnp.where(bucket == b, rb_ref[b, h], acc)
        out_ref[h] = acc


def _bias_table(rel_bias, bucket_np):
    r, c = bucket_np.shape
    return pl.pallas_call(
        _bias_kernel,
        out_shape=jax.ShapeDtypeStruct((N_HEADS, r, c), F32),
        in_specs=[pl.BlockSpec(memory_space=pltpu.SMEM), pl.BlockSpec(memory_space=pltpu.VMEM)],
        out_specs=pl.BlockSpec(memory_space=pltpu.VMEM),
        name="bias_table",
    )(rel_bias, jnp.asarray(bucket_np))


def _proj_kernel(x_ref, w_ref, wkv_t_ref, q_ref, k_ref, v_ref, u_ref, g_ref, *kv_t_refs):
    x = x_ref[...].astype(BF16)
    col = 0
    for o in (q_ref, k_ref, v_ref, u_ref, g_ref):
        n = o.shape[1]
        o[...] = _dot(x, w_ref[:, col:col + n])
        col += n
    for i, o in enumerate(kv_t_refs):
        o[0] = _dot_nt(wkv_t_ref[i * D_ATTN:(i + 1) * D_ATTN], x)


def _proj(x, w, wkv_t, tm, seq=None):
    m, d = x.shape
    n_out = (D_ATTN, D_ATTN, D_ATTN, D_SGU, D_SGU)
    out_shape = [jax.ShapeDtypeStruct((m, n), F32) for n in n_out]
    out_specs = [pl.BlockSpec((tm, n), lambda i: (i, 0)) for n in n_out]
    if seq is not None:
        tiles = seq // tm
        out_shape += [jax.ShapeDtypeStruct((m // seq, D_ATTN, seq), F32)] * 2
        out_specs += [pl.BlockSpec((1, D_ATTN, tm), lambda i: (i // tiles, 0, i % tiles))] * 2
    return pl.pallas_call(
        _proj_kernel,
        out_shape=out_shape,
        grid=(m // tm,),
        in_specs=[pl.BlockSpec((tm, d), lambda i: (i, 0)), pl.BlockSpec(w.shape, lambda i: (0, 0)),
                  pl.BlockSpec(wkv_t.shape, lambda i: (0, 0))],
        out_specs=out_specs,
        compiler_params=_cparams("parallel"),
        name="in_proj",
    )(x, w, wkv_t)


def _band_attn(q, k, v, bias, even):
    q2 = jnp.concatenate([jnp.where(even, q, 0.0), jnp.where(even, 0.0, q)], axis=0).astype(BF16)
    s = _dot_nt(q2, k.astype(BF16)) + bias
    m = jnp.max(s, axis=-1, keepdims=True)
    p = jnp.exp(s - m)
    l = jnp.sum(p, axis=-1, keepdims=True)
    pv = _dot(p.astype(BF16), v.astype(BF16))
    return (jnp.where(even, m[:BAND], m[BAND:]), jnp.where(even, l[:BAND], l[BAND:]),
            jnp.where(even, pv[:BAND], pv[BAND:]))


def _attn_prompt_kernel(q_ref, k_ref, v_ref, b1_ref, b4_ref, b16_ref, o_ref,
                        m1, l1, a1, m4, l4, a4, m16, l16, a16, qc, kc, vc):
    seq = q_ref.shape[0]
    n4 = seq // 4
    per_trip = seq // BAND // 4
    even = lax.broadcasted_iota(jnp.int32, (BAND, LANES), 1) < HEAD_DIM

    def store_all(refs, rows_list, results):
        for rows, (m, l, a) in zip(rows_list, results):
            refs[0][rows, :], refs[1][rows, :], refs[2][rows, :] = m, l, a

    def body(r, carry):
        rows1, loaded1 = [], []
        for j in range(per_trip):
            i = r * per_trip + j
            r0 = pl.multiple_of(i * BAND, BAND)
            k0 = pl.multiple_of(jnp.maximum(i - 1, 0) * BAND, BAND)
            rows1.append(pl.ds(r0, BAND))
            loaded1.append((q_ref[pl.ds(r0, BAND)], k_ref[pl.ds(k0, 2 * BAND)], v_ref[pl.ds(k0, 2 * BAND)],
                            b1_ref[jnp.where(i == 0, 1, 0)]))
        base = pl.multiple_of(r * n4, BAND)
        q4 = q_ref[pl.ds(r, n4, stride=4), :]
        k4 = k_ref[pl.ds(r, n4, stride=4), :]
        v4 = v_ref[pl.ds(r, n4, stride=4), :]
        qc[pl.ds(base, n4)], kc[pl.ds(base, n4)], vc[pl.ds(base, n4)] = q4, k4, v4
        rows16 = [pl.ds(r * n4 + s, seq // 16, stride=4) for s in range(4)]
        loaded16 = [(qc[rows, :], kc[rows, :], vc[rows, :]) for rows in rows16]

        results1 = [_band_attn(q, k, v, bias, even) for q, k, v, bias in loaded1]
        results4, rows4 = [], []
        for i in range(n4 // BAND):
            lo = max(i - 1, 0) * BAND
            hi = (i + 1) * BAND
            col = 0 if i > 0 else BAND
            results4.append(_band_attn(q4[i * BAND:hi], k4[lo:hi], v4[lo:hi], b4_ref[:, col:], even))
            rows4.append(pl.ds(base + i * BAND, BAND))
        results16 = [_band_attn(q, k, v, b16_ref[:, BAND:], even) for q, k, v in loaded16]

        store_all((m1, l1, a1), rows1, results1)
        store_all((m4, l4, a4), rows4, results4)
        store_all((m16, l16, a16), rows16, results16)
        return carry

    lax.fori_loop(0, 4, body, 0)

    def merge_body(i, carry):
        r = i // (n4 // BAND)
        c = i % (n4 // BAND)
        rows = pl.ds(pl.multiple_of(i * BAND, BAND), BAND)
        nat = pl.ds(r + 4 * BAND * c, BAND, stride=4)
        ma, mb, mc = m1[nat, :], m4[rows], m16[rows]
        mx = jnp.maximum(jnp.maximum(ma, mb), mc)
        wa, wb, wc = jnp.exp(ma - mx), jnp.exp(mb - mx), jnp.exp(mc - mx)
        num = wa * a1[nat, :] + wb * a4[rows] + wc * a16[rows]
        den = wa * l1[nat, :] + wb * l4[rows] + wc * l16[rows]
        o_ref[nat, :] = num / den
        return carry

    lax.fori_loop(0, seq // BAND, merge_body, 0)


def _prompt_bias_tables(rel_bias):
    pairs = N_HEADS // 2
    first = _band_bucket_table(1)
    first = np.concatenate([first[:, BAND:], np.full((BAND, BAND), -1, np.int32)], axis=1)
    b1 = jnp.stack([_bias_table(rel_bias, _band_bucket_table(1)).reshape(pairs, 2 * BAND, 2 * BAND),
                    _bias_table(rel_bias, first).reshape(pairs, 2 * BAND, 2 * BAND)], axis=1)
    b4 = _bias_table(rel_bias, _band_bucket_table(4)).reshape(pairs, 2 * BAND, 2 * BAND)
    b16 = _bias_table(rel_bias, _band_bucket_table(16)).reshape(pairs, 2 * BAND, 2 * BAND)
    return b1, b4, b16


def _attn_prompt(q, k, v, b1, b4, b16, batch, seq):
    blk = pl.BlockSpec((seq, LANES), lambda b, j: (b, j))
    bias_spec = pl.BlockSpec((None, 2 * BAND, 2 * BAND), lambda b, j: (j, 0, 0))
    return pl.pallas_call(
        _attn_prompt_kernel,
        out_shape=jax.ShapeDtypeStruct(q.shape, F32),
        grid=(batch, D_ATTN // LANES),
        in_specs=[blk, blk, blk, pl.BlockSpec((None, 2, 2 * BAND, 2 * BAND), lambda b, j: (j, 0, 0, 0)),
                  bias_spec, bias_spec],
        out_specs=blk,
        scratch_shapes=[pltpu.VMEM((seq, LANES), F32) for _ in range(12)],
        compiler_params=_cparams("parallel", "parallel"),
        name="attn_prompt",
    )(q, k, v, b1, b4, b16)


def _attn_sample_kernel(q_ref, kn_ref, vn_ref, kt_ref, vt_ref, b1_ref, b4_ref, b16_ref, bn_ref, o_ref):
    t_len = q_ref.shape[1]
    rows = N_HEADS * t_len
    q = q_ref[0]
    head_of_row = lax.broadcasted_iota(jnp.int32, (rows, D_ATTN), 0) // t_len
    head_of_lane = lax.broadcasted_iota(jnp.int32, (rows, D_ATTN), 1) // HEAD_DIM
    own = head_of_row == head_of_lane
    qrows = jnp.where(own, jnp.concatenate([q] * N_HEADS, axis=0), 0.0).astype(BF16)
    pad = jnp.zeros((LANES - t_len, D_ATTN), F32)
    kn = jnp.concatenate([kn_ref[0], pad], axis=0).astype(BF16)
    vn = jnp.concatenate([vn_ref[0], pad], axis=0).astype(BF16)
    kt = kt_ref[0].reshape(D_ATTN, -1).astype(BF16)
    vt = vt_ref[0].reshape(D_ATTN, -1).astype(BF16)
    s_cache = _dot(qrows, kt)
    s_new = _dot_nt(qrows, kn)
    ms, ls, accs = [], [], []
    for p, bias_ref in enumerate((b1_ref, b4_ref, b16_ref)):
        w = bias_ref.shape[1]
        sc = s_cache[:, -w:] + bias_ref[...]
        sn = s_new + bn_ref[p]
        m = jnp.maximum(jnp.max(sc, axis=-1, keepdims=True), jnp.max(sn, axis=-1, keepdims=True))
        pc = jnp.exp(sc - m)
        pn = jnp.exp(sn - m)
        ls.append(jnp.sum(pc, axis=-1, keepdims=True) + jnp.sum(pn, axis=-1, keepdims=True))
        accs.append(_dot_nt(pc.astype(BF16), vt[:, -w:]) + _dot(pn.astype(BF16), vn))
        ms.append(m)
    mx = jnp.maximum(jnp.maximum(ms[0], ms[1]), ms[2])
    ws = [jnp.exp(m - mx) for m in ms]
    num = ws[0] * accs[0] + ws[1] * accs[1] + ws[2] * accs[2]
    den = ws[0] * ls[0] + ws[1] * ls[1] + ws[2] * ls[2]
    full = jnp.where(own, num / den, 0.0)
    out = full[0:t_len]
    for h in range(1, N_HEADS):
        out = out + full[h * t_len:(h + 1) * t_len]
    o_ref[0] = out


def _sample_attention(rel_bias, q, kn, vn, ck, cv):
    b, t_len, _ = q.shape
    w_buf = ck.shape[1]
    rows = N_HEADS * t_len
    tables, new_t = _sample_bucket_tables(w_buf, t_len)
    b1, b4, b16 = (_bias_table(rel_bias, tb).reshape(rows, tb.shape[1]) for tb in tables)
    bn = _bias_table(rel_bias, new_t.reshape(3 * t_len, LANES))
    bn = bn.reshape(N_HEADS, 3, t_len, LANES).transpose(1, 0, 2, 3).reshape(3, rows, LANES)
    new_spec = pl.BlockSpec((1, t_len, D_ATTN), lambda i: (i, 0, 0))
    cache_spec = pl.BlockSpec((1, N_HEADS, HEAD_DIM, w_buf), lambda i: (i, 0, 0, 0))
    const2 = lambda i: (0, 0)
    return pl.pallas_call(
        _attn_sample_kernel,
        out_shape=jax.ShapeDtypeStruct(q.shape, F32),
        grid=(b,),
        in_specs=[new_spec, new_spec, new_spec, cache_spec, cache_spec,
                  pl.BlockSpec(b1.shape, const2), pl.BlockSpec(b4.shape, const2), pl.BlockSpec(b16.shape, const2),
                  pl.BlockSpec(bn.shape, lambda i: (0, 0, 0))],
        out_specs=new_spec,
        compiler_params=_cparams("parallel"),
        name="attn_sample",
    )(q, kn, vn, ck.transpose(0, 2, 3, 1), cv.transpose(0, 2, 3, 1), b1, b4, b16, bn)


def _sgu_kernel(u_ref, g_ref, mix_ref, bias_ref, lng_ref, lnb_ref, sgu_ref, *z2_out):
    n_chunks = u_ref.shape[0] // CHUNK
    group_of_lane = lax.broadcasted_iota(jnp.int32, (CHUNK, D_SGU), 1) // (D_SGU // N_GROUPS_SGU)
    for c in range(n_chunks):
        rows = slice(c * CHUNK, (c + 1) * CHUNK)
        z1 = _gelu(u_ref[rows])
        z2 = _layer_norm(_gelu(g_ref[rows]), lng_ref[...], lnb_ref[...])
        if z2_out:
            z2_out[0][rows] = z2
        mixed = bias_ref[...]
        for g in range(N_GROUPS_SGU):
            mixed = mixed + _dot(mix_ref[g], jnp.where(group_of_lane == g, z2, 0.0).astype(BF16))
        sgu_ref[rows] = (z1 * mixed).astype(sgu_ref.dtype)


def _sgu(u, g, mix, bias, ln_g, ln_b, want_z2, chunks_per_step):
    m = u.shape[0]
    tm = CHUNK * chunks_per_step
    row_spec = pl.BlockSpec((tm, D_SGU), lambda i: (i, 0))
    const2 = lambda i: (0, 0)
    out_shape = [jax.ShapeDtypeStruct((m, D_SGU), BF16)]
    out_specs = [row_spec]
    if want_z2:
        out_shape.append(jax.ShapeDtypeStruct((m, D_SGU), F32))
        out_specs.append(row_spec)
    return pl.pallas_call(
        _sgu_kernel,
        out_shape=out_shape,
        grid=(m // tm,),
        in_specs=[row_spec, row_spec,
                  pl.BlockSpec(mix.shape, lambda i: (0, 0, 0)),
                  pl.BlockSpec(bias.shape, const2),
                  pl.BlockSpec(ln_g.shape, const2), pl.BlockSpec(ln_b.shape, const2)],
        out_specs=out_specs,
        compiler_params=_cparams("parallel"),
        name="sgu",
    )(u, g, mix, bias, ln_g, ln_b)


def _top1_rows(x, row):
    m = jnp.max(x, axis=0, keepdims=True)
    i = jnp.min(jnp.where(x == m, row, x.shape[0]), axis=0, keepdims=True)
    return m, i


def _pool_specs(n_prompt, tm, width):
    tiles_p = n_prompt // tm
    return (pl.BlockSpec((tm, width), lambda i, *_: (jnp.minimum(i, tiles_p - 1), 0)),
            pl.BlockSpec((tm, width), lambda i, *_: (jnp.maximum(i - tiles_p, 0), 0)))


def _pool_rows(tiles_p, prompt_ref, sample_ref, dtype):
    return jnp.where(pl.program_id(0) < tiles_p, prompt_ref[...].astype(dtype), sample_ref[...].astype(dtype))


def _mid_kernel(alpha, tiles_p, xp_ref, xs_ref, ap_ref, as_ref, sp_ref, ss_ref, wmix_ref, g1_ref, b1_ref,
                wrt_ref, rb_ref, x1_ref, x1p_ref, idx_ref, gate_ref):
    y = (_dot(_pool_rows(tiles_p, ap_ref, as_ref, BF16), wmix_ref[0:D_ATTN])
         + _dot(_pool_rows(tiles_p, sp_ref, ss_ref, BF16), wmix_ref[D_ATTN:])
         + alpha * _pool_rows(tiles_p, xp_ref, xs_ref, F32))
    x1 = _layer_norm(y, g1_ref[...], b1_ref[...])
    x1_ref[...] = x1
    tm = x1.shape[0]
    x1b = x1.astype(BF16)
    x1p_ref[...] = _pack_bf16_pairs(x1b)
    scores = _sigmoid(_dot_nt(wrt_ref[...], x1b))
    biased = scores + rb_ref[...]
    per_group = N_EXPERTS // N_EXPERT_GROUPS
    row_g = lax.broadcasted_iota(jnp.int32, (per_group, tm), 0)
    row_8 = lax.broadcasted_iota(jnp.int32, (N_EXPERT_GROUPS, tm), 0)
    row_e = lax.broadcasted_iota(jnp.int32, (N_EXPERTS, tm), 0)
    cur = jnp.full((N_EXPERT_GROUPS, tm), NEG_INF, F32)
    for g in range(N_EXPERT_GROUPS):
        blk = biased[g * per_group:(g + 1) * per_group]
        m_a, i_a = _top1_rows(blk, row_g)
        m_b = jnp.max(jnp.where(row_g == i_a, NEG_INF, blk), axis=0, keepdims=True)
        cur = jnp.where(row_8 == g, m_a + m_b, cur)
    chosen = jnp.zeros((N_EXPERT_GROUPS, tm), F32)
    for _ in range(TOPK_GROUPS):
        _, i_g = _top1_rows(cur, row_8)
        sel = row_8 == i_g
        chosen = jnp.where(sel, 1.0, chosen)
        cur = jnp.where(sel, NEG_INF, cur)
    group_of_row = row_e // per_group
    emask = jnp.zeros((N_EXPERTS, tm), F32)
    for g in range(N_EXPERT_GROUPS):
        emask = jnp.where(group_of_row == g, chosen[g:g + 1], emask)
    cur = jnp.where(emask > 0.0, biased, NEG_INF)
    idx = jnp.zeros((TOP_K, tm), jnp.int32)
    gates = jnp.zeros((TOP_K, tm), F32)
    for k in range(TOP_K):
        _, i_e = _top1_rows(cur, row_e)
        sel = row_e == i_e
        gate_k = jnp.sum(jnp.where(sel, scores, 0.0), axis=0, keepdims=True)
        cur = jnp.where(sel, NEG_INF, cur)
        idx = jnp.where(row_8 == k, i_e, idx)
        gates = jnp.where(row_8 == k, gate_k, gates)
    idx_ref[...] = idx
    gate_ref[...] = gates / jnp.sum(gates, axis=0, keepdims=True) * ROUTED_SCALE


def _mid(xp, xs, attn_p, attn_s, sgu_p, sgu_s, wmix, g1, b1, wrt, rb, alpha, tm):
    n_p, d = xp.shape
    t = n_p + xs.shape[0]
    const2 = lambda i: (0, 0)
    return pl.pallas_call(
        functools.partial(_mid_kernel, alpha, n_p // tm),
        out_shape=[jax.ShapeDtypeStruct((t, d), F32),
                   jax.ShapeDtypeStruct((t, d // 2), jnp.uint32),
                   jax.ShapeDtypeStruct((TOP_K, t), jnp.int32),
                   jax.ShapeDtypeStruct((TOP_K, t), F32)],
        grid=(t // tm,),
        in_specs=[*_pool_specs(n_p, tm, d), *_pool_specs(n_p, tm, D_ATTN), *_pool_specs(n_p, tm, D_SGU),
                  pl.BlockSpec(wmix.shape, const2), pl.BlockSpec(g1.shape, const2),
                  pl.BlockSpec(b1.shape, const2), pl.BlockSpec(wrt.shape, const2),
                  pl.BlockSpec(rb.shape, const2)],
        out_specs=[pl.BlockSpec((tm, d), lambda i: (i, 0)),
                   pl.BlockSpec((tm, d // 2), lambda i: (i, 0)),
                   pl.BlockSpec((TOP_K, tm), lambda i: (0, i)),
                   pl.BlockSpec((TOP_K, tm), lambda i: (0, i))],
        compiler_params=_cparams("parallel"),
        name="mix_ln_router",
    )(xp, xs, attn_p, attn_s, sgu_p, sgu_s, wmix, g1, b1, wrt, rb)


def _rank_kernel(idx_ref, rank_ref, counts_ref, run_ref):
    @pl.when(pl.program_id(0) == 0)
    def _():
        run_ref[...] = jnp.zeros_like(run_ref)

    tm = idx_ref.shape[1]
    row_e = lax.broadcasted_iota(jnp.int32, (N_EXPERTS, tm), 0)
    idx = idx_ref[...]
    onehot = jnp.zeros((N_EXPERTS, tm), F32)
    for k in range(TOP_K):
        onehot = onehot + jnp.where(row_e == idx[k:k + 1], 1.0, 0.0)
    earlier = (lax.broadcasted_iota(jnp.int32, (tm, tm), 0)
               < lax.broadcasted_iota(jnp.int32, (tm, tm), 1))
    before = run_ref[...] + _dot(onehot.astype(BF16), jnp.where(earlier, 1.0, 0.0).astype(BF16))
    row_k = lax.broadcasted_iota(jnp.int32, (TOP_K, tm), 0)
    ranks = jnp.zeros((TOP_K, tm), F32)
    for k in range(TOP_K):
        rank_k = jnp.sum(jnp.where(row_e == idx[k:k + 1], before, 0.0), axis=0, keepdims=True)
        ranks = jnp.where(row_k == k, rank_k, ranks)
    rank_ref[...] = ranks.astype(jnp.int32)
    run_ref[...] = run_ref[...] + jnp.sum(onehot, axis=1, keepdims=True)
    counts_ref[...] = run_ref[...]


def _rank(idx, tm):
    t = idx.shape[1]
    return pl.pallas_call(
        _rank_kernel,
        out_shape=[jax.ShapeDtypeStruct((TOP_K, t), jnp.int32),
                   jax.ShapeDtypeStruct((N_EXPERTS, 1), F32)],
        grid=(t // tm,),
        in_specs=[pl.BlockSpec((TOP_K, tm), lambda i: (0, i))],
        out_specs=[pl.BlockSpec((TOP_K, tm), lambda i: (0, i)),
                   pl.BlockSpec((N_EXPERTS, 1), lambda i: (0, 0))],
        scratch_shapes=[pltpu.VMEM((N_EXPERTS, 1), F32)],
        compiler_params=_cparams("arbitrary"),
        name="expert_rank",
    )(idx)


def _dest_kernel(idx_ref, rank_ref, pstart_ref, dest_ref):
    tm = idx_ref.shape[1]
    row_e = lax.broadcasted_iota(jnp.int32, (N_EXPERTS, tm), 0)
    row_k = lax.broadcasted_iota(jnp.int32, (TOP_K, tm), 0)
    idx = idx_ref[...]
    start = jnp.zeros((TOP_K, tm), F32)
    for k in range(TOP_K):
        start_k = jnp.sum(jnp.where(row_e == idx[k:k + 1], pstart_ref[...], 0.0), axis=0, keepdims=True)
        start = jnp.where(row_k == k, start_k, start)
    dest = start.astype(jnp.int32) + rank_ref[...]
    for c in range(tm // SC_CHUNK):
        dest_ref[c] = dest[:, c * SC_CHUNK:(c + 1) * SC_CHUNK]


def _dest(idx, rank, pstart_col, tm):
    t = idx.shape[1]
    tok_spec = pl.BlockSpec((TOP_K, tm), lambda i: (0, i))
    return pl.pallas_call(
        _dest_kernel,
        out_shape=jax.ShapeDtypeStruct((t // SC_CHUNK, TOP_K, SC_CHUNK), jnp.int32),
        grid=(t // tm,),
        in_specs=[tok_spec, tok_spec, pl.BlockSpec(pstart_col.shape, lambda i: (0, 0))],
        out_specs=pl.BlockSpec((tm // SC_CHUNK, TOP_K, SC_CHUNK), lambda i: (i, 0, 0)),
        compiler_params=_cparams("parallel"),
        name="moe_dest",
    )(idx, rank, pstart_col)


def _sc_mesh():
    return plsc.VectorSubcoreMesh(core_axis_name="c", subcore_axis_name="s",
                                  num_cores=SC_CORES, num_subcores=SC_SUBCORES)


def _sc_chunks(n_chunks):
    workers = SC_CORES * SC_SUBCORES
    wid = lax.axis_index("s") * SC_CORES + lax.axis_index("c")
    return wid, workers, (n_chunks - wid + workers - 1) // workers


def _sc_dispatch(x_rows, dest, n_rows):
    n_chunks = dest.shape[0]
    width = x_rows.shape[1]

    @functools.partial(
        pl.kernel, mesh=_sc_mesh(),
        out_type=jax.ShapeDtypeStruct((n_rows, width), x_rows.dtype),
        scratch_types=[pltpu.VMEM((TOP_K, SC_CHUNK), jnp.int32), pltpu.VMEM((SC_CHUNK, width), x_rows.dtype),
                       pltpu.SemaphoreType.DMA],
    )
    def scatter(x_hbm, dest_hbm, out_hbm, dest_v, rows_v, sem):
        wid, workers, n_own = _sc_chunks(n_chunks)

        @pl.loop(0, n_own)
        def _(j):
            ch = wid + j * workers
            pltpu.sync_copy(dest_hbm.at[ch], dest_v)
            pltpu.sync_copy(x_hbm.at[pl.ds(ch * SC_CHUNK, SC_CHUNK)], rows_v)
            copies = [pltpu.async_copy(rows_v, out_hbm.at[dest_v.at[k]], sem) for k in range(TOP_K)]
            for cp in copies:
                cp.wait()

    return scatter(x_rows, dest)


def _expert_kernel(first_ref, nblk_ref, count_ref, nused_ref, xs_hbm, wg_ref, wu_ref, wd_ref, ys_hbm,
                   xbuf, ybuf, xsem, ysem, wg_b, wu_b, wd_b):
    e = pl.program_id(0)
    n_used = nused_ref[0]
    slots, _, half = xbuf.shape

    def block_rows(j):
        return pl.ds(pl.multiple_of(j * EXPERT_ROWS, EXPERT_ROWS), EXPERT_ROWS)

    def x_copy(j, slot):
        return pltpu.make_async_copy(xs_hbm.at[block_rows(j)], xbuf.at[slot], xsem.at[slot])

    def y_copy(j, slot):
        return pltpu.make_async_copy(ybuf.at[slot], ys_hbm.at[block_rows(j)], ysem.at[slot])

    @pl.when(e == 0)
    def _():
        for k in range(slots - 1):
            @pl.when(k < n_used)
            def _():
                x_copy(k, k).start()

    @pl.when(nblk_ref[e] > 0)
    def _():
        wg_b[...] = wg_ref[0].astype(BF16)
        wu_b[...] = wu_ref[0].astype(BF16)
        wd_b[...] = wd_ref[0].astype(BF16)

    def block(b, carry):
        j = first_ref[e] + b
        slot = j % slots
        x_copy(j, slot).wait()
        ahead = j + slots - 1

        @pl.when(ahead < n_used)
        def _():
            x_copy(ahead, ahead % slots).start()

        @pl.when(j >= slots)
        def _():
            y_copy(j - slots, slot).wait()

        row = lax.broadcasted_iota(jnp.int32, (EXPERT_ROWS, half), 0)
        packed = jnp.where(row < count_ref[e] - b * EXPERT_ROWS, xbuf[slot], jnp.uint32(0))
        lo, hi = _unpack_bf16_pairs(packed)
        g = _dot(lo, wg_b[:half]) + _dot(hi, wg_b[half:])
        u = _dot(lo, wu_b[:half]) + _dot(hi, wu_b[half:])
        h = (g * _sigmoid(g) * u).astype(BF16)
        ybuf[slot] = _pack_bf16_pairs(_dot(h, wd_b[...]).astype(BF16))
        y_copy(j, slot).start()
        return carry

    lax.fori_loop(0, nblk_ref[e], block, 0)

    @pl.when(e == pl.num_programs(0) - 1)
    def _():
        for k in range(1, slots + 1):
            @pl.when(n_used >= k)
            def _():
                y_copy(n_used - k, (n_used - k) % slots).wait()


def _experts(first_block, n_block, counts, n_used, xs, wg, wu, wd):
    n_rows, half = xs.shape
    n_exp, d, de = wg.shape
    w_map = lambda e, *_: (e, 0, 0)
    return pl.pallas_call(
        _expert_kernel,
        out_shape=jax.ShapeDtypeStruct((n_rows, half), jnp.uint32),
        grid_spec=pltpu.PrefetchScalarGridSpec(
            num_scalar_prefetch=4,
            grid=(n_exp,),
            in_specs=[pl.BlockSpec(memory_space=pl.ANY),
                      pl.BlockSpec((1, d, de), w_map), pl.BlockSpec((1, d, de), w_map),
                      pl.BlockSpec((1, de, d), w_map)],
            out_specs=pl.BlockSpec(memory_space=pl.ANY),
            scratch_shapes=[pltpu.VMEM((EXPERT_SLOTS, EXPERT_ROWS, half), jnp.uint32),
                            pltpu.VMEM((EXPERT_SLOTS, EXPERT_ROWS, half), jnp.uint32),
                            pltpu.SemaphoreType.DMA((EXPERT_SLOTS,)), pltpu.SemaphoreType.DMA((EXPERT_SLOTS,)),
                            pltpu.VMEM((d, de), BF16), pltpu.VMEM((d, de), BF16), pltpu.VMEM((de, d), BF16)]),
        compiler_params=_cparams("arbitrary"),
        name="routed_experts",
    )(first_block, n_block, counts, n_used, xs, wg, wu, wd)


def _dense_kernel(alpha, tiles_p, x1_ref, pp_ref, ps_ref, wgs_ref, wus_ref, wds_ref, wpg_ref, wpp_ref, base_ref):
    x1 = x1_ref[...]
    xb = x1.astype(BF16)
    g = _dot(xb, wgs_ref[...])
    h = (g * _sigmoid(g) * _dot(xb, wus_ref[...])).astype(BF16)
    shared = _dot(h, wds_ref[...])
    ple = _sigmoid(_dot(xb, wpg_ref[...])) * _dot(_pool_rows(tiles_p, pp_ref, ps_ref, BF16), wpp_ref[...])
    base_ref[...] = alpha * x1 + shared + ple


def _dense(x1, pp, ps, wgs, wus, wds, wpg, wpp, alpha, tm):
    t, d = x1.shape
    const2 = lambda i: (0, 0)
    return pl.pallas_call(
        functools.partial(_dense_kernel, alpha, pp.shape[0] // tm),
        out_shape=jax.ShapeDtypeStruct((t, d), F32),
        grid=(t // tm,),
        in_specs=[pl.BlockSpec((tm, d), lambda i: (i, 0)), *_pool_specs(pp.shape[0], tm, pp.shape[1]),
                  pl.BlockSpec(wgs.shape, const2), pl.BlockSpec(wus.shape, const2),
                  pl.BlockSpec(wds.shape, const2), pl.BlockSpec(wpg.shape, const2),
                  pl.BlockSpec(wpp.shape, const2)],
        out_specs=pl.BlockSpec((tm, d), lambda i: (i, 0)),
        compiler_params=_cparams("parallel"),
        name="shared_ple",
    )(x1, pp, ps, wgs, wus, wds, wpg, wpp)


def _sc_gather(y_rows, dest, n_tok):
    n_chunks = dest.shape[0]
    width = y_rows.shape[1]
    items = [(k, h) for k in range(TOP_K) for h in range(SC_CHUNK // SC_GATHER_ROWS)]

    @functools.partial(
        pl.kernel, mesh=_sc_mesh(),
        out_type=jax.ShapeDtypeStruct((TOP_K, n_tok, width), y_rows.dtype),
        scratch_types=[pltpu.VMEM((TOP_K, SC_CHUNK), jnp.int32),
                       pltpu.VMEM((2, SC_GATHER_ROWS, width), y_rows.dtype),
                       pltpu.SemaphoreType.DMA, pltpu.SemaphoreType.DMA],
    )
    def gather(y_hbm, dest_hbm, out_hbm, dest_v, rows_v, gsem, wsem):
        wid, workers, n_own = _sc_chunks(n_chunks)

        @pl.loop(0, n_own)
        def _(j):
            ch = wid + j * workers
            pltpu.sync_copy(dest_hbm.at[ch], dest_v)

            def fetch(n):
                k, h = items[n]
                return pltpu.async_copy(y_hbm.at[dest_v.at[k, pl.ds(h * SC_GATHER_ROWS, SC_GATHER_ROWS)]],
                                        rows_v.at[n % 2], gsem)

            def write(n):
                k, h = items[n]
                rows = pl.ds(ch * SC_CHUNK + h * SC_GATHER_ROWS, SC_GATHER_ROWS)
                return pltpu.async_copy(rows_v.at[n % 2], out_hbm.at[k, rows], wsem)

            pending_fetch = fetch(0)
            pending_write = None
            for n in range(len(items)):
                pending_fetch.wait()
                if pending_write is not None:
                    pending_write.wait()
                if n + 1 < len(items):
                    pending_fetch = fetch(n + 1)
                pending_write = write(n)
            pending_write.wait()

    return gather(y_rows, dest)


def _combine_kernel(tiles_p, gate_ref, base_ref, yk_ref, g2_ref, b2_ref, outp_ref, outs_ref):
    gates = gate_ref[...]
    half = yk_ref.shape[2]
    acc_lo = base_ref[:, :half]
    acc_hi = base_ref[:, half:]
    for k in range(TOP_K):
        packed = yk_ref[k]
        gate = gates[:, k:k + 1]
        acc_lo = acc_lo + gate * lax.bitcast_convert_type(packed << 16, F32)
        acc_hi = acc_hi + gate * lax.bitcast_convert_type(packed & jnp.uint32(0xFFFF0000), F32)
    out = _layer_norm(jnp.concatenate([acc_lo, acc_hi], axis=1), g2_ref[...], b2_ref[...])
    is_prompt = pl.program_id(0) < tiles_p

    @pl.when(is_prompt)
    def _():
        outp_ref[...] = out

    @pl.when(jnp.logical_not(is_prompt))
    def _():
        outs_ref[...] = out


def _combine(gates_tok, base, yk, g2, b2, n_prompt, tm):
    t, d = base.shape
    const2 = lambda i: (0, 0)
    return pl.pallas_call(
        functools.partial(_combine_kernel, n_prompt // tm),
        out_shape=[jax.ShapeDtypeStruct((n_prompt, d), F32), jax.ShapeDtypeStruct((t - n_prompt, d), F32)],
        grid=(t // tm,),
        in_specs=[pl.BlockSpec((tm, TOP_K), lambda i: (i, 0)),
                  pl.BlockSpec((tm, d), lambda i: (i, 0)),
                  pl.BlockSpec((TOP_K, tm, yk.shape[2]), lambda i: (0, i, 0)),
                  pl.BlockSpec(g2.shape, const2), pl.BlockSpec(b2.shape, const2)],
        out_specs=list(_pool_specs(n_prompt, tm, d)),
        compiler_params=_cparams("arbitrary"),
        name="moe_combine",
    )(gates_tok, base, yk, g2, b2)


def _sgu_tables(sgu_w, sgu_b, rows_per_seq):
    reps = CHUNK // rows_per_seq
    tril = jnp.tril(sgu_w[:, :rows_per_seq, :rows_per_seq])
    eye = jnp.eye(reps, dtype=F32)
    mix = jnp.einsum("ab,gts->gatbs", eye, tril).reshape(N_GROUPS_SGU, CHUNK, CHUNK)
    bias = jnp.tile(jnp.repeat(sgu_b[:, :rows_per_seq].T, D_SGU // N_GROUPS_SGU, axis=1), (reps, 1))
    return mix.astype(BF16), bias


def _layer(xp, xs, ck, cv, pp, ps, w, rel_bias, alpha):
    batch, seq, d = xp.shape
    dec_b, dec_t, _ = xs.shape
    w_buf = ck.shape[1]
    n_p, n_s = batch * seq, dec_b * dec_t
    n_tok = n_p + n_s

    scale = jnp.concatenate([jnp.full((D_ATTN,), HEAD_DIM ** -0.5, F32),
                             jnp.ones((w["w_in"].shape[1] - D_ATTN,), F32)])
    w_in = (w["w_in"] * scale).astype(BF16)

    wkv_t = w_in[:, D_ATTN:3 * D_ATTN].T
    qp, kp, vp, up, gp, kp_t, vp_t = _proj(xp.reshape(n_p, d), w_in, wkv_t, 512, seq)
    qs, ks, vs, us, gs = _proj(xs.reshape(n_s, d), w_in, wkv_t, 512)

    b1, b4, b16 = _prompt_bias_tables(rel_bias)
    attn_p = _attn_prompt(qp, kp, vp, b1, b4, b16, batch, seq)

    attn_s = _sample_attention(rel_bias, qs.reshape(dec_b, dec_t, D_ATTN), ks.reshape(dec_b, dec_t, D_ATTN),
                               vs.reshape(dec_b, dec_t, D_ATTN), ck, cv).reshape(n_s, D_ATTN)

    ln_g, ln_b = w["sgu_ln_g"][None], w["sgu_ln_b"][None]
    mix_p, bias_p = _sgu_tables(w["sgu_w"], w["sgu_b"], CHUNK)
    mix_s, bias_s = _sgu_tables(w["sgu_w"], w["sgu_b"], dec_t)
    (sgu_p,) = _sgu(up, gp, mix_p, bias_p, ln_g, ln_b, False, 4)
    sgu_s, z2_s = _sgu(us, gs, mix_s, bias_s, ln_g, ln_b, True, 4)

    x1, x1_packed, idx, gates = _mid(xp.reshape(n_p, d), xs.reshape(n_s, d), attn_p, attn_s, sgu_p, sgu_s,
                                     w["w_mix_out"].astype(BF16), w["ln1_g"][None], w["ln1_b"][None],
                                     w["w_router"].T.astype(BF16), w["router_bias"][:, None], alpha, 512)

    rank, counts = _rank(idx, 512)
    counts = counts[:, 0].astype(jnp.int32)
    padded = (counts + EXPERT_ROWS - 1) // EXPERT_ROWS * EXPERT_ROWS
    pend = jnp.cumsum(padded)
    pstart = (pend - padded).astype(jnp.int32)
    n_blocks = (n_tok * TOP_K + N_EXPERTS * (EXPERT_ROWS - 1)) // EXPERT_ROWS
    n_used = (pend[-1:] // EXPERT_ROWS).astype(jnp.int32)

    dest = _dest(idx, rank, pstart.astype(F32)[:, None], 512)
    x_sorted = _sc_dispatch(x1_packed, dest, n_blocks * EXPERT_ROWS)
    y_sorted = _experts(pstart // EXPERT_ROWS, padded // EXPERT_ROWS, counts, n_used, x_sorted,
                        w["w_gate_e"], w["w_up_e"], w["w_down_e"])

    base = _dense(x1, pp.reshape(n_p, -1), ps.reshape(n_s, -1), w["w_gate_s"].astype(BF16), w["w_up_s"].astype(BF16),
                  w["w_down_s"].astype(BF16), w["w_ple_gate"].astype(BF16),
                  w["w_ple_proj"].astype(BF16), alpha, 512)

    y_slots = _sc_gather(y_sorted, dest, n_tok)
    y_p, y_s = _combine(gates.T, base, y_slots, w["ln2_g"][None], w["ln2_b"][None], n_p, 256)
    y_p = y_p.reshape(batch, seq, d)
    y_s = y_s.reshape(dec_b, dec_t, d)
    keep = min(MAX_DISTANCE, seq)
    k_rows = kp_t.reshape(batch, N_HEADS, HEAD_DIM, seq)[..., seq - keep:].transpose(0, 3, 1, 2)
    v_rows = vp_t.reshape(batch, N_HEADS, HEAD_DIM, seq)[..., seq - keep:].transpose(0, 3, 1, 2)
    return (y_p, y_s, k_rows, v_rows,
            ks.reshape(dec_b, dec_t, N_HEADS, HEAD_DIM), vs.reshape(dec_b, dec_t, N_HEADS, HEAD_DIM),
            z2_s.reshape(dec_b, dec_t, D_SGU))


def kernel(x_prompt, x_sample, cache_k, cache_v, p_prompt, p_sample, w_in, rel_bias, sgu_w, sgu_b, sgu_ln_g, sgu_ln_b, w_mix_out, ln1_g, ln1_b, w_router, router_bias, w_gate_e, w_up_e, w_down_e, w_gate_s, w_up_s, w_down_s, w_ple_gate, w_ple_proj, ln2_g, ln2_b):
    depth = w_in.shape[0]
    alpha = (2 * depth) ** 0.25
    xp, xs = x_prompt, x_sample
    outs = [[] for _ in range(5)]
    for i in range(depth):
        w = {"w_in": w_in[i], "sgu_w": sgu_w[i], "sgu_b": sgu_b[i], "sgu_ln_g": sgu_ln_g[i],
             "sgu_ln_b": sgu_ln_b[i], "w_mix_out": w_mix_out[i], "ln1_g": ln1_g[i], "ln1_b": ln1_b[i],
             "w_router": w_router[i], "router_bias": router_bias[i], "w_gate_e": w_gate_e[i],
             "w_up_e": w_up_e[i], "w_down_e": w_down_e[i], "w_gate_s": w_gate_s[i], "w_up_s": w_up_s[i],
             "w_down_s": w_down_s[i], "w_ple_gate": w_ple_gate[i], "w_ple_proj": w_ple_proj[i],
             "ln2_g": ln2_g[i], "ln2_b": ln2_b[i]}
        xp, xs, kp, vp, ks, vs, zs = _layer(xp, xs, cache_k[i], cache_v[i], p_prompt[i], p_sample[i],
                                            w, rel_bias, alpha)
        for lst, val in zip(outs, (kp, vp, ks, vs, zs)):
            lst.append(val)
    return (xp, xs) + tuple(jnp.stack(lst) for lst in outs)
```

```python
import functools
import math

import numpy as np
import jax
import jax.numpy as jnp
from jax import lax
from jax.experimental import pallas as pl
from jax.experimental.pallas import tpu as pltpu
from jax.experimental.pallas import tpu_sc as plsc

F32 = jnp.float32
BF16 = jnp.bfloat16
NEG_INF = float("-inf")

N_HEADS = 8
HEAD_DIM = 64
D_ATTN = N_HEADS * HEAD_DIM
PATTERNS = ((128, 1), (512, 4), (2048, 16))
BAND = 128
N_BUCKETS = 32
MAX_DISTANCE = 2048
N_GROUPS_SGU = 8
D_SGU = 512
CHUNK = 128
N_EXPERTS = 256
TOP_K = 8
N_EXPERT_GROUPS = 8
TOPK_GROUPS = 4
ROUTED_SCALE = 2.5
LN_EPS = 1e-5
EXPERT_ROWS = 256
EXPERT_SLOTS = 6

LANES = 128
SC_CORES = 2
SC_SUBCORES = 16
SC_CHUNK = 128
SC_GATHER_ROWS = 64
VMEM_LIMIT = 56 * 1024 * 1024


def _cparams(*sem):
    return pltpu.CompilerParams(dimension_semantics=sem, vmem_limit_bytes=VMEM_LIMIT)


def _layer_norm(x, g, b):
    mu = jnp.mean(x, axis=-1, keepdims=True)
    xc = x - mu
    var = jnp.mean(xc * xc, axis=-1, keepdims=True)
    return xc * lax.rsqrt(var + LN_EPS) * g + b


def _sigmoid(x):
    return 1.0 / (1.0 + jnp.exp(-x))


def _gelu(x):
    return 0.5 * x * (1.0 + lax.erf(x * math.sqrt(0.5)))


def _pack_bf16_pairs(xb):
    n = xb.shape[1] // 2
    bits = lax.bitcast_convert_type(xb.astype(F32), jnp.uint32)
    return (bits[:, :n] >> 16) | (bits[:, n:] & jnp.uint32(0xFFFF0000))


def _unpack_bf16_pairs(p):
    lo = lax.bitcast_convert_type(p << 16, F32).astype(BF16)
    hi = lax.bitcast_convert_type(p & jnp.uint32(0xFFFF0000), F32).astype(BF16)
    return lo, hi


def _dot(a, b):
    return jnp.dot(a, b, preferred_element_type=F32)


def _dot_nt(a, b):
    return lax.dot_general(a, b, (((1,), (1,)), ((), ())), preferred_element_type=F32)


def _t5_bucket_np(dist):
    max_exact = N_BUCKETS // 2
    df = np.maximum(dist, max_exact).astype(np.float32)
    large = max_exact + (np.log(df / np.float32(max_exact)) / np.float32(math.log(MAX_DISTANCE / max_exact))
                         * np.float32(N_BUCKETS - max_exact)).astype(np.int32)
    return np.where(dist < max_exact, dist, np.minimum(large, N_BUCKETS - 1)).astype(np.int32)


def _band_bucket_table(dilation):
    qi = np.arange(BAND)[:, None]
    ki = np.arange(2 * BAND)[None, :]
    dsub = qi + BAND - ki
    valid = (dsub >= 0) & (dsub <= BAND)
    return np.where(valid, _t5_bucket_np(np.clip(dsub, 0, BAND) * dilation), -1).astype(np.int32)


def _sample_bucket_tables(w_buf, t_len):
    t = np.arange(t_len)[:, None]

    def table(rows, window, dilation):
        d = w_buf + t - rows[None, :]
        ok = (d >= 0) & (d % dilation == 0) & (d <= window)
        return np.where(ok, _t5_bucket_np(np.maximum(d, 0)), -1).astype(np.int32)

    assert w_buf >= PATTERNS[-1][0]
    tables = [table(np.arange(w_buf - window, w_buf), window, dilation) for window, dilation in PATTERNS]
    new_rows = w_buf + np.arange(LANES)
    new = np.stack([table(new_rows, window, dilation) for window, dilation in PATTERNS])
    new[:, :, t_len:] = -1
    return tables, new


def _bias_kernel(rb_ref, bucket_ref, out_ref):
    bucket = bucket_ref[...]
    for h in range(N_HEADS):
        acc = jnp.full(bucket.shape, NEG_INF, F32)
        for b in range(N_BUCKETS):
            acc = jnp.where(bucket == b, rb_ref[b, h], acc)
        out_ref[h] = acc


def _bias_table(rel_bias, bucket_np):
    r, c = bucket_np.shape
    return pl.pallas_call(
        _bias_kernel,
        out_shape=jax.ShapeDtypeStruct((N_HEADS, r, c), F32),
        in_specs=[pl.BlockSpec(memory_space=pltpu.SMEM), pl.BlockSpec(memory_space=pltpu.VMEM)],
        out_specs=pl.BlockSpec(memory_space=pltpu.VMEM),
        name="bias_table",
    )(rel_bias, jnp.asarray(bucket_np))


def _proj_kernel(x_ref, w_ref, q_ref, k_ref, v_ref, u_ref, g_ref, *kv_t_refs):
    x = x_ref[...].astype(BF16)
    col = 0
    rows = []
    for o in (q_ref, k_ref, v_ref, u_ref, g_ref):
        n = o.shape[1]
        rows.append(_dot(x, w_ref[:, col:col + n]))
        o[...] = rows[-1]
        col += n
    for o, val in zip(kv_t_refs, rows[1:3]):
        o[0] = val.T


def _proj(x, w, tm, seq=None):
    m, d = x.shape
    n_out = (D_ATTN, D_ATTN, D_ATTN, D_SGU, D_SGU)
    out_shape = [jax.ShapeDtypeStruct((m, n), F32) for n in n_out]
    out_specs = [pl.BlockSpec((tm, n), lambda i: (i, 0)) for n in n_out]
    if seq is not None:
        tiles = seq // tm
        out_shape += [jax.ShapeDtypeStruct((m // seq, D_ATTN, seq), F32)] * 2
        out_specs += [pl.BlockSpec((1, D_ATTN, tm), lambda i: (i // tiles, 0, i % tiles))] * 2
    return pl.pallas_call(
        _proj_kernel,
        out_shape=out_shape,
        grid=(m // tm,),
        in_specs=[pl.BlockSpec((tm, d), lambda i: (i, 0)), pl.BlockSpec(w.shape, lambda i: (0, 0))],
        out_specs=out_specs,
        compiler_params=_cparams("parallel"),
        name="in_proj",
    )(x, w)


def _band_attn(q, k, v, bias, even):
    q2 = jnp.concatenate([jnp.where(even, q, 0.0), jnp.where(even, 0.0, q)], axis=0).astype(BF16)
    s = _dot_nt(q2, k.astype(BF16)) + bias
    m = jnp.max(s, axis=-1, keepdims=True)
    p = jnp.exp(s - m)
    l = jnp.sum(p, axis=-1, keepdims=True)
    pv = _dot(p.astype(BF16), v.astype(BF16))
    return (jnp.where(even, m[:BAND], m[BAND:]), jnp.where(even, l[:BAND], l[BAND:]),
            jnp.where(even, pv[:BAND], pv[BAND:]))


def _attn_prompt_kernel(q_ref, k_ref, v_ref, b1_ref, b4_ref, b16_ref, o_ref,
                        m1, l1, a1, m4, l4, a4, m16, l16, a16, qc, kc, vc):
    seq = q_ref.shape[0]
    n4 = seq // 4
    per_trip = seq // BAND // 4
    even = lax.broadcasted_iota(jnp.int32, (BAND, LANES), 1) < HEAD_DIM

    def store_all(refs, rows_list, results):
        for rows, (m, l, a) in zip(rows_list, results):
            refs[0][rows, :], refs[1][rows, :], refs[2][rows, :] = m, l, a

    def body(r, carry):
        rows1, loaded1 = [], []
        for j in range(per_trip):
            i = r * per_trip + j
            r0 = pl.multiple_of(i * BAND, BAND)
            k0 = pl.multiple_of(jnp.maximum(i - 1, 0) * BAND, BAND)
            rows1.append(pl.ds(r0, BAND))
            loaded1.append((q_ref[pl.ds(r0, BAND)], k_ref[pl.ds(k0, 2 * BAND)], v_ref[pl.ds(k0, 2 * BAND)],
                            b1_ref[jnp.where(i == 0, 1, 0)]))
        base = pl.multiple_of(r * n4, BAND)
        q4 = q_ref[pl.ds(r, n4, stride=4), :]
        k4 = k_ref[pl.ds(r, n4, stride=4), :]
        v4 = v_ref[pl.ds(r, n4, stride=4), :]
        qc[pl.ds(base, n4)], kc[pl.ds(base, n4)], vc[pl.ds(base, n4)] = q4, k4, v4
        rows16 = [pl.ds(r * n4 + s, seq // 16, stride=4) for s in range(4)]
        loaded16 = [(qc[rows, :], kc[rows, :], vc[rows, :]) for rows in rows16]

        results1 = [_band_attn(q, k, v, bias, even) for q, k, v, bias in loaded1]
        results4, rows4 = [], []
        for i in range(n4 // BAND):
            lo = max(i - 1, 0) * BAND
            hi = (i + 1) * BAND
            col = 0 if i > 0 else BAND
            results4.append(_band_attn(q4[i * BAND:hi], k4[lo:hi], v4[lo:hi], b4_ref[:, col:], even))
            rows4.append(pl.ds(base + i * BAND, BAND))
        results16 = [_band_attn(q, k, v, b16_ref[:, BAND:], even) for q, k, v in loaded16]

        store_all((m1, l1, a1), rows1, results1)
        store_all((m4, l4, a4), rows4, results4)
        store_all((m16, l16, a16), rows16, results16)
        return carry

    lax.fori_loop(0, 4, body, 0)

    def merge_body(i, carry):
        r = i // (n4 // BAND)
        c = i % (n4 // BAND)
        rows = pl.ds(pl.multiple_of(i * BAND, BAND), BAND)
        nat = pl.ds(r + 4 * BAND * c, BAND, stride=4)
        ma, mb, mc = m1[nat, :], m4[rows], m16[rows]
        mx = jnp.maximum(jnp.maximum(ma, mb), mc)
        wa, wb, wc = jnp.exp(ma - mx), jnp.exp(mb - mx), jnp.exp(mc - mx)
        num = wa * a1[nat, :] + wb * a4[rows] + wc * a16[rows]
        den = wa * l1[nat, :] + wb * l4[rows] + wc * l16[rows]
        o_ref[nat, :] = num / den
        return carry

    lax.fori_loop(0, seq // BAND, merge_body, 0)


def _prompt_bias_tables(rel_bias):
    pairs = N_HEADS // 2
    first = _band_bucket_table(1)
    first = np.concatenate([first[:, BAND:], np.full((BAND, BAND), -1, np.int32)], axis=1)
    buckets = np.concatenate([_band_bucket_table(1), first, _band_bucket_table(4), _band_bucket_table(16)], axis=0)
    tables = _bias_table(rel_bias, buckets).reshape(N_HEADS, 4, BAND, 2 * BAND)
    b1, b1_first, b4, b16 = (tables[:, i].reshape(pairs, 2 * BAND, 2 * BAND) for i in range(4))
    return jnp.stack([b1, b1_first], axis=1), b4, b16


def _attn_prompt(q, k, v, b1, b4, b16, batch, seq):
    blk = pl.BlockSpec((seq, LANES), lambda b, j: (b, j))
    bias_spec = pl.BlockSpec((None, 2 * BAND, 2 * BAND), lambda b, j: (j, 0, 0))
    return pl.pallas_call(
        _attn_prompt_kernel,
        out_shape=jax.ShapeDtypeStruct(q.shape, F32),
        grid=(batch, D_ATTN // LANES),
        in_specs=[blk, blk, blk, pl.BlockSpec((None, 2, 2 * BAND, 2 * BAND), lambda b, j: (j, 0, 0, 0)),
                  bias_spec, bias_spec],
        out_specs=blk,
        scratch_shapes=[pltpu.VMEM((seq, LANES), F32) for _ in range(12)],
        compiler_params=_cparams("parallel", "parallel"),
        name="attn_prompt",
    )(q, k, v, b1, b4, b16)


def _attn_sample_kernel(q_ref, kn_ref, vn_ref, kt_ref, vt_ref, b1_ref, b4_ref, b16_ref, bn_ref, o_ref):
    t_len = q_ref.shape[1]
    rows = N_HEADS * t_len
    q = q_ref[0]
    head_of_row = lax.broadcasted_iota(jnp.int32, (rows, D_ATTN), 0) // t_len
    head_of_lane = lax.broadcasted_iota(jnp.int32, (rows, D_ATTN), 1) // HEAD_DIM
    own = head_of_row == head_of_lane
    qrows = jnp.where(own, jnp.concatenate([q] * N_HEADS, axis=0), 0.0).astype(BF16)
    pad = jnp.zeros((LANES - t_len, D_ATTN), F32)
    kn = jnp.concatenate([kn_ref[0], pad], axis=0).astype(BF16)
    vn = jnp.concatenate([vn_ref[0], pad], axis=0).astype(BF16)
    kt = kt_ref[0].reshape(D_ATTN, -1).astype(BF16)
    vt = vt_ref[0].reshape(D_ATTN, -1).astype(BF16)
    s_cache = _dot(qrows, kt)
    s_new = _dot_nt(qrows, kn)
    ms, ls, accs = [], [], []
    for p, bias_ref in enumerate((b1_ref, b4_ref, b16_ref)):
        w = bias_ref.shape[1]
        sc = s_cache[:, -w:] + bias_ref[...]
        sn = s_new + bn_ref[p]
        m = jnp.maximum(jnp.max(sc, axis=-1, keepdims=True), jnp.max(sn, axis=-1, keepdims=True))
        pc = jnp.exp(sc - m)
        pn = jnp.exp(sn - m)
        ls.append(jnp.sum(pc, axis=-1, keepdims=True) + jnp.sum(pn, axis=-1, keepdims=True))
        accs.append(_dot_nt(pc.astype(BF16), vt[:, -w:]) + _dot(pn.astype(BF16), vn))
        ms.append(m)
    mx = jnp.maximum(jnp.maximum(ms[0], ms[1]), ms[2])
    ws = [jnp.exp(m - mx) for m in ms]
    num = ws[0] * accs[0] + ws[1] * accs[1] + ws[2] * accs[2]
    den = ws[0] * ls[0] + ws[1] * ls[1] + ws[2] * ls[2]
    full = jnp.where(own, num / den, 0.0)
    out = full[0:t_len]
    for h in range(1, N_HEADS):
        out = out + full[h * t_len:(h + 1) * t_len]
    o_ref[0] = out


def _sample_attention(rel_bias, q, kn, vn, ck, cv):
    b, t_len, _ = q.shape
    w_buf = ck.shape[1]
    rows = N_HEADS * t_len
    tables, new_t = _sample_bucket_tables(w_buf, t_len)
    widest = max(tb.shape[1] for tb in tables)
    pieces = tables + [new_t.reshape(3 * t_len, LANES)]
    stacked = np.concatenate([np.pad(tb, ((0, 0), (0, widest - tb.shape[1])), constant_values=-1) for tb in pieces])
    bias = _bias_table(rel_bias, stacked)
    b1, b4, b16 = (bias[:, i * t_len:(i + 1) * t_len, :tb.shape[1]].reshape(rows, tb.shape[1])
                   for i, tb in enumerate(tables))
    bn = bias[:, 3 * t_len:, :LANES].reshape(N_HEADS, 3, t_len, LANES).transpose(1, 0, 2, 3).reshape(3, rows, LANES)
    new_spec = pl.BlockSpec((1, t_len, D_ATTN), lambda i: (i, 0, 0))
    cache_spec = pl.BlockSpec((1, N_HEADS, HEAD_DIM, w_buf), lambda i: (i, 0, 0, 0))
    const2 = lambda i: (0, 0)
    return pl.pallas_call(
        _attn_sample_kernel,
        out_shape=jax.ShapeDtypeStruct(q.shape, F32),
        grid=(b,),
        in_specs=[new_spec, new_spec, new_spec, cache_spec, cache_spec,
                  pl.BlockSpec(b1.shape, const2), pl.BlockSpec(b4.shape, const2), pl.BlockSpec(b16.shape, const2),
                  pl.BlockSpec(bn.shape, lambda i: (0, 0, 0))],
        out_specs=new_spec,
        compiler_params=_cparams("parallel"),
        name="attn_sample",
    )(q, kn, vn, ck.transpose(0, 2, 3, 1), cv.transpose(0, 2, 3, 1), b1, b4, b16, bn)


def _sgu_kernel(u_ref, g_ref, mix_ref, bias_ref, lng_ref, lnb_ref, sgu_ref, *z2_out):
    n_chunks = u_ref.shape[0] // CHUNK
    group_of_lane = lax.broadcasted_iota(jnp.int32, (CHUNK, D_SGU), 1) // (D_SGU // N_GROUPS_SGU)
    for c in range(n_chunks):
        rows = slice(c * CHUNK, (c + 1) * CHUNK)
        z1 = _gelu(u_ref[rows])
        z2 = _layer_norm(_gelu(g_ref[rows]), lng_ref[...], lnb_ref[...])
        if z2_out:
            z2_out[0][rows] = z2
        mixed = bias_ref[...]
        for g in range(N_GROUPS_SGU):
            mixed = mixed + _dot(mix_ref[g], jnp.where(group_of_lane == g, z2, 0.0).astype(BF16))
        sgu_ref[rows] = (z1 * mixed).astype(sgu_ref.dtype)


def _sgu(u, g, mix, bias, ln_g, ln_b, want_z2, chunks_per_step):
    m = u.shape[0]
    tm = CHUNK * chunks_per_step
    row_spec = pl.BlockSpec((tm, D_SGU), lambda i: (i, 0))
    const2 = lambda i: (0, 0)
    out_shape = [jax.ShapeDtypeStruct((m, D_SGU), BF16)]
    out_specs = [row_spec]
    if want_z2:
        out_shape.append(jax.ShapeDtypeStruct((m, D_SGU), F32))
        out_specs.append(row_spec)
    return pl.pallas_call(
        _sgu_kernel,
        out_shape=out_shape,
        grid=(m // tm,),
        in_specs=[row_spec, row_spec,
                  pl.BlockSpec(mix.shape, lambda i: (0, 0, 0)),
                  pl.BlockSpec(bias.shape, const2),
                  pl.BlockSpec(ln_g.shape, const2), pl.BlockSpec(ln_b.shape, const2)],
        out_specs=out_specs,
        compiler_params=_cparams("parallel"),
        name="sgu",
    )(u, g, mix, bias, ln_g, ln_b)


def _top1_rows(x, row):
    m = jnp.max(x, axis=0, keepdims=True)
    i = jnp.min(jnp.where(x == m, row, x.shape[0]), axis=0, keepdims=True)
    return m, i


def _pool_specs(n_prompt, tm, width):
    tiles_p = n_prompt // tm
    return (pl.BlockSpec((tm, width), lambda i, *_: (jnp.minimum(i, tiles_p - 1), 0)),
            pl.BlockSpec((tm, width), lambda i, *_: (jnp.maximum(i - tiles_p, 0), 0)))


def _pool_rows(tiles_p, prompt_ref, sample_ref, dtype):
    return jnp.where(pl.program_id(0) < tiles_p, prompt_ref[...].astype(dtype), sample_ref[...].astype(dtype))


def _mid_kernel(alpha, tiles_p, xp_ref, xs_ref, ap_ref, as_ref, sp_ref, ss_ref, wmix_ref, g1_ref, b1_ref,
                wrt_ref, rb_ref, x1_ref, x1p_ref, idx_ref, gate_ref):
    y = (_dot(_pool_rows(tiles_p, ap_ref, as_ref, BF16), wmix_ref[0:D_ATTN])
         + _dot(_pool_rows(tiles_p, sp_ref, ss_ref, BF16), wmix_ref[D_ATTN:])
         + alpha * _pool_rows(tiles_p, xp_ref, xs_ref, F32))
    x1 = _layer_norm(y, g1_ref[...], b1_ref[...])
    x1_ref[...] = x1
    tm = x1.shape[0]
    x1b = x1.astype(BF16)
    x1p_ref[...] = _pack_bf16_pairs(x1b)
    scores = _sigmoid(_dot_nt(wrt_ref[...], x1b))
    biased = scores + rb_ref[...]
    per_group = N_EXPERTS // N_EXPERT_GROUPS
    row_g = lax.broadcasted_iota(jnp.int32, (per_group, tm), 0)
    row_8 = lax.broadcasted_iota(jnp.int32, (N_EXPERT_GROUPS, tm), 0)
    row_e = lax.broadcasted_iota(jnp.int32, (N_EXPERTS, tm), 0)
    cur = jnp.full((N_EXPERT_GROUPS, tm), NEG_INF, F32)
    for g in range(N_EXPERT_GROUPS):
        blk = biased[g * per_group:(g + 1) * per_group]
        m_a, i_a = _top1_rows(blk, row_g)
        m_b = jnp.max(jnp.where(row_g == i_a, NEG_INF, blk), axis=0, keepdims=True)
        cur = jnp.where(row_8 == g, m_a + m_b, cur)
    chosen = jnp.zeros((N_EXPERT_GROUPS, tm), F32)
    for _ in range(TOPK_GROUPS):
        _, i_g = _top1_rows(cur, row_8)
        sel = row_8 == i_g
        chosen = jnp.where(sel, 1.0, chosen)
        cur = jnp.where(sel, NEG_INF, cur)
    cur = jnp.concatenate([jnp.where(chosen[g:g + 1] > 0.0, biased[g * per_group:(g + 1) * per_group], NEG_INF)
                           for g in range(N_EXPERT_GROUPS)], axis=0)
    idx = jnp.zeros((TOP_K, tm), jnp.int32)
    gates = jnp.zeros((TOP_K, tm), F32)
    for k in range(TOP_K):
        _, i_e = _top1_rows(cur, row_e)
        sel = row_e == i_e
        gate_k = jnp.sum(jnp.where(sel, scores, 0.0), axis=0, keepdims=True)
        cur = jnp.where(sel, NEG_INF, cur)
        idx = jnp.where(row_8 == k, i_e, idx)
        gates = jnp.where(row_8 == k, gate_k, gates)
    idx_ref[...] = idx
    gate_ref[...] = gates / jnp.sum(gates, axis=0, keepdims=True) * ROUTED_SCALE


def _mid(xp, xs, attn_p, attn_s, sgu_p, sgu_s, wmix, g1, b1, wrt, rb, alpha, tm):
    n_p, d = xp.shape
    t = n_p + xs.shape[0]
    const2 = lambda i: (0, 0)
    return pl.pallas_call(
        functools.partial(_mid_kernel, alpha, n_p // tm),
        out_shape=[jax.ShapeDtypeStruct((t, d), F32),
                   jax.ShapeDtypeStruct((t, d // 2), jnp.uint32),
                   jax.ShapeDtypeStruct((TOP_K, t), jnp.int32),
                   jax.ShapeDtypeStruct((TOP_K, t), F32)],
        grid=(t // tm,),
        in_specs=[*_pool_specs(n_p, tm, d), *_pool_specs(n_p, tm, D_ATTN), *_pool_specs(n_p, tm, D_SGU),
                  pl.BlockSpec(wmix.shape, const2), pl.BlockSpec(g1.shape, const2),
                  pl.BlockSpec(b1.shape, const2), pl.BlockSpec(wrt.shape, const2),
                  pl.BlockSpec(rb.shape, const2)],
        out_specs=[pl.BlockSpec((tm, d), lambda i: (i, 0)),
                   pl.BlockSpec((tm, d // 2), lambda i: (i, 0)),
                   pl.BlockSpec((TOP_K, tm), lambda i: (0, i)),
                   pl.BlockSpec((TOP_K, tm), lambda i: (0, i))],
        compiler_params=_cparams("parallel"),
        name="mix_ln_router",
    )(xp, xs, attn_p, attn_s, sgu_p, sgu_s, wmix, g1, b1, wrt, rb)


def _rank_kernel(idx_ref, rank_ref, counts_ref, run_ref):
    @pl.when(pl.program_id(0) == 0)
    def _():
        run_ref[...] = jnp.zeros_like(run_ref)

    tm = idx_ref.shape[1]
    row_e = lax.broadcasted_iota(jnp.int32, (N_EXPERTS, tm), 0)
    idx = idx_ref[...]
    onehot = jnp.zeros((N_EXPERTS, tm), F32)
    for k in range(TOP_K):
        onehot = onehot + jnp.where(row_e == idx[k:k + 1], 1.0, 0.0)
    earlier = (lax.broadcasted_iota(jnp.int32, (tm, tm), 0)
               < lax.broadcasted_iota(jnp.int32, (tm, tm), 1))
    before = run_ref[...] + _dot(onehot.astype(BF16), jnp.where(earlier, 1.0, 0.0).astype(BF16))
    row_k = lax.broadcasted_iota(jnp.int32, (TOP_K, tm), 0)
    ranks = jnp.zeros((TOP_K, tm), F32)
    for k in range(TOP_K):
        rank_k = jnp.sum(jnp.where(row_e == idx[k:k + 1], before, 0.0), axis=0, keepdims=True)
        ranks = jnp.where(row_k == k, rank_k, ranks)
    rank_ref[...] = ranks.astype(jnp.int32)
    run_ref[...] = run_ref[...] + jnp.sum(onehot, axis=1, keepdims=True)
    counts_ref[...] = run_ref[...]


def _rank(idx, tm):
    t = idx.shape[1]
    return pl.pallas_call(
        _rank_kernel,
        out_shape=[jax.ShapeDtypeStruct((TOP_K, t), jnp.int32),
                   jax.ShapeDtypeStruct((N_EXPERTS, 1), F32)],
        grid=(t // tm,),
        in_specs=[pl.BlockSpec((TOP_K, tm), lambda i: (0, i))],
        out_specs=[pl.BlockSpec((TOP_K, tm), lambda i: (0, i)),
                   pl.BlockSpec((N_EXPERTS, 1), lambda i: (0, 0))],
        scratch_shapes=[pltpu.VMEM((N_EXPERTS, 1), F32)],
        compiler_params=_cparams("arbitrary"),
        name="expert_rank",
    )(idx)


def _dest_kernel(idx_ref, rank_ref, pstart_ref, dest_ref):
    tm = idx_ref.shape[1]
    row_e = lax.broadcasted_iota(jnp.int32, (N_EXPERTS, tm), 0)
    row_k = lax.broadcasted_iota(jnp.int32, (TOP_K, tm), 0)
    idx = idx_ref[...]
    start = jnp.zeros((TOP_K, tm), F32)
    for k in range(TOP_K):
        start_k = jnp.sum(jnp.where(row_e == idx[k:k + 1], pstart_ref[...], 0.0), axis=0, keepdims=True)
        start = jnp.where(row_k == k, start_k, start)
    dest = start.astype(jnp.int32) + rank_ref[...]
    for c in range(tm // SC_CHUNK):
        dest_ref[c] = dest[:, c * SC_CHUNK:(c + 1) * SC_CHUNK]


def _dest(idx, rank, pstart_col, tm):
    t = idx.shape[1]
    tok_spec = pl.BlockSpec((TOP_K, tm), lambda i: (0, i))
    return pl.pallas_call(
        _dest_kernel,
        out_shape=jax.ShapeDtypeStruct((t // SC_CHUNK, TOP_K, SC_CHUNK), jnp.int32),
        grid=(t // tm,),
        in_specs=[tok_spec, tok_spec, pl.BlockSpec(pstart_col.shape, lambda i: (0, 0))],
        out_specs=pl.BlockSpec((tm // SC_CHUNK, TOP_K, SC_CHUNK), lambda i: (i, 0, 0)),
        compiler_params=_cparams("parallel"),
        name="moe_dest",
    )(idx, rank, pstart_col)


def _sc_mesh():
    return plsc.VectorSubcoreMesh(core_axis_name="c", subcore_axis_name="s",
                                  num_cores=SC_CORES, num_subcores=SC_SUBCORES)


def _sc_chunks(n_chunks):
    workers = SC_CORES * SC_SUBCORES
    wid = lax.axis_index("s") * SC_CORES + lax.axis_index("c")
    return wid, workers, (n_chunks - wid + workers - 1) // workers


def _sc_dispatch(x_rows, dest, n_rows):
    n_chunks = dest.shape[0]
    width = x_rows.shape[1]

    @functools.partial(
        pl.kernel, mesh=_sc_mesh(),
        out_type=jax.ShapeDtypeStruct((n_rows, width), x_rows.dtype),
        scratch_types=[pltpu.VMEM((TOP_K, SC_CHUNK), jnp.int32), pltpu.VMEM((SC_CHUNK, width), x_rows.dtype),
                       pltpu.SemaphoreType.DMA],
    )
    def scatter(x_hbm, dest_hbm, out_hbm, dest_v, rows_v, sem):
        wid, workers, n_own = _sc_chunks(n_chunks)

        @pl.loop(0, n_own)
        def _(j):
            ch = wid + j * workers
            pltpu.sync_copy(dest_hbm.at[ch], dest_v)
            pltpu.sync_copy(x_hbm.at[pl.ds(ch * SC_CHUNK, SC_CHUNK)], rows_v)
            copies = [pltpu.async_copy(rows_v, out_hbm.at[dest_v.at[k]], sem) for k in range(TOP_K)]
            for cp in copies:
                cp.wait()

    return scatter(x_rows, dest)


def _expert_kernel(first_ref, nblk_ref, count_ref, nused_ref, xs_hbm, wg_ref, wu_ref, wd_ref, ys_hbm,
                   xbuf, ybuf, xsem, ysem, wg_b, wu_b, wd_b):
    e = pl.program_id(0)
    n_used = nused_ref[0]
    slots, _, half = xbuf.shape

    def block_rows(j):
        return pl.ds(pl.multiple_of(j * EXPERT_ROWS, EXPERT_ROWS), EXPERT_ROWS)

    def x_copy(j, slot):
        return pltpu.make_async_copy(xs_hbm.at[block_rows(j)], xbuf.at[slot], xsem.at[slot])

    def y_copy(j, slot):
        return pltpu.make_async_copy(ybuf.at[slot], ys_hbm.at[block_rows(j)], ysem.at[slot])

    @pl.when(e == 0)
    def _():
        for k in range(slots - 1):
            @pl.when(k < n_used)
            def _():
                x_copy(k, k).start()

    @pl.when(nblk_ref[e] > 0)
    def _():
        wg_b[...] = wg_ref[0].astype(BF16)
        wu_b[...] = wu_ref[0].astype(BF16)
        wd_b[...] = wd_ref[0].astype(BF16)

    def block(b, carry):
        j = first_ref[e] + b
        slot = j % slots
        x_copy(j, slot).wait()
        ahead = j + slots - 1

        @pl.when(ahead < n_used)
        def _():
            x_copy(ahead, ahead % slots).start()

        @pl.when(j >= slots)
        def _():
            y_copy(j - slots, slot).wait()

        row = lax.broadcasted_iota(jnp.int32, (EXPERT_ROWS, half), 0)
        packed = jnp.where(row < count_ref[e] - b * EXPERT_ROWS, xbuf[slot], jnp.uint32(0))
        lo, hi = _unpack_bf16_pairs(packed)
        g = _dot(lo, wg_b[:half]) + _dot(hi, wg_b[half:])
        u = _dot(lo, wu_b[:half]) + _dot(hi, wu_b[half:])
        h = (g * _sigmoid(g) * u).astype(BF16)
        ybuf[slot] = _pack_bf16_pairs(_dot(h, wd_b[...]).astype(BF16))
        y_copy(j, slot).start()
        return carry

    lax.fori_loop(0, nblk_ref[e], block, 0)

    @pl.when(e == pl.num_programs(0) - 1)
    def _():
        for k in range(1, slots + 1):
            @pl.when(n_used >= k)
            def _():
                y_copy(n_used - k, (n_used - k) % slots).wait()


def _experts(first_block, n_block, counts, n_used, xs, wg, wu, wd):
    n_rows, half = xs.shape
    n_exp, d, de = wg.shape
    w_map = lambda e, *_: (e, 0, 0)
    return pl.pallas_call(
        _expert_kernel,
        out_shape=jax.ShapeDtypeStruct((n_rows, half), jnp.uint32),
        grid_spec=pltpu.PrefetchScalarGridSpec(
            num_scalar_prefetch=4,
            grid=(n_exp,),
            in_specs=[pl.BlockSpec(memory_space=pl.ANY),
                      pl.BlockSpec((1, d, de), w_map), pl.BlockSpec((1, d, de), w_map),
                      pl.BlockSpec((1, de, d), w_map)],
            out_specs=pl.BlockSpec(memory_space=pl.ANY),
            scratch_shapes=[pltpu.VMEM((EXPERT_SLOTS, EXPERT_ROWS, half), jnp.uint32),
                            pltpu.VMEM((EXPERT_SLOTS, EXPERT_ROWS, half), jnp.uint32),
                            pltpu.SemaphoreType.DMA((EXPERT_SLOTS,)), pltpu.SemaphoreType.DMA((EXPERT_SLOTS,)),
                            pltpu.VMEM((d, de), BF16), pltpu.VMEM((d, de), BF16), pltpu.VMEM((de, d), BF16)]),
        compiler_params=_cparams("arbitrary"),
        name="routed_experts",
    )(first_block, n_block, counts, n_used, xs, wg, wu, wd)


def _dense_kernel(alpha, tiles_p, x1_ref, pp_ref, ps_ref, wgs_ref, wus_ref, wds_ref, wpg_ref, wpp_ref, base_ref):
    x1 = x1_ref[...]
    xb = x1.astype(BF16)
    g = _dot(xb, wgs_ref[...])
    h = (g * _sigmoid(g) * _dot(xb, wus_ref[...])).astype(BF16)
    shared = _dot(h, wds_ref[...])
    ple = _sigmoid(_dot(xb, wpg_ref[...])) * _dot(_pool_rows(tiles_p, pp_ref, ps_ref, BF16), wpp_ref[...])
    base_ref[...] = alpha * x1 + shared + ple


def _dense(x1, pp, ps, wgs, wus, wds, wpg, wpp, alpha, tm):
    t, d = x1.shape
    const2 = lambda i: (0, 0)
    return pl.pallas_call(
        functools.partial(_dense_kernel, alpha, pp.shape[0] // tm),
        out_shape=jax.ShapeDtypeStruct((t, d), F32),
        grid=(t // tm,),
        in_specs=[pl.BlockSpec((tm, d), lambda i: (i, 0)), *_pool_specs(pp.shape[0], tm, pp.shape[1]),
                  pl.BlockSpec(wgs.shape, const2), pl.BlockSpec(wus.shape, const2),
                  pl.BlockSpec(wds.shape, const2), pl.BlockSpec(wpg.shape, const2),
                  pl.BlockSpec(wpp.shape, const2)],
        out_specs=pl.BlockSpec((tm, d), lambda i: (i, 0)),
        compiler_params=_cparams("parallel"),
        name="shared_ple",
    )(x1, pp, ps, wgs, wus, wds, wpg, wpp)


def _sc_gather(y_rows, dest, n_tok):
    n_chunks = dest.shape[0]
    width = y_rows.shape[1]
    items = [(k, h) for k in range(TOP_K) for h in range(SC_CHUNK // SC_GATHER_ROWS)]

    @functools.partial(
        pl.kernel, mesh=_sc_mesh(),
        out_type=jax.ShapeDtypeStruct((TOP_K, n_tok, width), y_rows.dtype),
        scratch_types=[pltpu.VMEM((TOP_K, SC_CHUNK), jnp.int32),
                       pltpu.VMEM((2, SC_GATHER_ROWS, width), y_rows.dtype),
                       pltpu.SemaphoreType.DMA, pltpu.SemaphoreType.DMA],
    )
    def gather(y_hbm, dest_hbm, out_hbm, dest_v, rows_v, gsem, wsem):
        wid, workers, n_own = _sc_chunks(n_chunks)

        @pl.loop(0, n_own)
        def _(j):
            ch = wid + j * workers
            pltpu.sync_copy(dest_hbm.at[ch], dest_v)

            def fetch(n):
                k, h = items[n]
                return pltpu.async_copy(y_hbm.at[dest_v.at[k, pl.ds(h * SC_GATHER_ROWS, SC_GATHER_ROWS)]],
                                        rows_v.at[n % 2], gsem)

            def write(n):
                k, h = items[n]
                rows = pl.ds(ch * SC_CHUNK + h * SC_GATHER_ROWS, SC_GATHER_ROWS)
                return pltpu.async_copy(rows_v.at[n % 2], out_hbm.at[k, rows], wsem)

            pending_fetch = fetch(0)
            pending_write = None
            for n in range(len(items)):
                pending_fetch.wait()
                if pending_write is not None:
                    pending_write.wait()
                if n + 1 < len(items):
                    pending_fetch = fetch(n + 1)
                pending_write = write(n)
            pending_write.wait()

    return gather(y_rows, dest)


def _combine_kernel(tiles_p, gate_ref, base_ref, yk_ref, g2_ref, b2_ref, outp_ref, outs_ref):
    gates = gate_ref[...]
    half = yk_ref.shape[2]
    acc_lo = base_ref[:, :half]
    acc_hi = base_ref[:, half:]
    for k in range(TOP_K):
        packed = yk_ref[k]
        gate = gates[:, k:k + 1]
        acc_lo = acc_lo + gate * lax.bitcast_convert_type(packed << 16, F32)
        acc_hi = acc_hi + gate * lax.bitcast_convert_type(packed & jnp.uint32(0xFFFF0000), F32)
    out = _layer_norm(jnp.concatenate([acc_lo, acc_hi], axis=1), g2_ref[...], b2_ref[...])
    is_prompt = pl.program_id(0) < tiles_p

    @pl.when(is_prompt)
    def _():
        outp_ref[...] = out

    @pl.when(jnp.logical_not(is_prompt))
    def _():
        outs_ref[...] = out


def _combine(gates_tok, base, yk, g2, b2, n_prompt, tm):
    t, d = base.shape
    const2 = lambda i: (0, 0)
    return pl.pallas_call(
        functools.partial(_combine_kernel, n_prompt // tm),
        out_shape=[jax.ShapeDtypeStruct((n_prompt, d), F32), jax.ShapeDtypeStruct((t - n_prompt, d), F32)],
        grid=(t // tm,),
        in_specs=[pl.BlockSpec((tm, TOP_K), lambda i: (i, 0)),
                  pl.BlockSpec((tm, d), lambda i: (i, 0)),
                  pl.BlockSpec((TOP_K, tm, yk.shape[2]), lambda i: (0, i, 0)),
                  pl.BlockSpec(g2.shape, const2), pl.BlockSpec(b2.shape, const2)],
        out_specs=list(_pool_specs(n_prompt, tm, d)),
        compiler_params=_cparams("arbitrary"),
        name="moe_combine",
    )(gates_tok, base, yk, g2, b2)


def _sgu_tables(sgu_w, sgu_b, rows_per_seq):
    reps = CHUNK // rows_per_seq
    tril = jnp.tril(sgu_w[:, :rows_per_seq, :rows_per_seq])
    eye = jnp.eye(reps, dtype=F32)
    mix = jnp.einsum("ab,gts->gatbs", eye, tril).reshape(N_GROUPS_SGU, CHUNK, CHUNK)
    bias = jnp.tile(jnp.repeat(sgu_b[:, :rows_per_seq].T, D_SGU // N_GROUPS_SGU, axis=1), (reps, 1))
    return mix.astype(BF16), bias


def _layer(xp, xs, ck, cv, pp, ps, w, rel_bias, alpha):
    batch, seq, d = xp.shape
    dec_b, dec_t, _ = xs.shape
    w_buf = ck.shape[1]
    n_p, n_s = batch * seq, dec_b * dec_t
    n_tok = n_p + n_s

    scale = jnp.concatenate([jnp.full((D_ATTN,), HEAD_DIM ** -0.5, F32),
                             jnp.ones((w["w_in"].shape[1] - D_ATTN,), F32)])
    w_in = (w["w_in"] * scale).astype(BF16)

    qp, kp, vp, up, gp, kp_t, vp_t = _proj(xp.reshape(n_p, d), w_in, 512, seq)
    qs, ks, vs, us, gs = _proj(xs.reshape(n_s, d), w_in, 512)

    b1, b4, b16 = _prompt_bias_tables(rel_bias)
    attn_p = _attn_prompt(qp, kp, vp, b1, b4, b16, batch, seq)

    attn_s = _sample_attention(rel_bias, qs.reshape(dec_b, dec_t, D_ATTN), ks.reshape(dec_b, dec_t, D_ATTN),
                               vs.reshape(dec_b, dec_t, D_ATTN), ck, cv).reshape(n_s, D_ATTN)

    ln_g, ln_b = w["sgu_ln_g"][None], w["sgu_ln_b"][None]
    mix_p, bias_p = _sgu_tables(w["sgu_w"], w["sgu_b"], CHUNK)
    mix_s, bias_s = _sgu_tables(w["sgu_w"], w["sgu_b"], dec_t)
    (sgu_p,) = _sgu(up, gp, mix_p, bias_p, ln_g, ln_b, False, 4)
    sgu_s, z2_s = _sgu(us, gs, mix_s, bias_s, ln_g, ln_b, True, 4)

    x1, x1_packed, idx, gates = _mid(xp.reshape(n_p, d), xs.reshape(n_s, d), attn_p, attn_s, sgu_p, sgu_s,
                                     w["w_mix_out"].astype(BF16), w["ln1_g"][None], w["ln1_b"][None],
                                     w["w_router"].T.astype(BF16), w["router_bias"][:, None], alpha, 512)

    rank, counts = _rank(idx, 512)
    counts = counts[:, 0].astype(jnp.int32)
    padded = (counts + EXPERT_ROWS - 1) // EXPERT_ROWS * EXPERT_ROWS
    pend = jnp.cumsum(padded)
    pstart = (pend - padded).astype(jnp.int32)
    n_blocks = (n_tok * TOP_K + N_EXPERTS * (EXPERT_ROWS - 1)) // EXPERT_ROWS
    n_used = (pend[-1:] // EXPERT_ROWS).astype(jnp.int32)

    dest = _dest(idx, rank, pstart.astype(F32)[:, None], 512)
    x_sorted = _sc_dispatch(x1_packed, dest, n_blocks * EXPERT_ROWS)
    y_sorted = _experts(pstart // EXPERT_ROWS, padded // EXPERT_ROWS, counts, n_used, x_sorted,
                        w["w_gate_e"], w["w_up_e"], w["w_down_e"])

    base = _dense(x1, pp.reshape(n_p, -1), ps.reshape(n_s, -1), w["w_gate_s"].astype(BF16), w["w_up_s"].astype(BF16),
                  w["w_down_s"].astype(BF16), w["w_ple_gate"].astype(BF16),
                  w["w_ple_proj"].astype(BF16), alpha, 1024)

    y_slots = _sc_gather(y_sorted, dest, n_tok)
    y_p, y_s = _combine(gates.T, base, y_slots, w["ln2_g"][None], w["ln2_b"][None], n_p, 512)
    y_p = y_p.reshape(batch, seq, d)
    y_s = y_s.reshape(dec_b, dec_t, d)
    keep = min(MAX_DISTANCE, seq)
    k_rows = kp_t.reshape(batch, N_HEADS, HEAD_DIM, seq)[..., seq - keep:].transpose(0, 3, 1, 2)
    v_rows = vp_t.reshape(batch, N_HEADS, HEAD_DIM, seq)[..., seq - keep:].transpose(0, 3, 1, 2)
    return (y_p, y_s, k_rows, v_rows,
            ks.reshape(dec_b, dec_t, N_HEADS, HEAD_DIM), vs.reshape(dec_b, dec_t, N_HEADS, HEAD_DIM),
            z2_s.reshape(dec_b, dec_t, D_SGU))


def kernel(x_prompt, x_sample, cache_k, cache_v, p_prompt, p_sample, w_in, rel_bias, sgu_w, sgu_b, sgu_ln_g, sgu_ln_b, w_mix_out, ln1_g, ln1_b, w_router, router_bias, w_gate_e, w_up_e, w_down_e, w_gate_s, w_up_s, w_down_s, w_ple_gate, w_ple_proj, ln2_g, ln2_b):
    depth = w_in.shape[0]
    alpha = (2 * depth) ** 0.25
    xp, xs = x_prompt, x_sample
    outs = [[] for _ in range(5)]
    for i in range(depth):
        w = {"w_in": w_in[i], "sgu_w": sgu_w[i], "sgu_b": sgu_b[i], "sgu_ln_g": sgu_ln_g[i],
             "sgu_ln_b": sgu_ln_b[i], "w_mix_out": w_mix_out[i], "ln1_g": ln1_g[i], "ln1_b": ln1_b[i],
             "w_router": w_router[i], "router_bias": router_bias[i], "w_gate_e": w_gate_e[i],
             "w_up_e": w_up_e[i], "w_down_e": w_down_e[i], "w_gate_s": w_gate_s[i], "w_up_s": w_up_s[i],
             "w_down_s": w_down_s[i], "w_ple_gate": w_ple_gate[i], "w_ple_proj": w_ple_proj[i],
             "ln2_g": ln2_g[i], "ln2_b": ln2_b[i]}
        xp, xs, kp, vp, ks, vs, zs = _layer(xp, xs, cache_k[i], cache_v[i], p_prompt[i], p_sample[i],
                                            w, rel_bias, alpha)
        for lst, val in zip(outs, (kp, vp, ks, vs, zs)):
            lst.append(val)
    return (xp, xs) + tuple(jnp.stack(lst) for lst in outs)
```

```python
import functools
import math

import numpy as np
import jax
import jax.numpy as jnp
from jax import lax
from jax.experimental import pallas as pl
from jax.experimental.pallas import tpu as pltpu
from jax.experimental.pallas import tpu_sc as plsc

F32 = jnp.float32
BF16 = jnp.bfloat16
NEG_INF = float("-inf")

N_HEADS = 8
HEAD_DIM = 64
D_ATTN = N_HEADS * HEAD_DIM
PATTERNS = ((128, 1), (512, 4), (2048, 16))
BAND = 128
N_BUCKETS = 32
MAX_DISTANCE = 2048
N_GROUPS_SGU = 8
D_SGU = 512
CHUNK = 128
N_EXPERTS = 256
TOP_K = 8
N_EXPERT_GROUPS = 8
TOPK_GROUPS = 4
ROUTED_SCALE = 2.5
LN_EPS = 1e-5
EXPERT_ROWS = 256
EXPERT_SLOTS = 6
COMBINE_PARTS = 4

LANES = 128
SC_CORES = 2
SC_SUBCORES = 16
SC_CHUNK = 128
SC_GATHER_ROWS = 64
VMEM_LIMIT = 56 * 1024 * 1024


def _cparams(*sem):
    return pltpu.CompilerParams(dimension_semantics=sem, vmem_limit_bytes=VMEM_LIMIT)


def _layer_norm(x, g, b):
    mu = jnp.mean(x, axis=-1, keepdims=True)
    xc = x - mu
    var = jnp.mean(xc * xc, axis=-1, keepdims=True)
    return xc * lax.rsqrt(var + LN_EPS) * g + b


def _sigmoid(x):
    return 1.0 / (1.0 + jnp.exp(-x))


def _gelu(x):
    return 0.5 * x * (1.0 + lax.erf(x * math.sqrt(0.5)))


def _pack_bf16_pairs(xb):
    n = xb.shape[1] // 2
    bits = lax.bitcast_convert_type(xb.astype(F32), jnp.uint32)
    return (bits[:, :n] >> 16) | (bits[:, n:] & jnp.uint32(0xFFFF0000))


def _unpack_bf16_pairs(p):
    lo = lax.bitcast_convert_type(p << 16, F32).astype(BF16)
    hi = lax.bitcast_convert_type(p & jnp.uint32(0xFFFF0000), F32).astype(BF16)
    return lo, hi


def _dot(a, b):
    return jnp.dot(a, b, preferred_element_type=F32)


def _dot_nt(a, b):
    return lax.dot_general(a, b, (((1,), (1,)), ((), ())), preferred_element_type=F32)


def _t5_bucket_np(dist):
    max_exact = N_BUCKETS // 2
    df = np.maximum(dist, max_exact).astype(np.float32)
    large = max_exact + (np.log(df / np.float32(max_exact)) / np.float32(math.log(MAX_DISTANCE / max_exact))
                         * np.float32(N_BUCKETS - max_exact)).astype(np.int32)
    return np.where(dist < max_exact, dist, np.minimum(large, N_BUCKETS - 1)).astype(np.int32)


def _band_bucket_table(dilation):
    qi = np.arange(BAND)[:, None]
    ki = np.arange(2 * BAND)[None, :]
    dsub = qi + BAND - ki
    valid = (dsub >= 0) & (dsub <= BAND)
    return np.where(valid, _t5_bucket_np(np.clip(dsub, 0, BAND) * dilation), -1).astype(np.int32)


def _sample_bucket_tables(w_buf, t_len):
    t = np.arange(t_len)[:, None]

    def table(rows, window, dilation):
        d = w_buf + t - rows[None, :]
        ok = (d >= 0) & (d % dilation == 0) & (d <= window)
        return np.where(ok, _t5_bucket_np(np.maximum(d, 0)), -1).astype(np.int32)

    assert w_buf >= PATTERNS[-1][0]
    tables = [table(np.arange(w_buf - window, w_buf), window, dilation) for window, dilation in PATTERNS]
    new_rows = w_buf + np.arange(LANES)
    new = np.stack([table(new_rows, window, dilation) for window, dilation in PATTERNS])
    new[:, :, t_len:] = -1
    return tables, new


def _bias_kernel(rb_ref, bucket_ref, out_ref):
    bucket = bucket_ref[...]
    for h in range(N_HEADS):
        acc = jnp.full(bucket.shape, NEG_INF, F32)
        for b in range(N_BUCKETS):
            acc = jnp.where(bucket == b, rb_ref[b, h], acc)
        out_ref[h] = acc


def _bias_table(rel_bias, bucket_np):
    r, c = bucket_np.shape
    return pl.pallas_call(
        _bias_kernel,
        out_shape=jax.ShapeDtypeStruct((N_HEADS, r, c), F32),
        in_specs=[pl.BlockSpec(memory_space=pltpu.SMEM), pl.BlockSpec(memory_space=pltpu.VMEM)],
        out_specs=pl.BlockSpec(memory_space=pltpu.VMEM),
        name="bias_table",
    )(rel_bias, jnp.asarray(bucket_np))


def _proj_kernel(x_ref, w_ref, q_ref, k_ref, v_ref, u_ref, g_ref, *kv_t_refs):
    x = x_ref[...].astype(BF16)
    col = 0
    rows = []
    for o in (q_ref, k_ref, v_ref, u_ref, g_ref):
        n = o.shape[1]
        rows.append(_dot(x, w_ref[:, col:col + n]))
        o[...] = rows[-1]
        col += n
    for o, val in zip(kv_t_refs, rows[1:3]):
        o[0] = val.T


def _proj(x, w, tm, seq=None):
    m, d = x.shape
    n_out = (D_ATTN, D_ATTN, D_ATTN, D_SGU, D_SGU)
    out_shape = [jax.ShapeDtypeStruct((m, n), F32) for n in n_out]
    out_specs = [pl.BlockSpec((tm, n), lambda i: (i, 0)) for n in n_out]
    if seq is not None:
        tiles = seq // tm
        out_shape += [jax.ShapeDtypeStruct((m // seq, D_ATTN, seq), F32)] * 2
        out_specs += [pl.BlockSpec((1, D_ATTN, tm), lambda i: (i // tiles, 0, i % tiles))] * 2
    return pl.pallas_call(
        _proj_kernel,
        out_shape=out_shape,
        grid=(m // tm,),
        in_specs=[pl.BlockSpec((tm, d), lambda i: (i, 0)), pl.BlockSpec(w.shape, lambda i: (0, 0))],
        out_specs=out_specs,
        compiler_params=_cparams("parallel"),
        name="in_proj",
    )(x, w)


def _band_attn(q, k, v, bias, even):
    q2 = jnp.concatenate([jnp.where(even, q, 0.0), jnp.where(even, 0.0, q)], axis=0).astype(BF16)
    s = _dot_nt(q2, k.astype(BF16)) + bias
    m = jnp.max(s, axis=-1, keepdims=True)
    p = jnp.exp(s - m)
    l = jnp.sum(p, axis=-1, keepdims=True)
    pv = _dot(p.astype(BF16), v.astype(BF16))
    return (jnp.where(even, m[:BAND], m[BAND:]), jnp.where(even, l[:BAND], l[BAND:]),
            jnp.where(even, pv[:BAND], pv[BAND:]))


def _attn_prompt_kernel(q_ref, k_ref, v_ref, b1_ref, b4_ref, b16_ref, o_ref,
                        m1, l1, a1, m4, l4, a4, m16, l16, a16, qc, kc, vc):
    seq = q_ref.shape[0]
    n4 = seq // 4
    per_trip = seq // BAND // 4
    even = lax.broadcasted_iota(jnp.int32, (BAND, LANES), 1) < HEAD_DIM

    def store_all(refs, rows_list, results):
        for rows, (m, l, a) in zip(rows_list, results):
            refs[0][rows, :], refs[1][rows, :], refs[2][rows, :] = m, l, a

    def body(r, carry):
        rows1, loaded1 = [], []
        for j in range(per_trip):
            i = r * per_trip + j
            r0 = pl.multiple_of(i * BAND, BAND)
            k0 = pl.multiple_of(jnp.maximum(i - 1, 0) * BAND, BAND)
            rows1.append(pl.ds(r0, BAND))
            loaded1.append((q_ref[pl.ds(r0, BAND)], k_ref[pl.ds(k0, 2 * BAND)], v_ref[pl.ds(k0, 2 * BAND)],
                            b1_ref[jnp.where(i == 0, 1, 0)]))
        base = pl.multiple_of(r * n4, BAND)
        q4 = q_ref[pl.ds(r, n4, stride=4), :]
        k4 = k_ref[pl.ds(r, n4, stride=4), :]
        v4 = v_ref[pl.ds(r, n4, stride=4), :]
        qc[pl.ds(base, n4)], kc[pl.ds(base, n4)], vc[pl.ds(base, n4)] = q4, k4, v4
        rows16 = [pl.ds(r * n4 + s, seq // 16, stride=4) for s in range(4)]
        loaded16 = [(qc[rows, :], kc[rows, :], vc[rows, :]) for rows in rows16]

        results1 = [_band_attn(q, k, v, bias, even) for q, k, v, bias in loaded1]
        results4, rows4 = [], []
        for i in range(n4 // BAND):
            lo = max(i - 1, 0) * BAND
            hi = (i + 1) * BAND
            col = 0 if i > 0 else BAND
            results4.append(_band_attn(q4[i * BAND:hi], k4[lo:hi], v4[lo:hi], b4_ref[:, col:], even))
            rows4.append(pl.ds(base + i * BAND, BAND))
        results16 = [_band_attn(q, k, v, b16_ref[:, BAND:], even) for q, k, v in loaded16]

        store_all((m1, l1, a1), rows1, results1)
        store_all((m4, l4, a4), rows4, results4)
        store_all((m16, l16, a16), rows16, results16)
        return carry

    lax.fori_loop(0, 4, body, 0)

    def merge_body(i, carry):
        r = i // (n4 // BAND)
        c = i % (n4 // BAND)
        rows = pl.ds(pl.multiple_of(i * BAND, BAND), BAND)
        nat = pl.ds(r + 4 * BAND * c, BAND, stride=4)
        ma, mb, mc = m1[nat, :], m4[rows], m16[rows]
        mx = jnp.maximum(jnp.maximum(ma, mb), mc)
        wa, wb, wc = jnp.exp(ma - mx), jnp.exp(mb - mx), jnp.exp(mc - mx)
        num = wa * a1[nat, :] + wb * a4[rows] + wc * a16[rows]
        den = wa * l1[nat, :] + wb * l4[rows] + wc * l16[rows]
        o_ref[nat, :] = num / den
        return carry

    lax.fori_loop(0, seq // BAND, merge_body, 0)


def _prompt_bias_tables(rel_bias):
    pairs = N_HEADS // 2
    first = _band_bucket_table(1)
    first = np.concatenate([first[:, BAND:], np.full((BAND, BAND), -1, np.int32)], axis=1)
    buckets = np.concatenate([_band_bucket_table(1), first, _band_bucket_table(4), _band_bucket_table(16)], axis=0)
    tables = _bias_table(rel_bias, buckets).reshape(N_HEADS, 4, BAND, 2 * BAND)
    b1, b1_first, b4, b16 = (tables[:, i].reshape(pairs, 2 * BAND, 2 * BAND) for i in range(4))
    return jnp.stack([b1, b1_first], axis=1), b4, b16


def _attn_prompt(q, k, v, b1, b4, b16, batch, seq):
    blk = pl.BlockSpec((seq, LANES), lambda b, j: (b, j))
    bias_spec = pl.BlockSpec((None, 2 * BAND, 2 * BAND), lambda b, j: (j, 0, 0))
    return pl.pallas_call(
        _attn_prompt_kernel,
        out_shape=jax.ShapeDtypeStruct(q.shape, F32),
        grid=(batch, D_ATTN // LANES),
        in_specs=[blk, blk, blk, pl.BlockSpec((None, 2, 2 * BAND, 2 * BAND), lambda b, j: (j, 0, 0, 0)),
                  bias_spec, bias_spec],
        out_specs=blk,
        scratch_shapes=[pltpu.VMEM((seq, LANES), F32) for _ in range(12)],
        compiler_params=_cparams("parallel", "parallel"),
        name="attn_prompt",
    )(q, k, v, b1, b4, b16)


def _attn_sample_kernel(q_ref, kn_ref, vn_ref, kt_ref, vt_ref, b1_ref, b4_ref, b16_ref, bn_ref, o_ref):
    t_len = q_ref.shape[1]
    rows = N_HEADS * t_len
    q = q_ref[0]
    head_of_row = lax.broadcasted_iota(jnp.int32, (rows, D_ATTN), 0) // t_len
    head_of_lane = lax.broadcasted_iota(jnp.int32, (rows, D_ATTN), 1) // HEAD_DIM
    own = head_of_row == head_of_lane
    qrows = jnp.where(own, jnp.concatenate([q] * N_HEADS, axis=0), 0.0).astype(BF16)
    pad = jnp.zeros((LANES - t_len, D_ATTN), F32)
    kn = jnp.concatenate([kn_ref[0], pad], axis=0).astype(BF16)
    vn = jnp.concatenate([vn_ref[0], pad], axis=0).astype(BF16)
    kt = kt_ref[0].reshape(D_ATTN, -1).astype(BF16)
    vt = vt_ref[0].reshape(D_ATTN, -1).astype(BF16)
    s_cache = _dot(qrows, kt)
    s_new = _dot_nt(qrows, kn)
    ms, ls, accs = [], [], []
    for p, bias_ref in enumerate((b1_ref, b4_ref, b16_ref)):
        w = bias_ref.shape[1]
        sc = s_cache[:, -w:] + bias_ref[...]
        sn = s_new + bn_ref[p]
        m = jnp.maximum(jnp.max(sc, axis=-1, keepdims=True), jnp.max(sn, axis=-1, keepdims=True))
        pc = jnp.exp(sc - m)
        pn = jnp.exp(sn - m)
        ls.append(jnp.sum(pc, axis=-1, keepdims=True) + jnp.sum(pn, axis=-1, keepdims=True))
        accs.append(_dot_nt(pc.astype(BF16), vt[:, -w:]) + _dot(pn.astype(BF16), vn))
        ms.append(m)
    mx = jnp.maximum(jnp.maximum(ms[0], ms[1]), ms[2])
    ws = [jnp.exp(m - mx) for m in ms]
    num = ws[0] * accs[0] + ws[1] * accs[1] + ws[2] * accs[2]
    den = ws[0] * ls[0] + ws[1] * ls[1] + ws[2] * ls[2]
    full = jnp.where(own, num / den, 0.0)
    out = full[0:t_len]
    for h in range(1, N_HEADS):
        out = out + full[h * t_len:(h + 1) * t_len]
    o_ref[0] = out


def _sample_attention(rel_bias, q, kn, vn, ck, cv):
    b, t_len, _ = q.shape
    w_buf = ck.shape[1]
    rows = N_HEADS * t_len
    tables, new_t = _sample_bucket_tables(w_buf, t_len)
    widest = max(tb.shape[1] for tb in tables)
    pieces = tables + [new_t.reshape(3 * t_len, LANES)]
    stacked = np.concatenate([np.pad(tb, ((0, 0), (0, widest - tb.shape[1])), constant_values=-1) for tb in pieces])
    bias = _bias_table(rel_bias, stacked)
    b1, b4, b16 = (bias[:, i * t_len:(i + 1) * t_len, :tb.shape[1]].reshape(rows, tb.shape[1])
                   for i, tb in enumerate(tables))
    bn = bias[:, 3 * t_len:, :LANES].reshape(N_HEADS, 3, t_len, LANES).transpose(1, 0, 2, 3).reshape(3, rows, LANES)
    new_spec = pl.BlockSpec((1, t_len, D_ATTN), lambda i: (i, 0, 0))
    cache_spec = pl.BlockSpec((1, N_HEADS, HEAD_DIM, w_buf), lambda i: (i, 0, 0, 0))
    const2 = lambda i: (0, 0)
    return pl.pallas_call(
        _attn_sample_kernel,
        out_shape=jax.ShapeDtypeStruct(q.shape, F32),
        grid=(b,),
        in_specs=[new_spec, new_spec, new_spec, cache_spec, cache_spec,
                  pl.BlockSpec(b1.shape, const2), pl.BlockSpec(b4.shape, const2), pl.BlockSpec(b16.shape, const2),
                  pl.BlockSpec(bn.shape, lambda i: (0, 0, 0))],
        out_specs=new_spec,
        compiler_params=_cparams("parallel"),
        name="attn_sample",
    )(q, kn, vn, ck.transpose(0, 2, 3, 1), cv.transpose(0, 2, 3, 1), b1, b4, b16, bn)


def _sgu_kernel(u_ref, g_ref, mix_ref, bias_ref, lng_ref, lnb_ref, sgu_ref, *z2_out):
    n_chunks = u_ref.shape[0] // CHUNK
    group_of_lane = lax.broadcasted_iota(jnp.int32, (CHUNK, D_SGU), 1) // (D_SGU // N_GROUPS_SGU)
    for c in range(n_chunks):
        rows = slice(c * CHUNK, (c + 1) * CHUNK)
        z1 = _gelu(u_ref[rows])
        z2 = _layer_norm(_gelu(g_ref[rows]), lng_ref[...], lnb_ref[...])
        if z2_out:
            z2_out[0][rows] = z2
        mixed = bias_ref[...]
        for g in range(N_GROUPS_SGU):
            mixed = mixed + _dot(mix_ref[g], jnp.where(group_of_lane == g, z2, 0.0).astype(BF16))
        sgu_ref[rows] = (z1 * mixed).astype(sgu_ref.dtype)


def _sgu(u, g, mix, bias, ln_g, ln_b, want_z2, chunks_per_step):
    m = u.shape[0]
    tm = CHUNK * chunks_per_step
    row_spec = pl.BlockSpec((tm, D_SGU), lambda i: (i, 0))
    const2 = lambda i: (0, 0)
    out_shape = [jax.ShapeDtypeStruct((m, D_SGU), BF16)]
    out_specs = [row_spec]
    if want_z2:
        out_shape.append(jax.ShapeDtypeStruct((m, D_SGU), F32))
        out_specs.append(row_spec)
    return pl.pallas_call(
        _sgu_kernel,
        out_shape=out_shape,
        grid=(m // tm,),
        in_specs=[row_spec, row_spec,
                  pl.BlockSpec(mix.shape, lambda i: (0, 0, 0)),
                  pl.BlockSpec(bias.shape, const2),
                  pl.BlockSpec(ln_g.shape, const2), pl.BlockSpec(ln_b.shape, const2)],
        out_specs=out_specs,
        compiler_params=_cparams("parallel"),
        name="sgu",
    )(u, g, mix, bias, ln_g, ln_b)


def _top1_rows(x, row):
    m = jnp.max(x, axis=0, keepdims=True)
    i = jnp.min(jnp.where(x == m, row, x.shape[0]), axis=0, keepdims=True)
    return m, i


def _pool_specs(n_prompt, tm, width):
    tiles_p = n_prompt // tm
    return (pl.BlockSpec((tm, width), lambda i, *_: (jnp.minimum(i, tiles_p - 1), 0)),
            pl.BlockSpec((tm, width), lambda i, *_: (jnp.maximum(i - tiles_p, 0), 0)))


def _pool_rows(tiles_p, prompt_ref, sample_ref, dtype):
    return jnp.where(pl.program_id(0) < tiles_p, prompt_ref[...].astype(dtype), sample_ref[...].astype(dtype))


def _mid_kernel(alpha, tiles_p, xp_ref, xs_ref, ap_ref, as_ref, sp_ref, ss_ref, wmix_ref, g1_ref, b1_ref,
                wrt_ref, rb_ref, x1_ref, x1p_ref, idx_ref, gate_ref):
    y = (_dot(_pool_rows(tiles_p, ap_ref, as_ref, BF16), wmix_ref[0:D_ATTN])
         + _dot(_pool_rows(tiles_p, sp_ref, ss_ref, BF16), wmix_ref[D_ATTN:])
         + alpha * _pool_rows(tiles_p, xp_ref, xs_ref, F32))
    x1 = _layer_norm(y, g1_ref[...], b1_ref[...])
    x1_ref[...] = x1
    tm = x1.shape[0]
    x1b = x1.astype(BF16)
    x1p_ref[...] = _pack_bf16_pairs(x1b)
    scores = _sigmoid(_dot_nt(wrt_ref[...], x1b))
    biased = scores + rb_ref[...]
    per_group = N_EXPERTS // N_EXPERT_GROUPS
    row_g = lax.broadcasted_iota(jnp.int32, (per_group, tm), 0)
    row_8 = lax.broadcasted_iota(jnp.int32, (N_EXPERT_GROUPS, tm), 0)
    row_e = lax.broadcasted_iota(jnp.int32, (N_EXPERTS, tm), 0)
    cur = jnp.full((N_EXPERT_GROUPS, tm), NEG_INF, F32)
    for g in range(N_EXPERT_GROUPS):
        blk = biased[g * per_group:(g + 1) * per_group]
        m_a, i_a = _top1_rows(blk, row_g)
        m_b = jnp.max(jnp.where(row_g == i_a, NEG_INF, blk), axis=0, keepdims=True)
        cur = jnp.where(row_8 == g, m_a + m_b, cur)
    chosen = jnp.zeros((N_EXPERT_GROUPS, tm), F32)
    for _ in range(TOPK_GROUPS):
        _, i_g = _top1_rows(cur, row_8)
        sel = row_8 == i_g
        chosen = jnp.where(sel, 1.0, chosen)
        cur = jnp.where(sel, NEG_INF, cur)
    cur = jnp.concatenate([jnp.where(chosen[g:g + 1] > 0.0, biased[g * per_group:(g + 1) * per_group], NEG_INF)
                           for g in range(N_EXPERT_GROUPS)], axis=0)
    idx = jnp.zeros((TOP_K, tm), jnp.int32)
    gates = jnp.zeros((TOP_K, tm), F32)
    for k in range(TOP_K):
        _, i_e = _top1_rows(cur, row_e)
        sel = row_e == i_e
        gate_k = jnp.sum(jnp.where(sel, scores, 0.0), axis=0, keepdims=True)
        cur = jnp.where(sel, NEG_INF, cur)
        idx = jnp.where(row_8 == k, i_e, idx)
        gates = jnp.where(row_8 == k, gate_k, gates)
    idx_ref[...] = idx
    gate_ref[...] = gates / jnp.sum(gates, axis=0, keepdims=True) * ROUTED_SCALE


def _mid(xp, xs, attn_p, attn_s, sgu_p, sgu_s, wmix, g1, b1, wrt, rb, alpha, tm):
    n_p, d = xp.shape
    t = n_p + xs.shape[0]
    const2 = lambda i: (0, 0)
    return pl.pallas_call(
        functools.partial(_mid_kernel, alpha, n_p // tm),
        out_shape=[jax.ShapeDtypeStruct((t, d), F32),
                   jax.ShapeDtypeStruct((t, d // 2), jnp.uint32),
                   jax.ShapeDtypeStruct((TOP_K, t), jnp.int32),
                   jax.ShapeDtypeStruct((TOP_K, t), F32)],
        grid=(t // tm,),
        in_specs=[*_pool_specs(n_p, tm, d), *_pool_specs(n_p, tm, D_ATTN), *_pool_specs(n_p, tm, D_SGU),
                  pl.BlockSpec(wmix.shape, const2), pl.BlockSpec(g1.shape, const2),
                  pl.BlockSpec(b1.shape, const2), pl.BlockSpec(wrt.shape, const2),
                  pl.BlockSpec(rb.shape, const2)],
        out_specs=[pl.BlockSpec((tm, d), lambda i: (i, 0)),
                   pl.BlockSpec((tm, d // 2), lambda i: (i, 0)),
                   pl.BlockSpec((TOP_K, tm), lambda i: (0, i)),
                   pl.BlockSpec((TOP_K, tm), lambda i: (0, i))],
        compiler_params=_cparams("parallel"),
        name="mix_ln_router",
    )(xp, xs, attn_p, attn_s, sgu_p, sgu_s, wmix, g1, b1, wrt, rb)


def _rank_kernel(idx_ref, rank_ref, counts_ref, run_ref):
    @pl.when(pl.program_id(0) == 0)
    def _():
        run_ref[...] = jnp.zeros_like(run_ref)

    tm = idx_ref.shape[1]
    row_e = lax.broadcasted_iota(jnp.int32, (N_EXPERTS, tm), 0)
    idx = idx_ref[...]
    onehot = jnp.zeros((N_EXPERTS, tm), F32)
    for k in range(TOP_K):
        onehot = onehot + jnp.where(row_e == idx[k:k + 1], 1.0, 0.0)
    earlier = (lax.broadcasted_iota(jnp.int32, (tm, tm), 0)
               < lax.broadcasted_iota(jnp.int32, (tm, tm), 1))
    before = run_ref[...] + _dot(onehot.astype(BF16), jnp.where(earlier, 1.0, 0.0).astype(BF16))
    row_k = lax.broadcasted_iota(jnp.int32, (TOP_K, tm), 0)
    ranks = jnp.zeros((TOP_K, tm), F32)
    for k in range(TOP_K):
        rank_k = jnp.sum(jnp.where(row_e == idx[k:k + 1], before, 0.0), axis=0, keepdims=True)
        ranks = jnp.where(row_k == k, rank_k, ranks)
    rank_ref[...] = ranks.astype(jnp.int32)
    run_ref[...] = run_ref[...] + jnp.sum(onehot, axis=1, keepdims=True)
    counts_ref[...] = run_ref[...]


def _rank(idx, tm):
    t = idx.shape[1]
    return pl.pallas_call(
        _rank_kernel,
        out_shape=[jax.ShapeDtypeStruct((TOP_K, t), jnp.int32),
                   jax.ShapeDtypeStruct((N_EXPERTS, 1), F32)],
        grid=(t // tm,),
        in_specs=[pl.BlockSpec((TOP_K, tm), lambda i: (0, i))],
        out_specs=[pl.BlockSpec((TOP_K, tm), lambda i: (0, i)),
                   pl.BlockSpec((N_EXPERTS, 1), lambda i: (0, 0))],
        scratch_shapes=[pltpu.VMEM((N_EXPERTS, 1), F32)],
        compiler_params=_cparams("arbitrary"),
        name="expert_rank",
    )(idx)


def _dest_kernel(idx_ref, rank_ref, pstart_ref, dest_ref):
    tm = idx_ref.shape[1]
    row_e = lax.broadcasted_iota(jnp.int32, (N_EXPERTS, tm), 0)
    row_k = lax.broadcasted_iota(jnp.int32, (TOP_K, tm), 0)
    idx = idx_ref[...]
    start = jnp.zeros((TOP_K, tm), F32)
    for k in range(TOP_K):
        start_k = jnp.sum(jnp.where(row_e == idx[k:k + 1], pstart_ref[...], 0.0), axis=0, keepdims=True)
        start = jnp.where(row_k == k, start_k, start)
    dest = start.astype(jnp.int32) + rank_ref[...]
    for c in range(tm // SC_CHUNK):
        dest_ref[c] = dest[:, c * SC_CHUNK:(c + 1) * SC_CHUNK]


def _dest(idx, rank, pstart_col, tm):
    t = idx.shape[1]
    tok_spec = pl.BlockSpec((TOP_K, tm), lambda i: (0, i))
    return pl.pallas_call(
        _dest_kernel,
        out_shape=jax.ShapeDtypeStruct((t // SC_CHUNK, TOP_K, SC_CHUNK), jnp.int32),
        grid=(t // tm,),
        in_specs=[tok_spec, tok_spec, pl.BlockSpec(pstart_col.shape, lambda i: (0, 0))],
        out_specs=pl.BlockSpec((tm // SC_CHUNK, TOP_K, SC_CHUNK), lambda i: (i, 0, 0)),
        compiler_params=_cparams("parallel"),
        name="moe_dest",
    )(idx, rank, pstart_col)


def _sc_mesh():
    return plsc.VectorSubcoreMesh(core_axis_name="c", subcore_axis_name="s",
                                  num_cores=SC_CORES, num_subcores=SC_SUBCORES)


def _sc_chunks(n_chunks):
    workers = SC_CORES * SC_SUBCORES
    wid = lax.axis_index("s") * SC_CORES + lax.axis_index("c")
    return wid, workers, (n_chunks - wid + workers - 1) // workers


def _sc_dispatch(x_rows, dest, n_rows):
    n_chunks = dest.shape[0]
    width = x_rows.shape[1]

    @functools.partial(
        pl.kernel, mesh=_sc_mesh(),
        out_type=jax.ShapeDtypeStruct((n_rows, width), x_rows.dtype),
        scratch_types=[pltpu.VMEM((TOP_K, SC_CHUNK), jnp.int32), pltpu.VMEM((SC_CHUNK, width), x_rows.dtype),
                       pltpu.SemaphoreType.DMA],
    )
    def scatter(x_hbm, dest_hbm, out_hbm, dest_v, rows_v, sem):
        wid, workers, n_own = _sc_chunks(n_chunks)

        @pl.loop(0, n_own)
        def _(j):
            ch = wid + j * workers
            pltpu.sync_copy(dest_hbm.at[ch], dest_v)
            pltpu.sync_copy(x_hbm.at[pl.ds(ch * SC_CHUNK, SC_CHUNK)], rows_v)
            copies = [pltpu.async_copy(rows_v, out_hbm.at[dest_v.at[k]], sem) for k in range(TOP_K)]
            for cp in copies:
                cp.wait()

    return scatter(x_rows, dest)


def _expert_kernel(first_ref, nblk_ref, count_ref, nused_ref, xs_hbm, wg_ref, wu_ref, wd_ref, after_hbm, ys_hbm,
                   xbuf, ybuf, xsem, ysem, wg_b, wu_b, wd_b):
    e = pl.program_id(0)
    n_used = nused_ref[0]
    slots, _, half = xbuf.shape

    def block_rows(j):
        return pl.ds(pl.multiple_of(j * EXPERT_ROWS, EXPERT_ROWS), EXPERT_ROWS)

    def x_copy(j, slot):
        return pltpu.make_async_copy(xs_hbm.at[block_rows(j)], xbuf.at[slot], xsem.at[slot])

    def y_copy(j, slot):
        return pltpu.make_async_copy(ybuf.at[slot], ys_hbm.at[block_rows(j)], ysem.at[slot])

    @pl.when(e == 0)
    def _():
        for k in range(slots - 1):
            @pl.when(k < n_used)
            def _():
                x_copy(k, k).start()

    @pl.when(nblk_ref[e] > 0)
    def _():
        wg_b[...] = wg_ref[0].astype(BF16)
        wu_b[...] = wu_ref[0].astype(BF16)
        wd_b[...] = wd_ref[0].astype(BF16)

    def block(b, carry):
        j = first_ref[e] + b
        slot = j % slots
        x_copy(j, slot).wait()
        ahead = j + slots - 1

        @pl.when(ahead < n_used)
        def _():
            x_copy(ahead, ahead % slots).start()

        @pl.when(j >= slots)
        def _():
            y_copy(j - slots, slot).wait()

        row = lax.broadcasted_iota(jnp.int32, (EXPERT_ROWS, half), 0)
        packed = jnp.where(row < count_ref[e] - b * EXPERT_ROWS, xbuf[slot], jnp.uint32(0))
        lo, hi = _unpack_bf16_pairs(packed)
        g = _dot(lo, wg_b[:half]) + _dot(hi, wg_b[half:])
        u = _dot(lo, wu_b[:half]) + _dot(hi, wu_b[half:])
        h = (g * _sigmoid(g) * u).astype(BF16)
        ybuf[slot] = _pack_bf16_pairs(_dot(h, wd_b[...]).astype(BF16))
        y_copy(j, slot).start()
        return carry

    lax.fori_loop(0, nblk_ref[e], block, 0)

    @pl.when(e == pl.num_programs(0) - 1)
    def _():
        for k in range(1, slots + 1):
            @pl.when(n_used >= k)
            def _():
                y_copy(n_used - k, (n_used - k) % slots).wait()


def _experts(first_block, n_block, counts, n_used, xs, wg, wu, wd, after):
    n_rows, half = xs.shape
    n_exp, d, de = wg.shape
    w_map = lambda e, *_: (e, 0, 0)
    return pl.pallas_call(
        _expert_kernel,
        out_shape=jax.ShapeDtypeStruct((n_rows, half), jnp.uint32),
        grid_spec=pltpu.PrefetchScalarGridSpec(
            num_scalar_prefetch=4,
            grid=(n_exp,),
            in_specs=[pl.BlockSpec(memory_space=pl.ANY),
                      pl.BlockSpec((1, d, de), w_map), pl.BlockSpec((1, d, de), w_map),
                      pl.BlockSpec((1, de, d), w_map), pl.BlockSpec(memory_space=pl.ANY)],
            out_specs=pl.BlockSpec(memory_space=pl.ANY),
            scratch_shapes=[pltpu.VMEM((EXPERT_SLOTS, EXPERT_ROWS, half), jnp.uint32),
                            pltpu.VMEM((EXPERT_SLOTS, EXPERT_ROWS, half), jnp.uint32),
                            pltpu.SemaphoreType.DMA((EXPERT_SLOTS,)), pltpu.SemaphoreType.DMA((EXPERT_SLOTS,)),
                            pltpu.VMEM((d, de), BF16), pltpu.VMEM((d, de), BF16), pltpu.VMEM((de, d), BF16)]),
        compiler_params=_cparams("arbitrary"),
        name="routed_experts",
    )(first_block, n_block, counts, n_used, xs, wg, wu, wd, after)


def _dense_kernel(alpha, tiles_p, x1_ref, pp_ref, ps_ref, wgs_ref, wus_ref, wds_ref, wpg_ref, wpp_ref, base_ref):
    x1 = x1_ref[...]
    xb = x1.astype(BF16)
    g = _dot(xb, wgs_ref[...])
    h = (g * _sigmoid(g) * _dot(xb, wus_ref[...])).astype(BF16)
    shared = _dot(h, wds_ref[...])
    ple = _sigmoid(_dot(xb, wpg_ref[...])) * _dot(_pool_rows(tiles_p, pp_ref, ps_ref, BF16), wpp_ref[...])
    base_ref[...] = alpha * x1 + shared + ple


def _dense(x1, pp, ps, wgs, wus, wds, wpg, wpp, alpha, tm):
    t, d = x1.shape
    const2 = lambda i: (0, 0)
    return pl.pallas_call(
        functools.partial(_dense_kernel, alpha, pp.shape[0] // tm),
        out_shape=jax.ShapeDtypeStruct((t, d), F32),
        grid=(t // tm,),
        in_specs=[pl.BlockSpec((tm, d), lambda i: (i, 0)), *_pool_specs(pp.shape[0], tm, pp.shape[1]),
                  pl.BlockSpec(wgs.shape, const2), pl.BlockSpec(wus.shape, const2),
                  pl.BlockSpec(wds.shape, const2), pl.BlockSpec(wpg.shape, const2),
                  pl.BlockSpec(wpp.shape, const2)],
        out_specs=pl.BlockSpec((tm, d), lambda i: (i, 0)),
        compiler_params=_cparams("parallel"),
        name="shared_ple",
    )(x1, pp, ps, wgs, wus, wds, wpg, wpp)


def _sc_gather(y_rows, dest, chunk0, n_chunks):
    width = y_rows.shape[1]
    items = [(k, h) for k in range(TOP_K) for h in range(SC_CHUNK // SC_GATHER_ROWS)]

    @functools.partial(
        pl.kernel, mesh=_sc_mesh(),
        out_type=jax.ShapeDtypeStruct((TOP_K, n_chunks * SC_CHUNK, width), y_rows.dtype),
        scratch_types=[pltpu.VMEM((TOP_K, SC_CHUNK), jnp.int32),
                       pltpu.VMEM((2, SC_GATHER_ROWS, width), y_rows.dtype),
                       pltpu.SemaphoreType.DMA, pltpu.SemaphoreType.DMA],
    )
    def gather(y_hbm, dest_hbm, out_hbm, dest_v, rows_v, gsem, wsem):
        wid, workers, n_own = _sc_chunks(n_chunks)

        @pl.loop(0, n_own)
        def _(j):
            ch = wid + j * workers
            pltpu.sync_copy(dest_hbm.at[chunk0 + ch], dest_v)

            def fetch(n):
                k, h = items[n]
                return pltpu.async_copy(y_hbm.at[dest_v.at[k, pl.ds(h * SC_GATHER_ROWS, SC_GATHER_ROWS)]],
                                        rows_v.at[n % 2], gsem)

            def write(n):
                k, h = items[n]
                rows = pl.ds(ch * SC_CHUNK + h * SC_GATHER_ROWS, SC_GATHER_ROWS)
                return pltpu.async_copy(rows_v.at[n % 2], out_hbm.at[k, rows], wsem)

            pending_fetch = fetch(0)
            pending_write = None
            for n in range(len(items)):
                pending_fetch.wait()
                if pending_write is not None:
                    pending_write.wait()
                if n + 1 < len(items):
                    pending_fetch = fetch(n + 1)
                pending_write = write(n)
            pending_write.wait()

    return gather(y_rows, dest)


def _combine_kernel(tiles_p, tile0, gate_ref, base_ref, yk_ref, g2_ref, b2_ref, *rest):
    outs = rest[-2:] if tile0 + pl.num_programs(0) > tiles_p else rest[-1:]
    gates = gate_ref[...]
    half = yk_ref.shape[2]
    acc_lo = base_ref[:, :half]
    acc_hi = base_ref[:, half:]
    for k in range(TOP_K):
        packed = yk_ref[k]
        gate = gates[:, k:k + 1]
        acc_lo = acc_lo + gate * lax.bitcast_convert_type(packed << 16, F32)
        acc_hi = acc_hi + gate * lax.bitcast_convert_type(packed & jnp.uint32(0xFFFF0000), F32)
    out = _layer_norm(jnp.concatenate([acc_lo, acc_hi], axis=1), g2_ref[...], b2_ref[...])
    if len(outs) == 1:
        outs[0][...] = out
    else:
        is_prompt = tile0 + pl.program_id(0) < tiles_p

        @pl.when(is_prompt)
        def _():
            outs[0][...] = out

        @pl.when(jnp.logical_not(is_prompt))
        def _():
            outs[1][...] = out


def _combine(gates_tok, base, yk, g2, b2, n_prompt, tm, tile0, prev_prompt_out):
    t, d = base.shape
    n_tiles = yk.shape[1] // tm
    tiles_p = n_prompt // tm
    has_sample = tile0 + n_tiles > tiles_p
    assert tile0 < tiles_p and (not has_sample or tile0 + n_tiles == t // tm)
    const2 = lambda i: (0, 0)
    in_specs = [pl.BlockSpec((tm, TOP_K), lambda i: (tile0 + i, 0)),
                pl.BlockSpec((tm, d), lambda i: (tile0 + i, 0)),
                pl.BlockSpec((TOP_K, tm, yk.shape[2]), lambda i: (0, i, 0)),
                pl.BlockSpec(g2.shape, const2), pl.BlockSpec(b2.shape, const2)]
    args = [gates_tok, base, yk, g2, b2]
    aliases = {}
    if prev_prompt_out is not None:
        in_specs.append(pl.BlockSpec(memory_space=pl.ANY))
        args.append(prev_prompt_out)
        aliases = {len(args) - 1: 0}
    out_shape = [jax.ShapeDtypeStruct((n_prompt, d), F32)]
    out_specs = [pl.BlockSpec((tm, d), lambda i: (jnp.minimum(tile0 + i, tiles_p - 1), 0))]
    if has_sample:
        out_shape.append(jax.ShapeDtypeStruct((t - n_prompt, d), F32))
        out_specs.append(pl.BlockSpec((tm, d), lambda i: (jnp.maximum(tile0 + i - tiles_p, 0), 0)))
    return pl.pallas_call(
        functools.partial(_combine_kernel, tiles_p, tile0),
        out_shape=out_shape,
        grid=(n_tiles,),
        in_specs=in_specs,
        out_specs=out_specs,
        input_output_aliases=aliases,
        compiler_params=_cparams("arbitrary"),
        name="moe_combine",
    )(*args)


def _sgu_tables(sgu_w, sgu_b, rows_per_seq):
    reps = CHUNK // rows_per_seq
    tril = jnp.tril(sgu_w[:, :rows_per_seq, :rows_per_seq])
    eye = jnp.eye(reps, dtype=F32)
    mix = jnp.einsum("ab,gts->gatbs", eye, tril).reshape(N_GROUPS_SGU, CHUNK, CHUNK)
    bias = jnp.tile(jnp.repeat(sgu_b[:, :rows_per_seq].T, D_SGU // N_GROUPS_SGU, axis=1), (reps, 1))
    return mix.astype(BF16), bias


def _layer(xp, xs, ck, cv, pp, ps, w, rel_bias, alpha):
    batch, seq, d = xp.shape
    dec_b, dec_t, _ = xs.shape
    w_buf = ck.shape[1]
    n_p, n_s = batch * seq, dec_b * dec_t
    n_tok = n_p + n_s

    scale = jnp.concatenate([jnp.full((D_ATTN,), HEAD_DIM ** -0.5, F32),
                             jnp.ones((w["w_in"].shape[1] - D_ATTN,), F32)])
    w_in = (w["w_in"] * scale).astype(BF16)

    qp, kp, vp, up, gp, kp_t, vp_t = _proj(xp.reshape(n_p, d), w_in, 512, seq)
    qs, ks, vs, us, gs = _proj(xs.reshape(n_s, d), w_in, 512)

    b1, b4, b16 = _prompt_bias_tables(rel_bias)
    attn_p = _attn_prompt(qp, kp, vp, b1, b4, b16, batch, seq)

    attn_s = _sample_attention(rel_bias, qs.reshape(dec_b, dec_t, D_ATTN), ks.reshape(dec_b, dec_t, D_ATTN),
                               vs.reshape(dec_b, dec_t, D_ATTN), ck, cv).reshape(n_s, D_ATTN)

    ln_g, ln_b = w["sgu_ln_g"][None], w["sgu_ln_b"][None]
    mix_p, bias_p = _sgu_tables(w["sgu_w"], w["sgu_b"], CHUNK)
    mix_s, bias_s = _sgu_tables(w["sgu_w"], w["sgu_b"], dec_t)
    (sgu_p,) = _sgu(up, gp, mix_p, bias_p, ln_g, ln_b, False, 4)
    sgu_s, z2_s = _sgu(us, gs, mix_s, bias_s, ln_g, ln_b, True, 4)

    x1, x1_packed, idx, gates = _mid(xp.reshape(n_p, d), xs.reshape(n_s, d), attn_p, attn_s, sgu_p, sgu_s,
                                     w["w_mix_out"].astype(BF16), w["ln1_g"][None], w["ln1_b"][None],
                                     w["w_router"].T.astype(BF16), w["router_bias"][:, None], alpha, 512)

    rank, counts = _rank(idx, 512)
    counts = counts[:, 0].astype(jnp.int32)
    padded = (counts + EXPERT_ROWS - 1) // EXPERT_ROWS * EXPERT_ROWS
    pend = jnp.cumsum(padded)
    pstart = (pend - padded).astype(jnp.int32)
    n_blocks = (n_tok * TOP_K + N_EXPERTS * (EXPERT_ROWS - 1)) // EXPERT_ROWS
    n_used = (pend[-1:] // EXPERT_ROWS).astype(jnp.int32)

    dest = _dest(idx, rank, pstart.astype(F32)[:, None], 512)
    x_sorted = _sc_dispatch(x1_packed, dest, n_blocks * EXPERT_ROWS)
    base = _dense(x1, pp.reshape(n_p, -1), ps.reshape(n_s, -1), w["w_gate_s"].astype(BF16), w["w_up_s"].astype(BF16),
                  w["w_down_s"].astype(BF16), w["w_ple_gate"].astype(BF16),
                  w["w_ple_proj"].astype(BF16), alpha, 1024)
    y_sorted = _experts(pstart // EXPERT_ROWS, padded // EXPERT_ROWS, counts, n_used, x_sorted,
                        w["w_gate_e"], w["w_up_e"], w["w_down_e"], base)

    tm_c = 512
    n_tiles = n_tok // tm_c
    bounds = [round(i * n_tiles / COMBINE_PARTS) for i in range(COMBINE_PARTS + 1)]
    gates_tok = gates.T
    y_p = None
    for t0, t1 in zip(bounds[:-1], bounds[1:]):
        per_tile = tm_c // SC_CHUNK
        y_slots = _sc_gather(y_sorted, dest, t0 * per_tile, (t1 - t0) * per_tile)
        outs = _combine(gates_tok, base, y_slots, w["ln2_g"][None], w["ln2_b"][None], n_p, tm_c, t0, y_p)
        y_p = outs[0]
    y_s = outs[1]
    y_p = y_p.reshape(batch, seq, d)
    y_s = y_s.reshape(dec_b, dec_t, d)
    keep = min(MAX_DISTANCE, seq)
    k_rows = kp_t.reshape(batch, N_HEADS, HEAD_DIM, seq)[..., seq - keep:].transpose(0, 3, 1, 2)
    v_rows = vp_t.reshape(batch, N_HEADS, HEAD_DIM, seq)[..., seq - keep:].transpose(0, 3, 1, 2)
    return (y_p, y_s, k_rows, v_rows,
            ks.reshape(dec_b, dec_t, N_HEADS, HEAD_DIM), vs.reshape(dec_b, dec_t, N_HEADS, HEAD_DIM),
            z2_s.reshape(dec_b, dec_t, D_SGU))


def kernel(x_prompt, x_sample, cache_k, cache_v, p_prompt, p_sample, w_in, rel_bias, sgu_w, sgu_b, sgu_ln_g, sgu_ln_b, w_mix_out, ln1_g, ln1_b, w_router, router_bias, w_gate_e, w_up_e, w_down_e, w_gate_s, w_up_s, w_down_s, w_ple_gate, w_ple_proj, ln2_g, ln2_b):
    depth = w_in.shape[0]
    alpha = (2 * depth) ** 0.25
    xp, xs = x_prompt, x_sample
    outs = [[] for _ in range(5)]
    for i in range(depth):
        w = {"w_in": w_in[i], "sgu_w": sgu_w[i], "sgu_b": sgu_b[i], "sgu_ln_g": sgu_ln_g[i],
             "sgu_ln_b": sgu_ln_b[i], "w_mix_out": w_mix_out[i], "ln1_g": ln1_g[i], "ln1_b": ln1_b[i],
             "w_router": w_router[i], "router_bias": router_bias[i], "w_gate_e": w_gate_e[i],
             "w_up_e": w_up_e[i], "w_down_e": w_down_e[i], "w_gate_s": w_gate_s[i], "w_up_s": w_up_s[i],
             "w_down_s": w_down_s[i], "w_ple_gate": w_ple_gate[i], "w_ple_proj": w_ple_proj[i],
             "ln2_g": ln2_g[i], "ln2_b": ln2_b[i]}
        xp, xs, kp, vp, ks, vs, zs = _layer(xp, xs, cache_k[i], cache_v[i], p_prompt[i], p_sample[i],
                                            w, rel_bias, alpha)
        for lst, val in zip(outs, (kp, vp, ks, vs, zs)):
            lst.append(val)
    return (xp, xs) + tuple(jnp.stack(lst) for lst in outs)
```

```python
import functools
import math

import numpy as np
import jax
import jax.numpy as jnp
from jax import lax
from jax.experimental import pallas as pl
from jax.experimental.pallas import tpu as pltpu
from jax.experimental.pallas import tpu_sc as plsc

F32 = jnp.float32
BF16 = jnp.bfloat16
NEG_INF = float("-inf")

N_HEADS = 8
HEAD_DIM = 64
D_ATTN = N_HEADS * HEAD_DIM
PATTERNS = ((128, 1), (512, 4), (2048, 16))
BAND = 128
N_BUCKETS = 32
MAX_DISTANCE = 2048
N_GROUPS_SGU = 8
D_SGU = 512
CHUNK = 128
N_EXPERTS = 256
TOP_K = 8
N_EXPERT_GROUPS = 8
TOPK_GROUPS = 4
ROUTED_SCALE = 2.5
LN_EPS = 1e-5
EXPERT_ROWS = 256
EXPERT_SLOTS = 6
WEIGHT_SLOTS = 3

LANES = 128
SC_CORES = 2
SC_SUBCORES = 16
SC_CHUNK = 128
SC_GATHER_ROWS = 64
VMEM_LIMIT = 56 * 1024 * 1024


def _cparams(*sem):
    return pltpu.CompilerParams(dimension_semantics=sem, vmem_limit_bytes=VMEM_LIMIT)


def _layer_norm(x, g, b):
    mu = jnp.mean(x, axis=-1, keepdims=True)
    xc = x - mu
    var = jnp.mean(xc * xc, axis=-1, keepdims=True)
    return xc * lax.rsqrt(var + LN_EPS) * g + b


def _sigmoid(x):
    return 1.0 / (1.0 + jnp.exp(-x))


def _gelu(x):
    return 0.5 * x * (1.0 + lax.erf(x * math.sqrt(0.5)))


def _pack_bf16_pairs(xb):
    n = xb.shape[1] // 2
    bits = lax.bitcast_convert_type(xb.astype(F32), jnp.uint32)
    return (bits[:, :n] >> 16) | (bits[:, n:] & jnp.uint32(0xFFFF0000))


def _unpack_bf16_pairs(p):
    lo = lax.bitcast_convert_type(p << 16, F32).astype(BF16)
    hi = lax.bitcast_convert_type(p & jnp.uint32(0xFFFF0000), F32).astype(BF16)
    return lo, hi


def _dot(a, b):
    return jnp.dot(a, b, preferred_element_type=F32)


def _dot_nt(a, b):
    return lax.dot_general(a, b, (((1,), (1,)), ((), ())), preferred_element_type=F32)


def _t5_bucket_np(dist):
    max_exact = N_BUCKETS // 2
    df = np.maximum(dist, max_exact).astype(np.float32)
    large = max_exact + (np.log(df / np.float32(max_exact)) / np.float32(math.log(MAX_DISTANCE / max_exact))
                         * np.float32(N_BUCKETS - max_exact)).astype(np.int32)
    return np.where(dist < max_exact, dist, np.minimum(large, N_BUCKETS - 1)).astype(np.int32)


def _band_bucket_table(dilation):
    qi = np.arange(BAND)[:, None]
    ki = np.arange(2 * BAND)[None, :]
    dsub = qi + BAND - ki
    valid = (dsub >= 0) & (dsub <= BAND)
    return np.where(valid, _t5_bucket_np(np.clip(dsub, 0, BAND) * dilation), -1).astype(np.int32)


def _sample_bucket_tables(w_buf, t_len):
    t = np.arange(t_len)[:, None]

    def table(rows, window, dilation):
        d = w_buf + t - rows[None, :]
        ok = (d >= 0) & (d % dilation == 0) & (d <= window)
        return np.where(ok, _t5_bucket_np(np.maximum(d, 0)), -1).astype(np.int32)

    assert w_buf >= PATTERNS[-1][0]
    tables = [table(np.arange(w_buf - window, w_buf), window, dilation) for window, dilation in PATTERNS]
    new_rows = w_buf + np.arange(LANES)
    new = np.stack([table(new_rows, window, dilation) for window, dilation in PATTERNS])
    new[:, :, t_len:] = -1
    return tables, new


def _bias_kernel(rb_ref, bucket_ref, out_ref):
    bucket = bucket_ref[...]
    for h in range(N_HEADS):
        acc = jnp.full(bucket.shape, NEG_INF, F32)
        for b in range(N_BUCKETS):
            acc = jnp.where(bucket == b, rb_ref[b, h], acc)
        out_ref[h] = acc


def _bias_table(rel_bias, bucket_np):
    r, c = bucket_np.shape
    return pl.pallas_call(
        _bias_kernel,
        out_shape=jax.ShapeDtypeStruct((N_HEADS, r, c), F32),
        in_specs=[pl.BlockSpec(memory_space=pltpu.SMEM), pl.BlockSpec(memory_space=pltpu.VMEM)],
        out_specs=pl.BlockSpec(memory_space=pltpu.VMEM),
        name="bias_table",
    )(rel_bias, jnp.asarray(bucket_np))


def _proj_kernel(x_ref, w_ref, q_ref, k_ref, v_ref, u_ref, g_ref, *kv_t_refs):
    x = x_ref[...].astype(BF16)
    col = 0
    rows = []
    for o in (q_ref, k_ref, v_ref, u_ref, g_ref):
        n = o.shape[1]
        rows.append(_dot(x, w_ref[:, col:col + n]))
        o[...] = rows[-1]
        col += n
    for o, val in zip(kv_t_refs, rows[1:3]):
        o[0] = val.T


def _proj(x, w, tm, seq=None):
    m, d = x.shape
    n_out = (D_ATTN, D_ATTN, D_ATTN, D_SGU, D_SGU)
    out_shape = [jax.ShapeDtypeStruct((m, n), F32) for n in n_out]
    out_specs = [pl.BlockSpec((tm, n), lambda i: (i, 0)) for n in n_out]
    if seq is not None:
        tiles = seq // tm
        out_shape += [jax.ShapeDtypeStruct((m // seq, D_ATTN, seq), F32)] * 2
        out_specs += [pl.BlockSpec((1, D_ATTN, tm), lambda i: (i // tiles, 0, i % tiles))] * 2
    return pl.pallas_call(
        _proj_kernel,
        out_shape=out_shape,
        grid=(m // tm,),
        in_specs=[pl.BlockSpec((tm, d), lambda i: (i, 0)), pl.BlockSpec(w.shape, lambda i: (0, 0))],
        out_specs=out_specs,
        compiler_params=_cparams("parallel"),
        name="in_proj",
    )(x, w)


def _band_attn(q, k, v, bias, even):
    q2 = jnp.concatenate([jnp.where(even, q, 0.0), jnp.where(even, 0.0, q)], axis=0).astype(BF16)
    s = _dot_nt(q2, k.astype(BF16)) + bias
    m = jnp.max(s, axis=-1, keepdims=True)
    p = jnp.exp(s - m)
    l = jnp.sum(p, axis=-1, keepdims=True)
    pv = _dot(p.astype(BF16), v.astype(BF16))
    return (jnp.where(even, m[:BAND], m[BAND:]), jnp.where(even, l[:BAND], l[BAND:]),
            jnp.where(even, pv[:BAND], pv[BAND:]))


def _attn_prompt_kernel(q_ref, k_ref, v_ref, b1_ref, b4_ref, b16_ref, o_ref,
                        m1, l1, a1, m4, l4, a4, m16, l16, a16, qc, kc, vc):
    seq = q_ref.shape[0]
    n4 = seq // 4
    per_trip = seq // BAND // 4
    even = lax.broadcasted_iota(jnp.int32, (BAND, LANES), 1) < HEAD_DIM

    def store_all(refs, rows_list, results):
        for rows, (m, l, a) in zip(rows_list, results):
            refs[0][rows, :], refs[1][rows, :], refs[2][rows, :] = m, l, a

    def body(r, carry):
        rows1, loaded1 = [], []
        for j in range(per_trip):
            i = r * per_trip + j
            r0 = pl.multiple_of(i * BAND, BAND)
            k0 = pl.multiple_of(jnp.maximum(i - 1, 0) * BAND, BAND)
            rows1.append(pl.ds(r0, BAND))
            loaded1.append((q_ref[pl.ds(r0, BAND)], k_ref[pl.ds(k0, 2 * BAND)], v_ref[pl.ds(k0, 2 * BAND)],
                            b1_ref[jnp.where(i == 0, 1, 0)]))
        base = pl.multiple_of(r * n4, BAND)
        q4 = q_ref[pl.ds(r, n4, stride=4), :]
        k4 = k_ref[pl.ds(r, n4, stride=4), :]
        v4 = v_ref[pl.ds(r, n4, stride=4), :]
        qc[pl.ds(base, n4)], kc[pl.ds(base, n4)], vc[pl.ds(base, n4)] = q4, k4, v4
        rows16 = [pl.ds(r * n4 + s, seq // 16, stride=4) for s in range(4)]
        loaded16 = [(qc[rows, :], kc[rows, :], vc[rows, :]) for rows in rows16]

        results1 = [_band_attn(q, k, v, bias, even) for q, k, v, bias in loaded1]
        results4, rows4 = [], []
        for i in range(n4 // BAND):
            lo = max(i - 1, 0) * BAND
            hi = (i + 1) * BAND
            col = 0 if i > 0 else BAND
            results4.append(_band_attn(q4[i * BAND:hi], k4[lo:hi], v4[lo:hi], b4_ref[:, col:], even))
            rows4.append(pl.ds(base + i * BAND, BAND))
        results16 = [_band_attn(q, k, v, b16_ref[:, BAND:], even) for q, k, v in loaded16]

        store_all((m1, l1, a1), rows1, results1)
        store_all((m4, l4, a4), rows4, results4)
        store_all((m16, l16, a16), rows16, results16)
        return carry

    lax.fori_loop(0, 4, body, 0)

    def merge_body(i, carry):
        r = i // (n4 // BAND)
        c = i % (n4 // BAND)
        rows = pl.ds(pl.multiple_of(i * BAND, BAND), BAND)
        nat = pl.ds(r + 4 * BAND * c, BAND, stride=4)
        ma, mb, mc = m1[nat, :], m4[rows], m16[rows]
        mx = jnp.maximum(jnp.maximum(ma, mb), mc)
        wa, wb, wc = jnp.exp(ma - mx), jnp.exp(mb - mx), jnp.exp(mc - mx)
        num = wa * a1[nat, :] + wb * a4[rows] + wc * a16[rows]
        den = wa * l1[nat, :] + wb * l4[rows] + wc * l16[rows]
        o_ref[nat, :] = num / den
        return carry

    lax.fori_loop(0, seq // BAND, merge_body, 0)


def _prompt_bias_tables(rel_bias):
    pairs = N_HEADS // 2
    first = _band_bucket_table(1)
    first = np.concatenate([first[:, BAND:], np.full((BAND, BAND), -1, np.int32)], axis=1)
    buckets = np.concatenate([_band_bucket_table(1), first, _band_bucket_table(4), _band_bucket_table(16)], axis=0)
    tables = _bias_table(rel_bias, buckets).reshape(N_HEADS, 4, BAND, 2 * BAND)
    b1, b1_first, b4, b16 = (tables[:, i].reshape(pairs, 2 * BAND, 2 * BAND) for i in range(4))
    return jnp.stack([b1, b1_first], axis=1), b4, b16


def _attn_prompt(q, k, v, b1, b4, b16, batch, seq):
    blk = pl.BlockSpec((seq, LANES), lambda b, j: (b, j))
    bias_spec = pl.BlockSpec((None, 2 * BAND, 2 * BAND), lambda b, j: (j, 0, 0))
    return pl.pallas_call(
        _attn_prompt_kernel,
        out_shape=jax.ShapeDtypeStruct(q.shape, F32),
        grid=(batch, D_ATTN // LANES),
        in_specs=[blk, blk, blk, pl.BlockSpec((None, 2, 2 * BAND, 2 * BAND), lambda b, j: (j, 0, 0, 0)),
                  bias_spec, bias_spec],
        out_specs=blk,
        scratch_shapes=[pltpu.VMEM((seq, LANES), F32) for _ in range(12)],
        compiler_params=_cparams("parallel", "parallel"),
        name="attn_prompt",
    )(q, k, v, b1, b4, b16)


def _attn_sample_kernel(q_ref, kn_ref, vn_ref, kt_ref, vt_ref, b1_ref, b4_ref, b16_ref, bn_ref, o_ref):
    t_len = q_ref.shape[1]
    rows = N_HEADS * t_len
    q = q_ref[0]
    head_of_row = lax.broadcasted_iota(jnp.int32, (rows, D_ATTN), 0) // t_len
    head_of_lane = lax.broadcasted_iota(jnp.int32, (rows, D_ATTN), 1) // HEAD_DIM
    own = head_of_row == head_of_lane
    qrows = jnp.where(own, jnp.concatenate([q] * N_HEADS, axis=0), 0.0).astype(BF16)
    pad = jnp.zeros((LANES - t_len, D_ATTN), F32)
    kn = jnp.concatenate([kn_ref[0], pad], axis=0).astype(BF16)
    vn = jnp.concatenate([vn_ref[0], pad], axis=0).astype(BF16)
    kt = kt_ref[0].reshape(D_ATTN, -1).astype(BF16)
    vt = vt_ref[0].reshape(D_ATTN, -1).astype(BF16)
    s_cache = _dot(qrows, kt)
    s_new = _dot_nt(qrows, kn)
    ms, ls, accs = [], [], []
    for p, bias_ref in enumerate((b1_ref, b4_ref, b16_ref)):
        w = bias_ref.shape[1]
        sc = s_cache[:, -w:] + bias_ref[...]
        sn = s_new + bn_ref[p]
        m = jnp.maximum(jnp.max(sc, axis=-1, keepdims=True), jnp.max(sn, axis=-1, keepdims=True))
        pc = jnp.exp(sc - m)
        pn = jnp.exp(sn - m)
        ls.append(jnp.sum(pc, axis=-1, keepdims=True) + jnp.sum(pn, axis=-1, keepdims=True))
        accs.append(_dot_nt(pc.astype(BF16), vt[:, -w:]) + _dot(pn.astype(BF16), vn))
        ms.append(m)
    mx = jnp.maximum(jnp.maximum(ms[0], ms[1]), ms[2])
    ws = [jnp.exp(m - mx) for m in ms]
    num = ws[0] * accs[0] + ws[1] * accs[1] + ws[2] * accs[2]
    den = ws[0] * ls[0] + ws[1] * ls[1] + ws[2] * ls[2]
    full = jnp.where(own, num / den, 0.0)
    out = full[0:t_len]
    for h in range(1, N_HEADS):
        out = out + full[h * t_len:(h + 1) * t_len]
    o_ref[0] = out


def _sample_attention(rel_bias, q, kn, vn, ck, cv):
    b, t_len, _ = q.shape
    w_buf = ck.shape[1]
    rows = N_HEADS * t_len
    tables, new_t = _sample_bucket_tables(w_buf, t_len)
    widest = max(tb.shape[1] for tb in tables)
    pieces = tables + [new_t.reshape(3 * t_len, LANES)]
    stacked = np.concatenate([np.pad(tb, ((0, 0), (0, widest - tb.shape[1])), constant_values=-1) for tb in pieces])
    bias = _bias_table(rel_bias, stacked)
    b1, b4, b16 = (bias[:, i * t_len:(i + 1) * t_len, :tb.shape[1]].reshape(rows, tb.shape[1])
                   for i, tb in enumerate(tables))
    bn = bias[:, 3 * t_len:, :LANES].reshape(N_HEADS, 3, t_len, LANES).transpose(1, 0, 2, 3).reshape(3, rows, LANES)
    new_spec = pl.BlockSpec((1, t_len, D_ATTN), lambda i: (i, 0, 0))
    cache_spec = pl.BlockSpec((1, N_HEADS, HEAD_DIM, w_buf), lambda i: (i, 0, 0, 0))
    const2 = lambda i: (0, 0)
    return pl.pallas_call(
        _attn_sample_kernel,
        out_shape=jax.ShapeDtypeStruct(q.shape, F32),
        grid=(b,),
        in_specs=[new_spec, new_spec, new_spec, cache_spec, cache_spec,
                  pl.BlockSpec(b1.shape, const2), pl.BlockSpec(b4.shape, const2), pl.BlockSpec(b16.shape, const2),
                  pl.BlockSpec(bn.shape, lambda i: (0, 0, 0))],
        out_specs=new_spec,
        compiler_params=_cparams("parallel"),
        name="attn_sample",
    )(q, kn, vn, ck.transpose(0, 2, 3, 1), cv.transpose(0, 2, 3, 1), b1, b4, b16, bn)


def _sgu_kernel(u_ref, g_ref, mix_ref, bias_ref, lng_ref, lnb_ref, sgu_ref, *z2_out):
    n_chunks = u_ref.shape[0] // CHUNK
    group_of_lane = lax.broadcasted_iota(jnp.int32, (CHUNK, D_SGU), 1) // (D_SGU // N_GROUPS_SGU)
    for c in range(n_chunks):
        rows = slice(c * CHUNK, (c + 1) * CHUNK)
        z1 = _gelu(u_ref[rows])
        z2 = _layer_norm(_gelu(g_ref[rows]), lng_ref[...], lnb_ref[...])
        if z2_out:
            z2_out[0][rows] = z2
        mixed = bias_ref[...]
        for g in range(N_GROUPS_SGU):
            mixed = mixed + _dot(mix_ref[g], jnp.where(group_of_lane == g, z2, 0.0).astype(BF16))
        sgu_ref[rows] = (z1 * mixed).astype(sgu_ref.dtype)


def _sgu(u, g, mix, bias, ln_g, ln_b, want_z2, chunks_per_step):
    m = u.shape[0]
    tm = CHUNK * chunks_per_step
    row_spec = pl.BlockSpec((tm, D_SGU), lambda i: (i, 0))
    const2 = lambda i: (0, 0)
    out_shape = [jax.ShapeDtypeStruct((m, D_SGU), BF16)]
    out_specs = [row_spec]
    if want_z2:
        out_shape.append(jax.ShapeDtypeStruct((m, D_SGU), F32))
        out_specs.append(row_spec)
    return pl.pallas_call(
        _sgu_kernel,
        out_shape=out_shape,
        grid=(m // tm,),
        in_specs=[row_spec, row_spec,
                  pl.BlockSpec(mix.shape, lambda i: (0, 0, 0)),
                  pl.BlockSpec(bias.shape, const2),
                  pl.BlockSpec(ln_g.shape, const2), pl.BlockSpec(ln_b.shape, const2)],
        out_specs=out_specs,
        compiler_params=_cparams("parallel"),
        name="sgu",
    )(u, g, mix, bias, ln_g, ln_b)


def _top1_rows(x, row):
    m = jnp.max(x, axis=0, keepdims=True)
    i = jnp.min(jnp.where(x == m, row, x.shape[0]), axis=0, keepdims=True)
    return m, i


def _pool_specs(n_prompt, tm, width):
    tiles_p = n_prompt // tm
    return (pl.BlockSpec((tm, width), lambda i, *_: (jnp.minimum(i, tiles_p - 1), 0)),
            pl.BlockSpec((tm, width), lambda i, *_: (jnp.maximum(i - tiles_p, 0), 0)))


def _pool_rows(tiles_p, prompt_ref, sample_ref, dtype):
    return jnp.where(pl.program_id(0) < tiles_p, prompt_ref[...].astype(dtype), sample_ref[...].astype(dtype))


def _mid_kernel(alpha, tiles_p, xp_ref, xs_ref, ap_ref, as_ref, sp_ref, ss_ref, wmix_ref, g1_ref, b1_ref,
                wrt_ref, rb_ref, x1_ref, x1p_ref, idx_ref, gate_ref):
    y = (_dot(_pool_rows(tiles_p, ap_ref, as_ref, BF16), wmix_ref[0:D_ATTN])
         + _dot(_pool_rows(tiles_p, sp_ref, ss_ref, BF16), wmix_ref[D_ATTN:])
         + alpha * _pool_rows(tiles_p, xp_ref, xs_ref, F32))
    x1 = _layer_norm(y, g1_ref[...], b1_ref[...])
    x1_ref[...] = x1
    tm = x1.shape[0]
    x1b = x1.astype(BF16)
    x1p_ref[...] = _pack_bf16_pairs(x1b)
    scores = _sigmoid(_dot_nt(wrt_ref[...], x1b))
    biased = scores + rb_ref[...]
    per_group = N_EXPERTS // N_EXPERT_GROUPS
    row_g = lax.broadcasted_iota(jnp.int32, (per_group, tm), 0)
    row_8 = lax.broadcasted_iota(jnp.int32, (N_EXPERT_GROUPS, tm), 0)
    row_e = lax.broadcasted_iota(jnp.int32, (N_EXPERTS, tm), 0)
    cur = jnp.full((N_EXPERT_GROUPS, tm), NEG_INF, F32)
    for g in range(N_EXPERT_GROUPS):
        blk = biased[g * per_group:(g + 1) * per_group]
        m_a, i_a = _top1_rows(blk, row_g)
        m_b = jnp.max(jnp.where(row_g == i_a, NEG_INF, blk), axis=0, keepdims=True)
        cur = jnp.where(row_8 == g, m_a + m_b, cur)
    chosen = jnp.zeros((N_EXPERT_GROUPS, tm), F32)
    for _ in range(TOPK_GROUPS):
        _, i_g = _top1_rows(cur, row_8)
        sel = row_8 == i_g
        chosen = jnp.where(sel, 1.0, chosen)
        cur = jnp.where(sel, NEG_INF, cur)
    cur = jnp.concatenate([jnp.where(chosen[g:g + 1] > 0.0, biased[g * per_group:(g + 1) * per_group], NEG_INF)
                           for g in range(N_EXPERT_GROUPS)], axis=0)
    idx = jnp.zeros((TOP_K, tm), jnp.int32)
    gates = jnp.zeros((TOP_K, tm), F32)
    for k in range(TOP_K):
        _, i_e = _top1_rows(cur, row_e)
        sel = row_e == i_e
        gate_k = jnp.sum(jnp.where(sel, scores, 0.0), axis=0, keepdims=True)
        cur = jnp.where(sel, NEG_INF, cur)
        idx = jnp.where(row_8 == k, i_e, idx)
        gates = jnp.where(row_8 == k, gate_k, gates)
    idx_ref[...] = idx
    gate_ref[...] = gates / jnp.sum(gates, axis=0, keepdims=True) * ROUTED_SCALE


def _mid(xp, xs, attn_p, attn_s, sgu_p, sgu_s, wmix, g1, b1, wrt, rb, alpha, tm):
    n_p, d = xp.shape
    t = n_p + xs.shape[0]
    const2 = lambda i: (0, 0)
    return pl.pallas_call(
        functools.partial(_mid_kernel, alpha, n_p // tm),
        out_shape=[jax.ShapeDtypeStruct((t, d), F32),
                   jax.ShapeDtypeStruct((t, d // 2), jnp.uint32),
                   jax.ShapeDtypeStruct((TOP_K, t), jnp.int32),
                   jax.ShapeDtypeStruct((TOP_K, t), F32)],
        grid=(t // tm,),
        in_specs=[*_pool_specs(n_p, tm, d), *_pool_specs(n_p, tm, D_ATTN), *_pool_specs(n_p, tm, D_SGU),
                  pl.BlockSpec(wmix.shape, const2), pl.BlockSpec(g1.shape, const2),
                  pl.BlockSpec(b1.shape, const2), pl.BlockSpec(wrt.shape, const2),
                  pl.BlockSpec(rb.shape, const2)],
        out_specs=[pl.BlockSpec((tm, d), lambda i: (i, 0)),
                   pl.BlockSpec((tm, d // 2), lambda i: (i, 0)),
                   pl.BlockSpec((TOP_K, tm), lambda i: (0, i)),
                   pl.BlockSpec((TOP_K, tm), lambda i: (0, i))],
        compiler_params=_cparams("parallel"),
        name="mix_ln_router",
    )(xp, xs, attn_p, attn_s, sgu_p, sgu_s, wmix, g1, b1, wrt, rb)


def _rank_kernel(idx_ref, rank_ref, counts_ref, run_ref):
    @pl.when(pl.program_id(0) == 0)
    def _():
        run_ref[...] = jnp.zeros_like(run_ref)

    tm = idx_ref.shape[1]
    row_e = lax.broadcasted_iota(jnp.int32, (N_EXPERTS, tm), 0)
    idx = idx_ref[...]
    onehot = jnp.zeros((N_EXPERTS, tm), F32)
    for k in range(TOP_K):
        onehot = onehot + jnp.where(row_e == idx[k:k + 1], 1.0, 0.0)
    earlier = (lax.broadcasted_iota(jnp.int32, (tm, tm), 0)
               < lax.broadcasted_iota(jnp.int32, (tm, tm), 1))
    before = run_ref[...] + _dot(onehot.astype(BF16), jnp.where(earlier, 1.0, 0.0).astype(BF16))
    row_k = lax.broadcasted_iota(jnp.int32, (TOP_K, tm), 0)
    ranks = jnp.zeros((TOP_K, tm), F32)
    for k in range(TOP_K):
        rank_k = jnp.sum(jnp.where(row_e == idx[k:k + 1], before, 0.0), axis=0, keepdims=True)
        ranks = jnp.where(row_k == k, rank_k, ranks)
    rank_ref[...] = ranks.astype(jnp.int32)
    run_ref[...] = run_ref[...] + jnp.sum(onehot, axis=1, keepdims=True)
    counts_ref[...] = run_ref[...]


def _rank(idx, tm):
    t = idx.shape[1]
    return pl.pallas_call(
        _rank_kernel,
        out_shape=[jax.ShapeDtypeStruct((TOP_K, t), jnp.int32),
                   jax.ShapeDtypeStruct((N_EXPERTS, 1), F32)],
        grid=(t // tm,),
        in_specs=[pl.BlockSpec((TOP_K, tm), lambda i: (0, i))],
        out_specs=[pl.BlockSpec((TOP_K, tm), lambda i: (0, i)),
                   pl.BlockSpec((N_EXPERTS, 1), lambda i: (0, 0))],
        scratch_shapes=[pltpu.VMEM((N_EXPERTS, 1), F32)],
        compiler_params=_cparams("arbitrary"),
        name="expert_rank",
    )(idx)


def _dest_kernel(idx_ref, rank_ref, pstart_ref, dest_ref):
    tm = idx_ref.shape[1]
    row_e = lax.broadcasted_iota(jnp.int32, (N_EXPERTS, tm), 0)
    row_k = lax.broadcasted_iota(jnp.int32, (TOP_K, tm), 0)
    idx = idx_ref[...]
    start = jnp.zeros((TOP_K, tm), F32)
    for k in range(TOP_K):
        start_k = jnp.sum(jnp.where(row_e == idx[k:k + 1], pstart_ref[...], 0.0), axis=0, keepdims=True)
        start = jnp.where(row_k == k, start_k, start)
    dest = start.astype(jnp.int32) + rank_ref[...]
    for c in range(tm // SC_CHUNK):
        dest_ref[c] = dest[:, c * SC_CHUNK:(c + 1) * SC_CHUNK]


def _dest(idx, rank, pstart_col, tm):
    t = idx.shape[1]
    tok_spec = pl.BlockSpec((TOP_K, tm), lambda i: (0, i))
    return pl.pallas_call(
        _dest_kernel,
        out_shape=jax.ShapeDtypeStruct((t // SC_CHUNK, TOP_K, SC_CHUNK), jnp.int32),
        grid=(t // tm,),
        in_specs=[tok_spec, tok_spec, pl.BlockSpec(pstart_col.shape, lambda i: (0, 0))],
        out_specs=pl.BlockSpec((tm // SC_CHUNK, TOP_K, SC_CHUNK), lambda i: (i, 0, 0)),
        compiler_params=_cparams("parallel"),
        name="moe_dest",
    )(idx, rank, pstart_col)


def _sc_mesh():
    return plsc.VectorSubcoreMesh(core_axis_name="c", subcore_axis_name="s",
                                  num_cores=SC_CORES, num_subcores=SC_SUBCORES)


def _sc_chunks(n_chunks):
    workers = SC_CORES * SC_SUBCORES
    wid = lax.axis_index("s") * SC_CORES + lax.axis_index("c")
    return wid, workers, (n_chunks - wid + workers - 1) // workers


def _sc_dispatch(x_rows, dest, n_rows):
    n_chunks = dest.shape[0]
    width = x_rows.shape[1]

    @functools.partial(
        pl.kernel, mesh=_sc_mesh(),
        out_type=jax.ShapeDtypeStruct((n_rows, width), x_rows.dtype),
        scratch_types=[pltpu.VMEM((TOP_K, SC_CHUNK), jnp.int32), pltpu.VMEM((SC_CHUNK, width), x_rows.dtype),
                       pltpu.SemaphoreType.DMA],
    )
    def scatter(x_hbm, dest_hbm, out_hbm, dest_v, rows_v, sem):
        wid, workers, n_own = _sc_chunks(n_chunks)

        @pl.loop(0, n_own)
        def _(j):
            ch = wid + j * workers
            pltpu.sync_copy(dest_hbm.at[ch], dest_v)
            pltpu.sync_copy(x_hbm.at[pl.ds(ch * SC_CHUNK, SC_CHUNK)], rows_v)
            copies = [pltpu.async_copy(rows_v, out_hbm.at[dest_v.at[k]], sem) for k in range(TOP_K)]
            for cp in copies:
                cp.wait()

    return scatter(x_rows, dest)


def _expert_kernel(first_ref, nblk_ref, count_ref, nused_ref, xs_hbm, wg_hbm, wu_hbm, wd_hbm, ys_hbm,
                   xbuf, ybuf, xsem, ysem, wg_f, wu_f, wd_f, wsem, wg_b, wu_b, wd_b):
    n_exp = wg_hbm.shape[0]
    n_used = nused_ref[0]
    slots, _, half = xbuf.shape
    weight_slots = wg_f.shape[0]

    def block_rows(j):
        return pl.ds(pl.multiple_of(j * EXPERT_ROWS, EXPERT_ROWS), EXPERT_ROWS)

    def x_copy(j, slot):
        return pltpu.make_async_copy(xs_hbm.at[block_rows(j)], xbuf.at[slot], xsem.at[slot])

    def y_copy(j, slot):
        return pltpu.make_async_copy(ybuf.at[slot], ys_hbm.at[block_rows(j)], ysem.at[slot])

    def weight_copies(e, slot):
        return [pltpu.make_async_copy(src.at[e], dst.at[slot], wsem.at[i, slot])
                for i, (src, dst) in enumerate(((wg_hbm, wg_f), (wu_hbm, wu_f), (wd_hbm, wd_f)))]

    for k in range(slots - 1):
        @pl.when(k < n_used)
        def _():
            x_copy(k, k).start()
    for k in range(min(weight_slots - 1, n_exp)):
        for cp in weight_copies(k, k):
            cp.start()

    def expert(e, carry):
        wslot = e % weight_slots
        for cp in weight_copies(e, wslot):
            cp.wait()
        ahead_e = e + weight_slots - 1

        @pl.when(ahead_e < n_exp)
        def _():
            for cp in weight_copies(ahead_e, ahead_e % weight_slots):
                cp.start()

        @pl.when(nblk_ref[e] > 0)
        def _():
            wg_b[...] = wg_f[wslot].astype(BF16)
            wu_b[...] = wu_f[wslot].astype(BF16)
            wd_b[...] = wd_f[wslot].astype(BF16)

        def block(b, carry):
            j = first_ref[e] + b
            slot = j % slots
            x_copy(j, slot).wait()
            ahead = j + slots - 1

            @pl.when(ahead < n_used)
            def _():
                x_copy(ahead, ahead % slots).start()

            @pl.when(j >= slots)
            def _():
                y_copy(j - slots, slot).wait()

            row = lax.broadcasted_iota(jnp.int32, (EXPERT_ROWS, half), 0)
            packed = jnp.where(row < count_ref[e] - b * EXPERT_ROWS, xbuf[slot], jnp.uint32(0))
            lo, hi = _unpack_bf16_pairs(packed)
            g = _dot(lo, wg_b[:half]) + _dot(hi, wg_b[half:])
            u = _dot(lo, wu_b[:half]) + _dot(hi, wu_b[half:])
            h = (g * _sigmoid(g) * u).astype(BF16)
            ybuf[slot] = _pack_bf16_pairs(_dot(h, wd_b[...]).astype(BF16))
            y_copy(j, slot).start()
            return carry

        lax.fori_loop(0, nblk_ref[e], block, 0)
        return carry

    lax.fori_loop(0, n_exp, expert, 0)

    for k in range(1, slots + 1):
        @pl.when(n_used >= k)
        def _():
            y_copy(n_used - k, (n_used - k) % slots).wait()


def _experts(first_block, n_block, counts, n_used, xs, wg, wu, wd):
    n_rows, half = xs.shape
    n_exp, d, de = wg.shape
    any_spec = pl.BlockSpec(memory_space=pl.ANY)
    return pl.pallas_call(
        _expert_kernel,
        out_shape=jax.ShapeDtypeStruct((n_rows, half), jnp.uint32),
        grid_spec=pltpu.PrefetchScalarGridSpec(
            num_scalar_prefetch=4,
            grid=(1,),
            in_specs=[any_spec, any_spec, any_spec, any_spec],
            out_specs=any_spec,
            scratch_shapes=[pltpu.VMEM((EXPERT_SLOTS, EXPERT_ROWS, half), jnp.uint32),
                            pltpu.VMEM((EXPERT_SLOTS, EXPERT_ROWS, half), jnp.uint32),
                            pltpu.SemaphoreType.DMA((EXPERT_SLOTS,)), pltpu.SemaphoreType.DMA((EXPERT_SLOTS,)),
                            pltpu.VMEM((WEIGHT_SLOTS, d, de), F32), pltpu.VMEM((WEIGHT_SLOTS, d, de), F32),
                            pltpu.VMEM((WEIGHT_SLOTS, de, d), F32), pltpu.SemaphoreType.DMA((3, WEIGHT_SLOTS)),
                            pltpu.VMEM((d, de), BF16), pltpu.VMEM((d, de), BF16), pltpu.VMEM((de, d), BF16)]),
        compiler_params=_cparams("arbitrary"),
        name="routed_experts",
    )(first_block, n_block, counts, n_used, xs, wg, wu, wd)


def _dense_kernel(alpha, tiles_p, x1_ref, pp_ref, ps_ref, wgs_ref, wus_ref, wds_ref, wpg_ref, wpp_ref, base_ref):
    x1 = x1_ref[...]
    xb = x1.astype(BF16)
    g = _dot(xb, wgs_ref[...])
    h = (g * _sigmoid(g) * _dot(xb, wus_ref[...])).astype(BF16)
    shared = _dot(h, wds_ref[...])
    ple = _sigmoid(_dot(xb, wpg_ref[...])) * _dot(_pool_rows(tiles_p, pp_ref, ps_ref, BF16), wpp_ref[...])
    base_ref[...] = alpha * x1 + shared + ple


def _dense(x1, pp, ps, wgs, wus, wds, wpg, wpp, alpha, tm):
    t, d = x1.shape
    const2 = lambda i: (0, 0)
    return pl.pallas_call(
        functools.partial(_dense_kernel, alpha, pp.shape[0] // tm),
        out_shape=jax.ShapeDtypeStruct((t, d), F32),
        grid=(t // tm,),
        in_specs=[pl.BlockSpec((tm, d), lambda i: (i, 0)), *_pool_specs(pp.shape[0], tm, pp.shape[1]),
                  pl.BlockSpec(wgs.shape, const2), pl.BlockSpec(wus.shape, const2),
                  pl.BlockSpec(wds.shape, const2), pl.BlockSpec(wpg.shape, const2),
                  pl.BlockSpec(wpp.shape, const2)],
        out_specs=pl.BlockSpec((tm, d), lambda i: (i, 0)),
        compiler_params=_cparams("parallel"),
        name="shared_ple",
    )(x1, pp, ps, wgs, wus, wds, wpg, wpp)


def _sc_gather(y_rows, dest, n_tok):
    n_chunks = dest.shape[0]
    width = y_rows.shape[1]
    items = [(k, h) for k in range(TOP_K) for h in range(SC_CHUNK // SC_GATHER_ROWS)]

    @functools.partial(
        pl.kernel, mesh=_sc_mesh(),
        out_type=jax.ShapeDtypeStruct((TOP_K, n_tok, width), y_rows.dtype),
        scratch_types=[pltpu.VMEM((TOP_K, SC_CHUNK), jnp.int32),
                       pltpu.VMEM((2, SC_GATHER_ROWS, width), y_rows.dtype),
                       pltpu.SemaphoreType.DMA, pltpu.SemaphoreType.DMA],
    )
    def gather(y_hbm, dest_hbm, out_hbm, dest_v, rows_v, gsem, wsem):
        wid, workers, n_own = _sc_chunks(n_chunks)

        @pl.loop(0, n_own)
        def _(j):
            ch = wid + j * workers
            pltpu.sync_copy(dest_hbm.at[ch], dest_v)

            def fetch(n):
                k, h = items[n]
                return pltpu.async_copy(y_hbm.at[dest_v.at[k, pl.ds(h * SC_GATHER_ROWS, SC_GATHER_ROWS)]],
                                        rows_v.at[n % 2], gsem)

            def write(n):
                k, h = items[n]
                rows = pl.ds(ch * SC_CHUNK + h * SC_GATHER_ROWS, SC_GATHER_ROWS)
                return pltpu.async_copy(rows_v.at[n % 2], out_hbm.at[k, rows], wsem)

            pending_fetch = fetch(0)
            pending_write = None
            for n in range(len(items)):
                pending_fetch.wait()
                if pending_write is not None:
                    pending_write.wait()
                if n + 1 < len(items):
                    pending_fetch = fetch(n + 1)
                pending_write = write(n)
            pending_write.wait()

    return gather(y_rows, dest)


def _combine_kernel(tiles_p, gate_ref, base_ref, yk_ref, g2_ref, b2_ref, outp_ref, outs_ref):
    gates = gate_ref[...]
    half = yk_ref.shape[2]
    acc_lo = base_ref[:, :half]
    acc_hi = base_ref[:, half:]
    for k in range(TOP_K):
        packed = yk_ref[k]
        gate = gates[:, k:k + 1]
        acc_lo = acc_lo + gate * lax.bitcast_convert_type(packed << 16, F32)
        acc_hi = acc_hi + gate * lax.bitcast_convert_type(packed & jnp.uint32(0xFFFF0000), F32)
    out = _layer_norm(jnp.concatenate([acc_lo, acc_hi], axis=1), g2_ref[...], b2_ref[...])
    is_prompt = pl.program_id(0) < tiles_p

    @pl.when(is_prompt)
    def _():
        outp_ref[...] = out

    @pl.when(jnp.logical_not(is_prompt))
    def _():
        outs_ref[...] = out


def _combine(gates_tok, base, yk, g2, b2, n_prompt, tm):
    t, d = base.shape
    const2 = lambda i: (0, 0)
    return pl.pallas_call(
        functools.partial(_combine_kernel, n_prompt // tm),
        out_shape=[jax.ShapeDtypeStruct((n_prompt, d), F32), jax.ShapeDtypeStruct((t - n_prompt, d), F32)],
        grid=(t // tm,),
        in_specs=[pl.BlockSpec((tm, TOP_K), lambda i: (i, 0)),
                  pl.BlockSpec((tm, d), lambda i: (i, 0)),
                  pl.BlockSpec((TOP_K, tm, yk.shape[2]), lambda i: (0, i, 0)),
                  pl.BlockSpec(g2.shape, const2), pl.BlockSpec(b2.shape, const2)],
        out_specs=list(_pool_specs(n_prompt, tm, d)),
        compiler_params=_cparams("arbitrary"),
        name="moe_combine",
    )(gates_tok, base, yk, g2, b2)


def _sgu_tables(sgu_w, sgu_b, rows_per_seq):
    reps = CHUNK // rows_per_seq
    tril = jnp.tril(sgu_w[:, :rows_per_seq, :rows_per_seq])
    eye = jnp.eye(reps, dtype=F32)
    mix = jnp.einsum("ab,gts->gatbs", eye, tril).reshape(N_GROUPS_SGU, CHUNK, CHUNK)
    bias = jnp.tile(jnp.repeat(sgu_b[:, :rows_per_seq].T, D_SGU // N_GROUPS_SGU, axis=1), (reps, 1))
    return mix.astype(BF16), bias


def _layer(xp, xs, ck, cv, pp, ps, w, rel_bias, alpha):
    batch, seq, d = xp.shape
    dec_b, dec_t, _ = xs.shape
    w_buf = ck.shape[1]
    n_p, n_s = batch * seq, dec_b * dec_t
    n_tok = n_p + n_s

    scale = jnp.concatenate([jnp.full((D_ATTN,), HEAD_DIM ** -0.5, F32),
                             jnp.ones((w["w_in"].shape[1] - D_ATTN,), F32)])
    w_in = (w["w_in"] * scale).astype(BF16)

    qp, kp, vp, up, gp, kp_t, vp_t = _proj(xp.reshape(n_p, d), w_in, 512, seq)
    qs, ks, vs, us, gs = _proj(xs.reshape(n_s, d), w_in, 512)

    b1, b4, b16 = _prompt_bias_tables(rel_bias)
    attn_p = _attn_prompt(qp, kp, vp, b1, b4, b16, batch, seq)

    attn_s = _sample_attention(rel_bias, qs.reshape(dec_b, dec_t, D_ATTN), ks.reshape(dec_b, dec_t, D_ATTN),
                               vs.reshape(dec_b, dec_t, D_ATTN), ck, cv).reshape(n_s, D_ATTN)

    ln_g, ln_b = w["sgu_ln_g"][None], w["sgu_ln_b"][None]
    mix_p, bias_p = _sgu_tables(w["sgu_w"], w["sgu_b"], CHUNK)
    mix_s, bias_s = _sgu_tables(w["sgu_w"], w["sgu_b"], dec_t)
    (sgu_p,) = _sgu(up, gp, mix_p, bias_p, ln_g, ln_b, False, 4)
    sgu_s, z2_s = _sgu(us, gs, mix_s, bias_s, ln_g, ln_b, True, 4)

    x1, x1_packed, idx, gates = _mid(xp.reshape(n_p, d), xs.reshape(n_s, d), attn_p, attn_s, sgu_p, sgu_s,
                                     w["w_mix_out"].astype(BF16), w["ln1_g"][None], w["ln1_b"][None],
                                     w["w_router"].T.astype(BF16), w["router_bias"][:, None], alpha, 512)

    rank, counts = _rank(idx, 512)
    counts = counts[:, 0].astype(jnp.int32)
    padded = (counts + EXPERT_ROWS - 1) // EXPERT_ROWS * EXPERT_ROWS
    pend = jnp.cumsum(padded)
    pstart = (pend - padded).astype(jnp.int32)
    n_blocks = (n_tok * TOP_K + N_EXPERTS * (EXPERT_ROWS - 1)) // EXPERT_ROWS
    n_used = (pend[-1:] // EXPERT_ROWS).astype(jnp.int32)

    dest = _dest(idx, rank, pstart.astype(F32)[:, None], 512)
    x_sorted = _sc_dispatch(x1_packed, dest, n_blocks * EXPERT_ROWS)
    y_sorted = _experts(pstart // EXPERT_ROWS, padded // EXPERT_ROWS, counts, n_used, x_sorted,
                        w["w_gate_e"], w["w_up_e"], w["w_down_e"])

    base = _dense(x1, pp.reshape(n_p, -1), ps.reshape(n_s, -1), w["w_gate_s"].astype(BF16), w["w_up_s"].astype(BF16),
                  w["w_down_s"].astype(BF16), w["w_ple_gate"].astype(BF16),
                  w["w_ple_proj"].astype(BF16), alpha, 1024)

    y_slots = _sc_gather(y_sorted, dest, n_tok)
    y_p, y_s = _combine(gates.T, base, y_slots, w["ln2_g"][None], w["ln2_b"][None], n_p, 512)
    y_p = y_p.reshape(batch, seq, d)
    y_s = y_s.reshape(dec_b, dec_t, d)
    keep = min(MAX_DISTANCE, seq)
    k_rows = kp_t.reshape(batch, N_HEADS, HEAD_DIM, seq)[..., seq - keep:].transpose(0, 3, 1, 2)
    v_rows = vp_t.reshape(batch, N_HEADS, HEAD_DIM, seq)[..., seq - keep:].transpose(0, 3, 1, 2)
    return (y_p, y_s, k_rows, v_rows,
            ks.reshape(dec_b, dec_t, N_HEADS, HEAD_DIM), vs.reshape(dec_b, dec_t, N_HEADS, HEAD_DIM),
            z2_s.reshape(dec_b, dec_t, D_SGU))


def kernel(x_prompt, x_sample, cache_k, cache_v, p_prompt, p_sample, w_in, rel_bias, sgu_w, sgu_b, sgu_ln_g, sgu_ln_b, w_mix_out, ln1_g, ln1_b, w_router, router_bias, w_gate_e, w_up_e, w_down_e, w_gate_s, w_up_s, w_down_s, w_ple_gate, w_ple_proj, ln2_g, ln2_b):
    depth = w_in.shape[0]
    alpha = (2 * depth) ** 0.25
    xp, xs = x_prompt, x_sample
    outs = [[] for _ in range(5)]
    for i in range(depth):
        w = {"w_in": w_in[i], "sgu_w": sgu_w[i], "sgu_b": sgu_b[i], "sgu_ln_g": sgu_ln_g[i],
             "sgu_ln_b": sgu_ln_b[i], "w_mix_out": w_mix_out[i], "ln1_g": ln1_g[i], "ln1_b": ln1_b[i],
             "w_router": w_router[i], "router_bias": router_bias[i], "w_gate_e": w_gate_e[i],
             "w_up_e": w_up_e[i], "w_down_e": w_down_e[i], "w_gate_s": w_gate_s[i], "w_up_s": w_up_s[i],
             "w_down_s": w_down_s[i], "w_ple_gate": w_ple_gate[i], "w_ple_proj": w_ple_proj[i],
             "ln2_g": ln2_g[i], "ln2_b": ln2_b[i]}
        xp, xs, kp, vp, ks, vs, zs = _layer(xp, xs, cache_k[i], cache_v[i], p_prompt[i], p_sample[i],
                                            w, rel_bias, alpha)
        for lst, val in zip(outs, (kp, vp, ks, vs, zs)):
            lst.append(val)
    return (xp, xs) + tuple(jnp.stack(lst) for lst in outs)
```

```python
import functools
import math

import numpy as np
import jax
import jax.numpy as jnp
from jax import lax
from jax.experimental import pallas as pl
from jax.experimental.pallas import tpu as pltpu
from jax.experimental.pallas import tpu_sc as plsc

F32 = jnp.float32
BF16 = jnp.bfloat16
NEG_INF = float("-inf")

N_HEADS = 8
HEAD_DIM = 64
D_ATTN = N_HEADS * HEAD_DIM
PATTERNS = ((128, 1), (512, 4), (2048, 16))
BAND = 128
N_BUCKETS = 32
MAX_DISTANCE = 2048
N_GROUPS_SGU = 8
D_SGU = 512
CHUNK = 128
N_EXPERTS = 256
TOP_K = 8
N_EXPERT_GROUPS = 8
TOPK_GROUPS = 4
ROUTED_SCALE = 2.5
LN_EPS = 1e-5
EXPERT_ROWS = 256
EXPERT_SLOTS = 6
WEIGHT_SLOTS = 3

LANES = 128
SC_CORES = 2
SC_SUBCORES = 16
SC_CHUNK = 128
SC_GATHER_ROWS = 64
VMEM_LIMIT = 56 * 1024 * 1024


def _cparams(*sem):
    return pltpu.CompilerParams(dimension_semantics=sem, vmem_limit_bytes=VMEM_LIMIT)


def _layer_norm(x, g, b):
    mu = jnp.mean(x, axis=-1, keepdims=True)
    xc = x - mu
    var = jnp.mean(xc * xc, axis=-1, keepdims=True)
    return xc * lax.rsqrt(var + LN_EPS) * g + b


def _sigmoid(x):
    return 1.0 / (1.0 + jnp.exp(-x))


def _gelu(x):
    return 0.5 * x * (1.0 + lax.erf(x * math.sqrt(0.5)))


def _pack_bf16_pairs(xb):
    n = xb.shape[1] // 2
    bits = lax.bitcast_convert_type(xb.astype(F32), jnp.uint32)
    return (bits[:, :n] >> 16) | (bits[:, n:] & jnp.uint32(0xFFFF0000))


def _unpack_bf16_pairs(p):
    lo = lax.bitcast_convert_type(p << 16, F32).astype(BF16)
    hi = lax.bitcast_convert_type(p & jnp.uint32(0xFFFF0000), F32).astype(BF16)
    return lo, hi


def _dot(a, b):
    return jnp.dot(a, b, preferred_element_type=F32)


def _dot_nt(a, b):
    return lax.dot_general(a, b, (((1,), (1,)), ((), ())), preferred_element_type=F32)


def _t5_bucket_np(dist):
    max_exact = N_BUCKETS // 2
    df = np.maximum(dist, max_exact).astype(np.float32)
    large = max_exact + (np.log(df / np.float32(max_exact)) / np.float32(math.log(MAX_DISTANCE / max_exact))
                         * np.float32(N_BUCKETS - max_exact)).astype(np.int32)
    return np.where(dist < max_exact, dist, np.minimum(large, N_BUCKETS - 1)).astype(np.int32)


def _band_bucket_table(dilation):
    qi = np.arange(BAND)[:, None]
    ki = np.arange(2 * BAND)[None, :]
    dsub = qi + BAND - ki
    valid = (dsub >= 0) & (dsub <= BAND)
    return np.where(valid, _t5_bucket_np(np.clip(dsub, 0, BAND) * dilation), -1).astype(np.int32)


def _sample_bucket_tables(w_buf, t_len):
    t = np.arange(t_len)[:, None]

    def table(rows, window, dilation):
        d = w_buf + t - rows[None, :]
        ok = (d >= 0) & (d % dilation == 0) & (d <= window)
        return np.where(ok, _t5_bucket_np(np.maximum(d, 0)), -1).astype(np.int32)

    assert w_buf >= PATTERNS[-1][0]
    tables = [table(np.arange(w_buf - window, w_buf), window, dilation) for window, dilation in PATTERNS]
    new_rows = w_buf + np.arange(LANES)
    new = np.stack([table(new_rows, window, dilation) for window, dilation in PATTERNS])
    new[:, :, t_len:] = -1
    return tables, new


def _bias_kernel(rb_ref, bucket_ref, out_ref):
    bucket = bucket_ref[...]
    for h in range(N_HEADS):
        acc = jnp.full(bucket.shape, NEG_INF, F32)
        for b in range(N_BUCKETS):
            acc = jnp.where(bucket == b, rb_ref[b, h], acc)
        out_ref[h] = acc


def _bias_table(rel_bias, bucket_np):
    r, c = bucket_np.shape
    return pl.pallas_call(
        _bias_kernel,
        out_shape=jax.ShapeDtypeStruct((N_HEADS, r, c), F32),
        in_specs=[pl.BlockSpec(memory_space=pltpu.SMEM), pl.BlockSpec(memory_space=pltpu.VMEM)],
        out_specs=pl.BlockSpec(memory_space=pltpu.VMEM),
        name="bias_table",
    )(rel_bias, jnp.asarray(bucket_np))


def _proj_kernel(x_ref, w_ref, q_ref, k_ref, v_ref, u_ref, g_ref, *kv_t_refs):
    x = x_ref[...].astype(BF16)
    col = 0
    rows = []
    for o in (q_ref, k_ref, v_ref, u_ref, g_ref):
        n = o.shape[1]
        rows.append(_dot(x, w_ref[:, col:col + n]))
        o[...] = rows[-1]
        col += n
    for o, val in zip(kv_t_refs, rows[1:3]):
        o[0] = val.T


def _proj(x, w, tm, seq=None):
    m, d = x.shape
    n_out = (D_ATTN, D_ATTN, D_ATTN, D_SGU, D_SGU)
    out_shape = [jax.ShapeDtypeStruct((m, n), F32) for n in n_out]
    out_specs = [pl.BlockSpec((tm, n), lambda i: (i, 0)) for n in n_out]
    if seq is not None:
        tiles = seq // tm
        out_shape += [jax.ShapeDtypeStruct((m // seq, D_ATTN, seq), F32)] * 2
        out_specs += [pl.BlockSpec((1, D_ATTN, tm), lambda i: (i // tiles, 0, i % tiles))] * 2
    return pl.pallas_call(
        _proj_kernel,
        out_shape=out_shape,
        grid=(m // tm,),
        in_specs=[pl.BlockSpec((tm, d), lambda i: (i, 0)), pl.BlockSpec(w.shape, lambda i: (0, 0))],
        out_specs=out_specs,
        compiler_params=_cparams("parallel"),
        name="in_proj",
    )(x, w)


def _band_attn(q, k, v, bias, even):
    q2 = jnp.concatenate([jnp.where(even, q, 0.0), jnp.where(even, 0.0, q)], axis=0).astype(BF16)
    s = _dot_nt(q2, k.astype(BF16)) + bias
    m = jnp.max(s, axis=-1, keepdims=True)
    p = jnp.exp(s - m)
    l = jnp.sum(p, axis=-1, keepdims=True)
    pv = _dot(p.astype(BF16), v.astype(BF16))
    return (jnp.where(even, m[:BAND], m[BAND:]), jnp.where(even, l[:BAND], l[BAND:]),
            jnp.where(even, pv[:BAND], pv[BAND:]))


def _attn_prompt_kernel(q_ref, k_ref, v_ref, b1_ref, b4_ref, b16_ref, o_ref,
                        m1, l1, a1, m4, l4, a4, m16, l16, a16, qc, kc, vc):
    seq = q_ref.shape[0]
    n4 = seq // 4
    per_trip = seq // BAND // 4
    even = lax.broadcasted_iota(jnp.int32, (BAND, LANES), 1) < HEAD_DIM

    def store_all(refs, rows_list, results):
        for rows, (m, l, a) in zip(rows_list, results):
            refs[0][rows, :], refs[1][rows, :], refs[2][rows, :] = m, l, a

    def body(r, carry):
        rows1, loaded1 = [], []
        for j in range(per_trip):
            i = r * per_trip + j
            r0 = pl.multiple_of(i * BAND, BAND)
            k0 = pl.multiple_of(jnp.maximum(i - 1, 0) * BAND, BAND)
            rows1.append(pl.ds(r0, BAND))
            loaded1.append((q_ref[pl.ds(r0, BAND)], k_ref[pl.ds(k0, 2 * BAND)], v_ref[pl.ds(k0, 2 * BAND)],
                            b1_ref[jnp.where(i == 0, 1, 0)]))
        base = pl.multiple_of(r * n4, BAND)
        q4 = q_ref[pl.ds(r, n4, stride=4), :]
        k4 = k_ref[pl.ds(r, n4, stride=4), :]
        v4 = v_ref[pl.ds(r, n4, stride=4), :]
        qc[pl.ds(base, n4)], kc[pl.ds(base, n4)], vc[pl.ds(base, n4)] = q4, k4, v4
        rows16 = [pl.ds(r * n4 + s, seq // 16, stride=4) for s in range(4)]
        loaded16 = [(qc[rows, :], kc[rows, :], vc[rows, :]) for rows in rows16]

        results1 = [_band_attn(q, k, v, bias, even) for q, k, v, bias in loaded1]
        results4, rows4 = [], []
        for i in range(n4 // BAND):
            lo = max(i - 1, 0) * BAND
            hi = (i + 1) * BAND
            col = 0 if i > 0 else BAND
            results4.append(_band_attn(q4[i * BAND:hi], k4[lo:hi], v4[lo:hi], b4_ref[:, col:], even))
            rows4.append(pl.ds(base + i * BAND, BAND))
        results16 = [_band_attn(q, k, v, b16_ref[:, BAND:], even) for q, k, v in loaded16]

        store_all((m1, l1, a1), rows1, results1)
        store_all((m4, l4, a4), rows4, results4)
        store_all((m16, l16, a16), rows16, results16)
        return carry

    lax.fori_loop(0, 4, body, 0)

    def merge_body(i, carry):
        r = i // (n4 // BAND)
        c = i % (n4 // BAND)
        rows = pl.ds(pl.multiple_of(i * BAND, BAND), BAND)
        nat = pl.ds(r + 4 * BAND * c, BAND, stride=4)
        ma, mb, mc = m1[nat, :], m4[rows], m16[rows]
        mx = jnp.maximum(jnp.maximum(ma, mb), mc)
        wa, wb, wc = jnp.exp(ma - mx), jnp.exp(mb - mx), jnp.exp(mc - mx)
        num = wa * a1[nat, :] + wb * a4[rows] + wc * a16[rows]
        den = wa * l1[nat, :] + wb * l4[rows] + wc * l16[rows]
        o_ref[nat, :] = num / den
        return carry

    lax.fori_loop(0, seq // BAND, merge_body, 0)


def _prompt_bias_tables(rel_bias):
    pairs = N_HEADS // 2
    first = _band_bucket_table(1)
    first = np.concatenate([first[:, BAND:], np.full((BAND, BAND), -1, np.int32)], axis=1)
    buckets = np.concatenate([_band_bucket_table(1), first, _band_bucket_table(4), _band_bucket_table(16)], axis=0)
    tables = _bias_table(rel_bias, buckets).reshape(N_HEADS, 4, BAND, 2 * BAND)
    b1, b1_first, b4, b16 = (tables[:, i].reshape(pairs, 2 * BAND, 2 * BAND) for i in range(4))
    return jnp.stack([b1, b1_first], axis=1), b4, b16


def _attn_prompt(q, k, v, b1, b4, b16, batch, seq):
    blk = pl.BlockSpec((seq, LANES), lambda b, j: (b, j))
    bias_spec = pl.BlockSpec((None, 2 * BAND, 2 * BAND), lambda b, j: (j, 0, 0))
    return pl.pallas_call(
        _attn_prompt_kernel,
        out_shape=jax.ShapeDtypeStruct(q.shape, F32),
        grid=(batch, D_ATTN // LANES),
        in_specs=[blk, blk, blk, pl.BlockSpec((None, 2, 2 * BAND, 2 * BAND), lambda b, j: (j, 0, 0, 0)),
                  bias_spec, bias_spec],
        out_specs=blk,
        scratch_shapes=[pltpu.VMEM((seq, LANES), F32) for _ in range(12)],
        compiler_params=_cparams("parallel", "parallel"),
        name="attn_prompt",
    )(q, k, v, b1, b4, b16)


def _attn_sample_kernel(q_ref, kn_ref, vn_ref, kt_ref, vt_ref, b1_ref, b4_ref, b16_ref, bn_ref, o_ref):
    t_len = q_ref.shape[1]
    rows = N_HEADS * t_len
    q = q_ref[0]
    head_of_row = lax.broadcasted_iota(jnp.int32, (rows, D_ATTN), 0) // t_len
    head_of_lane = lax.broadcasted_iota(jnp.int32, (rows, D_ATTN), 1) // HEAD_DIM
    own = head_of_row == head_of_lane
    qrows = jnp.where(own, jnp.concatenate([q] * N_HEADS, axis=0), 0.0).astype(BF16)
    pad = jnp.zeros((LANES - t_len, D_ATTN), F32)
    kn = jnp.concatenate([kn_ref[0], pad], axis=0).astype(BF16)
    vn = jnp.concatenate([vn_ref[0], pad], axis=0).astype(BF16)
    kt = kt_ref[0].reshape(D_ATTN, -1).astype(BF16)
    vt = vt_ref[0].reshape(D_ATTN, -1).astype(BF16)
    s_cache = _dot(qrows, kt)
    s_new = _dot_nt(qrows, kn)
    ms, ls, accs = [], [], []
    for p, bias_ref in enumerate((b1_ref, b4_ref, b16_ref)):
        w = bias_ref.shape[1]
        sc = s_cache[:, -w:] + bias_ref[...]
        sn = s_new + bn_ref[p]
        m = jnp.maximum(jnp.max(sc, axis=-1, keepdims=True), jnp.max(sn, axis=-1, keepdims=True))
        pc = jnp.exp(sc - m)
        pn = jnp.exp(sn - m)
        ls.append(jnp.sum(pc, axis=-1, keepdims=True) + jnp.sum(pn, axis=-1, keepdims=True))
        accs.append(_dot_nt(pc.astype(BF16), vt[:, -w:]) + _dot(pn.astype(BF16), vn))
        ms.append(m)
    mx = jnp.maximum(jnp.maximum(ms[0], ms[1]), ms[2])
    ws = [jnp.exp(m - mx) for m in ms]
    num = ws[0] * accs[0] + ws[1] * accs[1] + ws[2] * accs[2]
    den = ws[0] * ls[0] + ws[1] * ls[1] + ws[2] * ls[2]
    full = jnp.where(own, num / den, 0.0)
    out = full[0:t_len]
    for h in range(1, N_HEADS):
        out = out + full[h * t_len:(h + 1) * t_len]
    o_ref[0] = out


def _sample_attention(rel_bias, q, kn, vn, ck, cv):
    b, t_len, _ = q.shape
    w_buf = ck.shape[1]
    rows = N_HEADS * t_len
    tables, new_t = _sample_bucket_tables(w_buf, t_len)
    widest = max(tb.shape[1] for tb in tables)
    pieces = tables + [new_t.reshape(3 * t_len, LANES)]
    stacked = np.concatenate([np.pad(tb, ((0, 0), (0, widest - tb.shape[1])), constant_values=-1) for tb in pieces])
    bias = _bias_table(rel_bias, stacked)
    b1, b4, b16 = (bias[:, i * t_len:(i + 1) * t_len, :tb.shape[1]].reshape(rows, tb.shape[1])
                   for i, tb in enumerate(tables))
    bn = bias[:, 3 * t_len:, :LANES].reshape(N_HEADS, 3, t_len, LANES).transpose(1, 0, 2, 3).reshape(3, rows, LANES)
    new_spec = pl.BlockSpec((1, t_len, D_ATTN), lambda i: (i, 0, 0))
    cache_spec = pl.BlockSpec((1, N_HEADS, HEAD_DIM, w_buf), lambda i: (i, 0, 0, 0))
    const2 = lambda i: (0, 0)
    return pl.pallas_call(
        _attn_sample_kernel,
        out_shape=jax.ShapeDtypeStruct(q.shape, F32),
        grid=(b,),
        in_specs=[new_spec, new_spec, new_spec, cache_spec, cache_spec,
                  pl.BlockSpec(b1.shape, const2), pl.BlockSpec(b4.shape, const2), pl.BlockSpec(b16.shape, const2),
                  pl.BlockSpec(bn.shape, lambda i: (0, 0, 0))],
        out_specs=new_spec,
        compiler_params=_cparams("parallel"),
        name="attn_sample",
    )(q, kn, vn, ck.transpose(0, 2, 3, 1), cv.transpose(0, 2, 3, 1), b1, b4, b16, bn)


def _sgu_kernel(u_ref, g_ref, mix_ref, bias_ref, lng_ref, lnb_ref, sgu_ref, *z2_out):
    n_chunks = u_ref.shape[0] // CHUNK
    group_of_lane = lax.broadcasted_iota(jnp.int32, (CHUNK, D_SGU), 1) // (D_SGU // N_GROUPS_SGU)
    for c in range(n_chunks):
        rows = slice(c * CHUNK, (c + 1) * CHUNK)
        z1 = _gelu(u_ref[rows])
        z2 = _layer_norm(_gelu(g_ref[rows]), lng_ref[...], lnb_ref[...])
        if z2_out:
            z2_out[0][rows] = z2
        mixed = bias_ref[...]
        for g in range(N_GROUPS_SGU):
            mixed = mixed + _dot(mix_ref[g], jnp.where(group_of_lane == g, z2, 0.0).astype(BF16))
        sgu_ref[rows] = (z1 * mixed).astype(sgu_ref.dtype)


def _sgu(u, g, mix, bias, ln_g, ln_b, want_z2, chunks_per_step):
    m = u.shape[0]
    tm = CHUNK * chunks_per_step
    row_spec = pl.BlockSpec((tm, D_SGU), lambda i: (i, 0))
    const2 = lambda i: (0, 0)
    out_shape = [jax.ShapeDtypeStruct((m, D_SGU), BF16)]
    out_specs = [row_spec]
    if want_z2:
        out_shape.append(jax.ShapeDtypeStruct((m, D_SGU), F32))
        out_specs.append(row_spec)
    return pl.pallas_call(
        _sgu_kernel,
        out_shape=out_shape,
        grid=(m // tm,),
        in_specs=[row_spec, row_spec,
                  pl.BlockSpec(mix.shape, lambda i: (0, 0, 0)),
                  pl.BlockSpec(bias.shape, const2),
                  pl.BlockSpec(ln_g.shape, const2), pl.BlockSpec(ln_b.shape, const2)],
        out_specs=out_specs,
        compiler_params=_cparams("parallel"),
        name="sgu",
    )(u, g, mix, bias, ln_g, ln_b)


def _top1_rows(x, row):
    m = jnp.max(x, axis=0, keepdims=True)
    i = jnp.min(jnp.where(x == m, row, x.shape[0]), axis=0, keepdims=True)
    return m, i


def _pool_specs(n_prompt, tm, width):
    tiles_p = n_prompt // tm
    return (pl.BlockSpec((tm, width), lambda i, *_: (jnp.minimum(i, tiles_p - 1), 0)),
            pl.BlockSpec((tm, width), lambda i, *_: (jnp.maximum(i - tiles_p, 0), 0)))


def _pool_rows(tiles_p, prompt_ref, sample_ref, dtype):
    return jnp.where(pl.program_id(0) < tiles_p, prompt_ref[...].astype(dtype), sample_ref[...].astype(dtype))


def _mid_kernel(alpha, tiles_p, xp_ref, xs_ref, ap_ref, as_ref, sp_ref, ss_ref, wmix_ref, g1_ref, b1_ref,
                wrt_ref, rb_ref, x1_ref, x1p_ref, idx_ref, gate_ref):
    y = (_dot(_pool_rows(tiles_p, ap_ref, as_ref, BF16), wmix_ref[0:D_ATTN])
         + _dot(_pool_rows(tiles_p, sp_ref, ss_ref, BF16), wmix_ref[D_ATTN:])
         + alpha * _pool_rows(tiles_p, xp_ref, xs_ref, F32))
    x1 = _layer_norm(y, g1_ref[...], b1_ref[...])
    x1_ref[...] = x1
    tm = x1.shape[0]
    x1b = x1.astype(BF16)
    x1p_ref[...] = _pack_bf16_pairs(x1b)
    scores = _sigmoid(_dot_nt(wrt_ref[...], x1b))
    biased = scores + rb_ref[...]
    per_group = N_EXPERTS // N_EXPERT_GROUPS
    row_g = lax.broadcasted_iota(jnp.int32, (per_group, tm), 0)
    row_8 = lax.broadcasted_iota(jnp.int32, (N_EXPERT_GROUPS, tm), 0)
    row_e = lax.broadcasted_iota(jnp.int32, (N_EXPERTS, tm), 0)
    cur = jnp.full((N_EXPERT_GROUPS, tm), NEG_INF, F32)
    for g in range(N_EXPERT_GROUPS):
        blk = biased[g * per_group:(g + 1) * per_group]
        m_a, i_a = _top1_rows(blk, row_g)
        m_b = jnp.max(jnp.where(row_g == i_a, NEG_INF, blk), axis=0, keepdims=True)
        cur = jnp.where(row_8 == g, m_a + m_b, cur)
    chosen = jnp.zeros((N_EXPERT_GROUPS, tm), F32)
    for _ in range(TOPK_GROUPS):
        _, i_g = _top1_rows(cur, row_8)
        sel = row_8 == i_g
        chosen = jnp.where(sel, 1.0, chosen)
        cur = jnp.where(sel, NEG_INF, cur)
    cur = jnp.concatenate([jnp.where(chosen[g:g + 1] > 0.0, biased[g * per_group:(g + 1) * per_group], NEG_INF)
                           for g in range(N_EXPERT_GROUPS)], axis=0)
    idx = jnp.zeros((TOP_K, tm), jnp.int32)
    gates = jnp.zeros((TOP_K, tm), F32)
    for k in range(TOP_K):
        _, i_e = _top1_rows(cur, row_e)
        sel = row_e == i_e
        gate_k = jnp.sum(jnp.where(sel, scores, 0.0), axis=0, keepdims=True)
        cur = jnp.where(sel, NEG_INF, cur)
        idx = jnp.where(row_8 == k, i_e, idx)
        gates = jnp.where(row_8 == k, gate_k, gates)
    idx_ref[...] = idx
    gate_ref[...] = gates / jnp.sum(gates, axis=0, keepdims=True) * ROUTED_SCALE


def _mid(xp, xs, attn_p, attn_s, sgu_p, sgu_s, wmix, g1, b1, wrt, rb, alpha, tm):
    n_p, d = xp.shape
    t = n_p + xs.shape[0]
    const2 = lambda i: (0, 0)
    return pl.pallas_call(
        functools.partial(_mid_kernel, alpha, n_p // tm),
        out_shape=[jax.ShapeDtypeStruct((t, d), F32),
                   jax.ShapeDtypeStruct((t, d // 2), jnp.uint32),
                   jax.ShapeDtypeStruct((TOP_K, t), jnp.int32),
                   jax.ShapeDtypeStruct((TOP_K, t), F32)],
        grid=(t // tm,),
        in_specs=[*_pool_specs(n_p, tm, d), *_pool_specs(n_p, tm, D_ATTN), *_pool_specs(n_p, tm, D_SGU),
                  pl.BlockSpec(wmix.shape, const2), pl.BlockSpec(g1.shape, const2),
                  pl.BlockSpec(b1.shape, const2), pl.BlockSpec(wrt.shape, const2),
                  pl.BlockSpec(rb.shape, const2)],
        out_specs=[pl.BlockSpec((tm, d), lambda i: (i, 0)),
                   pl.BlockSpec((tm, d // 2), lambda i: (i, 0)),
                   pl.BlockSpec((TOP_K, tm), lambda i: (0, i)),
                   pl.BlockSpec((TOP_K, tm), lambda i: (0, i))],
        compiler_params=_cparams("parallel"),
        name="mix_ln_router",
    )(xp, xs, attn_p, attn_s, sgu_p, sgu_s, wmix, g1, b1, wrt, rb)


def _rank_kernel(idx_ref, rank_ref, counts_ref, run_ref):
    @pl.when(pl.program_id(0) == 0)
    def _():
        run_ref[...] = jnp.zeros_like(run_ref)

    tm = idx_ref.shape[1]
    row_e = lax.broadcasted_iota(jnp.int32, (N_EXPERTS, tm), 0)
    idx = idx_ref[...]
    onehot = jnp.zeros((N_EXPERTS, tm), F32)
    for k in range(TOP_K):
        onehot = onehot + jnp.where(row_e == idx[k:k + 1], 1.0, 0.0)
    earlier = (lax.broadcasted_iota(jnp.int32, (tm, tm), 0)
               < lax.broadcasted_iota(jnp.int32, (tm, tm), 1))
    before = run_ref[...] + _dot(onehot.astype(BF16), jnp.where(earlier, 1.0, 0.0).astype(BF16))
    row_k = lax.broadcasted_iota(jnp.int32, (TOP_K, tm), 0)
    ranks = jnp.zeros((TOP_K, tm), F32)
    for k in range(TOP_K):
        rank_k = jnp.sum(jnp.where(row_e == idx[k:k + 1], before, 0.0), axis=0, keepdims=True)
        ranks = jnp.where(row_k == k, rank_k, ranks)
    rank_ref[...] = ranks.astype(jnp.int32)
    run_ref[...] = run_ref[...] + jnp.sum(onehot, axis=1, keepdims=True)
    counts_ref[...] = run_ref[...]


def _rank(idx, tm):
    t = idx.shape[1]
    return pl.pallas_call(
        _rank_kernel,
        out_shape=[jax.ShapeDtypeStruct((TOP_K, t), jnp.int32),
                   jax.ShapeDtypeStruct((N_EXPERTS, 1), F32)],
        grid=(t // tm,),
        in_specs=[pl.BlockSpec((TOP_K, tm), lambda i: (0, i))],
        out_specs=[pl.BlockSpec((TOP_K, tm), lambda i: (0, i)),
                   pl.BlockSpec((N_EXPERTS, 1), lambda i: (0, 0))],
        scratch_shapes=[pltpu.VMEM((N_EXPERTS, 1), F32)],
        compiler_params=_cparams("arbitrary"),
        name="expert_rank",
    )(idx)


def _dest_kernel(idx_ref, rank_ref, pstart_ref, dest_ref):
    tm = idx_ref.shape[1]
    row_e = lax.broadcasted_iota(jnp.int32, (N_EXPERTS, tm), 0)
    row_k = lax.broadcasted_iota(jnp.int32, (TOP_K, tm), 0)
    idx = idx_ref[...]
    start = jnp.zeros((TOP_K, tm), F32)
    for k in range(TOP_K):
        start_k = jnp.sum(jnp.where(row_e == idx[k:k + 1], pstart_ref[...], 0.0), axis=0, keepdims=True)
        start = jnp.where(row_k == k, start_k, start)
    dest = start.astype(jnp.int32) + rank_ref[...]
    for c in range(tm // SC_CHUNK):
        dest_ref[c] = dest[:, c * SC_CHUNK:(c + 1) * SC_CHUNK]


def _dest(idx, rank, pstart_col, tm):
    t = idx.shape[1]
    tok_spec = pl.BlockSpec((TOP_K, tm), lambda i: (0, i))
    return pl.pallas_call(
        _dest_kernel,
        out_shape=jax.ShapeDtypeStruct((t // SC_CHUNK, TOP_K, SC_CHUNK), jnp.int32),
        grid=(t // tm,),
        in_specs=[tok_spec, tok_spec, pl.BlockSpec(pstart_col.shape, lambda i: (0, 0))],
        out_specs=pl.BlockSpec((tm // SC_CHUNK, TOP_K, SC_CHUNK), lambda i: (i, 0, 0)),
        compiler_params=_cparams("parallel"),
        name="moe_dest",
    )(idx, rank, pstart_col)


def _sc_mesh():
    return plsc.VectorSubcoreMesh(core_axis_name="c", subcore_axis_name="s",
                                  num_cores=SC_CORES, num_subcores=SC_SUBCORES)


def _sc_chunks(n_chunks):
    workers = SC_CORES * SC_SUBCORES
    wid = lax.axis_index("s") * SC_CORES + lax.axis_index("c")
    return wid, workers, (n_chunks - wid + workers - 1) // workers


def _sc_dispatch(x_rows, dest, n_rows):
    n_chunks = dest.shape[0]
    width = x_rows.shape[1]

    @functools.partial(
        pl.kernel, mesh=_sc_mesh(),
        out_type=jax.ShapeDtypeStruct((n_rows, width), x_rows.dtype),
        scratch_types=[pltpu.VMEM((TOP_K, SC_CHUNK), jnp.int32), pltpu.VMEM((SC_CHUNK, width), x_rows.dtype),
                       pltpu.SemaphoreType.DMA],
    )
    def scatter(x_hbm, dest_hbm, out_hbm, dest_v, rows_v, sem):
        wid, workers, n_own = _sc_chunks(n_chunks)

        @pl.loop(0, n_own)
        def _(j):
            ch = wid + j * workers
            pltpu.sync_copy(dest_hbm.at[ch], dest_v)
            pltpu.sync_copy(x_hbm.at[pl.ds(ch * SC_CHUNK, SC_CHUNK)], rows_v)
            copies = [pltpu.async_copy(rows_v, out_hbm.at[dest_v.at[k]], sem) for k in range(TOP_K)]
            for cp in copies:
                cp.wait()

    return scatter(x_rows, dest)


def _expert_kernel(first_ref, nblk_ref, count_ref, nused_ref, xs_hbm, wg_hbm, wu_hbm, wd_hbm, ys_hbm,
                   xbuf, ybuf, xsem, ysem, wg_f, wu_f, wd_f, wsem):
    n_exp = wg_hbm.shape[0]
    n_used = nused_ref[0]
    slots, _, half = xbuf.shape
    weight_slots = wg_f.shape[0]

    def block_rows(j):
        return pl.ds(pl.multiple_of(j * EXPERT_ROWS, EXPERT_ROWS), EXPERT_ROWS)

    def x_copy(j, slot):
        return pltpu.make_async_copy(xs_hbm.at[block_rows(j)], xbuf.at[slot], xsem.at[slot])

    def y_copy(j, slot):
        return pltpu.make_async_copy(ybuf.at[slot], ys_hbm.at[block_rows(j)], ysem.at[slot])

    def weight_copies(e, slot):
        return [pltpu.make_async_copy(src.at[e], dst.at[slot], wsem.at[i, slot])
                for i, (src, dst) in enumerate(((wg_hbm, wg_f), (wu_hbm, wu_f), (wd_hbm, wd_f)))]

    for k in range(slots - 1):
        @pl.when(k < n_used)
        def _():
            x_copy(k, k).start()
    for k in range(min(weight_slots - 1, n_exp)):
        for cp in weight_copies(k, k):
            cp.start()

    def expert(e, carry):
        wslot = e % weight_slots
        for cp in weight_copies(e, wslot):
            cp.wait()
        ahead_e = e + weight_slots - 1

        @pl.when(ahead_e < n_exp)
        def _():
            for cp in weight_copies(ahead_e, ahead_e % weight_slots):
                cp.start()

        def block(b, carry):
            j = first_ref[e] + b
            slot = j % slots
            x_copy(j, slot).wait()
            ahead = j + slots - 1

            @pl.when(ahead < n_used)
            def _():
                x_copy(ahead, ahead % slots).start()

            @pl.when(j >= slots)
            def _():
                y_copy(j - slots, slot).wait()

            row = lax.broadcasted_iota(jnp.int32, (EXPERT_ROWS, half), 0)
            packed = jnp.where(row < count_ref[e] - b * EXPERT_ROWS, xbuf[slot], jnp.uint32(0))
            lo, hi = _unpack_bf16_pairs(packed)
            wg, wu = wg_f[wslot].astype(BF16), wu_f[wslot].astype(BF16)
            g = _dot(lo, wg[:half]) + _dot(hi, wg[half:])
            u = _dot(lo, wu[:half]) + _dot(hi, wu[half:])
            h = (g * _sigmoid(g) * u).astype(BF16)
            ybuf[slot] = _pack_bf16_pairs(_dot(h, wd_f[wslot].astype(BF16)).astype(BF16))
            y_copy(j, slot).start()
            return carry

        lax.fori_loop(0, nblk_ref[e], block, 0)
        return carry

    lax.fori_loop(0, n_exp, expert, 0)

    for k in range(1, slots + 1):
        @pl.when(n_used >= k)
        def _():
            y_copy(n_used - k, (n_used - k) % slots).wait()


def _experts(first_block, n_block, counts, n_used, xs, wg, wu, wd):
    n_rows, half = xs.shape
    n_exp, d, de = wg.shape
    any_spec = pl.BlockSpec(memory_space=pl.ANY)
    return pl.pallas_call(
        _expert_kernel,
        out_shape=jax.ShapeDtypeStruct((n_rows, half), jnp.uint32),
        grid_spec=pltpu.PrefetchScalarGridSpec(
            num_scalar_prefetch=4,
            grid=(1,),
            in_specs=[any_spec, any_spec, any_spec, any_spec],
            out_specs=any_spec,
            scratch_shapes=[pltpu.VMEM((EXPERT_SLOTS, EXPERT_ROWS, half), jnp.uint32),
                            pltpu.VMEM((EXPERT_SLOTS, EXPERT_ROWS, half), jnp.uint32),
                            pltpu.SemaphoreType.DMA((EXPERT_SLOTS,)), pltpu.SemaphoreType.DMA((EXPERT_SLOTS,)),
                            pltpu.VMEM((WEIGHT_SLOTS, d, de), F32), pltpu.VMEM((WEIGHT_SLOTS, d, de), F32),
                            pltpu.VMEM((WEIGHT_SLOTS, de, d), F32), pltpu.SemaphoreType.DMA((3, WEIGHT_SLOTS))]),
        compiler_params=_cparams("arbitrary"),
        name="routed_experts",
    )(first_block, n_block, counts, n_used, xs, wg, wu, wd)


def _dense_kernel(alpha, tiles_p, x1_ref, pp_ref, ps_ref, wgs_ref, wus_ref, wds_ref, wpg_ref, wpp_ref, base_ref):
    x1 = x1_ref[...]
    xb = x1.astype(BF16)
    g = _dot(xb, wgs_ref[...])
    h = (g * _sigmoid(g) * _dot(xb, wus_ref[...])).astype(BF16)
    shared = _dot(h, wds_ref[...])
    ple = _sigmoid(_dot(xb, wpg_ref[...])) * _dot(_pool_rows(tiles_p, pp_ref, ps_ref, BF16), wpp_ref[...])
    base_ref[...] = alpha * x1 + shared + ple


def _dense(x1, pp, ps, wgs, wus, wds, wpg, wpp, alpha, tm):
    t, d = x1.shape
    const2 = lambda i: (0, 0)
    return pl.pallas_call(
        functools.partial(_dense_kernel, alpha, pp.shape[0] // tm),
        out_shape=jax.ShapeDtypeStruct((t, d), F32),
        grid=(t // tm,),
        in_specs=[pl.BlockSpec((tm, d), lambda i: (i, 0)), *_pool_specs(pp.shape[0], tm, pp.shape[1]),
                  pl.BlockSpec(wgs.shape, const2), pl.BlockSpec(wus.shape, const2),
                  pl.BlockSpec(wds.shape, const2), pl.BlockSpec(wpg.shape, const2),
                  pl.BlockSpec(wpp.shape, const2)],
        out_specs=pl.BlockSpec((tm, d), lambda i: (i, 0)),
        compiler_params=_cparams("parallel"),
        name="shared_ple",
    )(x1, pp, ps, wgs, wus, wds, wpg, wpp)


def _sc_gather(y_rows, dest, n_tok):
    n_chunks = dest.shape[0]
    width = y_rows.shape[1]
    items = [(k, h) for k in range(TOP_K) for h in range(SC_CHUNK // SC_GATHER_ROWS)]

    @functools.partial(
        pl.kernel, mesh=_sc_mesh(),
        out_type=jax.ShapeDtypeStruct((TOP_K, n_tok, width), y_rows.dtype),
        scratch_types=[pltpu.VMEM((TOP_K, SC_CHUNK), jnp.int32),
                       pltpu.VMEM((2, SC_GATHER_ROWS, width), y_rows.dtype),
                       pltpu.SemaphoreType.DMA, pltpu.SemaphoreType.DMA],
    )
    def gather(y_hbm, dest_hbm, out_hbm, dest_v, rows_v, gsem, wsem):
        wid, workers, n_own = _sc_chunks(n_chunks)

        @pl.loop(0, n_own)
        def _(j):
            ch = wid + j * workers
            pltpu.sync_copy(dest_hbm.at[ch], dest_v)

            def fetch(n):
                k, h = items[n]
                return pltpu.async_copy(y_hbm.at[dest_v.at[k, pl.ds(h * SC_GATHER_ROWS, SC_GATHER_ROWS)]],
                                        rows_v.at[n % 2], gsem)

            def write(n):
                k, h = items[n]
                rows = pl.ds(ch * SC_CHUNK + h * SC_GATHER_ROWS, SC_GATHER_ROWS)
                return pltpu.async_copy(rows_v.at[n % 2], out_hbm.at[k, rows], wsem)

            pending_fetch = fetch(0)
            pending_write = None
            for n in range(len(items)):
                pending_fetch.wait()
                if pending_write is not None:
                    pending_write.wait()
                if n + 1 < len(items):
                    pending_fetch = fetch(n + 1)
                pending_write = write(n)
            pending_write.wait()

    return gather(y_rows, dest)


def _combine_kernel(tiles_p, gate_ref, base_ref, yk_ref, g2_ref, b2_ref, outp_ref, outs_ref):
    gates = gate_ref[...]
    half = yk_ref.shape[2]
    acc_lo = base_ref[:, :half]
    acc_hi = base_ref[:, half:]
    for k in range(TOP_K):
        packed = yk_ref[k]
        gate = gates[:, k:k + 1]
        acc_lo = acc_lo + gate * lax.bitcast_convert_type(packed << 16, F32)
        acc_hi = acc_hi + gate * lax.bitcast_convert_type(packed & jnp.uint32(0xFFFF0000), F32)
    out = _layer_norm(jnp.concatenate([acc_lo, acc_hi], axis=1), g2_ref[...], b2_ref[...])
    is_prompt = pl.program_id(0) < tiles_p

    @pl.when(is_prompt)
    def _():
        outp_ref[...] = out

    @pl.when(jnp.logical_not(is_prompt))
    def _():
        outs_ref[...] = out


def _combine(gates_tok, base, yk, g2, b2, n_prompt, tm):
    t, d = base.shape
    const2 = lambda i: (0, 0)
    return pl.pallas_call(
        functools.partial(_combine_kernel, n_prompt // tm),
        out_shape=[jax.ShapeDtypeStruct((n_prompt, d), F32), jax.ShapeDtypeStruct((t - n_prompt, d), F32)],
        grid=(t // tm,),
        in_specs=[pl.BlockSpec((tm, TOP_K), lambda i: (i, 0)),
                  pl.BlockSpec((tm, d), lambda i: (i, 0)),
                  pl.BlockSpec((TOP_K, tm, yk.shape[2]), lambda i: (0, i, 0)),
                  pl.BlockSpec(g2.shape, const2), pl.BlockSpec(b2.shape, const2)],
        out_specs=list(_pool_specs(n_prompt, tm, d)),
        compiler_params=_cparams("arbitrary"),
        name="moe_combine",
    )(gates_tok, base, yk, g2, b2)


def _sgu_tables(sgu_w, sgu_b, rows_per_seq):
    reps = CHUNK // rows_per_seq
    tril = jnp.tril(sgu_w[:, :rows_per_seq, :rows_per_seq])
    eye = jnp.eye(reps, dtype=F32)
    mix = jnp.einsum("ab,gts->gatbs", eye, tril).reshape(N_GROUPS_SGU, CHUNK, CHUNK)
    bias = jnp.tile(jnp.repeat(sgu_b[:, :rows_per_seq].T, D_SGU // N_GROUPS_SGU, axis=1), (reps, 1))
    return mix.astype(BF16), bias


def _layer(xp, xs, ck, cv, pp, ps, w, rel_bias, alpha):
    batch, seq, d = xp.shape
    dec_b, dec_t, _ = xs.shape
    w_buf = ck.shape[1]
    n_p, n_s = batch * seq, dec_b * dec_t
    n_tok = n_p + n_s

    scale = jnp.concatenate([jnp.full((D_ATTN,), HEAD_DIM ** -0.5, F32),
                             jnp.ones((w["w_in"].shape[1] - D_ATTN,), F32)])
    w_in = (w["w_in"] * scale).astype(BF16)

    qp, kp, vp, up, gp, kp_t, vp_t = _proj(xp.reshape(n_p, d), w_in, 512, seq)
    qs, ks, vs, us, gs = _proj(xs.reshape(n_s, d), w_in, 512)

    b1, b4, b16 = _prompt_bias_tables(rel_bias)
    attn_p = _attn_prompt(qp, kp, vp, b1, b4, b16, batch, seq)

    attn_s = _sample_attention(rel_bias, qs.reshape(dec_b, dec_t, D_ATTN), ks.reshape(dec_b, dec_t, D_ATTN),
                               vs.reshape(dec_b, dec_t, D_ATTN), ck, cv).reshape(n_s, D_ATTN)

    ln_g, ln_b = w["sgu_ln_g"][None], w["sgu_ln_b"][None]
    mix_p, bias_p = _sgu_tables(w["sgu_w"], w["sgu_b"], CHUNK)
    mix_s, bias_s = _sgu_tables(w["sgu_w"], w["sgu_b"], dec_t)
    (sgu_p,) = _sgu(up, gp, mix_p, bias_p, ln_g, ln_b, False, 8)
    sgu_s, z2_s = _sgu(us, gs, mix_s, bias_s, ln_g, ln_b, True, 4)

    x1, x1_packed, idx, gates = _mid(xp.reshape(n_p, d), xs.reshape(n_s, d), attn_p, attn_s, sgu_p, sgu_s,
                                     w["w_mix_out"].astype(BF16), w["ln1_g"][None], w["ln1_b"][None],
                                     w["w_router"].T.astype(BF16), w["router_bias"][:, None], alpha, 1024)

    rank, counts = _rank(idx, 512)
    counts = counts[:, 0].astype(jnp.int32)
    padded = (counts + EXPERT_ROWS - 1) // EXPERT_ROWS * EXPERT_ROWS
    pend = jnp.cumsum(padded)
    pstart = (pend - padded).astype(jnp.int32)
    n_blocks = (n_tok * TOP_K + N_EXPERTS * (EXPERT_ROWS - 1)) // EXPERT_ROWS
    n_used = (pend[-1:] // EXPERT_ROWS).astype(jnp.int32)

    dest = _dest(idx, rank, pstart.astype(F32)[:, None], 512)
    x_sorted = _sc_dispatch(x1_packed, dest, n_blocks * EXPERT_ROWS)
    y_sorted = _experts(pstart // EXPERT_ROWS, padded // EXPERT_ROWS, counts, n_used, x_sorted,
                        w["w_gate_e"], w["w_up_e"], w["w_down_e"])

    base = _dense(x1, pp.reshape(n_p, -1), ps.reshape(n_s, -1), w["w_gate_s"].astype(BF16), w["w_up_s"].astype(BF16),
                  w["w_down_s"].astype(BF16), w["w_ple_gate"].astype(BF16),
                  w["w_ple_proj"].astype(BF16), alpha, 1024)

    y_slots = _sc_gather(y_sorted, dest, n_tok)
    y_p, y_s = _combine(gates.T, base, y_slots, w["ln2_g"][None], w["ln2_b"][None], n_p, 512)
    y_p = y_p.reshape(batch, seq, d)
    y_s = y_s.reshape(dec_b, dec_t, d)
    keep = min(MAX_DISTANCE, seq)
    k_rows = kp_t.reshape(batch, N_HEADS, HEAD_DIM, seq)[..., seq - keep:].transpose(0, 3, 1, 2)
    v_rows = vp_t.reshape(batch, N_HEADS, HEAD_DIM, seq)[..., seq - keep:].transpose(0, 3, 1, 2)
    return (y_p, y_s, k_rows, v_rows,
            ks.reshape(dec_b, dec_t, N_HEADS, HEAD_DIM), vs.reshape(dec_b, dec_t, N_HEADS, HEAD_DIM),
            z2_s.reshape(dec_b, dec_t, D_SGU))


def kernel(x_prompt, x_sample, cache_k, cache_v, p_prompt, p_sample, w_in, rel_bias, sgu_w, sgu_b, sgu_ln_g, sgu_ln_b, w_mix_out, ln1_g, ln1_b, w_router, router_bias, w_gate_e, w_up_e, w_down_e, w_gate_s, w_up_s, w_down_s, w_ple_gate, w_ple_proj, ln2_g, ln2_b):
    depth = w_in.shape[0]
    alpha = (2 * depth) ** 0.25
    xp, xs = x_prompt, x_sample
    outs = [[] for _ in range(5)]
    for i in range(depth):
        w = {"w_in": w_in[i], "sgu_w": sgu_w[i], "sgu_b": sgu_b[i], "sgu_ln_g": sgu_ln_g[i],
             "sgu_ln_b": sgu_ln_b[i], "w_mix_out": w_mix_out[i], "ln1_g": ln1_g[i], "ln1_b": ln1_b[i],
             "w_router": w_router[i], "router_bias": router_bias[i], "w_gate_e": w_gate_e[i],
             "w_up_e": w_up_e[i], "w_down_e": w_down_e[i], "w_gate_s": w_gate_s[i], "w_up_s": w_up_s[i],
             "w_down_s": w_down_s[i], "w_ple_gate": w_ple_gate[i], "w_ple_proj": w_ple_proj[i],
             "ln2_g": ln2_g[i], "ln2_b": ln2_b[i]}
        xp, xs, kp, vp, ks, vs, zs = _layer(xp, xs, cache_k[i], cache_v[i], p_prompt[i], p_sample[i],
                                            w, rel_bias, alpha)
        for lst, val in zip(outs, (kp, vp, ks, vs, zs)):
            lst.append(val)
    return (xp, xs) + tuple(jnp.stack(lst) for lst in outs)
```

```python
import functools
import math

import numpy as np
import jax
import jax.numpy as jnp
from jax import lax
from jax.experimental import pallas as pl
from jax.experimental.pallas import tpu as pltpu
from jax.experimental.pallas import tpu_sc as plsc

F32 = jnp.float32
BF16 = jnp.bfloat16
NEG_INF = float("-inf")

N_HEADS = 8
HEAD_DIM = 64
D_ATTN = N_HEADS * HEAD_DIM
PATTERNS = ((128, 1), (512, 4), (2048, 16))
BAND = 128
N_BUCKETS = 32
MAX_DISTANCE = 2048
N_GROUPS_SGU = 8
D_SGU = 512
CHUNK = 128
N_EXPERTS = 256
TOP_K = 8
N_EXPERT_GROUPS = 8
TOPK_GROUPS = 4
ROUTED_SCALE = 2.5
LN_EPS = 1e-5
CLASSES_PER_TRIP = 2
EXPERT_ROWS = 256
EXPERT_SLOTS = 6
WEIGHT_SLOTS = 3

LANES = 128
SC_CORES = 2
SC_SUBCORES = 16
SC_CHUNK = 128
SC_GATHER_ROWS = 64
VMEM_LIMIT = 56 * 1024 * 1024


def _cparams(*sem):
    return pltpu.CompilerParams(dimension_semantics=sem, vmem_limit_bytes=VMEM_LIMIT)


def _layer_norm(x, g, b):
    mu = jnp.mean(x, axis=-1, keepdims=True)
    xc = x - mu
    var = jnp.mean(xc * xc, axis=-1, keepdims=True)
    return xc * lax.rsqrt(var + LN_EPS) * g + b


def _sigmoid(x):
    return 1.0 / (1.0 + jnp.exp(-x))


def _gelu(x):
    return 0.5 * x * (1.0 + lax.erf(x * math.sqrt(0.5)))


def _pack_bf16_pairs(xb):
    n = xb.shape[1] // 2
    bits = lax.bitcast_convert_type(xb.astype(F32), jnp.uint32)
    return (bits[:, :n] >> 16) | (bits[:, n:] & jnp.uint32(0xFFFF0000))


def _unpack_bf16_pairs(p):
    lo = lax.bitcast_convert_type(p << 16, F32).astype(BF16)
    hi = lax.bitcast_convert_type(p & jnp.uint32(0xFFFF0000), F32).astype(BF16)
    return lo, hi


def _dot(a, b):
    return jnp.dot(a, b, preferred_element_type=F32)


def _dot_nt(a, b):
    return lax.dot_general(a, b, (((1,), (1,)), ((), ())), preferred_element_type=F32)


def _t5_bucket_np(dist):
    max_exact = N_BUCKETS // 2
    df = np.maximum(dist, max_exact).astype(np.float32)
    large = max_exact + (np.log(df / np.float32(max_exact)) / np.float32(math.log(MAX_DISTANCE / max_exact))
                         * np.float32(N_BUCKETS - max_exact)).astype(np.int32)
    return np.where(dist < max_exact, dist, np.minimum(large, N_BUCKETS - 1)).astype(np.int32)


def _band_bucket_table(dilation):
    qi = np.arange(BAND)[:, None]
    ki = np.arange(2 * BAND)[None, :]
    dsub = qi + BAND - ki
    valid = (dsub >= 0) & (dsub <= BAND)
    return np.where(valid, _t5_bucket_np(np.clip(dsub, 0, BAND) * dilation), -1).astype(np.int32)


def _sample_bucket_tables(w_buf, t_len):
    t = np.arange(t_len)[:, None]

    def table(rows, window, dilation):
        d = w_buf + t - rows[None, :]
        ok = (d >= 0) & (d % dilation == 0) & (d <= window)
        return np.where(ok, _t5_bucket_np(np.maximum(d, 0)), -1).astype(np.int32)

    assert w_buf >= PATTERNS[-1][0]
    tables = [table(np.arange(w_buf - window, w_buf), window, dilation) for window, dilation in PATTERNS]
    new_rows = w_buf + np.arange(LANES)
    new = np.stack([table(new_rows, window, dilation) for window, dilation in PATTERNS])
    new[:, :, t_len:] = -1
    return tables, new


def _bias_kernel(rb_ref, bucket_ref, out_ref):
    bucket = bucket_ref[...]
    for h in range(N_HEADS):
        acc = jnp.full(bucket.shape, NEG_INF, F32)
        for b in range(N_BUCKETS):
            acc = jnp.where(bucket == b, rb_ref[b, h], acc)
        out_ref[h] = acc


def _bias_table(rel_bias, bucket_np):
    r, c = bucket_np.shape
    return pl.pallas_call(
        _bias_kernel,
        out_shape=jax.ShapeDtypeStruct((N_HEADS, r, c), F32),
        in_specs=[pl.BlockSpec(memory_space=pltpu.SMEM), pl.BlockSpec(memory_space=pltpu.VMEM)],
        out_specs=pl.BlockSpec(memory_space=pltpu.VMEM),
        name="bias_table",
    )(rel_bias, jnp.asarray(bucket_np))


def _proj_kernel(x_ref, w_ref, q_ref, k_ref, v_ref, u_ref, g_ref, *kv_t_refs):
    x = x_ref[...].astype(BF16)
    col = 0
    rows = []
    for o in (q_ref, k_ref, v_ref, u_ref, g_ref):
        n = o.shape[1]
        rows.append(_dot(x, w_ref[:, col:col + n]))
        o[...] = rows[-1]
        col += n
    for o, val in zip(kv_t_refs, rows[1:3]):
        o[0] = val.T


def _proj(x, w, tm, seq=None):
    m, d = x.shape
    n_out = (D_ATTN, D_ATTN, D_ATTN, D_SGU, D_SGU)
    out_shape = [jax.ShapeDtypeStruct((m, n), F32) for n in n_out]
    out_specs = [pl.BlockSpec((tm, n), lambda i: (i, 0)) for n in n_out]
    if seq is not None:
        tiles = seq // tm
        out_shape += [jax.ShapeDtypeStruct((m // seq, D_ATTN, seq), F32)] * 2
        out_specs += [pl.BlockSpec((1, D_ATTN, tm), lambda i: (i // tiles, 0, i % tiles))] * 2
    return pl.pallas_call(
        _proj_kernel,
        out_shape=out_shape,
        grid=(m // tm,),
        in_specs=[pl.BlockSpec((tm, d), lambda i: (i, 0)), pl.BlockSpec(w.shape, lambda i: (0, 0))],
        out_specs=out_specs,
        compiler_params=_cparams("parallel"),
        name="in_proj",
    )(x, w)


def _band_attn(q, k, v, bias, even):
    q2 = jnp.concatenate([jnp.where(even, q, 0.0), jnp.where(even, 0.0, q)], axis=0).astype(BF16)
    s = _dot_nt(q2, k.astype(BF16)) + bias
    m = jnp.max(s, axis=-1, keepdims=True)
    p = jnp.exp(s - m)
    l = jnp.sum(p, axis=-1, keepdims=True)
    pv = _dot(p.astype(BF16), v.astype(BF16))
    return (jnp.where(even, m[:BAND], m[BAND:]), jnp.where(even, l[:BAND], l[BAND:]),
            jnp.where(even, pv[:BAND], pv[BAND:]))


def _attn_prompt_kernel(q_ref, k_ref, v_ref, b1_ref, b4_ref, b16_ref, o_ref,
                        m1, l1, a1, m4, l4, a4, m16, l16, a16, qc, kc, vc):
    seq = q_ref.shape[0]
    n4 = seq // 4
    per_trip = seq // BAND // 4
    even = lax.broadcasted_iota(jnp.int32, (BAND, LANES), 1) < HEAD_DIM

    def store_all(refs, rows_list, results):
        for rows, (m, l, a) in zip(rows_list, results):
            refs[0][rows, :], refs[1][rows, :], refs[2][rows, :] = m, l, a

    def body(g, carry):
        for j in range(CLASSES_PER_TRIP):
            one_class(g * CLASSES_PER_TRIP + j)
        return carry

    def one_class(r):
        rows1, loaded1 = [], []
        for j in range(per_trip):
            i = r * per_trip + j
            r0 = pl.multiple_of(i * BAND, BAND)
            k0 = pl.multiple_of(jnp.maximum(i - 1, 0) * BAND, BAND)
            rows1.append(pl.ds(r0, BAND))
            loaded1.append((q_ref[pl.ds(r0, BAND)], k_ref[pl.ds(k0, 2 * BAND)], v_ref[pl.ds(k0, 2 * BAND)],
                            b1_ref[jnp.where(i == 0, 1, 0)]))
        base = pl.multiple_of(r * n4, BAND)
        q4 = q_ref[pl.ds(r, n4, stride=4), :]
        k4 = k_ref[pl.ds(r, n4, stride=4), :]
        v4 = v_ref[pl.ds(r, n4, stride=4), :]
        qc[pl.ds(base, n4)], kc[pl.ds(base, n4)], vc[pl.ds(base, n4)] = q4, k4, v4
        rows16 = [pl.ds(r * n4 + s, seq // 16, stride=4) for s in range(4)]
        loaded16 = [(qc[rows, :], kc[rows, :], vc[rows, :]) for rows in rows16]

        results1 = [_band_attn(q, k, v, bias, even) for q, k, v, bias in loaded1]
        results4, rows4 = [], []
        for i in range(n4 // BAND):
            lo = max(i - 1, 0) * BAND
            hi = (i + 1) * BAND
            col = 0 if i > 0 else BAND
            results4.append(_band_attn(q4[i * BAND:hi], k4[lo:hi], v4[lo:hi], b4_ref[:, col:], even))
            rows4.append(pl.ds(base + i * BAND, BAND))
        results16 = [_band_attn(q, k, v, b16_ref[:, BAND:], even) for q, k, v in loaded16]

        store_all((m1, l1, a1), rows1, results1)
        store_all((m4, l4, a4), rows4, results4)
        store_all((m16, l16, a16), rows16, results16)

    lax.fori_loop(0, 4 // CLASSES_PER_TRIP, body, 0)

    def merge_body(i, carry):
        r = i // (n4 // BAND)
        c = i % (n4 // BAND)
        rows = pl.ds(pl.multiple_of(i * BAND, BAND), BAND)
        nat = pl.ds(r + 4 * BAND * c, BAND, stride=4)
        ma, mb, mc = m1[nat, :], m4[rows], m16[rows]
        mx = jnp.maximum(jnp.maximum(ma, mb), mc)
        wa, wb, wc = jnp.exp(ma - mx), jnp.exp(mb - mx), jnp.exp(mc - mx)
        num = wa * a1[nat, :] + wb * a4[rows] + wc * a16[rows]
        den = wa * l1[nat, :] + wb * l4[rows] + wc * l16[rows]
        o_ref[nat, :] = num / den
        return carry

    lax.fori_loop(0, seq // BAND, merge_body, 0)


def _prompt_bias_tables(rel_bias):
    pairs = N_HEADS // 2
    first = _band_bucket_table(1)
    first = np.concatenate([first[:, BAND:], np.full((BAND, BAND), -1, np.int32)], axis=1)
    buckets = np.concatenate([_band_bucket_table(1), first, _band_bucket_table(4), _band_bucket_table(16)], axis=0)
    tables = _bias_table(rel_bias, buckets).reshape(N_HEADS, 4, BAND, 2 * BAND)
    b1, b1_first, b4, b16 = (tables[:, i].reshape(pairs, 2 * BAND, 2 * BAND) for i in range(4))
    return jnp.stack([b1, b1_first], axis=1), b4, b16


def _attn_prompt(q, k, v, b1, b4, b16, batch, seq):
    blk = pl.BlockSpec((seq, LANES), lambda b, j: (b, j))
    bias_spec = pl.BlockSpec((None, 2 * BAND, 2 * BAND), lambda b, j: (j, 0, 0))
    return pl.pallas_call(
        _attn_prompt_kernel,
        out_shape=jax.ShapeDtypeStruct(q.shape, F32),
        grid=(batch, D_ATTN // LANES),
        in_specs=[blk, blk, blk, pl.BlockSpec((None, 2, 2 * BAND, 2 * BAND), lambda b, j: (j, 0, 0, 0)),
                  bias_spec, bias_spec],
        out_specs=blk,
        scratch_shapes=[pltpu.VMEM((seq, LANES), F32) for _ in range(12)],
        compiler_params=_cparams("parallel", "parallel"),
        name="attn_prompt",
    )(q, k, v, b1, b4, b16)


def _attn_sample_kernel(q_ref, kn_ref, vn_ref, kt_ref, vt_ref, b1_ref, b4_ref, b16_ref, bn_ref, o_ref):
    t_len = q_ref.shape[1]
    rows = N_HEADS * t_len
    q = q_ref[0]
    head_of_row = lax.broadcasted_iota(jnp.int32, (rows, D_ATTN), 0) // t_len
    head_of_lane = lax.broadcasted_iota(jnp.int32, (rows, D_ATTN), 1) // HEAD_DIM
    own = head_of_row == head_of_lane
    qrows = jnp.where(own, jnp.concatenate([q] * N_HEADS, axis=0), 0.0).astype(BF16)
    pad = jnp.zeros((LANES - t_len, D_ATTN), F32)
    kn = jnp.concatenate([kn_ref[0], pad], axis=0).astype(BF16)
    vn = jnp.concatenate([vn_ref[0], pad], axis=0).astype(BF16)
    kt = kt_ref[0].reshape(D_ATTN, -1).astype(BF16)
    vt = vt_ref[0].reshape(D_ATTN, -1).astype(BF16)
    s_cache = _dot(qrows, kt)
    s_new = _dot_nt(qrows, kn)
    ms, ls, accs = [], [], []
    for p, bias_ref in enumerate((b1_ref, b4_ref, b16_ref)):
        w = bias_ref.shape[1]
        sc = s_cache[:, -w:] + bias_ref[...]
        sn = s_new + bn_ref[p]
        m = jnp.maximum(jnp.max(sc, axis=-1, keepdims=True), jnp.max(sn, axis=-1, keepdims=True))
        pc = jnp.exp(sc - m)
        pn = jnp.exp(sn - m)
        ls.append(jnp.sum(pc, axis=-1, keepdims=True) + jnp.sum(pn, axis=-1, keepdims=True))
        accs.append(_dot_nt(pc.astype(BF16), vt[:, -w:]) + _dot(pn.astype(BF16), vn))
        ms.append(m)
    mx = jnp.maximum(jnp.maximum(ms[0], ms[1]), ms[2])
    ws = [jnp.exp(m - mx) for m in ms]
    num = ws[0] * accs[0] + ws[1] * accs[1] + ws[2] * accs[2]
    den = ws[0] * ls[0] + ws[1] * ls[1] + ws[2] * ls[2]
    full = jnp.where(own, num / den, 0.0)
    out = full[0:t_len]
    for h in range(1, N_HEADS):
        out = out + full[h * t_len:(h + 1) * t_len]
    o_ref[0] = out


def _sample_attention(rel_bias, q, kn, vn, ck, cv):
    b, t_len, _ = q.shape
    w_buf = ck.shape[1]
    rows = N_HEADS * t_len
    tables, new_t = _sample_bucket_tables(w_buf, t_len)
    widest = max(tb.shape[1] for tb in tables)
    pieces = tables + [new_t.reshape(3 * t_len, LANES)]
    stacked = np.concatenate([np.pad(tb, ((0, 0), (0, widest - tb.shape[1])), constant_values=-1) for tb in pieces])
    bias = _bias_table(rel_bias, stacked)
    b1, b4, b16 = (bias[:, i * t_len:(i + 1) * t_len, :tb.shape[1]].reshape(rows, tb.shape[1])
                   for i, tb in enumerate(tables))
    bn = bias[:, 3 * t_len:, :LANES].reshape(N_HEADS, 3, t_len, LANES).transpose(1, 0, 2, 3).reshape(3, rows, LANES)
    new_spec = pl.BlockSpec((1, t_len, D_ATTN), lambda i: (i, 0, 0))
    cache_spec = pl.BlockSpec((1, N_HEADS, HEAD_DIM, w_buf), lambda i: (i, 0, 0, 0))
    const2 = lambda i: (0, 0)
    return pl.pallas_call(
        _attn_sample_kernel,
        out_shape=jax.ShapeDtypeStruct(q.shape, F32),
        grid=(b,),
        in_specs=[new_spec, new_spec, new_spec, cache_spec, cache_spec,
                  pl.BlockSpec(b1.shape, const2), pl.BlockSpec(b4.shape, const2), pl.BlockSpec(b16.shape, const2),
                  pl.BlockSpec(bn.shape, lambda i: (0, 0, 0))],
        out_specs=new_spec,
        compiler_params=_cparams("parallel"),
        name="attn_sample",
    )(q, kn, vn, ck.transpose(0, 2, 3, 1), cv.transpose(0, 2, 3, 1), b1, b4, b16, bn)


def _sgu_kernel(u_ref, g_ref, mix_ref, bias_ref, lng_ref, lnb_ref, sgu_ref, *z2_out):
    n_chunks = u_ref.shape[0] // CHUNK
    group_of_lane = lax.broadcasted_iota(jnp.int32, (CHUNK, D_SGU), 1) // (D_SGU // N_GROUPS_SGU)
    for c in range(n_chunks):
        rows = slice(c * CHUNK, (c + 1) * CHUNK)
        z1 = _gelu(u_ref[rows])
        z2 = _layer_norm(_gelu(g_ref[rows]), lng_ref[...], lnb_ref[...])
        if z2_out:
            z2_out[0][rows] = z2
        mixed = bias_ref[...]
        for g in range(N_GROUPS_SGU):
            mixed = mixed + _dot(mix_ref[g], jnp.where(group_of_lane == g, z2, 0.0).astype(BF16))
        sgu_ref[rows] = (z1 * mixed).astype(sgu_ref.dtype)


def _sgu(u, g, mix, bias, ln_g, ln_b, want_z2, chunks_per_step):
    m = u.shape[0]
    tm = CHUNK * chunks_per_step
    row_spec = pl.BlockSpec((tm, D_SGU), lambda i: (i, 0))
    const2 = lambda i: (0, 0)
    out_shape = [jax.ShapeDtypeStruct((m, D_SGU), BF16)]
    out_specs = [row_spec]
    if want_z2:
        out_shape.append(jax.ShapeDtypeStruct((m, D_SGU), F32))
        out_specs.append(row_spec)
    return pl.pallas_call(
        _sgu_kernel,
        out_shape=out_shape,
        grid=(m // tm,),
        in_specs=[row_spec, row_spec,
                  pl.BlockSpec(mix.shape, lambda i: (0, 0, 0)),
                  pl.BlockSpec(bias.shape, const2),
                  pl.BlockSpec(ln_g.shape, const2), pl.BlockSpec(ln_b.shape, const2)],
        out_specs=out_specs,
        compiler_params=_cparams("parallel"),
        name="sgu",
    )(u, g, mix, bias, ln_g, ln_b)


def _top1_rows(x, row):
    m = jnp.max(x, axis=0, keepdims=True)
    i = jnp.min(jnp.where(x == m, row, x.shape[0]), axis=0, keepdims=True)
    return m, i


def _pool_specs(n_prompt, tm, width):
    tiles_p = n_prompt // tm
    return (pl.BlockSpec((tm, width), lambda i, *_: (jnp.minimum(i, tiles_p - 1), 0)),
            pl.BlockSpec((tm, width), lambda i, *_: (jnp.maximum(i - tiles_p, 0), 0)))


def _pool_rows(tiles_p, prompt_ref, sample_ref, dtype):
    return jnp.where(pl.program_id(0) < tiles_p, prompt_ref[...].astype(dtype), sample_ref[...].astype(dtype))


def _mid_kernel(alpha, tiles_p, xp_ref, xs_ref, ap_ref, as_ref, sp_ref, ss_ref, wmix_ref, g1_ref, b1_ref,
                wrt_ref, rb_ref, x1_ref, x1p_ref, idx_ref, gate_ref):
    y = (_dot(_pool_rows(tiles_p, ap_ref, as_ref, BF16), wmix_ref[0:D_ATTN])
         + _dot(_pool_rows(tiles_p, sp_ref, ss_ref, BF16), wmix_ref[D_ATTN:])
         + alpha * _pool_rows(tiles_p, xp_ref, xs_ref, F32))
    x1 = _layer_norm(y, g1_ref[...], b1_ref[...])
    x1_ref[...] = x1
    tm = x1.shape[0]
    x1b = x1.astype(BF16)
    x1p_ref[...] = _pack_bf16_pairs(x1b)
    scores = _sigmoid(_dot_nt(wrt_ref[...], x1b))
    biased = scores + rb_ref[...]
    per_group = N_EXPERTS // N_EXPERT_GROUPS
    row_g = lax.broadcasted_iota(jnp.int32, (per_group, tm), 0)
    row_8 = lax.broadcasted_iota(jnp.int32, (N_EXPERT_GROUPS, tm), 0)
    row_e = lax.broadcasted_iota(jnp.int32, (N_EXPERTS, tm), 0)
    cur = jnp.full((N_EXPERT_GROUPS, tm), NEG_INF, F32)
    for g in range(N_EXPERT_GROUPS):
        blk = biased[g * per_group:(g + 1) * per_group]
        m_a, i_a = _top1_rows(blk, row_g)
        m_b = jnp.max(jnp.where(row_g == i_a, NEG_INF, blk), axis=0, keepdims=True)
        cur = jnp.where(row_8 == g, m_a + m_b, cur)
    chosen = jnp.zeros((N_EXPERT_GROUPS, tm), F32)
    for _ in range(TOPK_GROUPS):
        _, i_g = _top1_rows(cur, row_8)
        sel = row_8 == i_g
        chosen = jnp.where(sel, 1.0, chosen)
        cur = jnp.where(sel, NEG_INF, cur)
    cur = jnp.concatenate([jnp.where(chosen[g:g + 1] > 0.0, biased[g * per_group:(g + 1) * per_group], NEG_INF)
                           for g in range(N_EXPERT_GROUPS)], axis=0)
    idx = jnp.zeros((TOP_K, tm), jnp.int32)
    gates = jnp.zeros((TOP_K, tm), F32)
    for k in range(TOP_K):
        _, i_e = _top1_rows(cur, row_e)
        sel = row_e == i_e
        gate_k = jnp.sum(jnp.where(sel, scores, 0.0), axis=0, keepdims=True)
        cur = jnp.where(sel, NEG_INF, cur)
        idx = jnp.where(row_8 == k, i_e, idx)
        gates = jnp.where(row_8 == k, gate_k, gates)
    idx_ref[...] = idx
    gate_ref[...] = gates / jnp.sum(gates, axis=0, keepdims=True) * ROUTED_SCALE


def _mid(xp, xs, attn_p, attn_s, sgu_p, sgu_s, wmix, g1, b1, wrt, rb, alpha, tm):
    n_p, d = xp.shape
    t = n_p + xs.shape[0]
    const2 = lambda i: (0, 0)
    return pl.pallas_call(
        functools.partial(_mid_kernel, alpha, n_p // tm),
        out_shape=[jax.ShapeDtypeStruct((t, d), F32),
                   jax.ShapeDtypeStruct((t, d // 2), jnp.uint32),
                   jax.ShapeDtypeStruct((TOP_K, t), jnp.int32),
                   jax.ShapeDtypeStruct((TOP_K, t), F32)],
        grid=(t // tm,),
        in_specs=[*_pool_specs(n_p, tm, d), *_pool_specs(n_p, tm, D_ATTN), *_pool_specs(n_p, tm, D_SGU),
                  pl.BlockSpec(wmix.shape, const2), pl.BlockSpec(g1.shape, const2),
                  pl.BlockSpec(b1.shape, const2), pl.BlockSpec(wrt.shape, const2),
                  pl.BlockSpec(rb.shape, const2)],
        out_specs=[pl.BlockSpec((tm, d), lambda i: (i, 0)),
                   pl.BlockSpec((tm, d // 2), lambda i: (i, 0)),
                   pl.BlockSpec((TOP_K, tm), lambda i: (0, i)),
                   pl.BlockSpec((TOP_K, tm), lambda i: (0, i))],
        compiler_params=_cparams("parallel"),
        name="mix_ln_router",
    )(xp, xs, attn_p, attn_s, sgu_p, sgu_s, wmix, g1, b1, wrt, rb)


def _rank_kernel(idx_ref, rank_ref, counts_ref, run_ref):
    @pl.when(pl.program_id(0) == 0)
    def _():
        run_ref[...] = jnp.zeros_like(run_ref)

    tm = idx_ref.shape[1]
    row_e = lax.broadcasted_iota(jnp.int32, (N_EXPERTS, tm), 0)
    idx = idx_ref[...]
    onehot = jnp.zeros((N_EXPERTS, tm), F32)
    for k in range(TOP_K):
        onehot = onehot + jnp.where(row_e == idx[k:k + 1], 1.0, 0.0)
    earlier = (lax.broadcasted_iota(jnp.int32, (tm, tm), 0)
               < lax.broadcasted_iota(jnp.int32, (tm, tm), 1))
    before = run_ref[...] + _dot(onehot.astype(BF16), jnp.where(earlier, 1.0, 0.0).astype(BF16))
    row_k = lax.broadcasted_iota(jnp.int32, (TOP_K, tm), 0)
    ranks = jnp.zeros((TOP_K, tm), F32)
    for k in range(TOP_K):
        rank_k = jnp.sum(jnp.where(row_e == idx[k:k + 1], before, 0.0), axis=0, keepdims=True)
        ranks = jnp.where(row_k == k, rank_k, ranks)
    rank_ref[...] = ranks.astype(jnp.int32)
    run_ref[...] = run_ref[...] + jnp.sum(onehot, axis=1, keepdims=True)
    counts_ref[...] = run_ref[...]


def _rank(idx, tm):
    t = idx.shape[1]
    return pl.pallas_call(
        _rank_kernel,
        out_shape=[jax.ShapeDtypeStruct((TOP_K, t), jnp.int32),
                   jax.ShapeDtypeStruct((N_EXPERTS, 1), F32)],
        grid=(t // tm,),
        in_specs=[pl.BlockSpec((TOP_K, tm), lambda i: (0, i))],
        out_specs=[pl.BlockSpec((TOP_K, tm), lambda i: (0, i)),
                   pl.BlockSpec((N_EXPERTS, 1), lambda i: (0, 0))],
        scratch_shapes=[pltpu.VMEM((N_EXPERTS, 1), F32)],
        compiler_params=_cparams("arbitrary"),
        name="expert_rank",
    )(idx)


def _dest_kernel(idx_ref, rank_ref, pstart_ref, dest_ref):
    tm = idx_ref.shape[1]
    row_e = lax.broadcasted_iota(jnp.int32, (N_EXPERTS, tm), 0)
    row_k = lax.broadcasted_iota(jnp.int32, (TOP_K, tm), 0)
    idx = idx_ref[...]
    start = jnp.zeros((TOP_K, tm), F32)
    for k in range(TOP_K):
        start_k = jnp.sum(jnp.where(row_e == idx[k:k + 1], pstart_ref[...], 0.0), axis=0, keepdims=True)
        start = jnp.where(row_k == k, start_k, start)
    dest = start.astype(jnp.int32) + rank_ref[...]
    for c in range(tm // SC_CHUNK):
        dest_ref[c] = dest[:, c * SC_CHUNK:(c + 1) * SC_CHUNK]


def _dest(idx, rank, pstart_col, tm):
    t = idx.shape[1]
    tok_spec = pl.BlockSpec((TOP_K, tm), lambda i: (0, i))
    return pl.pallas_call(
        _dest_kernel,
        out_shape=jax.ShapeDtypeStruct((t // SC_CHUNK, TOP_K, SC_CHUNK), jnp.int32),
        grid=(t // tm,),
        in_specs=[tok_spec, tok_spec, pl.BlockSpec(pstart_col.shape, lambda i: (0, 0))],
        out_specs=pl.BlockSpec((tm // SC_CHUNK, TOP_K, SC_CHUNK), lambda i: (i, 0, 0)),
        compiler_params=_cparams("parallel"),
        name="moe_dest",
    )(idx, rank, pstart_col)


def _sc_mesh():
    return plsc.VectorSubcoreMesh(core_axis_name="c", subcore_axis_name="s",
                                  num_cores=SC_CORES, num_subcores=SC_SUBCORES)


def _sc_chunks(n_chunks):
    workers = SC_CORES * SC_SUBCORES
    wid = lax.axis_index("s") * SC_CORES + lax.axis_index("c")
    return wid, workers, (n_chunks - wid + workers - 1) // workers


def _sc_dispatch(x_rows, dest, n_rows):
    n_chunks = dest.shape[0]
    width = x_rows.shape[1]

    @functools.partial(
        pl.kernel, mesh=_sc_mesh(),
        out_type=jax.ShapeDtypeStruct((n_rows, width), x_rows.dtype),
        scratch_types=[pltpu.VMEM((TOP_K, SC_CHUNK), jnp.int32), pltpu.VMEM((SC_CHUNK, width), x_rows.dtype),
                       pltpu.SemaphoreType.DMA],
    )
    def scatter(x_hbm, dest_hbm, out_hbm, dest_v, rows_v, sem):
        wid, workers, n_own = _sc_chunks(n_chunks)

        @pl.loop(0, n_own)
        def _(j):
            ch = wid + j * workers
            pltpu.sync_copy(dest_hbm.at[ch], dest_v)
            pltpu.sync_copy(x_hbm.at[pl.ds(ch * SC_CHUNK, SC_CHUNK)], rows_v)
            copies = [pltpu.async_copy(rows_v, out_hbm.at[dest_v.at[k]], sem) for k in range(TOP_K)]
            for cp in copies:
                cp.wait()

    return scatter(x_rows, dest)


def _expert_kernel(first_ref, nblk_ref, count_ref, nused_ref, xs_hbm, wg_hbm, wu_hbm, wd_hbm, ys_hbm,
                   xbuf, ybuf, xsem, ysem, wg_f, wu_f, wd_f, wsem):
    n_exp = wg_hbm.shape[0]
    n_used = nused_ref[0]
    slots, _, half = xbuf.shape
    weight_slots = wg_f.shape[0]

    def block_rows(j):
        return pl.ds(pl.multiple_of(j * EXPERT_ROWS, EXPERT_ROWS), EXPERT_ROWS)

    def x_copy(j, slot):
        return pltpu.make_async_copy(xs_hbm.at[block_rows(j)], xbuf.at[slot], xsem.at[slot])

    def y_copy(j, slot):
        return pltpu.make_async_copy(ybuf.at[slot], ys_hbm.at[block_rows(j)], ysem.at[slot])

    def weight_copies(e, slot):
        return [pltpu.make_async_copy(src.at[e], dst.at[slot], wsem.at[i, slot])
                for i, (src, dst) in enumerate(((wg_hbm, wg_f), (wu_hbm, wu_f), (wd_hbm, wd_f)))]

    for k in range(slots - 1):
        @pl.when(k < n_used)
        def _():
            x_copy(k, k).start()
    for k in range(min(weight_slots - 1, n_exp)):
        for cp in weight_copies(k, k):
            cp.start()

    def expert(e, carry):
        wslot = e % weight_slots
        for cp in weight_copies(e, wslot):
            cp.wait()
        ahead_e = e + weight_slots - 1

        @pl.when(ahead_e < n_exp)
        def _():
            for cp in weight_copies(ahead_e, ahead_e % weight_slots):
                cp.start()

        def block(b, carry):
            j = first_ref[e] + b
            slot = j % slots
            x_copy(j, slot).wait()
            ahead = j + slots - 1

            @pl.when(ahead < n_used)
            def _():
                x_copy(ahead, ahead % slots).start()

            @pl.when(j >= slots)
            def _():
                y_copy(j - slots, slot).wait()

            row = lax.broadcasted_iota(jnp.int32, (EXPERT_ROWS, half), 0)
            packed = jnp.where(row < count_ref[e] - b * EXPERT_ROWS, xbuf[slot], jnp.uint32(0))
            lo, hi = _unpack_bf16_pairs(packed)
            wg, wu = wg_f[wslot].astype(BF16), wu_f[wslot].astype(BF16)
            g = _dot(lo, wg[:half]) + _dot(hi, wg[half:])
            u = _dot(lo, wu[:half]) + _dot(hi, wu[half:])
            h = (g * _sigmoid(g) * u).astype(BF16)
            ybuf[slot] = _pack_bf16_pairs(_dot(h, wd_f[wslot].astype(BF16)).astype(BF16))
            y_copy(j, slot).start()
            return carry

        lax.fori_loop(0, nblk_ref[e], block, 0)
        return carry

    lax.fori_loop(0, n_exp, expert, 0)

    for k in range(1, slots + 1):
        @pl.when(n_used >= k)
        def _():
            y_copy(n_used - k, (n_used - k) % slots).wait()


def _experts(first_block, n_block, counts, n_used, xs, wg, wu, wd):
    n_rows, half = xs.shape
    n_exp, d, de = wg.shape
    any_spec = pl.BlockSpec(memory_space=pl.ANY)
    return pl.pallas_call(
        _expert_kernel,
        out_shape=jax.ShapeDtypeStruct((n_rows, half), jnp.uint32),
        grid_spec=pltpu.PrefetchScalarGridSpec(
            num_scalar_prefetch=4,
            grid=(1,),
            in_specs=[any_spec, any_spec, any_spec, any_spec],
            out_specs=any_spec,
            scratch_shapes=[pltpu.VMEM((EXPERT_SLOTS, EXPERT_ROWS, half), jnp.uint32),
                            pltpu.VMEM((EXPERT_SLOTS, EXPERT_ROWS, half), jnp.uint32),
                            pltpu.SemaphoreType.DMA((EXPERT_SLOTS,)), pltpu.SemaphoreType.DMA((EXPERT_SLOTS,)),
                            pltpu.VMEM((WEIGHT_SLOTS, d, de), F32), pltpu.VMEM((WEIGHT_SLOTS, d, de), F32),
                            pltpu.VMEM((WEIGHT_SLOTS, de, d), F32), pltpu.SemaphoreType.DMA((3, WEIGHT_SLOTS))]),
        compiler_params=_cparams("arbitrary"),
        name="routed_experts",
    )(first_block, n_block, counts, n_used, xs, wg, wu, wd)


def _dense_kernel(alpha, tiles_p, x1_ref, pp_ref, ps_ref, wgs_ref, wus_ref, wds_ref, wpg_ref, wpp_ref, base_ref):
    x1 = x1_ref[...]
    xb = x1.astype(BF16)
    g = _dot(xb, wgs_ref[...])
    h = (g * _sigmoid(g) * _dot(xb, wus_ref[...])).astype(BF16)
    shared = _dot(h, wds_ref[...])
    ple = _sigmoid(_dot(xb, wpg_ref[...])) * _dot(_pool_rows(tiles_p, pp_ref, ps_ref, BF16), wpp_ref[...])
    base_ref[...] = alpha * x1 + shared + ple


def _dense(x1, pp, ps, wgs, wus, wds, wpg, wpp, alpha, tm):
    t, d = x1.shape
    const2 = lambda i: (0, 0)
    return pl.pallas_call(
        functools.partial(_dense_kernel, alpha, pp.shape[0] // tm),
        out_shape=jax.ShapeDtypeStruct((t, d), F32),
        grid=(t // tm,),
        in_specs=[pl.BlockSpec((tm, d), lambda i: (i, 0)), *_pool_specs(pp.shape[0], tm, pp.shape[1]),
                  pl.BlockSpec(wgs.shape, const2), pl.BlockSpec(wus.shape, const2),
                  pl.BlockSpec(wds.shape, const2), pl.BlockSpec(wpg.shape, const2),
                  pl.BlockSpec(wpp.shape, const2)],
        out_specs=pl.BlockSpec((tm, d), lambda i: (i, 0)),
        compiler_params=_cparams("parallel"),
        name="shared_ple",
    )(x1, pp, ps, wgs, wus, wds, wpg, wpp)


def _sc_gather(y_rows, dest, n_tok):
    n_chunks = dest.shape[0]
    width = y_rows.shape[1]
    items = [(k, h) for k in range(TOP_K) for h in range(SC_CHUNK // SC_GATHER_ROWS)]

    @functools.partial(
        pl.kernel, mesh=_sc_mesh(),
        out_type=jax.ShapeDtypeStruct((TOP_K, n_tok, width), y_rows.dtype),
        scratch_types=[pltpu.VMEM((TOP_K, SC_CHUNK), jnp.int32),
                       pltpu.VMEM((2, SC_GATHER_ROWS, width), y_rows.dtype),
                       pltpu.SemaphoreType.DMA, pltpu.SemaphoreType.DMA],
    )
    def gather(y_hbm, dest_hbm, out_hbm, dest_v, rows_v, gsem, wsem):
        wid, workers, n_own = _sc_chunks(n_chunks)

        @pl.loop(0, n_own)
        def _(j):
            ch = wid + j * workers
            pltpu.sync_copy(dest_hbm.at[ch], dest_v)

            def fetch(n):
                k, h = items[n]
                return pltpu.async_copy(y_hbm.at[dest_v.at[k, pl.ds(h * SC_GATHER_ROWS, SC_GATHER_ROWS)]],
                                        rows_v.at[n % 2], gsem)

            def write(n):
                k, h = items[n]
                rows = pl.ds(ch * SC_CHUNK + h * SC_GATHER_ROWS, SC_GATHER_ROWS)
                return pltpu.async_copy(rows_v.at[n % 2], out_hbm.at[k, rows], wsem)

            pending_fetch = fetch(0)
            pending_write = None
            for n in range(len(items)):
                pending_fetch.wait()
                if pending_write is not None:
                    pending_write.wait()
                if n + 1 < len(items):
                    pending_fetch = fetch(n + 1)
                pending_write = write(n)
            pending_write.wait()

    return gather(y_rows, dest)


def _combine_kernel(tiles_p, gate_ref, base_ref, yk_ref, g2_ref, b2_ref, outp_ref, outs_ref):
    gates = gate_ref[...]
    half = yk_ref.shape[2]
    acc_lo = base_ref[:, :half]
    acc_hi = base_ref[:, half:]
    for k in range(TOP_K):
        packed = yk_ref[k]
        gate = gates[:, k:k + 1]
        acc_lo = acc_lo + gate * lax.bitcast_convert_type(packed << 16, F32)
        acc_hi = acc_hi + gate * lax.bitcast_convert_type(packed & jnp.uint32(0xFFFF0000), F32)
    out = _layer_norm(jnp.concatenate([acc_lo, acc_hi], axis=1), g2_ref[...], b2_ref[...])
    is_prompt = pl.program_id(0) < tiles_p

    @pl.when(is_prompt)
    def _():
        outp_ref[...] = out

    @pl.when(jnp.logical_not(is_prompt))
    def _():
        outs_ref[...] = out


def _combine(gates_tok, base, yk, g2, b2, n_prompt, tm):
    t, d = base.shape
    const2 = lambda i: (0, 0)
    return pl.pallas_call(
        functools.partial(_combine_kernel, n_prompt // tm),
        out_shape=[jax.ShapeDtypeStruct((n_prompt, d), F32), jax.ShapeDtypeStruct((t - n_prompt, d), F32)],
        grid=(t // tm,),
        in_specs=[pl.BlockSpec((tm, TOP_K), lambda i: (i, 0)),
                  pl.BlockSpec((tm, d), lambda i: (i, 0)),
                  pl.BlockSpec((TOP_K, tm, yk.shape[2]), lambda i: (0, i, 0)),
                  pl.BlockSpec(g2.shape, const2), pl.BlockSpec(b2.shape, const2)],
        out_specs=list(_pool_specs(n_prompt, tm, d)),
        compiler_params=_cparams("arbitrary"),
        name="moe_combine",
    )(gates_tok, base, yk, g2, b2)


def _sgu_tables(sgu_w, sgu_b, rows_per_seq):
    reps = CHUNK // rows_per_seq
    tril = jnp.tril(sgu_w[:, :rows_per_seq, :rows_per_seq])
    eye = jnp.eye(reps, dtype=F32)
    mix = jnp.einsum("ab,gts->gatbs", eye, tril).reshape(N_GROUPS_SGU, CHUNK, CHUNK)
    bias = jnp.tile(jnp.repeat(sgu_b[:, :rows_per_seq].T, D_SGU // N_GROUPS_SGU, axis=1), (reps, 1))
    return mix.astype(BF16), bias


def _layer(xp, xs, ck, cv, pp, ps, w, rel_bias, alpha):
    batch, seq, d = xp.shape
    dec_b, dec_t, _ = xs.shape
    w_buf = ck.shape[1]
    n_p, n_s = batch * seq, dec_b * dec_t
    n_tok = n_p + n_s

    scale = jnp.concatenate([jnp.full((D_ATTN,), HEAD_DIM ** -0.5, F32),
                             jnp.ones((w["w_in"].shape[1] - D_ATTN,), F32)])
    w_in = (w["w_in"] * scale).astype(BF16)

    qp, kp, vp, up, gp, kp_t, vp_t = _proj(xp.reshape(n_p, d), w_in, 512, seq)
    qs, ks, vs, us, gs = _proj(xs.reshape(n_s, d), w_in, 512)

    b1, b4, b16 = _prompt_bias_tables(rel_bias)
    attn_p = _attn_prompt(qp, kp, vp, b1, b4, b16, batch, seq)

    attn_s = _sample_attention(rel_bias, qs.reshape(dec_b, dec_t, D_ATTN), ks.reshape(dec_b, dec_t, D_ATTN),
                               vs.reshape(dec_b, dec_t, D_ATTN), ck, cv).reshape(n_s, D_ATTN)

    ln_g, ln_b = w["sgu_ln_g"][None], w["sgu_ln_b"][None]
    mix_p, bias_p = _sgu_tables(w["sgu_w"], w["sgu_b"], CHUNK)
    mix_s, bias_s = _sgu_tables(w["sgu_w"], w["sgu_b"], dec_t)
    (sgu_p,) = _sgu(up, gp, mix_p, bias_p, ln_g, ln_b, False, 8)
    sgu_s, z2_s = _sgu(us, gs, mix_s, bias_s, ln_g, ln_b, True, 4)

    x1, x1_packed, idx, gates = _mid(xp.reshape(n_p, d), xs.reshape(n_s, d), attn_p, attn_s, sgu_p, sgu_s,
                                     w["w_mix_out"].astype(BF16), w["ln1_g"][None], w["ln1_b"][None],
                                     w["w_router"].T.astype(BF16), w["router_bias"][:, None], alpha, 512)

    rank, counts = _rank(idx, 512)
    counts = counts[:, 0].astype(jnp.int32)
    padded = (counts + EXPERT_ROWS - 1) // EXPERT_ROWS * EXPERT_ROWS
    pend = jnp.cumsum(padded)
    pstart = (pend - padded).astype(jnp.int32)
    n_blocks = (n_tok * TOP_K + N_EXPERTS * (EXPERT_ROWS - 1)) // EXPERT_ROWS
    n_used = (pend[-1:] // EXPERT_ROWS).astype(jnp.int32)

    dest = _dest(idx, rank, pstart.astype(F32)[:, None], 512)
    x_sorted = _sc_dispatch(x1_packed, dest, n_blocks * EXPERT_ROWS)
    y_sorted = _experts(pstart // EXPERT_ROWS, padded // EXPERT_ROWS, counts, n_used, x_sorted,
                        w["w_gate_e"], w["w_up_e"], w["w_down_e"])

    base = _dense(x1, pp.reshape(n_p, -1), ps.reshape(n_s, -1), w["w_gate_s"].astype(BF16), w["w_up_s"].astype(BF16),
                  w["w_down_s"].astype(BF16), w["w_ple_gate"].astype(BF16),
                  w["w_ple_proj"].astype(BF16), alpha, 1024)

    y_slots = _sc_gather(y_sorted, dest, n_tok)
    y_p, y_s = _combine(gates.T, base, y_slots, w["ln2_g"][None], w["ln2_b"][None], n_p, 512)
    y_p = y_p.reshape(batch, seq, d)
    y_s = y_s.reshape(dec_b, dec_t, d)
    keep = min(MAX_DISTANCE, seq)
    k_rows = kp_t.reshape(batch, N_HEADS, HEAD_DIM, seq)[..., seq - keep:].transpose(0, 3, 1, 2)
    v_rows = vp_t.reshape(batch, N_HEADS, HEAD_DIM, seq)[..., seq - keep:].transpose(0, 3, 1, 2)
    return (y_p, y_s, k_rows, v_rows,
            ks.reshape(dec_b, dec_t, N_HEADS, HEAD_DIM), vs.reshape(dec_b, dec_t, N_HEADS, HEAD_DIM),
            z2_s.reshape(dec_b, dec_t, D_SGU))


def kernel(x_prompt, x_sample, cache_k, cache_v, p_prompt, p_sample, w_in, rel_bias, sgu_w, sgu_b, sgu_ln_g, sgu_ln_b, w_mix_out, ln1_g, ln1_b, w_router, router_bias, w_gate_e, w_up_e, w_down_e, w_gate_s, w_up_s, w_down_s, w_ple_gate, w_ple_proj, ln2_g, ln2_b):
    depth = w_in.shape[0]
    alpha = (2 * depth) ** 0.25
    xp, xs = x_prompt, x_sample
    outs = [[] for _ in range(5)]
    for i in range(depth):
        w = {"w_in": w_in[i], "sgu_w": sgu_w[i], "sgu_b": sgu_b[i], "sgu_ln_g": sgu_ln_g[i],
             "sgu_ln_b": sgu_ln_b[i], "w_mix_out": w_mix_out[i], "ln1_g": ln1_g[i], "ln1_b": ln1_b[i],
             "w_router": w_router[i], "router_bias": router_bias[i], "w_gate_e": w_gate_e[i],
             "w_up_e": w_up_e[i], "w_down_e": w_down_e[i], "w_gate_s": w_gate_s[i], "w_up_s": w_up_s[i],
             "w_down_s": w_down_s[i], "w_ple_gate": w_ple_gate[i], "w_ple_proj": w_ple_proj[i],
             "ln2_g": ln2_g[i], "ln2_b": ln2_b[i]}
        xp, xs, kp, vp, ks, vs, zs = _layer(xp, xs, cache_k[i], cache_v[i], p_prompt[i], p_sample[i],
                                            w, rel_bias, alpha)
        for lst, val in zip(outs, (kp, vp, ks, vs, zs)):
            lst.append(val)
    return (xp, xs) + tuple(jnp.stack(lst) for lst in outs)
```

```python
import functools
import math

import numpy as np
import jax
import jax.numpy as jnp
from jax import lax
from jax.experimental import pallas as pl
from jax.experimental.pallas import tpu as pltpu
from jax.experimental.pallas import tpu_sc as plsc

F32 = jnp.float32
BF16 = jnp.bfloat16
NEG_INF = float("-inf")

N_HEADS = 8
HEAD_DIM = 64
D_ATTN = N_HEADS * HEAD_DIM
PATTERNS = ((128, 1), (512, 4), (2048, 16))
BAND = 128
N_BUCKETS = 32
MAX_DISTANCE = 2048
N_GROUPS_SGU = 8
D_SGU = 512
CHUNK = 128
N_EXPERTS = 256
TOP_K = 8
N_EXPERT_GROUPS = 8
TOPK_GROUPS = 4
ROUTED_SCALE = 2.5
LN_EPS = 1e-5
CLASSES_PER_TRIP = 2
EXPERT_ROWS = 256
EXPERT_SLOTS = 6
WEIGHT_SLOTS = 3

LANES = 128
SC_CORES = 2
SC_SUBCORES = 16
SC_CHUNK = 128
SC_GATHER_ROWS = 64
SC_GATHER_BUFFERS = 3
VMEM_LIMIT = 56 * 1024 * 1024


def _cparams(*sem):
    return pltpu.CompilerParams(dimension_semantics=sem, vmem_limit_bytes=VMEM_LIMIT)


def _layer_norm(x, g, b):
    mu = jnp.mean(x, axis=-1, keepdims=True)
    xc = x - mu
    var = jnp.mean(xc * xc, axis=-1, keepdims=True)
    return xc * lax.rsqrt(var + LN_EPS) * g + b


def _sigmoid(x):
    return 1.0 / (1.0 + jnp.exp(-x))


def _gelu(x):
    return 0.5 * x * (1.0 + lax.erf(x * math.sqrt(0.5)))


def _pack_bf16_pairs(xb):
    n = xb.shape[1] // 2
    bits = lax.bitcast_convert_type(xb.astype(F32), jnp.uint32)
    return (bits[:, :n] >> 16) | (bits[:, n:] & jnp.uint32(0xFFFF0000))


def _unpack_bf16_pairs(p):
    lo = lax.bitcast_convert_type(p << 16, F32).astype(BF16)
    hi = lax.bitcast_convert_type(p & jnp.uint32(0xFFFF0000), F32).astype(BF16)
    return lo, hi


def _dot(a, b):
    return jnp.dot(a, b, preferred_element_type=F32)


def _dot_nt(a, b):
    return lax.dot_general(a, b, (((1,), (1,)), ((), ())), preferred_element_type=F32)


def _t5_bucket_np(dist):
    max_exact = N_BUCKETS // 2
    df = np.maximum(dist, max_exact).astype(np.float32)
    large = max_exact + (np.log(df / np.float32(max_exact)) / np.float32(math.log(MAX_DISTANCE / max_exact))
                         * np.float32(N_BUCKETS - max_exact)).astype(np.int32)
    return np.where(dist < max_exact, dist, np.minimum(large, N_BUCKETS - 1)).astype(np.int32)


def _band_bucket_table(dilation):
    qi = np.arange(BAND)[:, None]
    ki = np.arange(2 * BAND)[None, :]
    dsub = qi + BAND - ki
    valid = (dsub >= 0) & (dsub <= BAND)
    return np.where(valid, _t5_bucket_np(np.clip(dsub, 0, BAND) * dilation), -1).astype(np.int32)


def _sample_bucket_tables(w_buf, t_len):
    t = np.arange(t_len)[:, None]

    def table(rows, window, dilation):
        d = w_buf + t - rows[None, :]
        ok = (d >= 0) & (d % dilation == 0) & (d <= window)
        return np.where(ok, _t5_bucket_np(np.maximum(d, 0)), -1).astype(np.int32)

    assert w_buf >= PATTERNS[-1][0]
    tables = [table(np.arange(w_buf - window, w_buf), window, dilation) for window, dilation in PATTERNS]
    new_rows = w_buf + np.arange(LANES)
    new = np.stack([table(new_rows, window, dilation) for window, dilation in PATTERNS])
    new[:, :, t_len:] = -1
    return tables, new


def _bias_kernel(rb_ref, bucket_ref, out_ref):
    bucket = bucket_ref[...]
    for h in range(N_HEADS):
        acc = jnp.full(bucket.shape, NEG_INF, F32)
        for b in range(N_BUCKETS):
            acc = jnp.where(bucket == b, rb_ref[b, h], acc)
        out_ref[h] = acc


def _bias_table(rel_bias, bucket_np):
    r, c = bucket_np.shape
    return pl.pallas_call(
        _bias_kernel,
        out_shape=jax.ShapeDtypeStruct((N_HEADS, r, c), F32),
        in_specs=[pl.BlockSpec(memory_space=pltpu.SMEM), pl.BlockSpec(memory_space=pltpu.VMEM)],
        out_specs=pl.BlockSpec(memory_space=pltpu.VMEM),
        name="bias_table",
    )(rel_bias, jnp.asarray(bucket_np))


def _proj_kernel(x_ref, w_ref, q_ref, k_ref, v_ref, u_ref, g_ref, *kv_t_refs):
    x = x_ref[...].astype(BF16)
    col = 0
    rows = []
    for o in (q_ref, k_ref, v_ref, u_ref, g_ref):
        n = o.shape[1]
        rows.append(_dot(x, w_ref[:, col:col + n]))
        o[...] = rows[-1]
        col += n
    for o, val in zip(kv_t_refs, rows[1:3]):
        o[0] = val.T


def _proj(x, w, tm, seq=None):
    m, d = x.shape
    n_out = (D_ATTN, D_ATTN, D_ATTN, D_SGU, D_SGU)
    out_shape = [jax.ShapeDtypeStruct((m, n), F32) for n in n_out]
    out_specs = [pl.BlockSpec((tm, n), lambda i: (i, 0)) for n in n_out]
    if seq is not None:
        tiles = seq // tm
        out_shape += [jax.ShapeDtypeStruct((m // seq, D_ATTN, seq), F32)] * 2
        out_specs += [pl.BlockSpec((1, D_ATTN, tm), lambda i: (i // tiles, 0, i % tiles))] * 2
    return pl.pallas_call(
        _proj_kernel,
        out_shape=out_shape,
        grid=(m // tm,),
        in_specs=[pl.BlockSpec((tm, d), lambda i: (i, 0)), pl.BlockSpec(w.shape, lambda i: (0, 0))],
        out_specs=out_specs,
        compiler_params=_cparams("parallel"),
        name="in_proj",
    )(x, w)


def _band_attn(q, k, v, bias, even):
    q2 = jnp.concatenate([jnp.where(even, q, 0.0), jnp.where(even, 0.0, q)], axis=0).astype(BF16)
    s = _dot_nt(q2, k.astype(BF16)) + bias
    m = jnp.max(s, axis=-1, keepdims=True)
    p = jnp.exp(s - m)
    l = jnp.sum(p, axis=-1, keepdims=True)
    pv = _dot(p.astype(BF16), v.astype(BF16))
    return (jnp.where(even, m[:BAND], m[BAND:]), jnp.where(even, l[:BAND], l[BAND:]),
            jnp.where(even, pv[:BAND], pv[BAND:]))


def _attn_prompt_kernel(q_ref, k_ref, v_ref, b1_ref, b4_ref, b16_ref, o_ref,
                        m1, l1, a1, m4, l4, a4, m16, l16, a16, qc, kc, vc):
    seq = q_ref.shape[0]
    n4 = seq // 4
    per_trip = seq // BAND // 4
    even = lax.broadcasted_iota(jnp.int32, (BAND, LANES), 1) < HEAD_DIM

    def store_all(refs, rows_list, results):
        for rows, (m, l, a) in zip(rows_list, results):
            refs[0][rows, :], refs[1][rows, :], refs[2][rows, :] = m, l, a

    def body(g, carry):
        for j in range(CLASSES_PER_TRIP):
            one_class(g * CLASSES_PER_TRIP + j)
        return carry

    def one_class(r):
        rows1, loaded1 = [], []
        for j in range(per_trip):
            i = r * per_trip + j
            r0 = pl.multiple_of(i * BAND, BAND)
            k0 = pl.multiple_of(jnp.maximum(i - 1, 0) * BAND, BAND)
            rows1.append(pl.ds(r0, BAND))
            loaded1.append((q_ref[pl.ds(r0, BAND)], k_ref[pl.ds(k0, 2 * BAND)], v_ref[pl.ds(k0, 2 * BAND)],
                            b1_ref[jnp.where(i == 0, 1, 0)]))
        base = pl.multiple_of(r * n4, BAND)
        q4 = q_ref[pl.ds(r, n4, stride=4), :]
        k4 = k_ref[pl.ds(r, n4, stride=4), :]
        v4 = v_ref[pl.ds(r, n4, stride=4), :]
        qc[pl.ds(base, n4)], kc[pl.ds(base, n4)], vc[pl.ds(base, n4)] = q4, k4, v4
        rows16 = [pl.ds(r * n4 + s, seq // 16, stride=4) for s in range(4)]
        loaded16 = [(qc[rows, :], kc[rows, :], vc[rows, :]) for rows in rows16]

        results1 = [_band_attn(q, k, v, bias, even) for q, k, v, bias in loaded1]
        results4, rows4 = [], []
        for i in range(n4 // BAND):
            lo = max(i - 1, 0) * BAND
            hi = (i + 1) * BAND
            col = 0 if i > 0 else BAND
            results4.append(_band_attn(q4[i * BAND:hi], k4[lo:hi], v4[lo:hi], b4_ref[:, col:], even))
            rows4.append(pl.ds(base + i * BAND, BAND))
        results16 = [_band_attn(q, k, v, b16_ref[:, BAND:], even) for q, k, v in loaded16]

        store_all((m1, l1, a1), rows1, results1)
        store_all((m4, l4, a4), rows4, results4)
        store_all((m16, l16, a16), rows16, results16)

    lax.fori_loop(0, 4 // CLASSES_PER_TRIP, body, 0)

    def merge_body(i, carry):
        r = i // (n4 // BAND)
        c = i % (n4 // BAND)
        rows = pl.ds(pl.multiple_of(i * BAND, BAND), BAND)
        nat = pl.ds(r + 4 * BAND * c, BAND, stride=4)
        ma, mb, mc = m1[nat, :], m4[rows], m16[rows]
        mx = jnp.maximum(jnp.maximum(ma, mb), mc)
        wa, wb, wc = jnp.exp(ma - mx), jnp.exp(mb - mx), jnp.exp(mc - mx)
        num = wa * a1[nat, :] + wb * a4[rows] + wc * a16[rows]
        den = wa * l1[nat, :] + wb * l4[rows] + wc * l16[rows]
        o_ref[nat, :] = num / den
        return carry

    lax.fori_loop(0, seq // BAND, merge_body, 0)


def _prompt_bias_tables(rel_bias):
    pairs = N_HEADS // 2
    first = _band_bucket_table(1)
    first = np.concatenate([first[:, BAND:], np.full((BAND, BAND), -1, np.int32)], axis=1)
    buckets = np.concatenate([_band_bucket_table(1), first, _band_bucket_table(4), _band_bucket_table(16)], axis=0)
    tables = _bias_table(rel_bias, buckets).reshape(N_HEADS, 4, BAND, 2 * BAND)
    b1, b1_first, b4, b16 = (tables[:, i].reshape(pairs, 2 * BAND, 2 * BAND) for i in range(4))
    return jnp.stack([b1, b1_first], axis=1), b4, b16


def _attn_prompt(q, k, v, b1, b4, b16, batch, seq):
    blk = pl.BlockSpec((seq, LANES), lambda b, j: (b, j))
    bias_spec = pl.BlockSpec((None, 2 * BAND, 2 * BAND), lambda b, j: (j, 0, 0))
    return pl.pallas_call(
        _attn_prompt_kernel,
        out_shape=jax.ShapeDtypeStruct(q.shape, F32),
        grid=(batch, D_ATTN // LANES),
        in_specs=[blk, blk, blk, pl.BlockSpec((None, 2, 2 * BAND, 2 * BAND), lambda b, j: (j, 0, 0, 0)),
                  bias_spec, bias_spec],
        out_specs=blk,
        scratch_shapes=[pltpu.VMEM((seq, LANES), F32) for _ in range(12)],
        compiler_params=_cparams("parallel", "parallel"),
        name="attn_prompt",
    )(q, k, v, b1, b4, b16)


def _attn_sample_kernel(q_ref, kn_ref, vn_ref, kt_ref, vt_ref, b1_ref, b4_ref, b16_ref, bn_ref, o_ref):
    t_len = q_ref.shape[1]
    rows = N_HEADS * t_len
    q = q_ref[0]
    head_of_row = lax.broadcasted_iota(jnp.int32, (rows, D_ATTN), 0) // t_len
    head_of_lane = lax.broadcasted_iota(jnp.int32, (rows, D_ATTN), 1) // HEAD_DIM
    own = head_of_row == head_of_lane
    qrows = jnp.where(own, jnp.concatenate([q] * N_HEADS, axis=0), 0.0).astype(BF16)
    pad = jnp.zeros((LANES - t_len, D_ATTN), F32)
    kn = jnp.concatenate([kn_ref[0], pad], axis=0).astype(BF16)
    vn = jnp.concatenate([vn_ref[0], pad], axis=0).astype(BF16)
    kt = kt_ref[0].reshape(D_ATTN, -1).astype(BF16)
    vt = vt_ref[0].reshape(D_ATTN, -1).astype(BF16)
    s_cache = _dot(qrows, kt)
    s_new = _dot_nt(qrows, kn)
    ms, ls, accs = [], [], []
    for p, bias_ref in enumerate((b1_ref, b4_ref, b16_ref)):
        w = bias_ref.shape[1]
        sc = s_cache[:, -w:] + bias_ref[...]
        sn = s_new + bn_ref[p]
        m = jnp.maximum(jnp.max(sc, axis=-1, keepdims=True), jnp.max(sn, axis=-1, keepdims=True))
        pc = jnp.exp(sc - m)
        pn = jnp.exp(sn - m)
        ls.append(jnp.sum(pc, axis=-1, keepdims=True) + jnp.sum(pn, axis=-1, keepdims=True))
        accs.append(_dot_nt(pc.astype(BF16), vt[:, -w:]) + _dot(pn.astype(BF16), vn))
        ms.append(m)
    mx = jnp.maximum(jnp.maximum(ms[0], ms[1]), ms[2])
    ws = [jnp.exp(m - mx) for m in ms]
    num = ws[0] * accs[0] + ws[1] * accs[1] + ws[2] * accs[2]
    den = ws[0] * ls[0] + ws[1] * ls[1] + ws[2] * ls[2]
    full = jnp.where(own, num / den, 0.0)
    out = full[0:t_len]
    for h in range(1, N_HEADS):
        out = out + full[h * t_len:(h + 1) * t_len]
    o_ref[0] = out


def _sample_attention(rel_bias, q, kn, vn, ck, cv):
    b, t_len, _ = q.shape
    w_buf = ck.shape[1]
    rows = N_HEADS * t_len
    tables, new_t = _sample_bucket_tables(w_buf, t_len)
    widest = max(tb.shape[1] for tb in tables)
    pieces = tables + [new_t.reshape(3 * t_len, LANES)]
    stacked = np.concatenate([np.pad(tb, ((0, 0), (0, widest - tb.shape[1])), constant_values=-1) for tb in pieces])
    bias = _bias_table(rel_bias, stacked)
    b1, b4, b16 = (bias[:, i * t_len:(i + 1) * t_len, :tb.shape[1]].reshape(rows, tb.shape[1])
                   for i, tb in enumerate(tables))
    bn = bias[:, 3 * t_len:, :LANES].reshape(N_HEADS, 3, t_len, LANES).transpose(1, 0, 2, 3).reshape(3, rows, LANES)
    new_spec = pl.BlockSpec((1, t_len, D_ATTN), lambda i: (i, 0, 0))
    cache_spec = pl.BlockSpec((1, N_HEADS, HEAD_DIM, w_buf), lambda i: (i, 0, 0, 0))
    const2 = lambda i: (0, 0)
    return pl.pallas_call(
        _attn_sample_kernel,
        out_shape=jax.ShapeDtypeStruct(q.shape, F32),
        grid=(b,),
        in_specs=[new_spec, new_spec, new_spec, cache_spec, cache_spec,
                  pl.BlockSpec(b1.shape, const2), pl.BlockSpec(b4.shape, const2), pl.BlockSpec(b16.shape, const2),
                  pl.BlockSpec(bn.shape, lambda i: (0, 0, 0))],
        out_specs=new_spec,
        compiler_params=_cparams("parallel"),
        name="attn_sample",
    )(q, kn, vn, ck.transpose(0, 2, 3, 1), cv.transpose(0, 2, 3, 1), b1, b4, b16, bn)


def _sgu_kernel(u_ref, g_ref, mix_ref, bias_ref, lng_ref, lnb_ref, sgu_ref, *z2_out):
    n_chunks = u_ref.shape[0] // CHUNK
    group_of_lane = lax.broadcasted_iota(jnp.int32, (CHUNK, D_SGU), 1) // (D_SGU // N_GROUPS_SGU)
    for c in range(n_chunks):
        rows = slice(c * CHUNK, (c + 1) * CHUNK)
        z1 = _gelu(u_ref[rows])
        z2 = _layer_norm(_gelu(g_ref[rows]), lng_ref[...], lnb_ref[...])
        if z2_out:
            z2_out[0][rows] = z2
        mixed = bias_ref[...]
        for g in range(N_GROUPS_SGU):
            mixed = mixed + _dot(mix_ref[g], jnp.where(group_of_lane == g, z2, 0.0).astype(BF16))
        sgu_ref[rows] = (z1 * mixed).astype(sgu_ref.dtype)


def _sgu(u, g, mix, bias, ln_g, ln_b, want_z2, chunks_per_step):
    m = u.shape[0]
    tm = CHUNK * chunks_per_step
    row_spec = pl.BlockSpec((tm, D_SGU), lambda i: (i, 0))
    const2 = lambda i: (0, 0)
    out_shape = [jax.ShapeDtypeStruct((m, D_SGU), BF16)]
    out_specs = [row_spec]
    if want_z2:
        out_shape.append(jax.ShapeDtypeStruct((m, D_SGU), F32))
        out_specs.append(row_spec)
    return pl.pallas_call(
        _sgu_kernel,
        out_shape=out_shape,
        grid=(m // tm,),
        in_specs=[row_spec, row_spec,
                  pl.BlockSpec(mix.shape, lambda i: (0, 0, 0)),
                  pl.BlockSpec(bias.shape, const2),
                  pl.BlockSpec(ln_g.shape, const2), pl.BlockSpec(ln_b.shape, const2)],
        out_specs=out_specs,
        compiler_params=_cparams("parallel"),
        name="sgu",
    )(u, g, mix, bias, ln_g, ln_b)


def _top1_rows(x, row):
    m = jnp.max(x, axis=0, keepdims=True)
    i = jnp.min(jnp.where(x == m, row, x.shape[0]), axis=0, keepdims=True)
    return m, i


def _pool_specs(n_prompt, tm, width):
    tiles_p = n_prompt // tm
    return (pl.BlockSpec((tm, width), lambda i, *_: (jnp.minimum(i, tiles_p - 1), 0)),
            pl.BlockSpec((tm, width), lambda i, *_: (jnp.maximum(i - tiles_p, 0), 0)))


def _pool_rows(tiles_p, prompt_ref, sample_ref, dtype):
    return jnp.where(pl.program_id(0) < tiles_p, prompt_ref[...].astype(dtype), sample_ref[...].astype(dtype))


def _mid_kernel(alpha, tiles_p, xp_ref, xs_ref, ap_ref, as_ref, sp_ref, ss_ref, wmix_ref, g1_ref, b1_ref,
                wrt_ref, rb_ref, x1_ref, x1p_ref, idx_ref, gate_ref):
    y = (_dot(_pool_rows(tiles_p, ap_ref, as_ref, BF16), wmix_ref[0:D_ATTN])
         + _dot(_pool_rows(tiles_p, sp_ref, ss_ref, BF16), wmix_ref[D_ATTN:])
         + alpha * _pool_rows(tiles_p, xp_ref, xs_ref, F32))
    x1 = _layer_norm(y, g1_ref[...], b1_ref[...])
    x1_ref[...] = x1
    tm = x1.shape[0]
    x1b = x1.astype(BF16)
    x1p_ref[...] = _pack_bf16_pairs(x1b)
    scores = _sigmoid(_dot_nt(wrt_ref[...], x1b))
    biased = scores + rb_ref[...]
    per_group = N_EXPERTS // N_EXPERT_GROUPS
    row_g = lax.broadcasted_iota(jnp.int32, (per_group, tm), 0)
    row_8 = lax.broadcasted_iota(jnp.int32, (N_EXPERT_GROUPS, tm), 0)
    row_e = lax.broadcasted_iota(jnp.int32, (N_EXPERTS, tm), 0)
    cur = jnp.full((N_EXPERT_GROUPS, tm), NEG_INF, F32)
    for g in range(N_EXPERT_GROUPS):
        blk = biased[g * per_group:(g + 1) * per_group]
        m_a, i_a = _top1_rows(blk, row_g)
        m_b = jnp.max(jnp.where(row_g == i_a, NEG_INF, blk), axis=0, keepdims=True)
        cur = jnp.where(row_8 == g, m_a + m_b, cur)
    chosen = jnp.zeros((N_EXPERT_GROUPS, tm), F32)
    for _ in range(TOPK_GROUPS):
        _, i_g = _top1_rows(cur, row_8)
        sel = row_8 == i_g
        chosen = jnp.where(sel, 1.0, chosen)
        cur = jnp.where(sel, NEG_INF, cur)
    cur = jnp.concatenate([jnp.where(chosen[g:g + 1] > 0.0, biased[g * per_group:(g + 1) * per_group], NEG_INF)
                           for g in range(N_EXPERT_GROUPS)], axis=0)
    idx = jnp.zeros((TOP_K, tm), jnp.int32)
    gates = jnp.zeros((TOP_K, tm), F32)
    for k in range(TOP_K):
        _, i_e = _top1_rows(cur, row_e)
        sel = row_e == i_e
        gate_k = jnp.sum(jnp.where(sel, scores, 0.0), axis=0, keepdims=True)
        cur = jnp.where(sel, NEG_INF, cur)
        idx = jnp.where(row_8 == k, i_e, idx)
        gates = jnp.where(row_8 == k, gate_k, gates)
    idx_ref[...] = idx
    gate_ref[...] = gates / jnp.sum(gates, axis=0, keepdims=True) * ROUTED_SCALE


def _mid(xp, xs, attn_p, attn_s, sgu_p, sgu_s, wmix, g1, b1, wrt, rb, alpha, tm):
    n_p, d = xp.shape
    t = n_p + xs.shape[0]
    const2 = lambda i: (0, 0)
    return pl.pallas_call(
        functools.partial(_mid_kernel, alpha, n_p // tm),
        out_shape=[jax.ShapeDtypeStruct((t, d), F32),
                   jax.ShapeDtypeStruct((t, d // 2), jnp.uint32),
                   jax.ShapeDtypeStruct((TOP_K, t), jnp.int32),
                   jax.ShapeDtypeStruct((TOP_K, t), F32)],
        grid=(t // tm,),
        in_specs=[*_pool_specs(n_p, tm, d), *_pool_specs(n_p, tm, D_ATTN), *_pool_specs(n_p, tm, D_SGU),
                  pl.BlockSpec(wmix.shape, const2), pl.BlockSpec(g1.shape, const2),
                  pl.BlockSpec(b1.shape, const2), pl.BlockSpec(wrt.shape, const2),
                  pl.BlockSpec(rb.shape, const2)],
        out_specs=[pl.BlockSpec((tm, d), lambda i: (i, 0)),
                   pl.BlockSpec((tm, d // 2), lambda i: (i, 0)),
                   pl.BlockSpec((TOP_K, tm), lambda i: (0, i)),
                   pl.BlockSpec((TOP_K, tm), lambda i: (0, i))],
        compiler_params=_cparams("parallel"),
        name="mix_ln_router",
    )(xp, xs, attn_p, attn_s, sgu_p, sgu_s, wmix, g1, b1, wrt, rb)


def _rank_kernel(idx_ref, rank_ref, counts_ref, run_ref):
    @pl.when(pl.program_id(0) == 0)
    def _():
        run_ref[...] = jnp.zeros_like(run_ref)

    tm = idx_ref.shape[1]
    row_e = lax.broadcasted_iota(jnp.int32, (N_EXPERTS, tm), 0)
    idx = idx_ref[...]
    onehot = jnp.zeros((N_EXPERTS, tm), F32)
    for k in range(TOP_K):
        onehot = onehot + jnp.where(row_e == idx[k:k + 1], 1.0, 0.0)
    earlier = (lax.broadcasted_iota(jnp.int32, (tm, tm), 0)
               < lax.broadcasted_iota(jnp.int32, (tm, tm), 1))
    before = run_ref[...] + _dot(onehot.astype(BF16), jnp.where(earlier, 1.0, 0.0).astype(BF16))
    row_k = lax.broadcasted_iota(jnp.int32, (TOP_K, tm), 0)
    ranks = jnp.zeros((TOP_K, tm), F32)
    for k in range(TOP_K):
        rank_k = jnp.sum(jnp.where(row_e == idx[k:k + 1], before, 0.0), axis=0, keepdims=True)
        ranks = jnp.where(row_k == k, rank_k, ranks)
    rank_ref[...] = ranks.astype(jnp.int32)
    run_ref[...] = run_ref[...] + jnp.sum(onehot, axis=1, keepdims=True)
    counts_ref[...] = run_ref[...]


def _rank(idx, tm):
    t = idx.shape[1]
    return pl.pallas_call(
        _rank_kernel,
        out_shape=[jax.ShapeDtypeStruct((TOP_K, t), jnp.int32),
                   jax.ShapeDtypeStruct((N_EXPERTS, 1), F32)],
        grid=(t // tm,),
        in_specs=[pl.BlockSpec((TOP_K, tm), lambda i: (0, i))],
        out_specs=[pl.BlockSpec((TOP_K, tm), lambda i: (0, i)),
                   pl.BlockSpec((N_EXPERTS, 1), lambda i: (0, 0))],
        scratch_shapes=[pltpu.VMEM((N_EXPERTS, 1), F32)],
        compiler_params=_cparams("arbitrary"),
        name="expert_rank",
    )(idx)


def _dest_kernel(idx_ref, rank_ref, pstart_ref, dest_ref):
    tm = idx_ref.shape[1]
    row_e = lax.broadcasted_iota(jnp.int32, (N_EXPERTS, tm), 0)
    row_k = lax.broadcasted_iota(jnp.int32, (TOP_K, tm), 0)
    idx = idx_ref[...]
    start = jnp.zeros((TOP_K, tm), F32)
    for k in range(TOP_K):
        start_k = jnp.sum(jnp.where(row_e == idx[k:k + 1], pstart_ref[...], 0.0), axis=0, keepdims=True)
        start = jnp.where(row_k == k, start_k, start)
    dest = start.astype(jnp.int32) + rank_ref[...]
    for c in range(tm // SC_CHUNK):
        dest_ref[c] = dest[:, c * SC_CHUNK:(c + 1) * SC_CHUNK]


def _dest(idx, rank, pstart_col, tm):
    t = idx.shape[1]
    tok_spec = pl.BlockSpec((TOP_K, tm), lambda i: (0, i))
    return pl.pallas_call(
        _dest_kernel,
        out_shape=jax.ShapeDtypeStruct((t // SC_CHUNK, TOP_K, SC_CHUNK), jnp.int32),
        grid=(t // tm,),
        in_specs=[tok_spec, tok_spec, pl.BlockSpec(pstart_col.shape, lambda i: (0, 0))],
        out_specs=pl.BlockSpec((tm // SC_CHUNK, TOP_K, SC_CHUNK), lambda i: (i, 0, 0)),
        compiler_params=_cparams("parallel"),
        name="moe_dest",
    )(idx, rank, pstart_col)


def _sc_mesh():
    return plsc.VectorSubcoreMesh(core_axis_name="c", subcore_axis_name="s",
                                  num_cores=SC_CORES, num_subcores=SC_SUBCORES)


def _sc_chunks(n_chunks):
    workers = SC_CORES * SC_SUBCORES
    wid = lax.axis_index("s") * SC_CORES + lax.axis_index("c")
    return wid, workers, (n_chunks - wid + workers - 1) // workers


def _sc_dispatch(x_rows, dest, n_rows):
    n_chunks = dest.shape[0]
    width = x_rows.shape[1]

    @functools.partial(
        pl.kernel, mesh=_sc_mesh(),
        out_type=jax.ShapeDtypeStruct((n_rows, width), x_rows.dtype),
        scratch_types=[pltpu.VMEM((TOP_K, SC_CHUNK), jnp.int32), pltpu.VMEM((SC_CHUNK, width), x_rows.dtype),
                       pltpu.SemaphoreType.DMA],
    )
    def scatter(x_hbm, dest_hbm, out_hbm, dest_v, rows_v, sem):
        wid, workers, n_own = _sc_chunks(n_chunks)

        @pl.loop(0, n_own)
        def _(j):
            ch = wid + j * workers
            pltpu.sync_copy(dest_hbm.at[ch], dest_v)
            pltpu.sync_copy(x_hbm.at[pl.ds(ch * SC_CHUNK, SC_CHUNK)], rows_v)
            copies = [pltpu.async_copy(rows_v, out_hbm.at[dest_v.at[k]], sem) for k in range(TOP_K)]
            for cp in copies:
                cp.wait()

    return scatter(x_rows, dest)


def _expert_kernel(first_ref, nblk_ref, count_ref, nused_ref, xs_hbm, wg_hbm, wu_hbm, wd_hbm, ys_hbm,
                   xbuf, ybuf, xsem, ysem, wg_f, wu_f, wd_f, wsem):
    n_exp = wg_hbm.shape[0]
    n_used = nused_ref[0]
    slots, _, half = xbuf.shape
    weight_slots = wg_f.shape[0]

    def block_rows(j):
        return pl.ds(pl.multiple_of(j * EXPERT_ROWS, EXPERT_ROWS), EXPERT_ROWS)

    def x_copy(j, slot):
        return pltpu.make_async_copy(xs_hbm.at[block_rows(j)], xbuf.at[slot], xsem.at[slot])

    def y_copy(j, slot):
        return pltpu.make_async_copy(ybuf.at[slot], ys_hbm.at[block_rows(j)], ysem.at[slot])

    def weight_copies(e, slot):
        return [pltpu.make_async_copy(src.at[e], dst.at[slot], wsem.at[i, slot])
                for i, (src, dst) in enumerate(((wg_hbm, wg_f), (wu_hbm, wu_f), (wd_hbm, wd_f)))]

    for k in range(slots - 1):
        @pl.when(k < n_used)
        def _():
            x_copy(k, k).start()
    for k in range(min(weight_slots - 1, n_exp)):
        for cp in weight_copies(k, k):
            cp.start()

    def expert(e, carry):
        wslot = e % weight_slots
        for cp in weight_copies(e, wslot):
            cp.wait()
        ahead_e = e + weight_slots - 1

        @pl.when(ahead_e < n_exp)
        def _():
            for cp in weight_copies(ahead_e, ahead_e % weight_slots):
                cp.start()

        def block(b, carry):
            j = first_ref[e] + b
            slot = j % slots
            x_copy(j, slot).wait()
            ahead = j + slots - 1

            @pl.when(ahead < n_used)
            def _():
                x_copy(ahead, ahead % slots).start()

            @pl.when(j >= slots)
            def _():
                y_copy(j - slots, slot).wait()

            row = lax.broadcasted_iota(jnp.int32, (EXPERT_ROWS, half), 0)
            packed = jnp.where(row < count_ref[e] - b * EXPERT_ROWS, xbuf[slot], jnp.uint32(0))
            lo, hi = _unpack_bf16_pairs(packed)
            wg, wu = wg_f[wslot].astype(BF16), wu_f[wslot].astype(BF16)
            g = _dot(lo, wg[:half]) + _dot(hi, wg[half:])
            u = _dot(lo, wu[:half]) + _dot(hi, wu[half:])
            h = (g * _sigmoid(g) * u).astype(BF16)
            ybuf[slot] = _pack_bf16_pairs(_dot(h, wd_f[wslot].astype(BF16)).astype(BF16))
            y_copy(j, slot).start()
            return carry

        lax.fori_loop(0, nblk_ref[e], block, 0)
        return carry

    lax.fori_loop(0, n_exp, expert, 0)

    for k in range(1, slots + 1):
        @pl.when(n_used >= k)
        def _():
            y_copy(n_used - k, (n_used - k) % slots).wait()


def _experts(first_block, n_block, counts, n_used, xs, wg, wu, wd):
    n_rows, half = xs.shape
    n_exp, d, de = wg.shape
    any_spec = pl.BlockSpec(memory_space=pl.ANY)
    return pl.pallas_call(
        _expert_kernel,
        out_shape=jax.ShapeDtypeStruct((n_rows, half), jnp.uint32),
        grid_spec=pltpu.PrefetchScalarGridSpec(
            num_scalar_prefetch=4,
            grid=(1,),
            in_specs=[any_spec, any_spec, any_spec, any_spec],
            out_specs=any_spec,
            scratch_shapes=[pltpu.VMEM((EXPERT_SLOTS, EXPERT_ROWS, half), jnp.uint32),
                            pltpu.VMEM((EXPERT_SLOTS, EXPERT_ROWS, half), jnp.uint32),
                            pltpu.SemaphoreType.DMA((EXPERT_SLOTS,)), pltpu.SemaphoreType.DMA((EXPERT_SLOTS,)),
                            pltpu.VMEM((WEIGHT_SLOTS, d, de), F32), pltpu.VMEM((WEIGHT_SLOTS, d, de), F32),
                            pltpu.VMEM((WEIGHT_SLOTS, de, d), F32), pltpu.SemaphoreType.DMA((3, WEIGHT_SLOTS))]),
        compiler_params=_cparams("arbitrary"),
        name="routed_experts",
    )(first_block, n_block, counts, n_used, xs, wg, wu, wd)


def _dense_kernel(alpha, tiles_p, x1_ref, pp_ref, ps_ref, wgs_ref, wus_ref, wds_ref, wpg_ref, wpp_ref, base_ref):
    x1 = x1_ref[...]
    xb = x1.astype(BF16)
    g = _dot(xb, wgs_ref[...])
    h = (g * _sigmoid(g) * _dot(xb, wus_ref[...])).astype(BF16)
    shared = _dot(h, wds_ref[...])
    ple = _sigmoid(_dot(xb, wpg_ref[...])) * _dot(_pool_rows(tiles_p, pp_ref, ps_ref, BF16), wpp_ref[...])
    base_ref[...] = alpha * x1 + shared + ple


def _dense(x1, pp, ps, wgs, wus, wds, wpg, wpp, alpha, tm):
    t, d = x1.shape
    const2 = lambda i: (0, 0)
    return pl.pallas_call(
        functools.partial(_dense_kernel, alpha, pp.shape[0] // tm),
        out_shape=jax.ShapeDtypeStruct((t, d), F32),
        grid=(t // tm,),
        in_specs=[pl.BlockSpec((tm, d), lambda i: (i, 0)), *_pool_specs(pp.shape[0], tm, pp.shape[1]),
                  pl.BlockSpec(wgs.shape, const2), pl.BlockSpec(wus.shape, const2),
                  pl.BlockSpec(wds.shape, const2), pl.BlockSpec(wpg.shape, const2),
                  pl.BlockSpec(wpp.shape, const2)],
        out_specs=pl.BlockSpec((tm, d), lambda i: (i, 0)),
        compiler_params=_cparams("parallel"),
        name="shared_ple",
    )(x1, pp, ps, wgs, wus, wds, wpg, wpp)


def _sc_gather(y_rows, dest, n_tok):
    n_chunks = dest.shape[0]
    width = y_rows.shape[1]
    nbuf = SC_GATHER_BUFFERS
    items = [(k, h) for k in range(TOP_K) for h in range(SC_CHUNK // SC_GATHER_ROWS)]

    @functools.partial(
        pl.kernel, mesh=_sc_mesh(),
        out_type=jax.ShapeDtypeStruct((TOP_K, n_tok, width), y_rows.dtype),
        scratch_types=[pltpu.VMEM((TOP_K, SC_CHUNK), jnp.int32),
                       pltpu.VMEM((nbuf, SC_GATHER_ROWS, width), y_rows.dtype),
                       pltpu.SemaphoreType.DMA((nbuf,)), pltpu.SemaphoreType.DMA((nbuf,))],
    )
    def gather(y_hbm, dest_hbm, out_hbm, dest_v, rows_v, gsem, wsem):
        wid, workers, n_own = _sc_chunks(n_chunks)

        @pl.loop(0, n_own)
        def _(j):
            ch = wid + j * workers
            pltpu.sync_copy(dest_hbm.at[ch], dest_v)

            def fetch(n):
                k, h = items[n]
                return pltpu.async_copy(y_hbm.at[dest_v.at[k, pl.ds(h * SC_GATHER_ROWS, SC_GATHER_ROWS)]],
                                        rows_v.at[n % nbuf], gsem.at[n % nbuf])

            def write(n):
                k, h = items[n]
                rows = pl.ds(ch * SC_CHUNK + h * SC_GATHER_ROWS, SC_GATHER_ROWS)
                return pltpu.async_copy(rows_v.at[n % nbuf], out_hbm.at[k, rows], wsem.at[n % nbuf])

            fetches = {n: fetch(n) for n in range(min(nbuf - 1, len(items)))}
            writes = {}
            for n in range(len(items)):
                fetches.pop(n).wait()
                ahead = n + nbuf - 1
                if ahead < len(items):
                    if ahead - nbuf in writes:
                        writes.pop(ahead - nbuf).wait()
                    fetches[ahead] = fetch(ahead)
                writes[n] = write(n)
            for n in sorted(writes):
                writes[n].wait()

    return gather(y_rows, dest)


def _combine_kernel(tiles_p, gate_ref, base_ref, yk_ref, g2_ref, b2_ref, outp_ref, outs_ref):
    gates = gate_ref[...]
    half = yk_ref.shape[2]
    acc_lo = base_ref[:, :half]
    acc_hi = base_ref[:, half:]
    for k in range(TOP_K):
        packed = yk_ref[k]
        gate = gates[:, k:k + 1]
        acc_lo = acc_lo + gate * lax.bitcast_convert_type(packed << 16, F32)
        acc_hi = acc_hi + gate * lax.bitcast_convert_type(packed & jnp.uint32(0xFFFF0000), F32)
    out = _layer_norm(jnp.concatenate([acc_lo, acc_hi], axis=1), g2_ref[...], b2_ref[...])
    is_prompt = pl.program_id(0) < tiles_p

    @pl.when(is_prompt)
    def _():
        outp_ref[...] = out

    @pl.when(jnp.logical_not(is_prompt))
    def _():
        outs_ref[...] = out


def _combine(gates_tok, base, yk, g2, b2, n_prompt, tm):
    t, d = base.shape
    const2 = lambda i: (0, 0)
    return pl.pallas_call(
        functools.partial(_combine_kernel, n_prompt // tm),
        out_shape=[jax.ShapeDtypeStruct((n_prompt, d), F32), jax.ShapeDtypeStruct((t - n_prompt, d), F32)],
        grid=(t // tm,),
        in_specs=[pl.BlockSpec((tm, TOP_K), lambda i: (i, 0)),
                  pl.BlockSpec((tm, d), lambda i: (i, 0)),
                  pl.BlockSpec((TOP_K, tm, yk.shape[2]), lambda i: (0, i, 0)),
                  pl.BlockSpec(g2.shape, const2), pl.BlockSpec(b2.shape, const2)],
        out_specs=list(_pool_specs(n_prompt, tm, d)),
        compiler_params=_cparams("arbitrary"),
        name="moe_combine",
    )(gates_tok, base, yk, g2, b2)


def _sgu_tables(sgu_w, sgu_b, rows_per_seq):
    reps = CHUNK // rows_per_seq
    tril = jnp.tril(sgu_w[:, :rows_per_seq, :rows_per_seq])
    eye = jnp.eye(reps, dtype=F32)
    mix = jnp.einsum("ab,gts->gatbs", eye, tril).reshape(N_GROUPS_SGU, CHUNK, CHUNK)
    bias = jnp.tile(jnp.repeat(sgu_b[:, :rows_per_seq].T, D_SGU // N_GROUPS_SGU, axis=1), (reps, 1))
    return mix.astype(BF16), bias


def _layer(xp, xs, ck, cv, pp, ps, w, rel_bias, alpha):
    batch, seq, d = xp.shape
    dec_b, dec_t, _ = xs.shape
    w_buf = ck.shape[1]
    n_p, n_s = batch * seq, dec_b * dec_t
    n_tok = n_p + n_s

    scale = jnp.concatenate([jnp.full((D_ATTN,), HEAD_DIM ** -0.5, F32),
                             jnp.ones((w["w_in"].shape[1] - D_ATTN,), F32)])
    w_in = (w["w_in"] * scale).astype(BF16)

    qp, kp, vp, up, gp, kp_t, vp_t = _proj(xp.reshape(n_p, d), w_in, 512, seq)
    qs, ks, vs, us, gs = _proj(xs.reshape(n_s, d), w_in, 512)

    b1, b4, b16 = _prompt_bias_tables(rel_bias)
    attn_p = _attn_prompt(qp, kp, vp, b1, b4, b16, batch, seq)

    attn_s = _sample_attention(rel_bias, qs.reshape(dec_b, dec_t, D_ATTN), ks.reshape(dec_b, dec_t, D_ATTN),
                               vs.reshape(dec_b, dec_t, D_ATTN), ck, cv).reshape(n_s, D_ATTN)

    ln_g, ln_b = w["sgu_ln_g"][None], w["sgu_ln_b"][None]
    mix_p, bias_p = _sgu_tables(w["sgu_w"], w["sgu_b"], CHUNK)
    mix_s, bias_s = _sgu_tables(w["sgu_w"], w["sgu_b"], dec_t)
    (sgu_p,) = _sgu(up, gp, mix_p, bias_p, ln_g, ln_b, False, 8)
    sgu_s, z2_s = _sgu(us, gs, mix_s, bias_s, ln_g, ln_b, True, 4)

    x1, x1_packed, idx, gates = _mid(xp.reshape(n_p, d), xs.reshape(n_s, d), attn_p, attn_s, sgu_p, sgu_s,
                                     w["w_mix_out"].astype(BF16), w["ln1_g"][None], w["ln1_b"][None],
                                     w["w_router"].T.astype(BF16), w["router_bias"][:, None], alpha, 512)

    rank, counts = _rank(idx, 512)
    counts = counts[:, 0].astype(jnp.int32)
    padded = (counts + EXPERT_ROWS - 1) // EXPERT_ROWS * EXPERT_ROWS
    pend = jnp.cumsum(padded)
    pstart = (pend - padded).astype(jnp.int32)
    n_blocks = (n_tok * TOP_K + N_EXPERTS * (EXPERT_ROWS - 1)) // EXPERT_ROWS
    n_used = (pend[-1:] // EXPERT_ROWS).astype(jnp.int32)

    dest = _dest(idx, rank, pstart.astype(F32)[:, None], 512)
    x_sorted = _sc_dispatch(x1_packed, dest, n_blocks * EXPERT_ROWS)
    y_sorted = _experts(pstart // EXPERT_ROWS, padded // EXPERT_ROWS, counts, n_used, x_sorted,
                        w["w_gate_e"], w["w_up_e"], w["w_down_e"])

    base = _dense(x1, pp.reshape(n_p, -1), ps.reshape(n_s, -1), w["w_gate_s"].astype(BF16), w["w_up_s"].astype(BF16),
                  w["w_down_s"].astype(BF16), w["w_ple_gate"].astype(BF16),
                  w["w_ple_proj"].astype(BF16), alpha, 1024)

    y_slots = _sc_gather(y_sorted, dest, n_tok)
    y_p, y_s = _combine(gates.T, base, y_slots, w["ln2_g"][None], w["ln2_b"][None], n_p, 512)
    y_p = y_p.reshape(batch, seq, d)
    y_s = y_s.reshape(dec_b, dec_t, d)
    keep = min(MAX_DISTANCE, seq)
    k_rows = kp_t.reshape(batch, N_HEADS, HEAD_DIM, seq)[..., seq - keep:].transpose(0, 3, 1, 2)
    v_rows = vp_t.reshape(batch, N_HEADS, HEAD_DIM, seq)[..., seq - keep:].transpose(0, 3, 1, 2)
    return (y_p, y_s, k_rows, v_rows,
            ks.reshape(dec_b, dec_t, N_HEADS, HEAD_DIM), vs.reshape(dec_b, dec_t, N_HEADS, HEAD_DIM),
            z2_s.reshape(dec_b, dec_t, D_SGU))


def kernel(x_prompt, x_sample, cache_k, cache_v, p_prompt, p_sample, w_in, rel_bias, sgu_w, sgu_b, sgu_ln_g, sgu_ln_b, w_mix_out, ln1_g, ln1_b, w_router, router_bias, w_gate_e, w_up_e, w_down_e, w_gate_s, w_up_s, w_down_s, w_ple_gate, w_ple_proj, ln2_g, ln2_b):
    depth = w_in.shape[0]
    alpha = (2 * depth) ** 0.25
    xp, xs = x_prompt, x_sample
    outs = [[] for _ in range(5)]
    for i in range(depth):
        w = {"w_in": w_in[i], "sgu_w": sgu_w[i], "sgu_b": sgu_b[i], "sgu_ln_g": sgu_ln_g[i],
             "sgu_ln_b": sgu_ln_b[i], "w_mix_out": w_mix_out[i], "ln1_g": ln1_g[i], "ln1_b": ln1_b[i],
             "w_router": w_router[i], "router_bias": router_bias[i], "w_gate_e": w_gate_e[i],
             "w_up_e": w_up_e[i], "w_down_e": w_down_e[i], "w_gate_s": w_gate_s[i], "w_up_s": w_up_s[i],
             "w_down_s": w_down_s[i], "w_ple_gate": w_ple_gate[i], "w_ple_proj": w_ple_proj[i],
             "ln2_g": ln2_g[i], "ln2_b": ln2_b[i]}
        xp, xs, kp, vp, ks, vs, zs = _layer(xp, xs, cache_k[i], cache_v[i], p_prompt[i], p_sample[i],
                                            w, rel_bias, alpha)
        for lst, val in zip(outs, (kp, vp, ks, vs, zs)):
            lst.append(val)
    return (xp, xs) + tuple(jnp.stack(lst) for lst in outs)
```

```python
import functools
import math

import numpy as np
import jax
import jax.numpy as jnp
from jax import lax
from jax.experimental import pallas as pl
from jax.experimental.pallas import tpu as pltpu
from jax.experimental.pallas import tpu_sc as plsc

F32 = jnp.float32
BF16 = jnp.bfloat16
NEG_INF = float("-inf")

N_HEADS = 8
HEAD_DIM = 64
D_ATTN = N_HEADS * HEAD_DIM
PATTERNS = ((128, 1), (512, 4), (2048, 16))
BAND = 128
N_BUCKETS = 32
MAX_DISTANCE = 2048
N_GROUPS_SGU = 8
D_SGU = 512
CHUNK = 128
N_EXPERTS = 256
TOP_K = 8
N_EXPERT_GROUPS = 8
TOPK_GROUPS = 4
ROUTED_SCALE = 2.5
LN_EPS = 1e-5
CLASSES_PER_TRIP = 2
EXPERT_ROWS = 256
EXPERT_SLOTS = 6
WEIGHT_SLOTS = 4

LANES = 128
SC_CORES = 2
SC_SUBCORES = 16
SC_CHUNK = 128
SC_GATHER_ROWS = 64
SC_GATHER_BUFFERS = 3
VMEM_LIMIT = 56 * 1024 * 1024


def _cparams(*sem):
    return pltpu.CompilerParams(dimension_semantics=sem, vmem_limit_bytes=VMEM_LIMIT)


def _layer_norm(x, g, b):
    mu = jnp.mean(x, axis=-1, keepdims=True)
    xc = x - mu
    var = jnp.mean(xc * xc, axis=-1, keepdims=True)
    return xc * lax.rsqrt(var + LN_EPS) * g + b


def _sigmoid(x):
    return 1.0 / (1.0 + jnp.exp(-x))


def _gelu(x):
    return 0.5 * x * (1.0 + lax.erf(x * math.sqrt(0.5)))


def _pack_bf16_pairs(xb):
    n = xb.shape[1] // 2
    bits = lax.bitcast_convert_type(xb.astype(F32), jnp.uint32)
    return (bits[:, :n] >> 16) | (bits[:, n:] & jnp.uint32(0xFFFF0000))


def _unpack_bf16_pairs(p):
    lo = lax.bitcast_convert_type(p << 16, F32).astype(BF16)
    hi = lax.bitcast_convert_type(p & jnp.uint32(0xFFFF0000), F32).astype(BF16)
    return lo, hi


def _dot(a, b):
    return jnp.dot(a, b, preferred_element_type=F32)


def _dot_nt(a, b):
    return lax.dot_general(a, b, (((1,), (1,)), ((), ())), preferred_element_type=F32)


def _t5_bucket_np(dist):
    max_exact = N_BUCKETS // 2
    df = np.maximum(dist, max_exact).astype(np.float32)
    large = max_exact + (np.log(df / np.float32(max_exact)) / np.float32(math.log(MAX_DISTANCE / max_exact))
                         * np.float32(N_BUCKETS - max_exact)).astype(np.int32)
    return np.where(dist < max_exact, dist, np.minimum(large, N_BUCKETS - 1)).astype(np.int32)


def _band_bucket_table(dilation):
    qi = np.arange(BAND)[:, None]
    ki = np.arange(2 * BAND)[None, :]
    dsub = qi + BAND - ki
    valid = (dsub >= 0) & (dsub <= BAND)
    return np.where(valid, _t5_bucket_np(np.clip(dsub, 0, BAND) * dilation), -1).astype(np.int32)


def _sample_bucket_tables(w_buf, t_len):
    t = np.arange(t_len)[:, None]

    def table(rows, window, dilation):
        d = w_buf + t - rows[None, :]
        ok = (d >= 0) & (d % dilation == 0) & (d <= window)
        return np.where(ok, _t5_bucket_np(np.maximum(d, 0)), -1).astype(np.int32)

    assert w_buf >= PATTERNS[-1][0]
    tables = [table(np.arange(w_buf - window, w_buf), window, dilation) for window, dilation in PATTERNS]
    new_rows = w_buf + np.arange(LANES)
    new = np.stack([table(new_rows, window, dilation) for window, dilation in PATTERNS])
    new[:, :, t_len:] = -1
    return tables, new


def _bias_kernel(rb_ref, bucket_ref, out_ref):
    bucket = bucket_ref[...]
    for h in range(N_HEADS):
        acc = jnp.full(bucket.shape, NEG_INF, F32)
        for b in range(N_BUCKETS):
            acc = jnp.where(bucket == b, rb_ref[b, h], acc)
        out_ref[h] = acc


def _bias_table(rel_bias, bucket_np):
    r, c = bucket_np.shape
    return pl.pallas_call(
        _bias_kernel,
        out_shape=jax.ShapeDtypeStruct((N_HEADS, r, c), F32),
        in_specs=[pl.BlockSpec(memory_space=pltpu.SMEM), pl.BlockSpec(memory_space=pltpu.VMEM)],
        out_specs=pl.BlockSpec(memory_space=pltpu.VMEM),
        name="bias_table",
    )(rel_bias, jnp.asarray(bucket_np))


def _proj_kernel(x_ref, w_ref, q_ref, k_ref, v_ref, u_ref, g_ref, *kv_t_refs):
    x = x_ref[...].astype(BF16)
    col = 0
    rows = []
    for o in (q_ref, k_ref, v_ref, u_ref, g_ref):
        n = o.shape[1]
        rows.append(_dot(x, w_ref[:, col:col + n]))
        o[...] = rows[-1]
        col += n
    for o, val in zip(kv_t_refs, rows[1:3]):
        o[0] = val.T


def _proj(x, w, tm, seq=None):
    m, d = x.shape
    n_out = (D_ATTN, D_ATTN, D_ATTN, D_SGU, D_SGU)
    out_shape = [jax.ShapeDtypeStruct((m, n), F32) for n in n_out]
    out_specs = [pl.BlockSpec((tm, n), lambda i: (i, 0)) for n in n_out]
    if seq is not None:
        tiles = seq // tm
        out_shape += [jax.ShapeDtypeStruct((m // seq, D_ATTN, seq), F32)] * 2
        out_specs += [pl.BlockSpec((1, D_ATTN, tm), lambda i: (i // tiles, 0, i % tiles))] * 2
    return pl.pallas_call(
        _proj_kernel,
        out_shape=out_shape,
        grid=(m // tm,),
        in_specs=[pl.BlockSpec((tm, d), lambda i: (i, 0)), pl.BlockSpec(w.shape, lambda i: (0, 0))],
        out_specs=out_specs,
        compiler_params=_cparams("parallel"),
        name="in_proj",
    )(x, w)


def _band_attn(q, k, v, bias, even):
    q2 = jnp.concatenate([jnp.where(even, q, 0.0), jnp.where(even, 0.0, q)], axis=0).astype(BF16)
    s = _dot_nt(q2, k.astype(BF16)) + bias
    m = jnp.max(s, axis=-1, keepdims=True)
    p = jnp.exp(s - m)
    l = jnp.sum(p, axis=-1, keepdims=True)
    pv = _dot(p.astype(BF16), v.astype(BF16))
    return (jnp.where(even, m[:BAND], m[BAND:]), jnp.where(even, l[:BAND], l[BAND:]),
            jnp.where(even, pv[:BAND], pv[BAND:]))


def _attn_prompt_kernel(q_ref, k_ref, v_ref, b1_ref, b4_ref, b16_ref, o_ref,
                        m1, l1, a1, m4, l4, a4, m16, l16, a16, qc, kc, vc):
    seq = q_ref.shape[0]
    n4 = seq // 4
    per_trip = seq // BAND // 4
    even = lax.broadcasted_iota(jnp.int32, (BAND, LANES), 1) < HEAD_DIM

    def store_all(refs, rows_list, results):
        for rows, (m, l, a) in zip(rows_list, results):
            refs[0][rows, :], refs[1][rows, :], refs[2][rows, :] = m, l, a

    def body(g, carry):
        for j in range(CLASSES_PER_TRIP):
            one_class(g * CLASSES_PER_TRIP + j)
        return carry

    def one_class(r):
        rows1, loaded1 = [], []
        for j in range(per_trip):
            i = r * per_trip + j
            r0 = pl.multiple_of(i * BAND, BAND)
            k0 = pl.multiple_of(jnp.maximum(i - 1, 0) * BAND, BAND)
            rows1.append(pl.ds(r0, BAND))
            loaded1.append((q_ref[pl.ds(r0, BAND)], k_ref[pl.ds(k0, 2 * BAND)], v_ref[pl.ds(k0, 2 * BAND)],
                            b1_ref[jnp.where(i == 0, 1, 0)]))
        base = pl.multiple_of(r * n4, BAND)
        q4 = q_ref[pl.ds(r, n4, stride=4), :]
        k4 = k_ref[pl.ds(r, n4, stride=4), :]
        v4 = v_ref[pl.ds(r, n4, stride=4), :]
        qc[pl.ds(base, n4)], kc[pl.ds(base, n4)], vc[pl.ds(base, n4)] = q4, k4, v4
        rows16 = [pl.ds(r * n4 + s, seq // 16, stride=4) for s in range(4)]
        loaded16 = [(qc[rows, :], kc[rows, :], vc[rows, :]) for rows in rows16]

        results1 = [_band_attn(q, k, v, bias, even) for q, k, v, bias in loaded1]
        results4, rows4 = [], []
        for i in range(n4 // BAND):
            lo = max(i - 1, 0) * BAND
            hi = (i + 1) * BAND
            col = 0 if i > 0 else BAND
            results4.append(_band_attn(q4[i * BAND:hi], k4[lo:hi], v4[lo:hi], b4_ref[:, col:], even))
            rows4.append(pl.ds(base + i * BAND, BAND))
        results16 = [_band_attn(q, k, v, b16_ref[:, BAND:], even) for q, k, v in loaded16]

        store_all((m1, l1, a1), rows1, results1)
        store_all((m4, l4, a4), rows4, results4)
        store_all((m16, l16, a16), rows16, results16)

    lax.fori_loop(0, 4 // CLASSES_PER_TRIP, body, 0)

    def merge_body(i, carry):
        r = i // (n4 // BAND)
        c = i % (n4 // BAND)
        rows = pl.ds(pl.multiple_of(i * BAND, BAND), BAND)
        nat = pl.ds(r + 4 * BAND * c, BAND, stride=4)
        ma, mb, mc = m1[nat, :], m4[rows], m16[rows]
        mx = jnp.maximum(jnp.maximum(ma, mb), mc)
        wa, wb, wc = jnp.exp(ma - mx), jnp.exp(mb - mx), jnp.exp(mc - mx)
        num = wa * a1[nat, :] + wb * a4[rows] + wc * a16[rows]
        den = wa * l1[nat, :] + wb * l4[rows] + wc * l16[rows]
        o_ref[nat, :] = num / den
        return carry

    lax.fori_loop(0, seq // BAND, merge_body, 0)


def _prompt_bias_tables(rel_bias):
    pairs = N_HEADS // 2
    first = _band_bucket_table(1)
    first = np.concatenate([first[:, BAND:], np.full((BAND, BAND), -1, np.int32)], axis=1)
    buckets = np.concatenate([_band_bucket_table(1), first, _band_bucket_table(4), _band_bucket_table(16)], axis=0)
    tables = _bias_table(rel_bias, buckets).reshape(N_HEADS, 4, BAND, 2 * BAND)
    b1, b1_first, b4, b16 = (tables[:, i].reshape(pairs, 2 * BAND, 2 * BAND) for i in range(4))
    return jnp.stack([b1, b1_first], axis=1), b4, b16


def _attn_prompt(q, k, v, b1, b4, b16, batch, seq):
    blk = pl.BlockSpec((seq, LANES), lambda b, j: (b, j))
    bias_spec = pl.BlockSpec((None, 2 * BAND, 2 * BAND), lambda b, j: (j, 0, 0))
    return pl.pallas_call(
        _attn_prompt_kernel,
        out_shape=jax.ShapeDtypeStruct(q.shape, F32),
        grid=(batch, D_ATTN // LANES),
        in_specs=[blk, blk, blk, pl.BlockSpec((None, 2, 2 * BAND, 2 * BAND), lambda b, j: (j, 0, 0, 0)),
                  bias_spec, bias_spec],
        out_specs=blk,
        scratch_shapes=[pltpu.VMEM((seq, LANES), F32) for _ in range(12)],
        compiler_params=_cparams("parallel", "parallel"),
        name="attn_prompt",
    )(q, k, v, b1, b4, b16)


def _attn_sample_kernel(q_ref, kn_ref, vn_ref, kt_ref, vt_ref, b1_ref, b4_ref, b16_ref, bn_ref, o_ref):
    t_len = q_ref.shape[1]
    rows = N_HEADS * t_len
    q = q_ref[0]
    head_of_row = lax.broadcasted_iota(jnp.int32, (rows, D_ATTN), 0) // t_len
    head_of_lane = lax.broadcasted_iota(jnp.int32, (rows, D_ATTN), 1) // HEAD_DIM
    own = head_of_row == head_of_lane
    qrows = jnp.where(own, jnp.concatenate([q] * N_HEADS, axis=0), 0.0).astype(BF16)
    pad = jnp.zeros((LANES - t_len, D_ATTN), F32)
    kn = jnp.concatenate([kn_ref[0], pad], axis=0).astype(BF16)
    vn = jnp.concatenate([vn_ref[0], pad], axis=0).astype(BF16)
    kt = kt_ref[0].reshape(D_ATTN, -1).astype(BF16)
    vt = vt_ref[0].reshape(D_ATTN, -1).astype(BF16)
    s_cache = _dot(qrows, kt)
    s_new = _dot_nt(qrows, kn)
    ms, ls, accs = [], [], []
    for p, bias_ref in enumerate((b1_ref, b4_ref, b16_ref)):
        w = bias_ref.shape[1]
        sc = s_cache[:, -w:] + bias_ref[...]
        sn = s_new + bn_ref[p]
        m = jnp.maximum(jnp.max(sc, axis=-1, keepdims=True), jnp.max(sn, axis=-1, keepdims=True))
        pc = jnp.exp(sc - m)
        pn = jnp.exp(sn - m)
        ls.append(jnp.sum(pc, axis=-1, keepdims=True) + jnp.sum(pn, axis=-1, keepdims=True))
        accs.append(_dot_nt(pc.astype(BF16), vt[:, -w:]) + _dot(pn.astype(BF16), vn))
        ms.append(m)
    mx = jnp.maximum(jnp.maximum(ms[0], ms[1]), ms[2])
    ws = [jnp.exp(m - mx) for m in ms]
    num = ws[0] * accs[0] + ws[1] * accs[1] + ws[2] * accs[2]
    den = ws[0] * ls[0] + ws[1] * ls[1] + ws[2] * ls[2]
    full = jnp.where(own, num / den, 0.0)
    out = full[0:t_len]
    for h in range(1, N_HEADS):
        out = out + full[h * t_len:(h + 1) * t_len]
    o_ref[0] = out


def _sample_attention(rel_bias, q, kn, vn, ck, cv):
    b, t_len, _ = q.shape
    w_buf = ck.shape[1]
    rows = N_HEADS * t_len
    tables, new_t = _sample_bucket_tables(w_buf, t_len)
    widest = max(tb.shape[1] for tb in tables)
    pieces = tables + [new_t.reshape(3 * t_len, LANES)]
    stacked = np.concatenate([np.pad(tb, ((0, 0), (0, widest - tb.shape[1])), constant_values=-1) for tb in pieces])
    bias = _bias_table(rel_bias, stacked)
    b1, b4, b16 = (bias[:, i * t_len:(i + 1) * t_len, :tb.shape[1]].reshape(rows, tb.shape[1])
                   for i, tb in enumerate(tables))
    bn = bias[:, 3 * t_len:, :LANES].reshape(N_HEADS, 3, t_len, LANES).transpose(1, 0, 2, 3).reshape(3, rows, LANES)
    new_spec = pl.BlockSpec((1, t_len, D_ATTN), lambda i: (i, 0, 0))
    cache_spec = pl.BlockSpec((1, N_HEADS, HEAD_DIM, w_buf), lambda i: (i, 0, 0, 0))
    const2 = lambda i: (0, 0)
    return pl.pallas_call(
        _attn_sample_kernel,
        out_shape=jax.ShapeDtypeStruct(q.shape, F32),
        grid=(b,),
        in_specs=[new_spec, new_spec, new_spec, cache_spec, cache_spec,
                  pl.BlockSpec(b1.shape, const2), pl.BlockSpec(b4.shape, const2), pl.BlockSpec(b16.shape, const2),
                  pl.BlockSpec(bn.shape, lambda i: (0, 0, 0))],
        out_specs=new_spec,
        compiler_params=_cparams("parallel"),
        name="attn_sample",
    )(q, kn, vn, ck.transpose(0, 2, 3, 1), cv.transpose(0, 2, 3, 1), b1, b4, b16, bn)


def _sgu_kernel(u_ref, g_ref, mix_ref, bias_ref, lng_ref, lnb_ref, sgu_ref, *z2_out):
    n_chunks = u_ref.shape[0] // CHUNK
    group_of_lane = lax.broadcasted_iota(jnp.int32, (CHUNK, D_SGU), 1) // (D_SGU // N_GROUPS_SGU)
    for c in range(n_chunks):
        rows = slice(c * CHUNK, (c + 1) * CHUNK)
        z1 = _gelu(u_ref[rows])
        z2 = _layer_norm(_gelu(g_ref[rows]), lng_ref[...], lnb_ref[...])
        if z2_out:
            z2_out[0][rows] = z2
        mixed = bias_ref[...]
        for g in range(N_GROUPS_SGU):
            mixed = mixed + _dot(mix_ref[g], jnp.where(group_of_lane == g, z2, 0.0).astype(BF16))
        sgu_ref[rows] = (z1 * mixed).astype(sgu_ref.dtype)


def _sgu(u, g, mix, bias, ln_g, ln_b, want_z2, chunks_per_step):
    m = u.shape[0]
    tm = CHUNK * chunks_per_step
    row_spec = pl.BlockSpec((tm, D_SGU), lambda i: (i, 0))
    const2 = lambda i: (0, 0)
    out_shape = [jax.ShapeDtypeStruct((m, D_SGU), BF16)]
    out_specs = [row_spec]
    if want_z2:
        out_shape.append(jax.ShapeDtypeStruct((m, D_SGU), F32))
        out_specs.append(row_spec)
    return pl.pallas_call(
        _sgu_kernel,
        out_shape=out_shape,
        grid=(m // tm,),
        in_specs=[row_spec, row_spec,
                  pl.BlockSpec(mix.shape, lambda i: (0, 0, 0)),
                  pl.BlockSpec(bias.shape, const2),
                  pl.BlockSpec(ln_g.shape, const2), pl.BlockSpec(ln_b.shape, const2)],
        out_specs=out_specs,
        compiler_params=_cparams("parallel"),
        name="sgu",
    )(u, g, mix, bias, ln_g, ln_b)


def _top1_rows(x, row):
    m = jnp.max(x, axis=0, keepdims=True)
    i = jnp.min(jnp.where(x == m, row, x.shape[0]), axis=0, keepdims=True)
    return m, i


def _pool_specs(n_prompt, tm, width):
    tiles_p = n_prompt // tm
    return (pl.BlockSpec((tm, width), lambda i, *_: (jnp.minimum(i, tiles_p - 1), 0)),
            pl.BlockSpec((tm, width), lambda i, *_: (jnp.maximum(i - tiles_p, 0), 0)))


def _pool_rows(tiles_p, prompt_ref, sample_ref, dtype):
    return jnp.where(pl.program_id(0) < tiles_p, prompt_ref[...].astype(dtype), sample_ref[...].astype(dtype))


def _mid_kernel(alpha, tiles_p, xp_ref, xs_ref, ap_ref, as_ref, sp_ref, ss_ref, wmix_ref, g1_ref, b1_ref,
                wrt_ref, rb_ref, x1_ref, x1p_ref, idx_ref, gate_ref):
    y = (_dot(_pool_rows(tiles_p, ap_ref, as_ref, BF16), wmix_ref[0:D_ATTN])
         + _dot(_pool_rows(tiles_p, sp_ref, ss_ref, BF16), wmix_ref[D_ATTN:])
         + alpha * _pool_rows(tiles_p, xp_ref, xs_ref, F32))
    x1 = _layer_norm(y, g1_ref[...], b1_ref[...])
    x1_ref[...] = x1
    tm = x1.shape[0]
    x1b = x1.astype(BF16)
    x1p_ref[...] = _pack_bf16_pairs(x1b)
    scores = _sigmoid(_dot_nt(wrt_ref[...], x1b))
    biased = scores + rb_ref[...]
    per_group = N_EXPERTS // N_EXPERT_GROUPS
    row_g = lax.broadcasted_iota(jnp.int32, (per_group, tm), 0)
    row_8 = lax.broadcasted_iota(jnp.int32, (N_EXPERT_GROUPS, tm), 0)
    row_e = lax.broadcasted_iota(jnp.int32, (N_EXPERTS, tm), 0)
    cur = jnp.full((N_EXPERT_GROUPS, tm), NEG_INF, F32)
    for g in range(N_EXPERT_GROUPS):
        blk = biased[g * per_group:(g + 1) * per_group]
        m_a, i_a = _top1_rows(blk, row_g)
        m_b = jnp.max(jnp.where(row_g == i_a, NEG_INF, blk), axis=0, keepdims=True)
        cur = jnp.where(row_8 == g, m_a + m_b, cur)
    chosen = jnp.zeros((N_EXPERT_GROUPS, tm), F32)
    for _ in range(TOPK_GROUPS):
        _, i_g = _top1_rows(cur, row_8)
        sel = row_8 == i_g
        chosen = jnp.where(sel, 1.0, chosen)
        cur = jnp.where(sel, NEG_INF, cur)
    cur = jnp.concatenate([jnp.where(chosen[g:g + 1] > 0.0, biased[g * per_group:(g + 1) * per_group], NEG_INF)
                           for g in range(N_EXPERT_GROUPS)], axis=0)
    idx = jnp.zeros((TOP_K, tm), jnp.int32)
    gates = jnp.zeros((TOP_K, tm), F32)
    for k in range(TOP_K):
        _, i_e = _top1_rows(cur, row_e)
        sel = row_e == i_e
        gate_k = jnp.sum(jnp.where(sel, scores, 0.0), axis=0, keepdims=True)
        cur = jnp.where(sel, NEG_INF, cur)
        idx = jnp.where(row_8 == k, i_e, idx)
        gates = jnp.where(row_8 == k, gate_k, gates)
    idx_ref[...] = idx
    gate_ref[...] = gates / jnp.sum(gates, axis=0, keepdims=True) * ROUTED_SCALE


def _mid(xp, xs, attn_p, attn_s, sgu_p, sgu_s, wmix, g1, b1, wrt, rb, alpha, tm):
    n_p, d = xp.shape
    t = n_p + xs.shape[0]
    const2 = lambda i: (0, 0)
    return pl.pallas_call(
        functools.partial(_mid_kernel, alpha, n_p // tm),
        out_shape=[jax.ShapeDtypeStruct((t, d), F32),
                   jax.ShapeDtypeStruct((t, d // 2), jnp.uint32),
                   jax.ShapeDtypeStruct((TOP_K, t), jnp.int32),
                   jax.ShapeDtypeStruct((TOP_K, t), F32)],
        grid=(t // tm,),
        in_specs=[*_pool_specs(n_p, tm, d), *_pool_specs(n_p, tm, D_ATTN), *_pool_specs(n_p, tm, D_SGU),
                  pl.BlockSpec(wmix.shape, const2), pl.BlockSpec(g1.shape, const2),
                  pl.BlockSpec(b1.shape, const2), pl.BlockSpec(wrt.shape, const2),
                  pl.BlockSpec(rb.shape, const2)],
        out_specs=[pl.BlockSpec((tm, d), lambda i: (i, 0)),
                   pl.BlockSpec((tm, d // 2), lambda i: (i, 0)),
                   pl.BlockSpec((TOP_K, tm), lambda i: (0, i)),
                   pl.BlockSpec((TOP_K, tm), lambda i: (0, i))],
        compiler_params=_cparams("parallel"),
        name="mix_ln_router",
    )(xp, xs, attn_p, attn_s, sgu_p, sgu_s, wmix, g1, b1, wrt, rb)


def _rank_kernel(idx_ref, rank_ref, counts_ref, run_ref):
    @pl.when(pl.program_id(0) == 0)
    def _():
        run_ref[...] = jnp.zeros_like(run_ref)

    tm = idx_ref.shape[1]
    row_e = lax.broadcasted_iota(jnp.int32, (N_EXPERTS, tm), 0)
    idx = idx_ref[...]
    onehot = jnp.zeros((N_EXPERTS, tm), F32)
    for k in range(TOP_K):
        onehot = onehot + jnp.where(row_e == idx[k:k + 1], 1.0, 0.0)
    earlier = (lax.broadcasted_iota(jnp.int32, (tm, tm), 0)
               < lax.broadcasted_iota(jnp.int32, (tm, tm), 1))
    before = run_ref[...] + _dot(onehot.astype(BF16), jnp.where(earlier, 1.0, 0.0).astype(BF16))
    row_k = lax.broadcasted_iota(jnp.int32, (TOP_K, tm), 0)
    ranks = jnp.zeros((TOP_K, tm), F32)
    for k in range(TOP_K):
        rank_k = jnp.sum(jnp.where(row_e == idx[k:k + 1], before, 0.0), axis=0, keepdims=True)
        ranks = jnp.where(row_k == k, rank_k, ranks)
    rank_ref[...] = ranks.astype(jnp.int32)
    run_ref[...] = run_ref[...] + jnp.sum(onehot, axis=1, keepdims=True)
    counts_ref[...] = run_ref[...]


def _rank(idx, tm):
    t = idx.shape[1]
    return pl.pallas_call(
        _rank_kernel,
        out_shape=[jax.ShapeDtypeStruct((TOP_K, t), jnp.int32),
                   jax.ShapeDtypeStruct((N_EXPERTS, 1), F32)],
        grid=(t // tm,),
        in_specs=[pl.BlockSpec((TOP_K, tm), lambda i: (0, i))],
        out_specs=[pl.BlockSpec((TOP_K, tm), lambda i: (0, i)),
                   pl.BlockSpec((N_EXPERTS, 1), lambda i: (0, 0))],
        scratch_shapes=[pltpu.VMEM((N_EXPERTS, 1), F32)],
        compiler_params=_cparams("arbitrary"),
        name="expert_rank",
    )(idx)


def _dest_kernel(idx_ref, rank_ref, pstart_ref, dest_ref):
    tm = idx_ref.shape[1]
    row_e = lax.broadcasted_iota(jnp.int32, (N_EXPERTS, tm), 0)
    row_k = lax.broadcasted_iota(jnp.int32, (TOP_K, tm), 0)
    idx = idx_ref[...]
    start = jnp.zeros((TOP_K, tm), F32)
    for k in range(TOP_K):
        start_k = jnp.sum(jnp.where(row_e == idx[k:k + 1], pstart_ref[...], 0.0), axis=0, keepdims=True)
        start = jnp.where(row_k == k, start_k, start)
    dest = start.astype(jnp.int32) + rank_ref[...]
    for c in range(tm // SC_CHUNK):
        dest_ref[c] = dest[:, c * SC_CHUNK:(c + 1) * SC_CHUNK]


def _dest(idx, rank, pstart_col, tm):
    t = idx.shape[1]
    tok_spec = pl.BlockSpec((TOP_K, tm), lambda i: (0, i))
    return pl.pallas_call(
        _dest_kernel,
        out_shape=jax.ShapeDtypeStruct((t // SC_CHUNK, TOP_K, SC_CHUNK), jnp.int32),
        grid=(t // tm,),
        in_specs=[tok_spec, tok_spec, pl.BlockSpec(pstart_col.shape, lambda i: (0, 0))],
        out_specs=pl.BlockSpec((tm // SC_CHUNK, TOP_K, SC_CHUNK), lambda i: (i, 0, 0)),
        compiler_params=_cparams("parallel"),
        name="moe_dest",
    )(idx, rank, pstart_col)


def _sc_mesh():
    return plsc.VectorSubcoreMesh(core_axis_name="c", subcore_axis_name="s",
                                  num_cores=SC_CORES, num_subcores=SC_SUBCORES)


def _sc_chunks(n_chunks):
    workers = SC_CORES * SC_SUBCORES
    wid = lax.axis_index("s") * SC_CORES + lax.axis_index("c")
    return wid, workers, (n_chunks - wid + workers - 1) // workers


def _sc_dispatch(x_rows, dest, n_rows):
    n_chunks = dest.shape[0]
    width = x_rows.shape[1]

    @functools.partial(
        pl.kernel, mesh=_sc_mesh(),
        out_type=jax.ShapeDtypeStruct((n_rows, width), x_rows.dtype),
        scratch_types=[pltpu.VMEM((TOP_K, SC_CHUNK), jnp.int32), pltpu.VMEM((SC_CHUNK, width), x_rows.dtype),
                       pltpu.SemaphoreType.DMA],
    )
    def scatter(x_hbm, dest_hbm, out_hbm, dest_v, rows_v, sem):
        wid, workers, n_own = _sc_chunks(n_chunks)

        @pl.loop(0, n_own)
        def _(j):
            ch = wid + j * workers
            pltpu.sync_copy(dest_hbm.at[ch], dest_v)
            pltpu.sync_copy(x_hbm.at[pl.ds(ch * SC_CHUNK, SC_CHUNK)], rows_v)
            copies = [pltpu.async_copy(rows_v, out_hbm.at[dest_v.at[k]], sem) for k in range(TOP_K)]
            for cp in copies:
                cp.wait()

    return scatter(x_rows, dest)


def _expert_kernel(first_ref, nblk_ref, count_ref, nused_ref, xs_hbm, wg_hbm, wu_hbm, wd_hbm, ys_hbm,
                   xbuf, ybuf, xsem, ysem, wg_f, wu_f, wd_f, wsem):
    n_exp = wg_hbm.shape[0]
    n_used = nused_ref[0]
    slots, _, half = xbuf.shape
    weight_slots = wg_f.shape[0]

    def block_rows(j):
        return pl.ds(pl.multiple_of(j * EXPERT_ROWS, EXPERT_ROWS), EXPERT_ROWS)

    def x_copy(j, slot):
        return pltpu.make_async_copy(xs_hbm.at[block_rows(j)], xbuf.at[slot], xsem.at[slot])

    def y_copy(j, slot):
        return pltpu.make_async_copy(ybuf.at[slot], ys_hbm.at[block_rows(j)], ysem.at[slot])

    def weight_copies(e, slot):
        return [pltpu.make_async_copy(src.at[e], dst.at[slot], wsem.at[i, slot])
                for i, (src, dst) in enumerate(((wg_hbm, wg_f), (wu_hbm, wu_f), (wd_hbm, wd_f)))]

    for k in range(slots - 1):
        @pl.when(k < n_used)
        def _():
            x_copy(k, k).start()
    for k in range(min(weight_slots - 1, n_exp)):
        for cp in weight_copies(k, k):
            cp.start()

    def expert(e, carry):
        wslot = e % weight_slots
        for cp in weight_copies(e, wslot):
            cp.wait()
        ahead_e = e + weight_slots - 1

        @pl.when(ahead_e < n_exp)
        def _():
            for cp in weight_copies(ahead_e, ahead_e % weight_slots):
                cp.start()

        def block(b, carry):
            j = first_ref[e] + b
            slot = j % slots
            x_copy(j, slot).wait()
            ahead = j + slots - 1

            @pl.when(ahead < n_used)
            def _():
                x_copy(ahead, ahead % slots).start()

            @pl.when(j >= slots)
            def _():
                y_copy(j - slots, slot).wait()

            row = lax.broadcasted_iota(jnp.int32, (EXPERT_ROWS, half), 0)
            packed = jnp.where(row < count_ref[e] - b * EXPERT_ROWS, xbuf[slot], jnp.uint32(0))
            lo, hi = _unpack_bf16_pairs(packed)
            wg, wu = wg_f[wslot].astype(BF16), wu_f[wslot].astype(BF16)
            g = _dot(lo, wg[:half]) + _dot(hi, wg[half:])
            u = _dot(lo, wu[:half]) + _dot(hi, wu[half:])
            h = (g * _sigmoid(g) * u).astype(BF16)
            ybuf[slot] = _pack_bf16_pairs(_dot(h, wd_f[wslot].astype(BF16)).astype(BF16))
            y_copy(j, slot).start()
            return carry

        lax.fori_loop(0, nblk_ref[e], block, 0)
        return carry

    lax.fori_loop(0, n_exp, expert, 0)

    for k in range(1, slots + 1):
        @pl.when(n_used >= k)
        def _():
            y_copy(n_used - k, (n_used - k) % slots).wait()


def _experts(first_block, n_block, counts, n_used, xs, wg, wu, wd):
    n_rows, half = xs.shape
    n_exp, d, de = wg.shape
    any_spec = pl.BlockSpec(memory_space=pl.ANY)
    return pl.pallas_call(
        _expert_kernel,
        out_shape=jax.ShapeDtypeStruct((n_rows, half), jnp.uint32),
        grid_spec=pltpu.PrefetchScalarGridSpec(
            num_scalar_prefetch=4,
            grid=(1,),
            in_specs=[any_spec, any_spec, any_spec, any_spec],
            out_specs=any_spec,
            scratch_shapes=[pltpu.VMEM((EXPERT_SLOTS, EXPERT_ROWS, half), jnp.uint32),
                            pltpu.VMEM((EXPERT_SLOTS, EXPERT_ROWS, half), jnp.uint32),
                            pltpu.SemaphoreType.DMA((EXPERT_SLOTS,)), pltpu.SemaphoreType.DMA((EXPERT_SLOTS,)),
                            pltpu.VMEM((WEIGHT_SLOTS, d, de), F32), pltpu.VMEM((WEIGHT_SLOTS, d, de), F32),
                            pltpu.VMEM((WEIGHT_SLOTS, de, d), F32), pltpu.SemaphoreType.DMA((3, WEIGHT_SLOTS))]),
        compiler_params=_cparams("arbitrary"),
        name="routed_experts",
    )(first_block, n_block, counts, n_used, xs, wg, wu, wd)


def _dense_kernel(alpha, tiles_p, x1_ref, pp_ref, ps_ref, wgs_ref, wus_ref, wds_ref, wpg_ref, wpp_ref, base_ref):
    x1 = x1_ref[...]
    xb = x1.astype(BF16)
    g = _dot(xb, wgs_ref[...])
    h = (g * _sigmoid(g) * _dot(xb, wus_ref[...])).astype(BF16)
    shared = _dot(h, wds_ref[...])
    ple = _sigmoid(_dot(xb, wpg_ref[...])) * _dot(_pool_rows(tiles_p, pp_ref, ps_ref, BF16), wpp_ref[...])
    base_ref[...] = alpha * x1 + shared + ple


def _dense(x1, pp, ps, wgs, wus, wds, wpg, wpp, alpha, tm):
    t, d = x1.shape
    const2 = lambda i: (0, 0)
    return pl.pallas_call(
        functools.partial(_dense_kernel, alpha, pp.shape[0] // tm),
        out_shape=jax.ShapeDtypeStruct((t, d), F32),
        grid=(t // tm,),
        in_specs=[pl.BlockSpec((tm, d), lambda i: (i, 0)), *_pool_specs(pp.shape[0], tm, pp.shape[1]),
                  pl.BlockSpec(wgs.shape, const2), pl.BlockSpec(wus.shape, const2),
                  pl.BlockSpec(wds.shape, const2), pl.BlockSpec(wpg.shape, const2),
                  pl.BlockSpec(wpp.shape, const2)],
        out_specs=pl.BlockSpec((tm, d), lambda i: (i, 0)),
        compiler_params=_cparams("parallel"),
        name="shared_ple",
    )(x1, pp, ps, wgs, wus, wds, wpg, wpp)


def _sc_gather(y_rows, dest, n_tok):
    n_chunks = dest.shape[0]
    width = y_rows.shape[1]
    nbuf = SC_GATHER_BUFFERS
    items = [(k, h) for k in range(TOP_K) for h in range(SC_CHUNK // SC_GATHER_ROWS)]

    @functools.partial(
        pl.kernel, mesh=_sc_mesh(),
        out_type=jax.ShapeDtypeStruct((TOP_K, n_tok, width), y_rows.dtype),
        scratch_types=[pltpu.VMEM((TOP_K, SC_CHUNK), jnp.int32),
                       pltpu.VMEM((nbuf, SC_GATHER_ROWS, width), y_rows.dtype),
                       pltpu.SemaphoreType.DMA((nbuf,)), pltpu.SemaphoreType.DMA((nbuf,))],
    )
    def gather(y_hbm, dest_hbm, out_hbm, dest_v, rows_v, gsem, wsem):
        wid, workers, n_own = _sc_chunks(n_chunks)

        @pl.loop(0, n_own)
        def _(j):
            ch = wid + j * workers
            pltpu.sync_copy(dest_hbm.at[ch], dest_v)

            def fetch(n):
                k, h = items[n]
                return pltpu.async_copy(y_hbm.at[dest_v.at[k, pl.ds(h * SC_GATHER_ROWS, SC_GATHER_ROWS)]],
                                        rows_v.at[n % nbuf], gsem.at[n % nbuf])

            def write(n):
                k, h = items[n]
                rows = pl.ds(ch * SC_CHUNK + h * SC_GATHER_ROWS, SC_GATHER_ROWS)
                return pltpu.async_copy(rows_v.at[n % nbuf], out_hbm.at[k, rows], wsem.at[n % nbuf])

            fetches = {n: fetch(n) for n in range(min(nbuf - 1, len(items)))}
            writes = {}
            for n in range(len(items)):
                fetches.pop(n).wait()
                ahead = n + nbuf - 1
                if ahead < len(items):
                    if ahead - nbuf in writes:
                        writes.pop(ahead - nbuf).wait()
                    fetches[ahead] = fetch(ahead)
                writes[n] = write(n)
            for n in sorted(writes):
                writes[n].wait()

    return gather(y_rows, dest)


def _combine_kernel(tiles_p, gate_ref, base_ref, yk_ref, g2_ref, b2_ref, outp_ref, outs_ref):
    gates = gate_ref[...]
    half = yk_ref.shape[2]
    acc_lo = base_ref[:, :half]
    acc_hi = base_ref[:, half:]
    for k in range(TOP_K):
        packed = yk_ref[k]
        gate = gates[:, k:k + 1]
        acc_lo = acc_lo + gate * lax.bitcast_convert_type(packed << 16, F32)
        acc_hi = acc_hi + gate * lax.bitcast_convert_type(packed & jnp.uint32(0xFFFF0000), F32)
    out = _layer_norm(jnp.concatenate([acc_lo, acc_hi], axis=1), g2_ref[...], b2_ref[...])
    is_prompt = pl.program_id(0) < tiles_p

    @pl.when(is_prompt)
    def _():
        outp_ref[...] = out

    @pl.when(jnp.logical_not(is_prompt))
    def _():
        outs_ref[...] = out


def _combine(gates_tok, base, yk, g2, b2, n_prompt, tm):
    t, d = base.shape
    const2 = lambda i: (0, 0)
    return pl.pallas_call(
        functools.partial(_combine_kernel, n_prompt // tm),
        out_shape=[jax.ShapeDtypeStruct((n_prompt, d), F32), jax.ShapeDtypeStruct((t - n_prompt, d), F32)],
        grid=(t // tm,),
        in_specs=[pl.BlockSpec((tm, TOP_K), lambda i: (i, 0)),
                  pl.BlockSpec((tm, d), lambda i: (i, 0)),
                  pl.BlockSpec((TOP_K, tm, yk.shape[2]), lambda i: (0, i, 0)),
                  pl.BlockSpec(g2.shape, const2), pl.BlockSpec(b2.shape, const2)],
        out_specs=list(_pool_specs(n_prompt, tm, d)),
        compiler_params=_cparams("arbitrary"),
        name="moe_combine",
    )(gates_tok, base, yk, g2, b2)


def _sgu_tables(sgu_w, sgu_b, rows_per_seq):
    reps = CHUNK // rows_per_seq
    tril = jnp.tril(sgu_w[:, :rows_per_seq, :rows_per_seq])
    eye = jnp.eye(reps, dtype=F32)
    mix = jnp.einsum("ab,gts->gatbs", eye, tril).reshape(N_GROUPS_SGU, CHUNK, CHUNK)
    bias = jnp.tile(jnp.repeat(sgu_b[:, :rows_per_seq].T, D_SGU // N_GROUPS_SGU, axis=1), (reps, 1))
    return mix.astype(BF16), bias


def _layer(xp, xs, ck, cv, pp, ps, w, rel_bias, alpha):
    batch, seq, d = xp.shape
    dec_b, dec_t, _ = xs.shape
    w_buf = ck.shape[1]
    n_p, n_s = batch * seq, dec_b * dec_t
    n_tok = n_p + n_s

    scale = jnp.concatenate([jnp.full((D_ATTN,), HEAD_DIM ** -0.5, F32),
                             jnp.ones((w["w_in"].shape[1] - D_ATTN,), F32)])
    w_in = (w["w_in"] * scale).astype(BF16)

    qp, kp, vp, up, gp, kp_t, vp_t = _proj(xp.reshape(n_p, d), w_in, 512, seq)
    qs, ks, vs, us, gs = _proj(xs.reshape(n_s, d), w_in, 512)

    b1, b4, b16 = _prompt_bias_tables(rel_bias)
    attn_p = _attn_prompt(qp, kp, vp, b1, b4, b16, batch, seq)

    attn_s = _sample_attention(rel_bias, qs.reshape(dec_b, dec_t, D_ATTN), ks.reshape(dec_b, dec_t, D_ATTN),
                               vs.reshape(dec_b, dec_t, D_ATTN), ck, cv).reshape(n_s, D_ATTN)

    ln_g, ln_b = w["sgu_ln_g"][None], w["sgu_ln_b"][None]
    mix_p, bias_p = _sgu_tables(w["sgu_w"], w["sgu_b"], CHUNK)
    mix_s, bias_s = _sgu_tables(w["sgu_w"], w["sgu_b"], dec_t)
    (sgu_p,) = _sgu(up, gp, mix_p, bias_p, ln_g, ln_b, False, 8)
    sgu_s, z2_s = _sgu(us, gs, mix_s, bias_s, ln_g, ln_b, True, 4)

    x1, x1_packed, idx, gates = _mid(xp.reshape(n_p, d), xs.reshape(n_s, d), attn_p, attn_s, sgu_p, sgu_s,
                                     w["w_mix_out"].astype(BF16), w["ln1_g"][None], w["ln1_b"][None],
                                     w["w_router"].T.astype(BF16), w["router_bias"][:, None], alpha, 512)

    rank, counts = _rank(idx, 512)
    counts = counts[:, 0].astype(jnp.int32)
    padded = (counts + EXPERT_ROWS - 1) // EXPERT_ROWS * EXPERT_ROWS
    pend = jnp.cumsum(padded)
    pstart = (pend - padded).astype(jnp.int32)
    n_blocks = (n_tok * TOP_K + N_EXPERTS * (EXPERT_ROWS - 1)) // EXPERT_ROWS
    n_used = (pend[-1:] // EXPERT_ROWS).astype(jnp.int32)

    dest = _dest(idx, rank, pstart.astype(F32)[:, None], 512)
    x_sorted = _sc_dispatch(x1_packed, dest, n_blocks * EXPERT_ROWS)
    y_sorted = _experts(pstart // EXPERT_ROWS, padded // EXPERT_ROWS, counts, n_used, x_sorted,
                        w["w_gate_e"], w["w_up_e"], w["w_down_e"])

    base = _dense(x1, pp.reshape(n_p, -1), ps.reshape(n_s, -1), w["w_gate_s"].astype(BF16), w["w_up_s"].astype(BF16),
                  w["w_down_s"].astype(BF16), w["w_ple_gate"].astype(BF16),
                  w["w_ple_proj"].astype(BF16), alpha, 1024)

    y_slots = _sc_gather(y_sorted, dest, n_tok)
    y_p, y_s = _combine(gates.T, base, y_slots, w["ln2_g"][None], w["ln2_b"][None], n_p, 512)
    y_p = y_p.reshape(batch, seq, d)
    y_s = y_s.reshape(dec_b, dec_t, d)
    keep = min(MAX_DISTANCE, seq)
    k_rows = kp_t.reshape(batch, N_HEADS, HEAD_DIM, seq)[..., seq - keep:].transpose(0, 3, 1, 2)
    v_rows = vp_t.reshape(batch, N_HEADS, HEAD_DIM, seq)[..., seq - keep:].transpose(0, 3, 1, 2)
    return (y_p, y_s, k_rows, v_rows,
            ks.reshape(dec_b, dec_t, N_HEADS, HEAD_DIM), vs.reshape(dec_b, dec_t, N_HEADS, HEAD_DIM),
            z2_s.reshape(dec_b, dec_t, D_SGU))


def kernel(x_prompt, x_sample, cache_k, cache_v, p_prompt, p_sample, w_in, rel_bias, sgu_w, sgu_b, sgu_ln_g, sgu_ln_b, w_mix_out, ln1_g, ln1_b, w_router, router_bias, w_gate_e, w_up_e, w_down_e, w_gate_s, w_up_s, w_down_s, w_ple_gate, w_ple_proj, ln2_g, ln2_b):
    depth = w_in.shape[0]
    alpha = (2 * depth) ** 0.25
    xp, xs = x_prompt, x_sample
    outs = [[] for _ in range(5)]
    for i in range(depth):
        w = {"w_in": w_in[i], "sgu_w": sgu_w[i], "sgu_b": sgu_b[i], "sgu_ln_g": sgu_ln_g[i],
             "sgu_ln_b": sgu_ln_b[i], "w_mix_out": w_mix_out[i], "ln1_g": ln1_g[i], "ln1_b": ln1_b[i],
             "w_router": w_router[i], "router_bias": router_bias[i], "w_gate_e": w_gate_e[i],
             "w_up_e": w_up_e[i], "w_down_e": w_down_e[i], "w_gate_s": w_gate_s[i], "w_up_s": w_up_s[i],
             "w_down_s": w_down_s[i], "w_ple_gate": w_ple_gate[i], "w_ple_proj": w_ple_proj[i],
             "ln2_g": ln2_g[i], "ln2_b": ln2_b[i]}
        xp, xs, kp, vp, ks, vs, zs = _layer(xp, xs, cache_k[i], cache_v[i], p_prompt[i], p_sample[i],
                                            w, rel_bias, alpha)
        for lst, val in zip(outs, (kp, vp, ks, vs, zs)):
            lst.append(val)
    return (xp, xs) + tuple(jnp.stack(lst) for lst in outs)
```

```python
import functools
import math

import numpy as np
import jax
import jax.numpy as jnp
from jax import lax
from jax.experimental import pallas as pl
from jax.experimental.pallas import tpu as pltpu
from jax.experimental.pallas import tpu_sc as plsc

F32 = jnp.float32
BF16 = jnp.bfloat16
NEG_INF = float("-inf")

N_HEADS = 8
HEAD_DIM = 64
D_ATTN = N_HEADS * HEAD_DIM
PATTERNS = ((128, 1), (512, 4), (2048, 16))
BAND = 128
N_BUCKETS = 32
MAX_DISTANCE = 2048
N_GROUPS_SGU = 8
D_SGU = 512
CHUNK = 128
N_EXPERTS = 256
TOP_K = 8
N_EXPERT_GROUPS = 8
TOPK_GROUPS = 4
ROUTED_SCALE = 2.5
LN_EPS = 1e-5
CLASSES_PER_TRIP = 2
EXPERT_ROWS = 256
EXPERT_SLOTS = 6
WEIGHT_SLOTS = 3

LANES = 128
SC_CORES = 2
SC_SUBCORES = 16
SC_CHUNK = 128
SC_GATHER_ROWS = 64
SC_GATHER_BUFFERS = 3
VMEM_LIMIT = 56 * 1024 * 1024


def _cparams(*sem):
    return pltpu.CompilerParams(dimension_semantics=sem, vmem_limit_bytes=VMEM_LIMIT)


def _layer_norm(x, g, b):
    mu = jnp.mean(x, axis=-1, keepdims=True)
    xc = x - mu
    var = jnp.mean(xc * xc, axis=-1, keepdims=True)
    return xc * lax.rsqrt(var + LN_EPS) * g + b


def _sigmoid(x):
    return 1.0 / (1.0 + jnp.exp(-x))


def _gelu(x):
    return 0.5 * x * (1.0 + lax.erf(x * math.sqrt(0.5)))


def _pack_bf16_pairs(xb):
    n = xb.shape[1] // 2
    bits = lax.bitcast_convert_type(xb.astype(F32), jnp.uint32)
    return (bits[:, :n] >> 16) | (bits[:, n:] & jnp.uint32(0xFFFF0000))


def _unpack_bf16_pairs(p):
    lo = lax.bitcast_convert_type(p << 16, F32).astype(BF16)
    hi = lax.bitcast_convert_type(p & jnp.uint32(0xFFFF0000), F32).astype(BF16)
    return lo, hi


def _dot(a, b):
    return jnp.dot(a, b, preferred_element_type=F32)


def _dot_nt(a, b):
    return lax.dot_general(a, b, (((1,), (1,)), ((), ())), preferred_element_type=F32)


def _t5_bucket_np(dist):
    max_exact = N_BUCKETS // 2
    df = np.maximum(dist, max_exact).astype(np.float32)
    large = max_exact + (np.log(df / np.float32(max_exact)) / np.float32(math.log(MAX_DISTANCE / max_exact))
                         * np.float32(N_BUCKETS - max_exact)).astype(np.int32)
    return np.where(dist < max_exact, dist, np.minimum(large, N_BUCKETS - 1)).astype(np.int32)


def _band_bucket_table(dilation):
    qi = np.arange(BAND)[:, None]
    ki = np.arange(2 * BAND)[None, :]
    dsub = qi + BAND - ki
    valid = (dsub >= 0) & (dsub <= BAND)
    return np.where(valid, _t5_bucket_np(np.clip(dsub, 0, BAND) * dilation), -1).astype(np.int32)


def _sample_bucket_tables(w_buf, t_len):
    t = np.arange(t_len)[:, None]

    def table(rows, window, dilation):
        d = w_buf + t - rows[None, :]
        ok = (d >= 0) & (d % dilation == 0) & (d <= window)
        return np.where(ok, _t5_bucket_np(np.maximum(d, 0)), -1).astype(np.int32)

    assert w_buf >= PATTERNS[-1][0]
    tables = [table(np.arange(w_buf - window, w_buf), window, dilation) for window, dilation in PATTERNS]
    new_rows = w_buf + np.arange(LANES)
    new = np.stack([table(new_rows, window, dilation) for window, dilation in PATTERNS])
    new[:, :, t_len:] = -1
    return tables, new


def _bias_kernel(rb_ref, bucket_ref, out_ref):
    bucket = bucket_ref[...]
    for h in range(N_HEADS):
        acc = jnp.full(bucket.shape, NEG_INF, F32)
        for b in range(N_BUCKETS):
            acc = jnp.where(bucket == b, rb_ref[b, h], acc)
        out_ref[h] = acc


def _bias_table(rel_bias, bucket_np):
    r, c = bucket_np.shape
    return pl.pallas_call(
        _bias_kernel,
        out_shape=jax.ShapeDtypeStruct((N_HEADS, r, c), F32),
        in_specs=[pl.BlockSpec(memory_space=pltpu.SMEM), pl.BlockSpec(memory_space=pltpu.VMEM)],
        out_specs=pl.BlockSpec(memory_space=pltpu.VMEM),
        name="bias_table",
    )(rel_bias, jnp.asarray(bucket_np))


def _proj_kernel(x_ref, w_ref, q_ref, k_ref, v_ref, u_ref, g_ref, *kv_t_refs):
    x = x_ref[...].astype(BF16)
    col = 0
    rows = []
    for o in (q_ref, k_ref, v_ref, u_ref, g_ref):
        n = o.shape[1]
        rows.append(_dot(x, w_ref[:, col:col + n]))
        o[...] = rows[-1]
        col += n
    for o, val in zip(kv_t_refs, rows[1:3]):
        o[0] = val.T


def _proj(x, w, tm, seq=None):
    m, d = x.shape
    n_out = (D_ATTN, D_ATTN, D_ATTN, D_SGU, D_SGU)
    out_shape = [jax.ShapeDtypeStruct((m, n), F32) for n in n_out]
    out_specs = [pl.BlockSpec((tm, n), lambda i: (i, 0)) for n in n_out]
    if seq is not None:
        tiles = seq // tm
        out_shape += [jax.ShapeDtypeStruct((m // seq, D_ATTN, seq), F32)] * 2
        out_specs += [pl.BlockSpec((1, D_ATTN, tm), lambda i: (i // tiles, 0, i % tiles))] * 2
    return pl.pallas_call(
        _proj_kernel,
        out_shape=out_shape,
        grid=(m // tm,),
        in_specs=[pl.BlockSpec((tm, d), lambda i: (i, 0)), pl.BlockSpec(w.shape, lambda i: (0, 0))],
        out_specs=out_specs,
        compiler_params=_cparams("parallel"),
        name="in_proj",
    )(x, w)


def _band_attn(q, k, v, bias, even):
    q2 = jnp.concatenate([jnp.where(even, q, 0.0), jnp.where(even, 0.0, q)], axis=0).astype(BF16)
    s = _dot_nt(q2, k.astype(BF16)) + bias
    m = jnp.max(s, axis=-1, keepdims=True)
    p = jnp.exp(s - m)
    l = jnp.sum(p, axis=-1, keepdims=True)
    pv = _dot(p.astype(BF16), v.astype(BF16))
    return (jnp.where(even, m[:BAND], m[BAND:]), jnp.where(even, l[:BAND], l[BAND:]),
            jnp.where(even, pv[:BAND], pv[BAND:]))


def _attn_prompt_kernel(q_ref, k_ref, v_ref, b1_ref, b4_ref, b16_ref, o_ref,
                        m1, l1, a1, m4, l4, a4, m16, l16, a16, qc, kc, vc):
    seq = q_ref.shape[0]
    n4 = seq // 4
    per_trip = seq // BAND // 4
    even = lax.broadcasted_iota(jnp.int32, (BAND, LANES), 1) < HEAD_DIM

    def store_all(refs, rows_list, results):
        for rows, (m, l, a) in zip(rows_list, results):
            refs[0][rows, :], refs[1][rows, :], refs[2][rows, :] = m, l, a

    def body(g, carry):
        for j in range(CLASSES_PER_TRIP):
            one_class(g * CLASSES_PER_TRIP + j)
        return carry

    def one_class(r):
        rows1, loaded1 = [], []
        for j in range(per_trip):
            i = r * per_trip + j
            r0 = pl.multiple_of(i * BAND, BAND)
            k0 = pl.multiple_of(jnp.maximum(i - 1, 0) * BAND, BAND)
            rows1.append(pl.ds(r0, BAND))
            loaded1.append((q_ref[pl.ds(r0, BAND)], k_ref[pl.ds(k0, 2 * BAND)], v_ref[pl.ds(k0, 2 * BAND)],
                            b1_ref[jnp.where(i == 0, 1, 0)]))
        base = pl.multiple_of(r * n4, BAND)
        q4 = q_ref[pl.ds(r, n4, stride=4), :]
        k4 = k_ref[pl.ds(r, n4, stride=4), :]
        v4 = v_ref[pl.ds(r, n4, stride=4), :]
        qc[pl.ds(base, n4)], kc[pl.ds(base, n4)], vc[pl.ds(base, n4)] = q4, k4, v4
        rows16 = [pl.ds(r * n4 + s, seq // 16, stride=4) for s in range(4)]
        loaded16 = [(qc[rows, :], kc[rows, :], vc[rows, :]) for rows in rows16]

        results1 = [_band_attn(q, k, v, bias, even) for q, k, v, bias in loaded1]
        results4, rows4 = [], []
        for i in range(n4 // BAND):
            lo = max(i - 1, 0) * BAND
            hi = (i + 1) * BAND
            col = 0 if i > 0 else BAND
            results4.append(_band_attn(q4[i * BAND:hi], k4[lo:hi], v4[lo:hi], b4_ref[:, col:], even))
            rows4.append(pl.ds(base + i * BAND, BAND))
        results16 = [_band_attn(q, k, v, b16_ref[:, BAND:], even) for q, k, v in loaded16]

        store_all((m1, l1, a1), rows1, results1)
        store_all((m4, l4, a4), rows4, results4)
        store_all((m16, l16, a16), rows16, results16)

    lax.fori_loop(0, 4 // CLASSES_PER_TRIP, body, 0)

    def merge_body(i, carry):
        r = i // (n4 // BAND)
        c = i % (n4 // BAND)
        rows = pl.ds(pl.multiple_of(i * BAND, BAND), BAND)
        nat = pl.ds(r + 4 * BAND * c, BAND, stride=4)
        ma, mb, mc = m1[nat, :], m4[rows], m16[rows]
        mx = jnp.maximum(jnp.maximum(ma, mb), mc)
        wa, wb, wc = jnp.exp(ma - mx), jnp.exp(mb - mx), jnp.exp(mc - mx)
        num = wa * a1[nat, :] + wb * a4[rows] + wc * a16[rows]
        den = wa * l1[nat, :] + wb * l4[rows] + wc * l16[rows]
        o_ref[nat, :] = num / den
        return carry

    lax.fori_loop(0, seq // BAND, merge_body, 0)


def _prompt_bias_tables(rel_bias):
    pairs = N_HEADS // 2
    first = _band_bucket_table(1)
    first = np.concatenate([first[:, BAND:], np.full((BAND, BAND), -1, np.int32)], axis=1)
    buckets = np.concatenate([_band_bucket_table(1), first, _band_bucket_table(4), _band_bucket_table(16)], axis=0)
    tables = _bias_table(rel_bias, buckets).reshape(N_HEADS, 4, BAND, 2 * BAND)
    b1, b1_first, b4, b16 = (tables[:, i].reshape(pairs, 2 * BAND, 2 * BAND) for i in range(4))
    return jnp.stack([b1, b1_first], axis=1), b4, b16


def _attn_prompt(q, k, v, b1, b4, b16, batch, seq):
    blk = pl.BlockSpec((seq, LANES), lambda b, j: (b, j))
    bias_spec = pl.BlockSpec((None, 2 * BAND, 2 * BAND), lambda b, j: (j, 0, 0))
    return pl.pallas_call(
        _attn_prompt_kernel,
        out_shape=jax.ShapeDtypeStruct(q.shape, F32),
        grid=(batch, D_ATTN // LANES),
        in_specs=[blk, blk, blk, pl.BlockSpec((None, 2, 2 * BAND, 2 * BAND), lambda b, j: (j, 0, 0, 0)),
                  bias_spec, bias_spec],
        out_specs=blk,
        scratch_shapes=[pltpu.VMEM((seq, LANES), F32) for _ in range(12)],
        compiler_params=_cparams("parallel", "parallel"),
        name="attn_prompt",
    )(q, k, v, b1, b4, b16)


def _attn_sample_kernel(q_ref, kn_ref, vn_ref, kt_ref, vt_ref, b1_ref, b4_ref, b16_ref, bn_ref, o_ref):
    t_len = q_ref.shape[1]
    rows = N_HEADS * t_len
    q = q_ref[0]
    head_of_row = lax.broadcasted_iota(jnp.int32, (rows, D_ATTN), 0) // t_len
    head_of_lane = lax.broadcasted_iota(jnp.int32, (rows, D_ATTN), 1) // HEAD_DIM
    own = head_of_row == head_of_lane
    qrows = jnp.where(own, jnp.concatenate([q] * N_HEADS, axis=0), 0.0).astype(BF16)
    pad = jnp.zeros((LANES - t_len, D_ATTN), F32)
    kn = jnp.concatenate([kn_ref[0], pad], axis=0).astype(BF16)
    vn = jnp.concatenate([vn_ref[0], pad], axis=0).astype(BF16)
    kt = kt_ref[0].reshape(D_ATTN, -1).astype(BF16)
    vt = vt_ref[0].reshape(D_ATTN, -1).astype(BF16)
    s_cache = _dot(qrows, kt)
    s_new = _dot_nt(qrows, kn)
    ms, ls, accs = [], [], []
    for p, bias_ref in enumerate((b1_ref, b4_ref, b16_ref)):
        w = bias_ref.shape[1]
        sc = s_cache[:, -w:] + bias_ref[...]
        sn = s_new + bn_ref[p]
        m = jnp.maximum(jnp.max(sc, axis=-1, keepdims=True), jnp.max(sn, axis=-1, keepdims=True))
        pc = jnp.exp(sc - m)
        pn = jnp.exp(sn - m)
        ls.append(jnp.sum(pc, axis=-1, keepdims=True) + jnp.sum(pn, axis=-1, keepdims=True))
        accs.append(_dot_nt(pc.astype(BF16), vt[:, -w:]) + _dot(pn.astype(BF16), vn))
        ms.append(m)
    mx = jnp.maximum(jnp.maximum(ms[0], ms[1]), ms[2])
    ws = [jnp.exp(m - mx) for m in ms]
    num = ws[0] * accs[0] + ws[1] * accs[1] + ws[2] * accs[2]
    den = ws[0] * ls[0] + ws[1] * ls[1] + ws[2] * ls[2]
    full = jnp.where(own, num / den, 0.0)
    out = full[0:t_len]
    for h in range(1, N_HEADS):
        out = out + full[h * t_len:(h + 1) * t_len]
    o_ref[0] = out


def _sample_attention(rel_bias, q, kn, vn, ck, cv):
    b, t_len, _ = q.shape
    w_buf = ck.shape[1]
    rows = N_HEADS * t_len
    tables, new_t = _sample_bucket_tables(w_buf, t_len)
    widest = max(tb.shape[1] for tb in tables)
    pieces = tables + [new_t.reshape(3 * t_len, LANES)]
    stacked = np.concatenate([np.pad(tb, ((0, 0), (0, widest - tb.shape[1])), constant_values=-1) for tb in pieces])
    bias = _bias_table(rel_bias, stacked)
    b1, b4, b16 = (bias[:, i * t_len:(i + 1) * t_len, :tb.shape[1]].reshape(rows, tb.shape[1])
                   for i, tb in enumerate(tables))
    bn = bias[:, 3 * t_len:, :LANES].reshape(N_HEADS, 3, t_len, LANES).transpose(1, 0, 2, 3).reshape(3, rows, LANES)
    new_spec = pl.BlockSpec((1, t_len, D_ATTN), lambda i: (i, 0, 0))
    cache_spec = pl.BlockSpec((1, N_HEADS, HEAD_DIM, w_buf), lambda i: (i, 0, 0, 0))
    const2 = lambda i: (0, 0)
    return pl.pallas_call(
        _attn_sample_kernel,
        out_shape=jax.ShapeDtypeStruct(q.shape, F32),
        grid=(b,),
        in_specs=[new_spec, new_spec, new_spec, cache_spec, cache_spec,
                  pl.BlockSpec(b1.shape, const2), pl.BlockSpec(b4.shape, const2), pl.BlockSpec(b16.shape, const2),
                  pl.BlockSpec(bn.shape, lambda i: (0, 0, 0))],
        out_specs=new_spec,
        compiler_params=_cparams("parallel"),
        name="attn_sample",
    )(q, kn, vn, ck.transpose(0, 2, 3, 1), cv.transpose(0, 2, 3, 1), b1, b4, b16, bn)


def _sgu_kernel(u_ref, g_ref, mix_ref, bias_ref, lng_ref, lnb_ref, sgu_ref, *z2_out):
    n_chunks = u_ref.shape[0] // CHUNK
    group_of_lane = lax.broadcasted_iota(jnp.int32, (CHUNK, D_SGU), 1) // (D_SGU // N_GROUPS_SGU)
    for c in range(n_chunks):
        rows = slice(c * CHUNK, (c + 1) * CHUNK)
        z1 = _gelu(u_ref[rows])
        z2 = _layer_norm(_gelu(g_ref[rows]), lng_ref[...], lnb_ref[...])
        if z2_out:
            z2_out[0][rows] = z2
        mixed = bias_ref[...]
        for g in range(N_GROUPS_SGU):
            mixed = mixed + _dot(mix_ref[g], jnp.where(group_of_lane == g, z2, 0.0).astype(BF16))
        sgu_ref[rows] = (z1 * mixed).astype(sgu_ref.dtype)


def _sgu(u, g, mix, bias, ln_g, ln_b, want_z2, chunks_per_step):
    m = u.shape[0]
    tm = CHUNK * chunks_per_step
    row_spec = pl.BlockSpec((tm, D_SGU), lambda i: (i, 0))
    const2 = lambda i: (0, 0)
    out_shape = [jax.ShapeDtypeStruct((m, D_SGU), BF16)]
    out_specs = [row_spec]
    if want_z2:
        out_shape.append(jax.ShapeDtypeStruct((m, D_SGU), F32))
        out_specs.append(row_spec)
    return pl.pallas_call(
        _sgu_kernel,
        out_shape=out_shape,
        grid=(m // tm,),
        in_specs=[row_spec, row_spec,
                  pl.BlockSpec(mix.shape, lambda i: (0, 0, 0)),
                  pl.BlockSpec(bias.shape, const2),
                  pl.BlockSpec(ln_g.shape, const2), pl.BlockSpec(ln_b.shape, const2)],
        out_specs=out_specs,
        compiler_params=_cparams("parallel"),
        name="sgu",
    )(u, g, mix, bias, ln_g, ln_b)


def _top1_rows(x, row):
    m = jnp.max(x, axis=0, keepdims=True)
    i = jnp.min(jnp.where(x == m, row, x.shape[0]), axis=0, keepdims=True)
    return m, i


def _pool_specs(n_prompt, tm, width):
    tiles_p = n_prompt // tm
    return (pl.BlockSpec((tm, width), lambda i, *_: (jnp.minimum(i, tiles_p - 1), 0)),
            pl.BlockSpec((tm, width), lambda i, *_: (jnp.maximum(i - tiles_p, 0), 0)))


def _pool_rows(tiles_p, prompt_ref, sample_ref, dtype):
    return jnp.where(pl.program_id(0) < tiles_p, prompt_ref[...].astype(dtype), sample_ref[...].astype(dtype))


def _mid_kernel(alpha, tiles_p, xp_ref, xs_ref, ap_ref, as_ref, sp_ref, ss_ref, wmix_ref, g1_ref, b1_ref,
                wrt_ref, rb_ref, x1_ref, x1p_ref, idx_ref, gate_ref):
    y = (_dot(_pool_rows(tiles_p, ap_ref, as_ref, BF16), wmix_ref[0:D_ATTN])
         + _dot(_pool_rows(tiles_p, sp_ref, ss_ref, BF16), wmix_ref[D_ATTN:])
         + alpha * _pool_rows(tiles_p, xp_ref, xs_ref, F32))
    x1 = _layer_norm(y, g1_ref[...], b1_ref[...])
    x1_ref[...] = x1
    tm = x1.shape[0]
    x1b = x1.astype(BF16)
    x1p_ref[...] = _pack_bf16_pairs(x1b)
    scores = _sigmoid(_dot_nt(wrt_ref[...], x1b))
    biased = scores + rb_ref[...]
    per_group = N_EXPERTS // N_EXPERT_GROUPS
    row_g = lax.broadcasted_iota(jnp.int32, (per_group, tm), 0)
    row_8 = lax.broadcasted_iota(jnp.int32, (N_EXPERT_GROUPS, tm), 0)
    row_e = lax.broadcasted_iota(jnp.int32, (N_EXPERTS, tm), 0)
    cur = jnp.full((N_EXPERT_GROUPS, tm), NEG_INF, F32)
    for g in range(N_EXPERT_GROUPS):
        blk = biased[g * per_group:(g + 1) * per_group]
        m_a, i_a = _top1_rows(blk, row_g)
        m_b = jnp.max(jnp.where(row_g == i_a, NEG_INF, blk), axis=0, keepdims=True)
        cur = jnp.where(row_8 == g, m_a + m_b, cur)
    chosen = jnp.zeros((N_EXPERT_GROUPS, tm), F32)
    for _ in range(TOPK_GROUPS):
        _, i_g = _top1_rows(cur, row_8)
        sel = row_8 == i_g
        chosen = jnp.where(sel, 1.0, chosen)
        cur = jnp.where(sel, NEG_INF, cur)
    cur = jnp.concatenate([jnp.where(chosen[g:g + 1] > 0.0, biased[g * per_group:(g + 1) * per_group], NEG_INF)
                           for g in range(N_EXPERT_GROUPS)], axis=0)
    idx = jnp.zeros((TOP_K, tm), jnp.int32)
    gates = jnp.zeros((TOP_K, tm), F32)
    for k in range(TOP_K):
        _, i_e = _top1_rows(cur, row_e)
        sel = row_e == i_e
        gate_k = jnp.sum(jnp.where(sel, scores, 0.0), axis=0, keepdims=True)
        cur = jnp.where(sel, NEG_INF, cur)
        idx = jnp.where(row_8 == k, i_e, idx)
        gates = jnp.where(row_8 == k, gate_k, gates)
    idx_ref[...] = idx
    gate_ref[...] = gates / jnp.sum(gates, axis=0, keepdims=True) * ROUTED_SCALE


def _mid(xp, xs, attn_p, attn_s, sgu_p, sgu_s, wmix, g1, b1, wrt, rb, alpha, tm):
    n_p, d = xp.shape
    t = n_p + xs.shape[0]
    const2 = lambda i: (0, 0)
    return pl.pallas_call(
        functools.partial(_mid_kernel, alpha, n_p // tm),
        out_shape=[jax.ShapeDtypeStruct((t, d), F32),
                   jax.ShapeDtypeStruct((t, d // 2), jnp.uint32),
                   jax.ShapeDtypeStruct((TOP_K, t), jnp.int32),
                   jax.ShapeDtypeStruct((TOP_K, t), F32)],
        grid=(t // tm,),
        in_specs=[*_pool_specs(n_p, tm, d), *_pool_specs(n_p, tm, D_ATTN), *_pool_specs(n_p, tm, D_SGU),
                  pl.BlockSpec(wmix.shape, const2), pl.BlockSpec(g1.shape, const2),
                  pl.BlockSpec(b1.shape, const2), pl.BlockSpec(wrt.shape, const2),
                  pl.BlockSpec(rb.shape, const2)],
        out_specs=[pl.BlockSpec((tm, d), lambda i: (i, 0)),
                   pl.BlockSpec((tm, d // 2), lambda i: (i, 0)),
                   pl.BlockSpec((TOP_K, tm), lambda i: (0, i)),
                   pl.BlockSpec((TOP_K, tm), lambda i: (0, i))],
        compiler_params=_cparams("parallel"),
        name="mix_ln_router",
    )(xp, xs, attn_p, attn_s, sgu_p, sgu_s, wmix, g1, b1, wrt, rb)


def _rank_kernel(idx_ref, rank_ref, counts_ref, run_ref):
    @pl.when(pl.program_id(0) == 0)
    def _():
        run_ref[...] = jnp.zeros_like(run_ref)

    tm = idx_ref.shape[1]
    row_e = lax.broadcasted_iota(jnp.int32, (N_EXPERTS, tm), 0)
    idx = idx_ref[...]
    onehot = jnp.zeros((N_EXPERTS, tm), F32)
    for k in range(TOP_K):
        onehot = onehot + jnp.where(row_e == idx[k:k + 1], 1.0, 0.0)
    earlier = (lax.broadcasted_iota(jnp.int32, (tm, tm), 0)
               < lax.broadcasted_iota(jnp.int32, (tm, tm), 1))
    before = run_ref[...] + _dot(onehot.astype(BF16), jnp.where(earlier, 1.0, 0.0).astype(BF16))
    row_k = lax.broadcasted_iota(jnp.int32, (TOP_K, tm), 0)
    ranks = jnp.zeros((TOP_K, tm), F32)
    for k in range(TOP_K):
        rank_k = jnp.sum(jnp.where(row_e == idx[k:k + 1], before, 0.0), axis=0, keepdims=True)
        ranks = jnp.where(row_k == k, rank_k, ranks)
    rank_ref[...] = ranks.astype(jnp.int32)
    run_ref[...] = run_ref[...] + jnp.sum(onehot, axis=1, keepdims=True)
    counts_ref[...] = run_ref[...]


def _rank(idx, tm):
    t = idx.shape[1]
    return pl.pallas_call(
        _rank_kernel,
        out_shape=[jax.ShapeDtypeStruct((TOP_K, t), jnp.int32),
                   jax.ShapeDtypeStruct((N_EXPERTS, 1), F32)],
        grid=(t // tm,),
        in_specs=[pl.BlockSpec((TOP_K, tm), lambda i: (0, i))],
        out_specs=[pl.BlockSpec((TOP_K, tm), lambda i: (0, i)),
                   pl.BlockSpec((N_EXPERTS, 1), lambda i: (0, 0))],
        scratch_shapes=[pltpu.VMEM((N_EXPERTS, 1), F32)],
        compiler_params=_cparams("arbitrary"),
        name="expert_rank",
    )(idx)


def _dest_kernel(idx_ref, rank_ref, pstart_ref, dest_ref):
    tm = idx_ref.shape[1]
    row_e = lax.broadcasted_iota(jnp.int32, (N_EXPERTS, tm), 0)
    row_k = lax.broadcasted_iota(jnp.int32, (TOP_K, tm), 0)
    idx = idx_ref[...]
    start = jnp.zeros((TOP_K, tm), F32)
    for k in range(TOP_K):
        start_k = jnp.sum(jnp.where(row_e == idx[k:k + 1], pstart_ref[...], 0.0), axis=0, keepdims=True)
        start = jnp.where(row_k == k, start_k, start)
    dest = start.astype(jnp.int32) + rank_ref[...]
    for c in range(tm // SC_CHUNK):
        dest_ref[c] = dest[:, c * SC_CHUNK:(c + 1) * SC_CHUNK]


def _dest(idx, rank, pstart_col, tm):
    t = idx.shape[1]
    tok_spec = pl.BlockSpec((TOP_K, tm), lambda i: (0, i))
    return pl.pallas_call(
        _dest_kernel,
        out_shape=jax.ShapeDtypeStruct((t // SC_CHUNK, TOP_K, SC_CHUNK), jnp.int32),
        grid=(t // tm,),
        in_specs=[tok_spec, tok_spec, pl.BlockSpec(pstart_col.shape, lambda i: (0, 0))],
        out_specs=pl.BlockSpec((tm // SC_CHUNK, TOP_K, SC_CHUNK), lambda i: (i, 0, 0)),
        compiler_params=_cparams("parallel"),
        name="moe_dest",
    )(idx, rank, pstart_col)


def _sc_mesh():
    return plsc.VectorSubcoreMesh(core_axis_name="c", subcore_axis_name="s",
                                  num_cores=SC_CORES, num_subcores=SC_SUBCORES)


def _sc_chunks(n_chunks):
    workers = SC_CORES * SC_SUBCORES
    wid = lax.axis_index("s") * SC_CORES + lax.axis_index("c")
    return wid, workers, (n_chunks - wid + workers - 1) // workers


def _sc_dispatch(x_rows, dest, n_rows):
    n_chunks = dest.shape[0]
    width = x_rows.shape[1]

    @functools.partial(
        pl.kernel, mesh=_sc_mesh(),
        out_type=jax.ShapeDtypeStruct((n_rows, width), x_rows.dtype),
        scratch_types=[pltpu.VMEM((TOP_K, SC_CHUNK), jnp.int32), pltpu.VMEM((SC_CHUNK, width), x_rows.dtype),
                       pltpu.SemaphoreType.DMA],
    )
    def scatter(x_hbm, dest_hbm, out_hbm, dest_v, rows_v, sem):
        wid, workers, n_own = _sc_chunks(n_chunks)

        @pl.loop(0, n_own)
        def _(j):
            ch = wid + j * workers
            pltpu.sync_copy(dest_hbm.at[ch], dest_v)
            pltpu.sync_copy(x_hbm.at[pl.ds(ch * SC_CHUNK, SC_CHUNK)], rows_v)
            copies = [pltpu.async_copy(rows_v, out_hbm.at[dest_v.at[k]], sem) for k in range(TOP_K)]
            for cp in copies:
                cp.wait()

    return scatter(x_rows, dest)


def _expert_kernel(first_ref, nblk_ref, count_ref, nused_ref, xs_hbm, wg_hbm, wu_hbm, wd_hbm, ys_hbm,
                   xbuf, ybuf, xsem, ysem, wg_f, wu_f, wd_f, wsem):
    n_exp = wg_hbm.shape[0]
    n_used = nused_ref[0]
    slots, _, half = xbuf.shape
    weight_slots = wg_f.shape[0]

    def block_rows(j):
        return pl.ds(pl.multiple_of(j * EXPERT_ROWS, EXPERT_ROWS), EXPERT_ROWS)

    def x_copy(j, slot):
        return pltpu.make_async_copy(xs_hbm.at[block_rows(j)], xbuf.at[slot], xsem.at[slot])

    def y_copy(j, slot):
        return pltpu.make_async_copy(ybuf.at[slot], ys_hbm.at[block_rows(j)], ysem.at[slot])

    def weight_copies(e, slot):
        return [pltpu.make_async_copy(src.at[e], dst.at[slot], wsem.at[i, slot])
                for i, (src, dst) in enumerate(((wg_hbm, wg_f), (wu_hbm, wu_f), (wd_hbm, wd_f)))]

    for k in range(slots - 1):
        @pl.when(k < n_used)
        def _():
            x_copy(k, k).start()
    for k in range(min(weight_slots - 1, n_exp)):
        for cp in weight_copies(k, k):
            cp.start()

    def expert(e, carry):
        wslot = e % weight_slots
        for cp in weight_copies(e, wslot):
            cp.wait()
        ahead_e = e + weight_slots - 1

        @pl.when(ahead_e < n_exp)
        def _():
            for cp in weight_copies(ahead_e, ahead_e % weight_slots):
                cp.start()

        def block(b, carry):
            j = first_ref[e] + b
            slot = j % slots
            x_copy(j, slot).wait()
            ahead = j + slots - 1

            @pl.when(ahead < n_used)
            def _():
                x_copy(ahead, ahead % slots).start()

            @pl.when(j >= slots)
            def _():
                y_copy(j - slots, slot).wait()

            row = lax.broadcasted_iota(jnp.int32, (EXPERT_ROWS, half), 0)
            packed = jnp.where(row < count_ref[e] - b * EXPERT_ROWS, xbuf[slot], jnp.uint32(0))
            lo, hi = _unpack_bf16_pairs(packed)
            wg, wu = wg_f[wslot].astype(BF16), wu_f[wslot].astype(BF16)
            g = _dot(lo, wg[:half]) + _dot(hi, wg[half:])
            u = _dot(lo, wu[:half]) + _dot(hi, wu[half:])
            h = (g * _sigmoid(g) * u).astype(BF16)
            ybuf[slot] = _pack_bf16_pairs(_dot(h, wd_f[wslot].astype(BF16)).astype(BF16))
            y_copy(j, slot).start(priority=1)
            return carry

        lax.fori_loop(0, nblk_ref[e], block, 0)
        return carry

    lax.fori_loop(0, n_exp, expert, 0)

    for k in range(1, slots + 1):
        @pl.when(n_used >= k)
        def _():
            y_copy(n_used - k, (n_used - k) % slots).wait()


def _experts(first_block, n_block, counts, n_used, xs, wg, wu, wd):
    n_rows, half = xs.shape
    n_exp, d, de = wg.shape
    any_spec = pl.BlockSpec(memory_space=pl.ANY)
    return pl.pallas_call(
        _expert_kernel,
        out_shape=jax.ShapeDtypeStruct((n_rows, half), jnp.uint32),
        grid_spec=pltpu.PrefetchScalarGridSpec(
            num_scalar_prefetch=4,
            grid=(1,),
            in_specs=[any_spec, any_spec, any_spec, any_spec],
            out_specs=any_spec,
            scratch_shapes=[pltpu.VMEM((EXPERT_SLOTS, EXPERT_ROWS, half), jnp.uint32),
                            pltpu.VMEM((EXPERT_SLOTS, EXPERT_ROWS, half), jnp.uint32),
                            pltpu.SemaphoreType.DMA((EXPERT_SLOTS,)), pltpu.SemaphoreType.DMA((EXPERT_SLOTS,)),
                            pltpu.VMEM((WEIGHT_SLOTS, d, de), F32), pltpu.VMEM((WEIGHT_SLOTS, d, de), F32),
                            pltpu.VMEM((WEIGHT_SLOTS, de, d), F32), pltpu.SemaphoreType.DMA((3, WEIGHT_SLOTS))]),
        compiler_params=_cparams("arbitrary"),
        name="routed_experts",
    )(first_block, n_block, counts, n_used, xs, wg, wu, wd)


def _dense_kernel(alpha, tiles_p, x1_ref, pp_ref, ps_ref, wgs_ref, wus_ref, wds_ref, wpg_ref, wpp_ref, base_ref):
    x1 = x1_ref[...]
    xb = x1.astype(BF16)
    g = _dot(xb, wgs_ref[...])
    h = (g * _sigmoid(g) * _dot(xb, wus_ref[...])).astype(BF16)
    shared = _dot(h, wds_ref[...])
    ple = _sigmoid(_dot(xb, wpg_ref[...])) * _dot(_pool_rows(tiles_p, pp_ref, ps_ref, BF16), wpp_ref[...])
    base_ref[...] = alpha * x1 + shared + ple


def _dense(x1, pp, ps, wgs, wus, wds, wpg, wpp, alpha, tm):
    t, d = x1.shape
    const2 = lambda i: (0, 0)
    return pl.pallas_call(
        functools.partial(_dense_kernel, alpha, pp.shape[0] // tm),
        out_shape=jax.ShapeDtypeStruct((t, d), F32),
        grid=(t // tm,),
        in_specs=[pl.BlockSpec((tm, d), lambda i: (i, 0)), *_pool_specs(pp.shape[0], tm, pp.shape[1]),
                  pl.BlockSpec(wgs.shape, const2), pl.BlockSpec(wus.shape, const2),
                  pl.BlockSpec(wds.shape, const2), pl.BlockSpec(wpg.shape, const2),
                  pl.BlockSpec(wpp.shape, const2)],
        out_specs=pl.BlockSpec((tm, d), lambda i: (i, 0)),
        compiler_params=_cparams("parallel"),
        name="shared_ple",
    )(x1, pp, ps, wgs, wus, wds, wpg, wpp)


def _sc_gather(y_rows, dest, n_tok):
    n_chunks = dest.shape[0]
    width = y_rows.shape[1]
    nbuf = SC_GATHER_BUFFERS
    items = [(k, h) for k in range(TOP_K) for h in range(SC_CHUNK // SC_GATHER_ROWS)]

    @functools.partial(
        pl.kernel, mesh=_sc_mesh(),
        out_type=jax.ShapeDtypeStruct((TOP_K, n_tok, width), y_rows.dtype),
        scratch_types=[pltpu.VMEM((TOP_K, SC_CHUNK), jnp.int32),
                       pltpu.VMEM((nbuf, SC_GATHER_ROWS, width), y_rows.dtype),
                       pltpu.SemaphoreType.DMA((nbuf,)), pltpu.SemaphoreType.DMA((nbuf,))],
    )
    def gather(y_hbm, dest_hbm, out_hbm, dest_v, rows_v, gsem, wsem):
        wid, workers, n_own = _sc_chunks(n_chunks)

        @pl.loop(0, n_own)
        def _(j):
            ch = wid + j * workers
            pltpu.sync_copy(dest_hbm.at[ch], dest_v)

            def fetch(n):
                k, h = items[n]
                return pltpu.async_copy(y_hbm.at[dest_v.at[k, pl.ds(h * SC_GATHER_ROWS, SC_GATHER_ROWS)]],
                                        rows_v.at[n % nbuf], gsem.at[n % nbuf])

            def write(n):
                k, h = items[n]
                rows = pl.ds(ch * SC_CHUNK + h * SC_GATHER_ROWS, SC_GATHER_ROWS)
                return pltpu.async_copy(rows_v.at[n % nbuf], out_hbm.at[k, rows], wsem.at[n % nbuf])

            fetches = {n: fetch(n) for n in range(min(nbuf - 1, len(items)))}
            writes = {}
            for n in range(len(items)):
                fetches.pop(n).wait()
                ahead = n + nbuf - 1
                if ahead < len(items):
                    if ahead - nbuf in writes:
                        writes.pop(ahead - nbuf).wait()
                    fetches[ahead] = fetch(ahead)
                writes[n] = write(n)
            for n in sorted(writes):
                writes[n].wait()

    return gather(y_rows, dest)


def _combine_kernel(tiles_p, gate_ref, base_ref, yk_ref, g2_ref, b2_ref, outp_ref, outs_ref):
    gates = gate_ref[...]
    half = yk_ref.shape[2]
    acc_lo = base_ref[:, :half]
    acc_hi = base_ref[:, half:]
    for k in range(TOP_K):
        packed = yk_ref[k]
        gate = gates[:, k:k + 1]
        acc_lo = acc_lo + gate * lax.bitcast_convert_type(packed << 16, F32)
        acc_hi = acc_hi + gate * lax.bitcast_convert_type(packed & jnp.uint32(0xFFFF0000), F32)
    out = _layer_norm(jnp.concatenate([acc_lo, acc_hi], axis=1), g2_ref[...], b2_ref[...])
    is_prompt = pl.program_id(0) < tiles_p

    @pl.when(is_prompt)
    def _():
        outp_ref[...] = out

    @pl.when(jnp.logical_not(is_prompt))
    def _():
        outs_ref[...] = out


def _combine(gates_tok, base, yk, g2, b2, n_prompt, tm):
    t, d = base.shape
    const2 = lambda i: (0, 0)
    return pl.pallas_call(
        functools.partial(_combine_kernel, n_prompt // tm),
        out_shape=[jax.ShapeDtypeStruct((n_prompt, d), F32), jax.ShapeDtypeStruct((t - n_prompt, d), F32)],
        grid=(t // tm,),
        in_specs=[pl.BlockSpec((tm, TOP_K), lambda i: (i, 0)),
                  pl.BlockSpec((tm, d), lambda i: (i, 0)),
                  pl.BlockSpec((TOP_K, tm, yk.shape[2]), lambda i: (0, i, 0)),
                  pl.BlockSpec(g2.shape, const2), pl.BlockSpec(b2.shape, const2)],
        out_specs=list(_pool_specs(n_prompt, tm, d)),
        compiler_params=_cparams("arbitrary"),
        name="moe_combine",
    )(gates_tok, base, yk, g2, b2)


def _sgu_tables(sgu_w, sgu_b, rows_per_seq):
    reps = CHUNK // rows_per_seq
    tril = jnp.tril(sgu_w[:, :rows_per_seq, :rows_per_seq])
    eye = jnp.eye(reps, dtype=F32)
    mix = jnp.einsum("ab,gts->gatbs", eye, tril).reshape(N_GROUPS_SGU, CHUNK, CHUNK)
    bias = jnp.tile(jnp.repeat(sgu_b[:, :rows_per_seq].T, D_SGU // N_GROUPS_SGU, axis=1), (reps, 1))
    return mix.astype(BF16), bias


def _layer(xp, xs, ck, cv, pp, ps, w, rel_bias, alpha):
    batch, seq, d = xp.shape
    dec_b, dec_t, _ = xs.shape
    w_buf = ck.shape[1]
    n_p, n_s = batch * seq, dec_b * dec_t
    n_tok = n_p + n_s

    scale = jnp.concatenate([jnp.full((D_ATTN,), HEAD_DIM ** -0.5, F32),
                             jnp.ones((w["w_in"].shape[1] - D_ATTN,), F32)])
    w_in = (w["w_in"] * scale).astype(BF16)

    qp, kp, vp, up, gp, kp_t, vp_t = _proj(xp.reshape(n_p, d), w_in, 512, seq)
    qs, ks, vs, us, gs = _proj(xs.reshape(n_s, d), w_in, 512)

    b1, b4, b16 = _prompt_bias_tables(rel_bias)
    attn_p = _attn_prompt(qp, kp, vp, b1, b4, b16, batch, seq)

    attn_s = _sample_attention(rel_bias, qs.reshape(dec_b, dec_t, D_ATTN), ks.reshape(dec_b, dec_t, D_ATTN),
                               vs.reshape(dec_b, dec_t, D_ATTN), ck, cv).reshape(n_s, D_ATTN)

    ln_g, ln_b = w["sgu_ln_g"][None], w["sgu_ln_b"][None]
    mix_p, bias_p = _sgu_tables(w["sgu_w"], w["sgu_b"], CHUNK)
    mix_s, bias_s = _sgu_tables(w["sgu_w"], w["sgu_b"], dec_t)
    (sgu_p,) = _sgu(up, gp, mix_p, bias_p, ln_g, ln_b, False, 8)
    sgu_s, z2_s = _sgu(us, gs, mix_s, bias_s, ln_g, ln_b, True, 4)

    x1, x1_packed, idx, gates = _mid(xp.reshape(n_p, d), xs.reshape(n_s, d), attn_p, attn_s, sgu_p, sgu_s,
                                     w["w_mix_out"].astype(BF16), w["ln1_g"][None], w["ln1_b"][None],
                                     w["w_router"].T.astype(BF16), w["router_bias"][:, None], alpha, 512)

    rank, counts = _rank(idx, 512)
    counts = counts[:, 0].astype(jnp.int32)
    padded = (counts + EXPERT_ROWS - 1) // EXPERT_ROWS * EXPERT_ROWS
    pend = jnp.cumsum(padded)
    pstart = (pend - padded).astype(jnp.int32)
    n_blocks = (n_tok * TOP_K + N_EXPERTS * (EXPERT_ROWS - 1)) // EXPERT_ROWS
    n_used = (pend[-1:] // EXPERT_ROWS).astype(jnp.int32)

    dest = _dest(idx, rank, pstart.astype(F32)[:, None], 512)
    x_sorted = _sc_dispatch(x1_packed, dest, n_blocks * EXPERT_ROWS)
    y_sorted = _experts(pstart // EXPERT_ROWS, padded // EXPERT_ROWS, counts, n_used, x_sorted,
                        w["w_gate_e"], w["w_up_e"], w["w_down_e"])

    base = _dense(x1, pp.reshape(n_p, -1), ps.reshape(n_s, -1), w["w_gate_s"].astype(BF16), w["w_up_s"].astype(BF16),
                  w["w_down_s"].astype(BF16), w["w_ple_gate"].astype(BF16),
                  w["w_ple_proj"].astype(BF16), alpha, 1024)

    y_slots = _sc_gather(y_sorted, dest, n_tok)
    y_p, y_s = _combine(gates.T, base, y_slots, w["ln2_g"][None], w["ln2_b"][None], n_p, 512)
    y_p = y_p.reshape(batch, seq, d)
    y_s = y_s.reshape(dec_b, dec_t, d)
    keep = min(MAX_DISTANCE, seq)
    k_rows = kp_t.reshape(batch, N_HEADS, HEAD_DIM, seq)[..., seq - keep:].transpose(0, 3, 1, 2)
    v_rows = vp_t.reshape(batch, N_HEADS, HEAD_DIM, seq)[..., seq - keep:].transpose(0, 3, 1, 2)
    return (y_p, y_s, k_rows, v_rows,
            ks.reshape(dec_b, dec_t, N_HEADS, HEAD_DIM), vs.reshape(dec_b, dec_t, N_HEADS, HEAD_DIM),
            z2_s.reshape(dec_b, dec_t, D_SGU))


def kernel(x_prompt, x_sample, cache_k, cache_v, p_prompt, p_sample, w_in, rel_bias, sgu_w, sgu_b, sgu_ln_g, sgu_ln_b, w_mix_out, ln1_g, ln1_b, w_router, router_bias, w_gate_e, w_up_e, w_down_e, w_gate_s, w_up_s, w_down_s, w_ple_gate, w_ple_proj, ln2_g, ln2_b):
    depth = w_in.shape[0]
    alpha = (2 * depth) ** 0.25
    xp, xs = x_prompt, x_sample
    outs = [[] for _ in range(5)]
    for i in range(depth):
        w = {"w_in": w_in[i], "sgu_w": sgu_w[i], "sgu_b": sgu_b[i], "sgu_ln_g": sgu_ln_g[i],
             "sgu_ln_b": sgu_ln_b[i], "w_mix_out": w_mix_out[i], "ln1_g": ln1_g[i], "ln1_b": ln1_b[i],
             "w_router": w_router[i], "router_bias": router_bias[i], "w_gate_e": w_gate_e[i],
             "w_up_e": w_up_e[i], "w_down_e": w_down_e[i], "w_gate_s": w_gate_s[i], "w_up_s": w_up_s[i],
             "w_down_s": w_down_s[i], "w_ple_gate": w_ple_gate[i], "w_ple_proj": w_ple_proj[i],
             "ln2_g": ln2_g[i], "ln2_b": ln2_b[i]}
        xp, xs, kp, vp, ks, vs, zs = _layer(xp, xs, cache_k[i], cache_v[i], p_prompt[i], p_sample[i],
                                            w, rel_bias, alpha)
        for lst, val in zip(outs, (kp, vp, ks, vs, zs)):
            lst.append(val)
    return (xp, xs) + tuple(jnp.stack(lst) for lst in outs)
```
